```python
import math
import jax, jax.numpy as jnp
from jax import lax
import numpy as np


D_MODEL = 2048
BATCH = 8
SEQ = 4096
DEPTH = 4

GRID_W = 64
CTX_LEN = 256
N_MIXERS = 3
N_HYENA = (DEPTH + 2) // 3
N_RWKV = (DEPTH + 1) // 3
N_ATTN = DEPTH // 3
N_DENSE = (DEPTH + 1) // 2
N_MOE = DEPTH // 2
N_MOD = 6
DEEPNORM_ALPHA = (2 * DEPTH) ** 0.25
DEEPNORM_BETA = (8 * DEPTH) ** -0.25
LN_EPS = 1e-5

HY_CONV_W = 3
HY_BANDS = 16
HY_EMB = 2 * HY_BANDS + 1
HY_FILT_HID = 64
HY_DECAY_TARGET = 1e-2
HY_FAST_PCT = 0.3
HY_SLOW_PCT = 1.5
HY_MIN_DECAY = -math.log(HY_DECAY_TARGET) / HY_SLOW_PCT
HY_MAX_DECAY = -math.log(HY_DECAY_TARGET) / HY_FAST_PCT

RW_HEAD = 64
RW_HEADS = D_MODEL // RW_HEAD
RW_DECAY_LORA = 96
RW_AAA_LORA = 96
RW_GATE_LORA = 256
RW_GN_EPS = 64e-5

AT_HEAD = 64
AT_Q_HEADS = D_MODEL // AT_HEAD
AT_KV_HEADS = 4
AT_GROUP = AT_Q_HEADS // AT_KV_HEADS
AT_QKV = (AT_Q_HEADS + 2 * AT_KV_HEADS) * AT_HEAD
AT_WINDOW = 128
AT_BLOCK = 128
AT_SPAN = AT_BLOCK + 2 * AT_WINDOW
ROPE_BASE = 10000.0
NEG_INF = -1e30

FF_DENSE = 5632
N_EXPERTS = 8
TOP_K = 2
FF_EXPERT = 4096

kernel_name = 'hybrid_flow_backbone_hyena_rwkv7_swa_moe'


def layer_norm(x, w, b):
    xf = x.astype(jnp.float32)
    mu = jnp.mean(xf, axis=-1, keepdims=True)
    var = jnp.mean(jnp.square(xf - mu), axis=-1, keepdims=True)
    return ((xf - mu) * lax.rsqrt(var + LN_EPS) * w + b).astype(x.dtype)


def modulate(x, shift, scale):
    return x * (1 + scale) + shift


def centred_conv3(x, w, b):
    xp = jnp.pad(x, ((0, 0), (1, 1), (0, 0)))
    return xp[:, :-2] * w[0] + xp[:, 1:-1] * w[1] + xp[:, 2:] * w[2] + b


def centred_shift_delta(x):
    xp = jnp.pad(x, ((0, 0), (1, 1), (0, 0)))
    return 0.5 * (xp[:, :-2] + xp[:, 2:]) - x


def hyena_filter(L, f_w1, f_b1, f_w2, f_b2, f_w3, f_freq):
    t = jnp.linspace(0.0, 1.0, L, dtype=jnp.float32)[:, None]
    ang = 2 * math.pi * jnp.arange(L, dtype=jnp.float32)[:, None] / L
    f = jnp.linspace(1e-4, HY_BANDS - 1, HY_BANDS, dtype=jnp.float32)[None, :]
    z = jnp.concatenate([t, jnp.cos(f * ang), -jnp.sin(f * ang)], axis=-1)
    h = jnp.sin(f_freq[0] * (z @ f_w1 + f_b1))
    h = jnp.sin(f_freq[1] * (h @ f_w2 + f_b2))
    h = (h @ f_w3).astype(jnp.float32)
    half = L // 2
    dist = jnp.abs(jnp.arange(L, dtype=jnp.float32) - half) / half
    deltas = jnp.linspace(HY_MIN_DECAY, HY_MAX_DECAY, D_MODEL, dtype=jnp.float32)
    h = h * jnp.exp(-dist[:, None] * deltas[None, :])
    return h / (jnp.sum(jnp.abs(h), axis=0, keepdims=True) + 1e-6)


def hyena_mix(u, w_in, b_in, conv_w, conv_b, f_w1, f_b1, f_w2, f_b2, f_w3, f_freq, skip, w_out, b_out):
    L = u.shape[1]
    z = centred_conv3(u @ w_in + b_in, conv_w, conv_b)
    x0, x1, v = jnp.split(z, 3, axis=-1)
    h = hyena_filter(L, f_w1, f_b1, f_w2, f_b2, f_w3, f_freq)
    vv = (x1 * v).astype(jnp.float32)
    n = 2 * L
    y = jnp.fft.irfft(jnp.fft.rfft(vv, n=n, axis=1) * jnp.fft.rfft(h, n=n, axis=0)[None], n=n, axis=1)
    y = y[:, L // 2: L // 2 + L] + vv * skip
    return (x0 * y.astype(u.dtype)) @ w_out + b_out


def wkv7_scan(r, w, k, v, kk, a, reverse):
    B, T, H, N = r.shape

    def step(S, inp):
        r_t, w_t, k_t, v_t, kk_t, a_t = inp
        sa = jnp.einsum('bhvk,bhk->bhv', S, -kk_t)
        S = (S * w_t[:, :, None, :] + sa[..., None] * (kk_t * a_t)[:, :, None, :]
             + v_t[..., None] * k_t[:, :, None, :])
        return S, jnp.einsum('bhvk,bhk->bhv', S, r_t)

    xs = tuple(jnp.moveaxis(t, 1, 0) for t in (r, w, k, v, kk, a))
    S0 = jnp.zeros((B, H, N, N), jnp.float32)
    _, ys = lax.scan(step, S0, xs, reverse=reverse)
    return jnp.moveaxis(ys, 0, 1)


def swap_segments(t, n):
    return jnp.concatenate([t[:, n:], t[:, :n]], axis=1)


def rwkv7_mix(h_lat, h_ctx, need_ctx_out, mu, w_rkv, w_o, w0, w1, w2, a0, a1, a2, g1, g2,
              k_k, k_a, r_k, gn_w, gn_b):
    B, L, D = h_lat.shape
    Lc = h_ctx.shape[1]
    T = Lc + L
    hs = jnp.concatenate([h_ctx, h_lat], axis=1)
    dx = jnp.concatenate([centred_shift_delta(h_ctx), centred_shift_delta(h_lat)], axis=1)
    xr, xw, xk, xv, xa, xg = [hs + dx * mu[j] for j in range(6)]

    def heads(t):
        return t.astype(jnp.float32).reshape(B, T, RW_HEADS, RW_HEAD)

    r = heads(xr @ w_rkv[0])
    k = heads(xk @ w_rkv[1])
    v = heads(xv @ w_rkv[2])
    g = jax.nn.sigmoid(xg @ g1) @ g2
    kk = k * k_k.astype(jnp.float32).reshape(RW_HEADS, RW_HEAD)
    kk = kk / jnp.maximum(jnp.sqrt(jnp.sum(jnp.square(kk), axis=-1, keepdims=True)), 1e-12)
    k_a_h = k_a.astype(jnp.float32).reshape(RW_HEADS, RW_HEAD)
    outs = []
    for d in range(2):
        w_log = heads(-jax.nn.softplus(-(w0[d] + jnp.tanh(xw @ w1[d]) @ w2[d])) - 0.5)
        decay = jnp.exp(-jnp.exp(w_log))
        a = jax.nn.sigmoid(heads(a0[d] + (xa @ a1[d]) @ a2[d]))
        k_d = k * (1.0 + (a - 1.0) * k_a_h)
        args = (r, decay, k_d, v, kk, a)
        if d == 0:
            outs.append(wkv7_scan(*args, reverse=False))
        else:
            ys = wkv7_scan(*[swap_segments(t, Lc) for t in args], reverse=True)
            outs.append(swap_segments(ys, L))
    o = outs[0] + outs[1]
    if not need_ctx_out:
        o, r, k, v, g = [t[:, Lc:] for t in (o, r, k, v, g)]
    Tn = o.shape[1]
    m = jnp.mean(o, axis=-1, keepdims=True)
    var = jnp.mean(jnp.square(o - m), axis=-1, keepdims=True)
    on = ((o - m) * lax.rsqrt(var + RW_GN_EPS)).reshape(B, Tn, D) * gn_w + gn_b
    bonus = (jnp.sum(r * k * r_k.astype(jnp.float32), axis=-1, keepdims=True) * v).reshape(B, Tn, D)
    y = ((on + bonus).astype(h_lat.dtype) * g) @ w_o
    if need_ctx_out:
        return y[:, Lc:], y[:, :Lc]
    return y, None


def rope_1d(x, pos):
    half = x.shape[-1] // 2
    inv = ROPE_BASE ** (-jnp.arange(half, dtype=jnp.float32) / half)
    ang = pos[:, None] * inv[None, :]
    cos = jnp.cos(ang)[None, :, None, :]
    sin = jnp.sin(ang)[None, :, None, :]
    x1, x2 = x[..., :half], x[..., half:]
    return jnp.concatenate([x1 * cos - x2 * sin, x1 * sin + x2 * cos], axis=-1).astype(x.dtype)


def rope_2d(x, rows, cols):
    d = x.shape[-1] // 2
    return jnp.concatenate([rope_1d(x[..., :d], rows), rope_1d(x[..., d:], cols)], axis=-1)


def attn_mix(h_lat, h_ctx, need_ctx_out, w_qkv, b_qkv, w_o, b_o, sink):
    B, L, D = h_lat.shape
    Lc = h_ctx.shape[1]
    QD = AT_Q_HEADS * AT_HEAD
    KD = AT_KV_HEADS * AT_HEAD

    def proj(h):
        T = h.shape[1]
        qkv = h @ w_qkv + b_qkv
        q = qkv[..., :QD].reshape(B, T, AT_Q_HEADS, AT_HEAD)
        k = qkv[..., QD:QD + KD].reshape(B, T, AT_KV_HEADS, AT_HEAD)
        v = qkv[..., QD + KD:].reshape(B, T, AT_KV_HEADS, AT_HEAD)
        return q, k, v

    q, k, v = proj(h_lat)
    qc, kc, vc = proj(h_ctx)
    ROWS = L // GRID_W
    rows = jnp.broadcast_to(jnp.arange(ROWS, dtype=jnp.float32)[:, None], (ROWS, GRID_W)).reshape(-1)
    cols = jnp.broadcast_to(jnp.arange(GRID_W, dtype=jnp.float32)[None, :], (ROWS, GRID_W)).reshape(-1)
    q = rope_2d(q, rows, cols).reshape(B, L, AT_KV_HEADS, AT_GROUP, AT_HEAD)
    k = rope_2d(k, rows, cols)
    scale = AT_HEAD ** -0.5
    sink_hg = sink.astype(jnp.float32).reshape(AT_KV_HEADS, AT_GROUP)
    pad = ((0, 0), (AT_WINDOW, AT_WINDOW), (0, 0), (0, 0))
    kp = jnp.pad(k, pad)
    vp = jnp.pad(v, pad)

    def block(n):
        start = n * AT_BLOCK
        qb = lax.dynamic_slice_in_dim(q, start, AT_BLOCK, axis=1)
        kb = lax.dynamic_slice_in_dim(kp, start, AT_SPAN, axis=1)
        vb = lax.dynamic_slice_in_dim(vp, start, AT_SPAN, axis=1)
        qpos = start + jnp.arange(AT_BLOCK)
        kpos = start - AT_WINDOW + jnp.arange(AT_SPAN)
        mask = (jnp.abs(qpos[:, None] - kpos[None, :]) <= AT_WINDOW) & (kpos >= 0)[None, :] & (kpos < L)[None, :]
        s_loc = jnp.einsum('bqhgd,bkhd->bhgqk', qb, kb).astype(jnp.float32) * scale
        s_loc = jnp.where(mask, s_loc, NEG_INF)
        s_ctx = jnp.einsum('bqhgd,bchd->bhgqc', qb, kc).astype(jnp.float32) * scale
        s_sink = jnp.broadcast_to(sink_hg[None, :, :, None, None], (B, AT_KV_HEADS, AT_GROUP, AT_BLOCK, 1))
        p = jax.nn.softmax(jnp.concatenate([s_loc, s_ctx, s_sink], axis=-1), axis=-1).astype(v.dtype)
        return (jnp.einsum('bhgqk,bkhd->bqhgd', p[..., :AT_SPAN], vb)
                + jnp.einsum('bhgqc,bchd->bqhgd', p[..., AT_SPAN:AT_SPAN + Lc], vc))

    o = lax.map(block, jnp.arange(L // AT_BLOCK))
    o = jnp.moveaxis(o, 0, 1).reshape(B, L, D)
    y_lat = o @ w_o + b_o
    if not need_ctx_out:
        return y_lat, None
    qc = qc.reshape(B, Lc, AT_KV_HEADS, AT_GROUP, AT_HEAD)
    s = jnp.einsum('bqhgd,bchd->bhgqc', qc, kc).astype(jnp.float32) * scale
    s_sink = jnp.broadcast_to(sink_hg[None, :, :, None, None], (B, AT_KV_HEADS, AT_GROUP, Lc, 1))
    p = jax.nn.softmax(jnp.concatenate([s, s_sink], axis=-1), axis=-1).astype(vc.dtype)
    oc = jnp.einsum('bhgqc,bchd->bqhgd', p[..., :Lc], vc).reshape(B, Lc, D)
    return y_lat, oc @ w_o + b_o


def swiglu(h, w1, w3, w2):
    return (jax.nn.silu(h @ w1) * (h @ w3)) @ w2


def moe_ffn(h, router, w1, w3, w2):
    logits = (h @ router).astype(jnp.float32)
    top_v, top_i = lax.top_k(logits, TOP_K)
    top_w = jax.nn.softmax(top_v, axis=-1)
    gates = jnp.sum(jax.nn.one_hot(top_i, N_EXPERTS, dtype=jnp.float32) * top_w[..., None], axis=-2)
    gates = gates.astype(h.dtype)
    out = gates[..., 0:1] * swiglu(h, w1[0], w3[0], w2[0])
    for e in range(1, N_EXPERTS):
        out = out + gates[..., e:e + 1] * swiglu(h, w1[e], w3[e], w2[e])
    return out


def setup_inputs(seed: int = 0) -> dict:
    key = jax.random.key(seed)
    ks = iter(jax.random.split(key, 64))
    f32 = jnp.float32
    D = D_MODEL
    inv = D ** -0.5
    beta = DEEPNORM_BETA

    def nrm(shape, s):
        return jax.random.normal(next(ks), shape, f32) * s

    def gain(shape):
        return 1.0 + nrm(shape, 0.02)

    decay_profile = -6.5 + 5.0 * (jnp.arange(D, dtype=f32) / (D - 1)) ** 1.35
    return {
        'x': nrm((BATCH, SEQ, D), 1.0),
        'c': nrm((BATCH, D), 1.0),
        'ctx': nrm((BATCH, CTX_LEN, D), 1.0),
        'c_ctx': nrm((D,), 1.0),
        'ada_w': nrm((DEPTH, D, N_MOD * D), 0.5 * inv),
        'ada_b': nrm((DEPTH, N_MOD * D), 0.02),
        'ln_w': gain((DEPTH, 2, D)),
        'ln_b': nrm((DEPTH, 2, D), 0.02),
        'hy_w_in': nrm((N_HYENA, D, 3 * D), inv),
        'hy_b_in': nrm((N_HYENA, 3 * D), 0.02),
        'hy_conv_w': nrm((N_HYENA, HY_CONV_W, 3 * D), HY_CONV_W ** -0.5),
        'hy_conv_b': nrm((N_HYENA, 3 * D), 0.02),
        'hy_f_w1': nrm((N_HYENA, HY_EMB, HY_FILT_HID), HY_EMB ** -0.5),
        'hy_f_b1': nrm((N_HYENA, HY_FILT_HID), 0.1),
        'hy_f_w2': nrm((N_HYENA, HY_FILT_HID, HY_FILT_HID), HY_FILT_HID ** -0.5),
        'hy_f_b2': nrm((N_HYENA, HY_FILT_HID), 0.1),
        'hy_f_w3': nrm((N_HYENA, HY_FILT_HID, D), HY_FILT_HID ** -0.5),
        'hy_f_freq': gain((N_HYENA, 2, HY_FILT_HID)),
        'hy_skip': nrm((N_HYENA, D), 0.5),
        'hy_w_out': nrm((N_HYENA, D, D), beta * inv),
        'hy_b_out': nrm((N_HYENA, D), 0.02),
        'rw_mu': jax.random.uniform(next(ks), (N_RWKV, 6, D), f32),
        'rw_w_rkv': nrm((N_RWKV, 3, D, D), inv),
        'rw_w_o': nrm((N_RWKV, D, D), beta * inv),
        'rw_w0': decay_profile + nrm((N_RWKV, 2, D), 0.1),
        'rw_w1': nrm((N_RWKV, 2, D, RW_DECAY_LORA), inv),
        'rw_w2': nrm((N_RWKV, 2, RW_DECAY_LORA, D), 0.1 * RW_DECAY_LORA ** -0.5),
        'rw_a0': nrm((N_RWKV, 2, D), 0.1),
        'rw_a1': nrm((N_RWKV, 2, D, RW_AAA_LORA), inv),
        'rw_a2': nrm((N_RWKV, 2, RW_AAA_LORA, D), 0.1 * RW_AAA_LORA ** -0.5),
        'rw_g1': nrm((N_RWKV, D, RW_GATE_LORA), inv),
        'rw_g2': nrm((N_RWKV, RW_GATE_LORA, D), RW_GATE_LORA ** -0.5),
        'rw_k_k': 0.85 + nrm((N_RWKV, D), 0.02),
        'rw_k_a': gain((N_RWKV, D)),
        'rw_r_k': nrm((N_RWKV, RW_HEADS, RW_HEAD), 0.1),
        'rw_gn_w': gain((N_RWKV, D)),
        'rw_gn_b': nrm((N_RWKV, D), 0.02),
        'at_w_qkv': nrm((N_ATTN, D, AT_QKV), inv),
        'at_b_qkv': nrm((N_ATTN, AT_QKV), 0.02),
        'at_w_o': nrm((N_ATTN, D, D), beta * inv),
        'at_b_o': nrm((N_ATTN, D), 0.02),
        'at_sink': nrm((N_ATTN, AT_Q_HEADS), 1.0),
        'ff_w1': nrm((N_DENSE, D, FF_DENSE), inv),
        'ff_w3': nrm((N_DENSE, D, FF_DENSE), inv),
        'ff_w2': nrm((N_DENSE, FF_DENSE, D), beta * FF_DENSE ** -0.5),
        'moe_router': nrm((N_MOE, D, N_EXPERTS), inv),
        'moe_w1': nrm((N_MOE, N_EXPERTS, D, FF_EXPERT), inv),
        'moe_w3': nrm((N_MOE, N_EXPERTS, D, FF_EXPERT), inv),
        'moe_w2': nrm((N_MOE, N_EXPERTS, FF_EXPERT, D), beta * FF_EXPERT ** -0.5),
    }


def reference(x, c, ctx, c_ctx, ada_w, ada_b, ln_w, ln_b,
              hy_w_in, hy_b_in, hy_conv_w, hy_conv_b, hy_f_w1, hy_f_b1, hy_f_w2, hy_f_b2, hy_f_w3,
              hy_f_freq, hy_skip, hy_w_out, hy_b_out,
              rw_mu, rw_w_rkv, rw_w_o, rw_w0, rw_w1, rw_w2, rw_a0, rw_a1, rw_a2, rw_g1, rw_g2,
              rw_k_k, rw_k_a, rw_r_k, rw_gn_w, rw_gn_b,
              at_w_qkv, at_b_qkv, at_w_o, at_b_o, at_sink,
              ff_w1, ff_w3, ff_w2, moe_router, moe_w1, moe_w3, moe_w2):
    B, L, D = x.shape
    Lc = ctx.shape[1]
    a = DEEPNORM_ALPHA
    x_lat, x_ctx = x, ctx
    for i in range(DEPTH):
        last = i == DEPTH - 1
        kind = i % N_MIXERS
        j = i // N_MIXERS
        m_lat = (jax.nn.silu(c) @ ada_w[i] + ada_b[i]).reshape(B, 1, N_MOD, D)
        sh1, sc1, g1, sh2, sc2, g2 = [m_lat[:, :, n] for n in range(N_MOD)]
        h_lat = modulate(x_lat, sh1, sc1)
        ctx_side = (not last) or kind != 0
        if ctx_side:
            m_ctx = (jax.nn.silu(c_ctx) @ ada_w[i] + ada_b[i]).reshape(N_MOD, D)
            csh1, csc1, cg1, csh2, csc2, cg2 = [m_ctx[n] for n in range(N_MOD)]
            h_ctx = modulate(x_ctx, csh1, csc1)
        if kind == 0:
            hy_args = (hy_w_in[j], hy_b_in[j], hy_conv_w[j], hy_conv_b[j], hy_f_w1[j], hy_f_b1[j],
                       hy_f_w2[j], hy_f_b2[j], hy_f_w3[j], hy_f_freq[j], hy_skip[j], hy_w_out[j], hy_b_out[j])
            y_lat = hyena_mix(h_lat, *hy_args)
            y_ctx = None if last else hyena_mix(h_ctx, *hy_args)
        elif kind == 1:
            y_lat, y_ctx = rwkv7_mix(h_lat, h_ctx, not last, rw_mu[j], rw_w_rkv[j], rw_w_o[j],
                                     rw_w0[j], rw_w1[j], rw_w2[j], rw_a0[j], rw_a1[j], rw_a2[j],
                                     rw_g1[j], rw_g2[j], rw_k_k[j], rw_k_a[j], rw_r_k[j],
                                     rw_gn_w[j], rw_gn_b[j])
        else:
            y_lat, y_ctx = attn_mix(h_lat, h_ctx, not last, at_w_qkv[j], at_b_qkv[j],
                                    at_w_o[j], at_b_o[j], at_sink[j])
        x_lat = layer_norm(a * x_lat + g1 * y_lat, ln_w[i, 0], ln_b[i, 0])
        if not last:
            x_ctx = layer_norm(a * x_ctx + cg1 * y_ctx, ln_w[i, 0], ln_b[i, 0])
            h2 = jnp.concatenate([modulate(x_ctx, csh2, csc2), modulate(x_lat, sh2, sc2)], axis=1)
        else:
            h2 = modulate(x_lat, sh2, sc2)
        fj = i // 2
        if i % 2 == 0:
            f = swiglu(h2, ff_w1[fj], ff_w3[fj], ff_w2[fj])
        else:
            f = moe_ffn(h2, moe_router[fj], moe_w1[fj], moe_w3[fj], moe_w2[fj])
        if not last:
            x_ctx = layer_norm(a * x_ctx + cg2 * f[:, :Lc], ln_w[i, 1], ln_b[i, 1])
            f_lat = f[:, Lc:]
        else:
            f_lat = f
        x_lat = layer_norm(a * x_lat + g2 * f_lat, ln_w[i, 1], ln_b[i, 1])
    return x_lat
```

```python
import functools
import math

import jax
import jax.numpy as jnp
from jax import lax
from jax.experimental import pallas as pl
from jax.experimental.pallas import tpu as pltpu

F32 = jnp.float32
BF16 = jnp.bfloat16
HIGHEST = lax.Precision.HIGHEST

VMEM_LIMIT_BYTES = 56 * 1024 * 1024
LANES = 128
ROW_TILE = 256

LN_EPS = 1e-5
N_MOD = 6
HEAD = 64
RW_GN_EPS = 64e-5
RW_CHUNK = 64
RW_GROUP = 4
AT_KV_HEADS = 4
AT_WINDOW = 128
AT_BLOCK = 128
GRID_W = 64
ROPE_BASE = 10000.0
NEG_INF = -1e30
TOP_K = 2
HY_BANDS = 16
HY_EMB = 2 * HY_BANDS + 1
HY_MIN_DECAY = -math.log(1e-2) / 1.5
HY_MAX_DECAY = -math.log(1e-2) / 0.3


def _params(*sem):
    return pltpu.CompilerParams(dimension_semantics=sem, vmem_limit_bytes=VMEM_LIMIT_BYTES)


def _tile(n, pref, mult=LANES):
    if n <= pref:
        return n
    t = (pref // mult) * mult
    while t >= mult:
        if n % t == 0:
            return t
        t -= mult
    return n


def _mm_kernel(*refs, n_w, has_bias, act):
    x = refs[0][...].astype(BF16)
    o_ref = refs[-1]
    acc = jnp.dot(x, refs[1][...].astype(BF16), preferred_element_type=F32)
    if has_bias:
        acc = acc + refs[1 + n_w][...]
    if act == "swiglu":
        acc3 = jnp.dot(x, refs[2][...].astype(BF16), preferred_element_type=F32)
        acc = acc * jax.nn.sigmoid(acc) * acc3
    o_ref[...] = acc.astype(o_ref.dtype)


def _mm(x, w, *, w3=None, bias=None, out_dtype=F32, tm=512, tn=512):
    M, K = x.shape
    N = w.shape[1]
    tm = _tile(M, tm, 8)
    tn = _tile(N, tn)
    ws = [w] if w3 is None else [w, w3]
    in_specs = [pl.BlockSpec((tm, K), lambda i, j: (i, 0))]
    in_specs += [pl.BlockSpec((K, tn), lambda i, j: (0, j)) for _ in ws]
    args = [x] + ws
    if bias is not None:
        in_specs.append(pl.BlockSpec((1, tn), lambda i, j: (0, j)))
        args.append(bias.reshape(1, N).astype(F32))
    return pl.pallas_call(
        functools.partial(_mm_kernel, n_w=len(ws), has_bias=bias is not None,
                          act="swiglu" if w3 is not None else None),
        grid=(M // tm, N // tn),
        in_specs=in_specs,
        out_specs=pl.BlockSpec((tm, tn), lambda i, j: (i, j)),
        out_shape=jax.ShapeDtypeStruct((M, N), out_dtype),
        compiler_params=_params("parallel", "parallel"),
        name="mm_swiglu" if w3 is not None else "mm",
    )(*args)


def _apply_act(v, kind):
    if kind == "sigmoid":
        return jax.nn.sigmoid(v)
    if kind == "tanh":
        return jnp.tanh(v)
    if kind == "logdecay":
        return -jnp.exp(-jax.nn.softplus(-v) - 0.5)
    return v


def _lora_kernel(x_ref, a_ref, b_ref, bias_ref, o_ref, *, mid, fin):
    t = jnp.dot(x_ref[...], a_ref[...], preferred_element_type=F32)
    t = _apply_act(t, mid).astype(BF16)
    y = jnp.dot(t, b_ref[...], preferred_element_type=F32) + bias_ref[...]
    o_ref[...] = _apply_act(y, fin).astype(o_ref.dtype)


def _lora(x, a, b, bias, *, mid, fin, tm=512):
    M, K = x.shape
    N = b.shape[1]
    R = -(-a.shape[1] // LANES) * LANES
    a = jnp.zeros((K, R), a.dtype).at[:, :a.shape[1]].set(a)
    b = jnp.zeros((R, N), b.dtype).at[:b.shape[0]].set(b)
    tm = _tile(M, tm, 8)
    return pl.pallas_call(
        functools.partial(_lora_kernel, mid=mid, fin=fin),
        grid=(M // tm,),
        in_specs=[pl.BlockSpec((tm, K), lambda i: (i, 0)),
                  pl.BlockSpec((K, R), lambda i: (0, 0)),
                  pl.BlockSpec((R, N), lambda i: (0, 0)),
                  pl.BlockSpec((1, N), lambda i: (0, 0))],
        out_specs=pl.BlockSpec((tm, N), lambda i: (i, 0)),
        out_shape=jax.ShapeDtypeStruct((M, N), F32),
        compiler_params=_params("parallel"),
        name="lora",
    )(x, a.astype(BF16), b.astype(BF16), bias.reshape(1, N).astype(F32))


def _ada_kernel(c_ref, w_ref, b_ref, o_ref):
    c = c_ref[...]
    s = (c * jax.nn.sigmoid(c)).astype(BF16)
    o_ref[0] = jnp.dot(s, w_ref[0].astype(BF16), preferred_element_type=F32) + b_ref[0]


def _ada(cond, ada_w, ada_b):
    depth, D, N = ada_w.shape
    R = cond.shape[0]
    tn = _tile(N, 1024)
    return pl.pallas_call(
        _ada_kernel,
        grid=(depth, N // tn),
        in_specs=[pl.BlockSpec((R, D), lambda i, j: (0, 0)),
                  pl.BlockSpec((1, D, tn), lambda i, j: (i, 0, j)),
                  pl.BlockSpec((1, 1, tn), lambda i, j: (i, 0, j))],
        out_specs=pl.BlockSpec((1, R, tn), lambda i, j: (i, 0, j)),
        out_shape=jax.ShapeDtypeStruct((depth, R, N), F32),
        compiler_params=_params("parallel", "parallel"),
        name="ada",
    )(cond, ada_w, ada_b.reshape(depth, 1, N))


def _mod_index(ctx_tiles, ctx_row, off):
    def index(b, t):
        return (jnp.where(t + off < ctx_tiles, ctx_row, b), 0, 0)
    return index


def _modulate_kernel(x_ref, mod_ref, h_ref, *, sh, sc):
    x = x_ref[0]
    h_ref[0] = (x * (1.0 + mod_ref[0, sc:sc + 1, :]) + mod_ref[0, sh:sh + 1, :]).astype(h_ref.dtype)


def _modulate(x, mod, *, sh, sc, ctx_tiles, ctx_row, out_dtype):
    B, T, D = x.shape
    return pl.pallas_call(
        functools.partial(_modulate_kernel, sh=sh, sc=sc),
        grid=(B, T // ROW_TILE),
        in_specs=[pl.BlockSpec((1, ROW_TILE, D), lambda b, t: (b, t, 0)),
                  pl.BlockSpec((1, N_MOD, D), _mod_index(ctx_tiles, ctx_row, 0))],
        out_specs=pl.BlockSpec((1, ROW_TILE, D), lambda b, t: (b, t, 0)),
        out_shape=jax.ShapeDtypeStruct((B, T, D), out_dtype),
        compiler_params=_params("parallel", "parallel"),
        name="modulate",
    )(x, mod)


def _ln_kernel(*refs, alpha, gate, sh, sc, has_h, has_router):
    x_ref, y_ref, mod_ref, modh_ref, w_ref, b_ref = refs[:6]
    pos = 6
    router_ref = None
    if has_router:
        router_ref = refs[pos]
        pos += 1
    xo_ref = refs[pos]
    pos += 1
    z = alpha * x_ref[0] + mod_ref[0, gate:gate + 1, :] * y_ref[0].astype(F32)
    mu = jnp.mean(z, axis=-1, keepdims=True)
    d = z - mu
    var = jnp.mean(d * d, axis=-1, keepdims=True)
    xn = d * lax.rsqrt(var + LN_EPS) * w_ref[...] + b_ref[...]
    xo_ref[0] = xn
    if has_h:
        h = xn * (1.0 + modh_ref[0, sc:sc + 1, :]) + modh_ref[0, sh:sh + 1, :]
        h_ref = refs[pos]
        pos += 1
        h_ref[0] = h.astype(h_ref.dtype)
        if has_router:
            refs[pos][0] = jnp.dot(h, router_ref[...], precision=HIGHEST, preferred_element_type=F32)


def _ln(x, y, mod, ln_w, ln_b, *, alpha, gate, h_mod=None, mod_h=None, h_dtype=BF16, router=None,
        ctx_tiles, ctx_row, drop_tiles=0):
    B, T, D = x.shape
    nt = T // ROW_TILE - drop_tiles
    To = nt * ROW_TILE
    off = drop_tiles
    in_specs = [pl.BlockSpec((1, ROW_TILE, D), lambda b, t: (b, t + off, 0)),
                pl.BlockSpec((1, ROW_TILE, D), lambda b, t: (b, t + off, 0)),
                pl.BlockSpec((1, N_MOD, D), _mod_index(ctx_tiles, ctx_row, off)),
                pl.BlockSpec((1, N_MOD, D), _mod_index(ctx_tiles, ctx_row, off)),
                pl.BlockSpec((1, D), lambda b, t: (0, 0)),
                pl.BlockSpec((1, D), lambda b, t: (0, 0))]
    args = [x, y, mod, mod if mod_h is None else mod_h, ln_w.reshape(1, D), ln_b.reshape(1, D)]
    out_specs = [pl.BlockSpec((1, ROW_TILE, D), lambda b, t: (b, t, 0))]
    out_shape = [jax.ShapeDtypeStruct((B, To, D), F32)]
    sh = sc = 0
    if h_mod is not None:
        sh, sc = h_mod
        out_specs.append(pl.BlockSpec((1, ROW_TILE, D), lambda b, t: (b, t, 0)))
        out_shape.append(jax.ShapeDtypeStruct((B, To, D), h_dtype))
    if router is not None:
        E = router.shape[1]
        router_p = jnp.zeros((D, LANES), F32).at[:, :E].set(router)
        in_specs.append(pl.BlockSpec((D, LANES), lambda b, t: (0, 0)))
        args.append(router_p)
        out_specs.append(pl.BlockSpec((1, ROW_TILE, LANES), lambda b, t: (b, t, 0)))
        out_shape.append(jax.ShapeDtypeStruct((B, To, LANES), F32))
    return pl.pallas_call(
        functools.partial(_ln_kernel, alpha=alpha, gate=gate, sh=sh, sc=sc,
                          has_h=h_mod is not None, has_router=router is not None),
        grid=(B, nt),
        in_specs=in_specs,
        out_specs=out_specs,
        out_shape=out_shape,
        compiler_params=_params("parallel", "parallel"),
        name="ln_residual",
    )(*args)


def _halo_specs(T, C, col):
    r8 = ROW_TILE // 8
    last8 = T // 8 - 1
    return [pl.BlockSpec((1, ROW_TILE, C), lambda b, t, j: (b, t, col(j))),
            pl.BlockSpec((1, 8, C), lambda b, t, j: (b, jnp.maximum(t * r8 - 1, 0), col(j))),
            pl.BlockSpec((1, 8, C), lambda b, t, j: (b, jnp.minimum(t * r8 + r8, last8), col(j)))]


def _neighbours(cur, prev8, next8, t, n_tiles, ctx_tiles):
    rows = lax.broadcasted_iota(jnp.int32, cur.shape, 0)
    has_prev = jnp.logical_and(t != 0, t != ctx_tiles)
    has_next = jnp.logical_and(t != n_tiles - 1, t != ctx_tiles - 1)
    top = jnp.where(has_prev, prev8[7:8, :], 0.0)
    bot = jnp.where(has_next, next8[0:1, :], 0.0)
    up = jnp.where(rows == 0, top, pltpu.roll(cur, 1, axis=0))
    dn = jnp.where(rows == cur.shape[0] - 1, bot, pltpu.roll(cur, cur.shape[0] - 1, axis=0))
    return up, dn


def _hy_gate_kernel(*refs, n_tiles, ctx_tiles):
    zs = refs[0:9]
    cw = refs[9:12]
    cb = refs[12:15]
    x0_ref, vv_ref, vvb_ref = refs[15:18]
    t = pl.program_id(1)
    out = []
    for s in range(3):
        cur = zs[3 * s][0]
        up, dn = _neighbours(cur, zs[3 * s + 1][0], zs[3 * s + 2][0], t, n_tiles, ctx_tiles)
        w = cw[s]
        out.append(up * w[0:1, :] + cur * w[1:2, :] + dn * w[2:3, :] + cb[s][...])
    x0_ref[0] = out[0]
    vv = out[1] * out[2]
    vv_ref[0] = vv
    vvb_ref[0] = vv.astype(BF16)


def _hy_gate(zp, conv_w, conv_b, *, ctx_tiles):
    B, T, D3 = zp.shape
    D = D3 // 3
    tc = _tile(D, 512)
    nj = D // tc
    n_tiles = T // ROW_TILE
    in_specs = []
    for s in range(3):
        in_specs += _halo_specs(T, tc, lambda j, s=s: s * nj + j)
    in_specs += [pl.BlockSpec((3, tc), lambda b, t, j, s=s: (0, s * nj + j)) for s in range(3)]
    in_specs += [pl.BlockSpec((1, tc), lambda b, t, j, s=s: (0, s * nj + j)) for s in range(3)]
    blk = pl.BlockSpec((1, ROW_TILE, tc), lambda b, t, j: (b, t, j))
    return pl.pallas_call(
        functools.partial(_hy_gate_kernel, n_tiles=n_tiles, ctx_tiles=ctx_tiles),
        grid=(B, n_tiles, nj),
        in_specs=in_specs,
        out_specs=[blk, blk, blk],
        out_shape=[jax.ShapeDtypeStruct((B, T, D), F32), jax.ShapeDtypeStruct((B, T, D), F32),
                   jax.ShapeDtypeStruct((B, T, D), BF16)],
        compiler_params=_params("parallel", "parallel", "parallel"),
        name="hy_gate",
    )(*([zp] * 9), conv_w, conv_w, conv_w, *([conv_b.reshape(1, D3)] * 3))


def _hy_filter_kernel(w1_ref, b1_ref, w2_ref, b2_ref, w3_ref, fr_ref, h_ref, s_ref, *, L, D, tl):
    i = pl.program_id(0)
    row = (lax.broadcasted_iota(jnp.int32, (tl, LANES), 0) + i * tl).astype(F32)
    lane = lax.broadcasted_iota(jnp.int32, (tl, LANES), 1)
    band = jnp.where(lane <= HY_BANDS, lane - 1, lane - 1 - HY_BANDS).astype(F32)
    freq = 1e-4 + band * ((HY_BANDS - 1 - 1e-4) / (HY_BANDS - 1))
    ang = freq * (row * (2.0 * math.pi / L))
    z = jnp.where(lane == 0, row / (L - 1),
                  jnp.where(lane <= HY_BANDS, jnp.cos(ang),
                            jnp.where(lane < HY_EMB, -jnp.sin(ang), 0.0)))
    h = jnp.sin(fr_ref[0:1, :] * (jnp.dot(z, w1_ref[...], precision=HIGHEST,
                                           preferred_element_type=F32) + b1_ref[...]))
    h = jnp.sin(fr_ref[1:2, :] * (jnp.dot(h, w2_ref[...], precision=HIGHEST,
                                           preferred_element_type=F32) + b2_ref[...]))
    h = jnp.dot(h, w3_ref[...], precision=HIGHEST, preferred_element_type=F32)
    half = L // 2
    rowd = (lax.broadcasted_iota(jnp.int32, (tl, D), 0) + i * tl).astype(F32)
    dist = jnp.abs(rowd - half) / half
    chan = lax.broadcasted_iota(jnp.int32, (tl, D), 1).astype(F32)
    deltas = HY_MIN_DECAY + chan * ((HY_MAX_DECAY - HY_MIN_DECAY) / (D - 1))
    h = h * jnp.exp(-dist * deltas)
    h_ref[...] = h

    @pl.when(i == 0)
    def _():
        s_ref[...] = jnp.zeros_like(s_ref)

    s_ref[...] += jnp.sum(jnp.abs(h), axis=0, keepdims=True)


def _hy_filter(L, f_w1, f_b1, f_w2, f_b2, f_w3, f_freq):
    D = f_w3.shape[1]
    hid = f_w1.shape[1]
    w1 = jnp.zeros((LANES, LANES), F32).at[:HY_EMB, :hid].set(f_w1)
    b1 = jnp.zeros((1, LANES), F32).at[0, :hid].set(f_b1)
    w2 = jnp.zeros((LANES, LANES), F32).at[:hid, :hid].set(f_w2)
    b2 = jnp.zeros((1, LANES), F32).at[0, :hid].set(f_b2)
    w3 = jnp.zeros((LANES, D), F32).at[:hid].set(f_w3)
    fr = jnp.zeros((2, LANES), F32).at[:, :hid].set(f_freq)
    tl = _tile(L, 256, 8)
    full = lambda shape: pl.BlockSpec(shape, lambda i: (0, 0))
    return pl.pallas_call(
        functools.partial(_hy_filter_kernel, L=L, D=D, tl=tl),
        grid=(L // tl,),
        in_specs=[full((LANES, LANES)), full((1, LANES)), full((LANES, LANES)), full((1, LANES)),
                  full((LANES, D)), full((2, LANES))],
        out_specs=[pl.BlockSpec((tl, D), lambda i: (i, 0)), full((1, D))],
        out_shape=[jax.ShapeDtypeStruct((L, D), F32), jax.ShapeDtypeStruct((1, D), F32)],
        compiler_params=_params("arbitrary"),
        name="hy_filter",
    )(w1, b1, w2, b2, w3, fr)


def _dft_tables(L):
    n = 2 * L
    k = jnp.arange(L, dtype=jnp.int32)
    w = 2.0 * math.pi / n
    ang = ((k[:, None] * k[None, :]) % n).astype(F32) * w
    sign_t = jnp.where(k % 2 == 0, 1.0, -1.0).astype(F32)
    fre = jnp.cos(ang)
    fim = jnp.where(k[:, None] == 0, sign_t[None, :], -jnp.sin(ang))
    fwd = jnp.stack([fre, fim]).astype(BF16)
    m = k + L // 2
    angi = ((m[:, None] * k[None, :]) % n).astype(F32) * w
    sign_m = jnp.where(m % 2 == 0, 1.0, -1.0).astype(F32)
    ire = jnp.where(k[None, :] == 0, 1.0 / n, (2.0 / n) * jnp.cos(angi))
    iim = jnp.where(k[None, :] == 0, sign_m[:, None] / n, (-2.0 / n) * jnp.sin(angi))
    inv = jnp.concatenate([ire, iim], axis=1).astype(BF16)
    return fwd, inv


def _dft_fwd_kernel(a_ref, w_ref, *rest, mode):
    o_ref = rest[-1]
    w = w_ref[0]
    vre = jnp.dot(a_ref[0], w, preferred_element_type=F32)
    vim = jnp.dot(a_ref[1], w, preferred_element_type=F32)
    if mode == "scale":
        inv = 1.0 / (rest[0][...] + 1e-6)
        o_ref[0, 0] = vre * inv
        o_ref[0, 1] = vim * inv
    else:
        hre = rest[0][0]
        him = rest[0][1]
        first = jnp.logical_and(pl.program_id(2) == 0,
                                lax.broadcasted_iota(jnp.int32, vre.shape, 0) == 0)
        zre = jnp.where(first, vre * hre, vre * hre - vim * him)
        zim = jnp.where(first, vim * him, vre * him + vim * hre)
        o_ref[0, 0] = zre.astype(o_ref.dtype)
        o_ref[0, 1] = zim.astype(o_ref.dtype)


def _dft_fwd(fwd, w, extra, *, mode, out_dtype):
    B, L, D = w.shape
    tm = _tile(L, 256, 8)
    tn = _tile(D, 512)
    if mode == "scale":
        extra_spec = pl.BlockSpec((1, tn), lambda b, j, m: (0, j))
    else:
        extra_spec = pl.BlockSpec((2, tm, tn), lambda b, j, m: (0, m, j))
    return pl.pallas_call(
        functools.partial(_dft_fwd_kernel, mode=mode),
        grid=(B, D // tn, L // tm),
        in_specs=[pl.BlockSpec((2, tm, L), lambda b, j, m: (0, m, 0)),
                  pl.BlockSpec((1, L, tn), lambda b, j, m: (b, 0, j)),
                  extra_spec],
        out_specs=pl.BlockSpec((1, 2, tm, tn), lambda b, j, m: (b, 0, m, j)),
        out_shape=jax.ShapeDtypeStruct((B, 2, L, D), out_dtype),
        compiler_params=_params("parallel", "parallel", "parallel"),
        name="dft_fwd",
    )(fwd, w, extra)


def _dft_inv_kernel(b_ref, z_ref, x0_ref, vv_ref, skip_ref, u_ref):
    y = jnp.dot(b_ref[...], z_ref[0], preferred_element_type=F32)
    u_ref[0] = (x0_ref[0] * (y + vv_ref[0] * skip_ref[...])).astype(u_ref.dtype)


def _dft_inv(inv, z, x0, vv, skip, *, row_off):
    B, n, D = z.shape
    L = n // 2
    tm = _tile(L, ROW_TILE, 8)
    tn = _tile(D, 512)
    off = row_off // tm
    return pl.pallas_call(
        _dft_inv_kernel,
        grid=(B, D // tn, L // tm),
        in_specs=[pl.BlockSpec((tm, n), lambda b, j, m: (m, 0)),
                  pl.BlockSpec((1, n, tn), lambda b, j, m: (b, 0, j)),
                  pl.BlockSpec((1, tm, tn), lambda b, j, m: (b, m + off, j)),
                  pl.BlockSpec((1, tm, tn), lambda b, j, m: (b, m + off, j)),
                  pl.BlockSpec((1, tn), lambda b, j, m: (0, j))],
        out_specs=pl.BlockSpec((1, tm, tn), lambda b, j, m: (b, m, j)),
        out_shape=jax.ShapeDtypeStruct((B, L, D), BF16),
        compiler_params=_params("parallel", "parallel", "parallel"),
        name="dft_inv",
    )(inv, z, x0, vv, skip.reshape(1, D))


def _hyena(h, p, tables, *, Lc, has_ctx):
    B, T, D = h.shape
    M = B * T
    zp = _mm(h.reshape(M, D), p["w_in"], bias=p["b_in"]).reshape(B, T, 3 * D)
    ctx_tiles = Lc // ROW_TILE if has_ctx else 0
    x0, vv, vvb = _hy_gate(zp, p["conv_w"], p["conv_b"], ctx_tiles=ctx_tiles)
    segs = [(Lc, T - Lc)] if has_ctx else [(0, T)]
    if has_ctx:
        segs = [(0, Lc)] + segs
    us = []
    for start, L in segs:
        fwd, inv = tables[L]
        filt, asum = _hy_filter(L, *p["filter"])
        hf = _dft_fwd(fwd, filt.astype(BF16)[None], asum, mode="scale", out_dtype=F32)[0]
        seg = vvb if (start == 0 and L == T) else lax.slice_in_dim(vvb, start, start + L, axis=1)
        z = _dft_fwd(fwd, seg, hf, mode="mul", out_dtype=BF16).reshape(B, 2 * L, D)
        us.append(_dft_inv(inv, z, x0, vv, p["skip"], row_off=start))
    u = us[0] if len(us) == 1 else jnp.concatenate(us, axis=1)
    return _mm(u.reshape(M, D), p["w_out"], bias=p["b_out"]).reshape(B, T, D)


def _rw_prep_kernel(cur_ref, prev_ref, next_ref, mu_ref, *o_refs, n_tiles, ctx_tiles):
    t = pl.program_id(1)
    cur = cur_ref[0]
    up, dn = _neighbours(cur, prev_ref[0], next_ref[0], t, n_tiles, ctx_tiles)
    dx = 0.5 * (up + dn) - cur
    for j, o_ref in enumerate(o_refs):
        o_ref[0] = (cur + dx * mu_ref[j:j + 1, :]).astype(o_ref.dtype)


def _rw_prep(hs, mu, *, ctx_tiles):
    B, T, D = hs.shape
    tc = _tile(D, 1024)
    n_tiles = T // ROW_TILE
    blk = pl.BlockSpec((1, ROW_TILE, tc), lambda b, t, j: (b, t, j))
    return pl.pallas_call(
        functools.partial(_rw_prep_kernel, n_tiles=n_tiles, ctx_tiles=ctx_tiles),
        grid=(B, n_tiles, D // tc),
        in_specs=_halo_specs(T, tc, lambda j: j) + [pl.BlockSpec((6, tc), lambda b, t, j: (0, j))],
        out_specs=[blk] * 6,
        out_shape=[jax.ShapeDtypeStruct((B, T, D), BF16)] * 6,
        compiler_params=_params("parallel", "parallel", "parallel"),
        name="rw_prep",
    )(hs, hs, hs, mu)


def _block_diag(x, mask):
    return jnp.where(mask, jnp.concatenate([x] * RW_GROUP, axis=0), 0.0).astype(BF16)


def _rw_scan_kernel(r_ref, k_ref, v_ref, lw_ref, a_ref, kk_ref, ka_ref, o_ref, state_ref, *, reverse):
    C = RW_CHUNK
    W = RW_GROUP * HEAD

    @pl.when(pl.program_id(2) == 0)
    def _():
        state_ref[...] = jnp.zeros_like(state_ref)

    r = r_ref[0]
    k = k_ref[0]
    v = v_ref[0]
    lw = lw_ref[0]
    a = a_ref[0]

    ri = lax.broadcasted_iota(jnp.int32, (W, W), 0)
    ci = lax.broadcasted_iota(jnp.int32, (W, W), 1)
    same_head = (ri // HEAD) == (ci // HEAD)
    t_i = lax.broadcasted_iota(jnp.int32, (C, W), 0)
    s_i = lax.broadcasted_iota(jnp.int32, (C, W), 1) % C
    incl = (s_i >= t_i) if reverse else (s_i <= t_i)
    strict = (s_i > t_i) if reverse else (s_i < t_i)

    kkr = k * kk_ref[...]
    ss = jnp.dot(kkr * kkr, same_head.astype(F32), precision=HIGHEST, preferred_element_type=F32)
    kk = kkr / jnp.maximum(jnp.sqrt(ss), 1e-12)
    kd = k * (1.0 + (a - 1.0) * ka_ref[...])

    lp = jnp.dot(incl[:, :C].astype(F32), lw, precision=HIGHEST, preferred_element_type=F32)
    lq = lp - lw
    lp_end = lp[0:1, :] if reverse else lp[C - 1:C, :]
    e_neg = jnp.exp(-lp)
    ab = -kk * jnp.exp(lq)
    bt = kk * a * e_neg
    kt = kd * e_neg
    rb = r * jnp.exp(lp)

    nt = (((1,), (1,)), ((), ()))
    lhs = jnp.concatenate([ab, rb], axis=0).astype(BF16)
    rhs = jnp.concatenate([_block_diag(bt, same_head), _block_diag(kt, same_head)], axis=0)
    cross = lax.dot_general(lhs, rhs, nt, preferred_element_type=F32)
    a_ab = jnp.where(strict, cross[:C, :W], 0.0)
    a_ak = jnp.where(strict, cross[:C, W:], 0.0)
    m_rb = jnp.where(incl, cross[C:, :W], 0.0)
    m_rk = jnp.where(incl, cross[C:, W:], 0.0)

    state = state_ref[...]
    from_state = lax.dot_general(lhs, state.astype(BF16), nt, preferred_element_type=F32)
    v_bd = _block_diag(v, same_head)
    x = from_state[:C] + jnp.dot(a_ak.astype(BF16), v_bd, preferred_element_type=F32)
    p = a_ab
    n = 1
    while n < C:
        x = x + jnp.dot(p.astype(BF16), _block_diag(x, same_head), preferred_element_type=F32)
        n *= 2
        if n < C:
            p = jnp.dot(p.astype(BF16), _block_diag(p, same_head), preferred_element_type=F32)
    u = x
    y = (from_state[C:]
         + jnp.dot(m_rb.astype(BF16), _block_diag(u, same_head), preferred_element_type=F32)
         + jnp.dot(m_rk.astype(BF16), v_bd, preferred_element_type=F32))
    o_ref[0] = y

    tail = jnp.exp(lp_end - lp)
    uv = jnp.concatenate([u, v], axis=0).astype(BF16)
    bk = jnp.concatenate([kk * a * tail, kd * tail], axis=0).astype(BF16)
    upd = lax.dot_general(uv, bk, (((0,), (0,)), ((), ())), preferred_element_type=F32)
    state_ref[...] = state * jnp.exp(lp_end) + jnp.where(same_head, upd, 0.0)


def _rw_scan(r, k, v, lw, a, k_k, k_a, *, Lc, reverse):
    B, T, D = r.shape
    C = RW_CHUNK
    W = RW_GROUP * HEAD
    nc = T // C
    ncc = Lc // C

    def chunk(c):
        if not reverse:
            return c
        return jnp.where(c < ncc, ncc - 1 - c, nc - 1 - (c - ncc))

    blk = pl.BlockSpec((1, C, W), lambda b, g, c: (b, chunk(c), g))
    vec = pl.BlockSpec((1, W), lambda b, g, c: (0, g))
    return pl.pallas_call(
        functools.partial(_rw_scan_kernel, reverse=reverse),
        grid=(B, D // W, nc),
        in_specs=[blk] * 5 + [vec, vec],
        out_specs=blk,
        out_shape=jax.ShapeDtypeStruct((B, T, D), F32),
        scratch_shapes=[pltpu.VMEM((W, W), F32)],
        compiler_params=_params("parallel", "parallel", "arbitrary"),
        name="rw_scan",
    )(r, k, v, lw, a, k_k.reshape(1, D), k_a.reshape(1, D))


def _rw_post_kernel(of_ref, ob_ref, r_ref, k_ref, v_ref, g_ref, rk_ref, gw_ref, gb_ref, o_ref):
    o = of_ref[0] + ob_ref[0]
    m = jnp.mean(o, axis=-1, keepdims=True)
    d = o - m
    var = jnp.mean(d * d, axis=-1, keepdims=True)
    on = d * lax.rsqrt(var + RW_GN_EPS) * gw_ref[...] + gb_ref[...]
    bonus = jnp.sum(r_ref[0] * k_ref[0] * rk_ref[...], axis=-1, keepdims=True) * v_ref[0]
    o_ref[0] = ((on + bonus) * g_ref[0]).astype(o_ref.dtype)


def _rw_post(o_f, o_b, r, k, v, g, r_k, gn_w, gn_b):
    B, T, D = r.shape
    H = D // HEAD
    tt = _tile(T, 128, 8)
    v4 = lambda t: t.reshape(B, T, H, HEAD)
    blk = pl.BlockSpec((1, tt, H, HEAD), lambda b, t: (b, t, 0, 0))
    par = pl.BlockSpec((H, HEAD), lambda b, t: (0, 0))
    out = pl.pallas_call(
        _rw_post_kernel,
        grid=(B, T // tt),
        in_specs=[blk] * 6 + [par] * 3,
        out_specs=blk,
        out_shape=jax.ShapeDtypeStruct((B, T, H, HEAD), BF16),
        compiler_params=_params("parallel", "parallel"),
        name="rw_post",
    )(v4(o_f), v4(o_b), v4(r), v4(k), v4(v), v4(g), r_k.reshape(H, HEAD), gn_w.reshape(H, HEAD),
      gn_b.reshape(H, HEAD))
    return out.reshape(B, T, D)


def _rwkv(hs, p, *, Lc):
    B, T, D = hs.shape
    M = B * T
    ctx_tiles = Lc // ROW_TILE
    xr, xw, xk, xv, xa, xg = [t.reshape(M, D) for t in _rw_prep(hs, p["mu"], ctx_tiles=ctx_tiles)]
    w_rkv = p["w_rkv"]
    r = _mm(xr, w_rkv[0]).reshape(B, T, D)
    k = _mm(xk, w_rkv[1]).reshape(B, T, D)
    v = _mm(xv, w_rkv[2]).reshape(B, T, D)
    g = _lora(xg, p["g1"], p["g2"], jnp.zeros((D,), F32), mid="sigmoid", fin=None).reshape(B, T, D)
    outs = []
    for d in range(2):
        lw = _lora(xw, p["w1"][d], p["w2"][d], p["w0"][d], mid="tanh", fin="logdecay").reshape(B, T, D)
        a = _lora(xa, p["a1"][d], p["a2"][d], p["a0"][d], mid=None, fin="sigmoid").reshape(B, T, D)
        outs.append(_rw_scan(r, k, v, lw, a, p["k_k"], p["k_a"], Lc=Lc, reverse=d == 1))
    y = _rw_post(outs[0], outs[1], r, k, v, g, p["r_k"], p["gn_w"], p["gn_b"])
    return _mm(y.reshape(M, D), p["w_o"]).reshape(B, T, D)


def _rope_tables(T, Lc):
    quarter = HEAD // 4
    pos = jnp.arange(T - Lc, dtype=jnp.int32)
    rows = (pos // GRID_W).astype(F32)
    cols = (pos % GRID_W).astype(F32)
    inv = ROPE_BASE ** (-jnp.arange(quarter, dtype=F32) / quarter)
    ang = jnp.concatenate([rows[:, None] * inv, rows[:, None] * inv,
                           cols[:, None] * inv, cols[:, None] * inv], axis=1)
    ang = jnp.concatenate([jnp.zeros((Lc, HEAD), F32), ang], axis=0)
    ang = jnp.concatenate([ang, ang], axis=1)
    return jnp.cos(ang), jnp.sin(ang)


def _rope_kernel(x_ref, cos_ref, sin_ref, q_ref, k_ref, v_ref, *, n_q, n_kv, scale):
    x = x_ref[0]
    n_rot = (n_q + n_kv) * HEAD
    xr = x[:, :n_rot]
    reps = n_rot // LANES
    cos = jnp.concatenate([cos_ref[...]] * reps, axis=1)
    sin = jnp.concatenate([sin_ref[...]] * reps, axis=1)
    quarter = HEAD // 4
    lane = lax.broadcasted_iota(jnp.int32, xr.shape, 1)
    first = (lane % (2 * quarter)) < quarter
    rot = jnp.where(first, -pltpu.roll(xr, n_rot - quarter, axis=1), pltpu.roll(xr, quarter, axis=1))
    y = xr * cos + rot * sin
    for h in range(n_q):
        q_ref[0, h] = (y[:, h * HEAD:(h + 1) * HEAD] * scale).astype(q_ref.dtype)
    for h in range(n_kv):
        lo = (n_q + h) * HEAD
        k_ref[0, h] = y[:, lo:lo + HEAD].astype(k_ref.dtype)
        lo = (n_q + n_kv + h) * HEAD
        v_ref[0, h] = x[:, lo:lo + HEAD].astype(v_ref.dtype)


def _rope(qkv, cos, sin, *, n_q, n_kv):
    B, T, W = qkv.shape
    tt = _tile(T, 128, 8)
    out = lambda n: pl.BlockSpec((1, n, tt, HEAD), lambda b, t: (b, 0, t, 0))
    return pl.pallas_call(
        functools.partial(_rope_kernel, n_q=n_q, n_kv=n_kv, scale=HEAD ** -0.5),
        grid=(B, T // tt),
        in_specs=[pl.BlockSpec((1, tt, W), lambda b, t: (b, t, 0)),
                  pl.BlockSpec((tt, LANES), lambda b, t: (t, 0)),
                  pl.BlockSpec((tt, LANES), lambda b, t: (t, 0))],
        out_specs=[out(n_q), out(n_kv), out(n_kv)],
        out_shape=[jax.ShapeDtypeStruct((B, n_q, T, HEAD), BF16),
                   jax.ShapeDtypeStruct((B, n_kv, T, HEAD), BF16),
                   jax.ShapeDtypeStruct((B, n_kv, T, HEAD), BF16)],
        compiler_params=_params("parallel", "parallel"),
        name="rope",
    )(qkv, cos, sin)


def _attn_kernel(*refs, n_q, n_kv, local, q_off, n_blocks):
    if local:
        q_ref, kp_ref, kc_ref, kn_ref, vp_ref, vc_ref, vn_ref, kx_ref, vx_ref, sink_ref, o_ref = refs
    else:
        q_ref, kx_ref, vx_ref, sink_ref, o_ref = refs
    G = n_q // n_kv
    R = G * AT_BLOCK
    n = pl.program_id(1)
    nt = (((1,), (1,)), ((), ()))
    if local:
        qi = lax.broadcasted_iota(jnp.int32, (R, 3 * AT_BLOCK), 0) % AT_BLOCK
        kj = lax.broadcasted_iota(jnp.int32, (R, 3 * AT_BLOCK), 1) - AT_BLOCK
        kpos = kj + n * AT_BLOCK
        ok = (jnp.abs(qi - kj) <= AT_WINDOW) & (kpos >= 0) & (kpos < n_blocks * AT_BLOCK)
    for h in range(n_kv):
        q = q_ref[0, h * G:(h + 1) * G].reshape(R, HEAD)
        sink = jnp.concatenate(
            [jnp.broadcast_to(sink_ref[h * G + g:h * G + g + 1, 0:1], (AT_BLOCK, 1)) for g in range(G)],
            axis=0)
        kx = kx_ref[0, h]
        s_ctx = lax.dot_general(q, kx, nt, preferred_element_type=F32)
        m = jnp.maximum(jnp.max(s_ctx, axis=-1, keepdims=True), sink)
        if local:
            k_loc = jnp.concatenate([kp_ref[0, h], kc_ref[0, h], kn_ref[0, h]], axis=0)
            v_loc = jnp.concatenate([vp_ref[0, h], vc_ref[0, h], vn_ref[0, h]], axis=0)
            s_loc = jnp.where(ok, lax.dot_general(q, k_loc, nt, preferred_element_type=F32), NEG_INF)
            m = jnp.maximum(m, jnp.max(s_loc, axis=-1, keepdims=True))
        p_ctx = jnp.exp(s_ctx - m)
        den = jnp.sum(p_ctx, axis=-1, keepdims=True) + jnp.exp(sink - m)
        if local:
            p_loc = jnp.exp(s_loc - m)
            den = den + jnp.sum(p_loc, axis=-1, keepdims=True)
        inv = 1.0 / den
        o = jnp.dot((p_ctx * inv).astype(BF16), vx_ref[0, h], preferred_element_type=F32)
        if local:
            o = o + jnp.dot((p_loc * inv).astype(BF16), v_loc, preferred_element_type=F32)
        for g in range(G):
            hh = h * G + g
            o_ref[0, :, hh * HEAD:(hh + 1) * HEAD] = o[g * AT_BLOCK:(g + 1) * AT_BLOCK].astype(o_ref.dtype)


def _attn(q, k, v, sink, *, Lc, local):
    B, n_q, T, _ = q.shape
    n_kv = k.shape[1]
    cb = Lc // AT_BLOCK
    nb = (T - Lc) // AT_BLOCK if local else cb
    q_off = cb if local else 0
    last = T // AT_BLOCK - 1
    qspec = pl.BlockSpec((1, n_q, AT_BLOCK, HEAD), lambda b, n: (b, 0, n + q_off, 0))
    xspec = pl.BlockSpec((1, n_kv, Lc, HEAD), lambda b, n: (b, 0, 0, 0))
    sspec = pl.BlockSpec((n_q, LANES), lambda b, n: (0, 0))
    sink_b = jnp.broadcast_to(sink.astype(F32)[:, None], (n_q, LANES))
    if local:
        blk = lambda f: pl.BlockSpec((1, n_kv, AT_BLOCK, HEAD), lambda b, n: (b, 0, f(n), 0))
        prev = blk(lambda n: jnp.maximum(n + cb - 1, cb))
        cur = blk(lambda n: n + cb)
        nxt = blk(lambda n: jnp.minimum(n + cb + 1, last))
        in_specs = [qspec, prev, cur, nxt, prev, cur, nxt, xspec, xspec, sspec]
        args = (q, k, k, k, v, v, v, k, v, sink_b)
    else:
        in_specs = [qspec, xspec, xspec, sspec]
        args = (q, k, v, sink_b)
    return pl.pallas_call(
        functools.partial(_attn_kernel, n_q=n_q, n_kv=n_kv, local=local, q_off=q_off, n_blocks=nb),
        grid=(B, nb),
        in_specs=in_specs,
        out_specs=pl.BlockSpec((1, AT_BLOCK, n_q * HEAD), lambda b, n: (b, n, 0)),
        out_shape=jax.ShapeDtypeStruct((B, nb * AT_BLOCK, n_q * HEAD), BF16),
        compiler_params=_params("parallel", "parallel"),
        name="attn_local" if local else "attn_ctx",
    )(*args)


def _attention(h, p, *, Lc):
    B, T, D = h.shape
    M = B * T
    n_q = D // HEAD
    qkv = _mm(h.reshape(M, D), p["w_qkv"], bias=p["b_qkv"]).reshape(B, T, -1)
    cos, sin = _rope_tables(T, Lc)
    q, k, v = _rope(qkv, cos, sin, n_q=n_q, n_kv=AT_KV_HEADS)
    o_ctx = _attn(q, k, v, p["sink"], Lc=Lc, local=False)
    o_lat = _attn(q, k, v, p["sink"], Lc=Lc, local=True)
    o = jnp.concatenate([o_ctx, o_lat], axis=1)
    return _mm(o.reshape(M, D), p["w_o"], bias=p["b_o"]).reshape(B, T, D)


def _swiglu_ffn(h, w1, w3, w2):
    mid = _mm(h, w1, w3=w3, out_dtype=BF16)
    return _mm(mid, w2)


def _gates_kernel(l_ref, g_ref, *, n_experts):
    l = l_ref[...]
    lane = lax.broadcasted_iota(jnp.int32, l.shape, 1)
    l = jnp.where(lane < n_experts, l, -jnp.inf)
    m1 = jnp.max(l, axis=-1, keepdims=True)
    i1 = jnp.min(jnp.where(l == m1, lane, LANES), axis=-1, keepdims=True)
    l2 = jnp.where(lane == i1, -jnp.inf, l)
    m2 = jnp.max(l2, axis=-1, keepdims=True)
    i2 = jnp.min(jnp.where(l2 == m2, lane, LANES), axis=-1, keepdims=True)
    e2 = jnp.exp(m2 - m1)
    den = 1.0 + e2
    g_ref[...] = jnp.where(lane == i1, 1.0 / den, jnp.where(lane == i2, e2 / den, 0.0))


def _gates(logits, n_experts):
    M = logits.shape[0]
    tm = _tile(M, 1024, 8)
    return pl.pallas_call(
        functools.partial(_gates_kernel, n_experts=n_experts),
        grid=(M // tm,),
        in_specs=[pl.BlockSpec((tm, LANES), lambda i: (i, 0))],
        out_specs=pl.BlockSpec((tm, LANES), lambda i: (i, 0)),
        out_shape=jax.ShapeDtypeStruct((M, LANES), F32),
        compiler_params=_params("parallel"),
        name="moe_gates",
    )(logits)


def _moe_out_kernel(mid_ref, w_ref, g_ref, o_ref, acc_ref):
    e = pl.program_id(2)

    @pl.when(e == 0)
    def _():
        acc_ref[...] = jnp.zeros_like(acc_ref)

    g = g_ref[...]
    lane = lax.broadcasted_iota(jnp.int32, g.shape, 1)
    ge = jnp.sum(jnp.where(lane == e, g, 0.0), axis=-1, keepdims=True)
    acc_ref[...] += ge * jnp.dot(mid_ref[0], w_ref[0], preferred_element_type=F32)

    @pl.when(e == pl.num_programs(2) - 1)
    def _():
        o_ref[...] = acc_ref[...]


def _moe(h, gates, w1, w3, w2):
    M, D = h.shape
    E, _, F = w1.shape
    mid = jnp.stack([_mm(h, w1[e], w3=w3[e], out_dtype=BF16) for e in range(E)])
    tm = _tile(M, 512, 8)
    tn = _tile(D, 1024)
    return pl.pallas_call(
        _moe_out_kernel,
        grid=(M // tm, D // tn, E),
        in_specs=[pl.BlockSpec((1, tm, F), lambda i, j, e: (e, i, 0)),
                  pl.BlockSpec((1, F, tn), lambda i, j, e: (e, 0, j)),
                  pl.BlockSpec((tm, LANES), lambda i, j, e: (i, 0))],
        out_specs=pl.BlockSpec((tm, tn), lambda i, j, e: (i, j)),
        out_shape=jax.ShapeDtypeStruct((M, D), F32),
        scratch_shapes=[pltpu.VMEM((tm, tn), F32)],
        compiler_params=_params("parallel", "parallel", "arbitrary"),
        name="moe_out",
    )(mid, w2, gates)


def kernel(x, c, ctx, c_ctx, ada_w, ada_b, ln_w, ln_b, hy_w_in, hy_b_in, hy_conv_w, hy_conv_b, hy_f_w1, hy_f_b1, hy_f_w2, hy_f_b2, hy_f_w3, hy_f_freq, hy_skip, hy_w_out, hy_b_out, rw_mu, rw_w_rkv, rw_w_o, rw_w0, rw_w1, rw_w2, rw_a0, rw_a1, rw_a2, rw_g1, rw_g2, rw_k_k, rw_k_a, rw_r_k, rw_gn_w, rw_gn_b, at_w_qkv, at_b_qkv, at_w_o, at_b_o, at_sink, ff_w1, ff_w3, ff_w2, moe_router, moe_w1, moe_w3, moe_w2):
    B, L, D = x.shape
    Lc = ctx.shape[1]
    depth = ada_w.shape[0]
    alpha = (2 * depth) ** 0.25
    assert Lc % ROW_TILE == 0 and L % ROW_TILE == 0 and D % (RW_GROUP * HEAD) == 0
    bf = lambda t: t.astype(BF16)

    ctx_row = B
    rows = -(-(B + 1) // 8) * 8
    cond = jnp.zeros((rows, D), F32).at[:B].set(c).at[B].set(c_ctx)
    mods = _ada(cond, ada_w, ada_b).reshape(depth, rows, N_MOD, D)

    tables = {}
    if depth > 0:
        tables[L] = _dft_tables(L)
        if depth > 1:
            tables[Lc] = _dft_tables(Lc)

    xs = jnp.concatenate([ctx, x], axis=1)
    ctx_tiles = Lc // ROW_TILE
    h = _modulate(xs, mods[0], sh=0, sc=1, ctx_tiles=ctx_tiles, ctx_row=ctx_row,
                  out_dtype=BF16)
    for i in range(depth):
        last = i == depth - 1
        kind = i % 3
        j = i // 3
        mod = mods[i]
        has_ctx = xs.shape[1] != L
        ct = ctx_tiles if has_ctx else 0
        T = xs.shape[1]
        if kind == 0:
            p = dict(w_in=bf(hy_w_in[j]), b_in=hy_b_in[j], conv_w=hy_conv_w[j], conv_b=hy_conv_b[j],
                     filter=(hy_f_w1[j], hy_f_b1[j], hy_f_w2[j], hy_f_b2[j], hy_f_w3[j], hy_f_freq[j]),
                     skip=hy_skip[j], w_out=bf(hy_w_out[j]), b_out=hy_b_out[j])
            y = _hyena(h, p, tables, Lc=Lc, has_ctx=has_ctx)
        elif kind == 1:
            assert has_ctx and not last
            p = dict(mu=rw_mu[j], w_rkv=bf(rw_w_rkv[j]), w_o=bf(rw_w_o[j]), w0=rw_w0[j], w1=rw_w1[j],
                     w2=rw_w2[j], a0=rw_a0[j], a1=rw_a1[j], a2=rw_a2[j], g1=rw_g1[j], g2=rw_g2[j],
                     k_k=rw_k_k[j], k_a=rw_k_a[j], r_k=rw_r_k[j], gn_w=rw_gn_w[j], gn_b=rw_gn_b[j])
            y = _rwkv(h, p, Lc=Lc)
        else:
            assert has_ctx and not last
            p = dict(w_qkv=bf(at_w_qkv[j]), b_qkv=at_b_qkv[j], w_o=bf(at_w_o[j]), b_o=at_b_o[j],
                     sink=at_sink[j])
            y = _attention(h, p, Lc=Lc)

        drop1 = ct if last else 0
        fj = i // 2
        moe = i % 2 == 1
        res = _ln(xs, y, mod, ln_w[i, 0], ln_b[i, 0], alpha=alpha, gate=2, h_mod=(3, 4),
                  router=moe_router[fj] if moe else None, ctx_tiles=ct, ctx_row=ctx_row,
                  drop_tiles=drop1)
        xs, h2 = res[0], res[1]
        ct = ct - drop1
        T = xs.shape[1]
        M = B * T
        if moe:
            gates = _gates(res[2].reshape(M, LANES), moe_router.shape[2])
            f = _moe(h2.reshape(M, D), gates, bf(moe_w1[fj]), bf(moe_w3[fj]), bf(moe_w2[fj]))
        else:
            f = _swiglu_ffn(h2.reshape(M, D), bf(ff_w1[fj]), bf(ff_w3[fj]), bf(ff_w2[fj]))
        f = f.reshape(B, T, D)
        if last:
            (xs,) = _ln(xs, f, mod, ln_w[i, 1], ln_b[i, 1], alpha=alpha, gate=5,
                        ctx_tiles=ct, ctx_row=ctx_row)
        else:
            nxt_last = i + 1 == depth - 1
            nkind = (i + 1) % 3
            drop2 = ct if (nxt_last and nkind == 0) else 0
            xs, h = _ln(xs, f, mod, ln_w[i, 1], ln_b[i, 1], alpha=alpha, gate=5, h_mod=(0, 1),
                        mod_h=mods[i + 1], h_dtype=F32 if nkind == 1 else BF16,
                        ctx_tiles=ct, ctx_row=ctx_row, drop_tiles=drop2)
    return xs
```

```python
import functools
import math

import jax
import jax.numpy as jnp
from jax import lax
from jax.experimental import pallas as pl
from jax.experimental.pallas import tpu as pltpu

F32 = jnp.float32
BF16 = jnp.bfloat16
HIGHEST = lax.Precision.HIGHEST

VMEM_LIMIT_BYTES = 56 * 1024 * 1024
LANES = 128
ROW_TILE = 256

LN_EPS = 1e-5
N_MOD = 6
HEAD = 64
RW_GN_EPS = 64e-5
RW_CHUNK = 64
RW_GROUP = 4
AT_KV_HEADS = 4
AT_WINDOW = 128
AT_BLOCK = 128
GRID_W = 64
ROPE_BASE = 10000.0
NEG_INF = -1e30
TOP_K = 2
HY_BANDS = 16
HY_EMB = 2 * HY_BANDS + 1
HY_MIN_DECAY = -math.log(1e-2) / 1.5
HY_MAX_DECAY = -math.log(1e-2) / 0.3


def _params(*sem):
    return pltpu.CompilerParams(dimension_semantics=sem, vmem_limit_bytes=VMEM_LIMIT_BYTES)


def _tile(n, pref, mult=LANES):
    if n <= pref:
        return n
    t = (pref // mult) * mult
    while t >= mult:
        if n % t == 0:
            return t
        t -= mult
    return n


def _mm_kernel(*refs, n_w, has_bias, act):
    x = refs[0][...].astype(BF16)
    o_ref = refs[-1]
    acc = jnp.dot(x, refs[1][...].astype(BF16), preferred_element_type=F32)
    if has_bias:
        acc = acc + refs[1 + n_w][...]
    if act == "swiglu":
        acc3 = jnp.dot(x, refs[2][...].astype(BF16), preferred_element_type=F32)
        acc = acc * jax.nn.sigmoid(acc) * acc3
    o_ref[...] = acc.astype(o_ref.dtype)


def _mm(x, w, *, w3=None, bias=None, out_dtype=F32, tm=512, tn=512):
    M, K = x.shape
    N = w.shape[1]
    tm = _tile(M, tm, 8)
    tn = _tile(N, tn)
    ws = [w] if w3 is None else [w, w3]
    in_specs = [pl.BlockSpec((tm, K), lambda i, j: (i, 0))]
    in_specs += [pl.BlockSpec((K, tn), lambda i, j: (0, j)) for _ in ws]
    args = [x] + ws
    if bias is not None:
        in_specs.append(pl.BlockSpec((1, tn), lambda i, j: (0, j)))
        args.append(bias.reshape(1, N).astype(F32))
    return pl.pallas_call(
        functools.partial(_mm_kernel, n_w=len(ws), has_bias=bias is not None,
                          act="swiglu" if w3 is not None else None),
        grid=(M // tm, N // tn),
        in_specs=in_specs,
        out_specs=pl.BlockSpec((tm, tn), lambda i, j: (i, j)),
        out_shape=jax.ShapeDtypeStruct((M, N), out_dtype),
        compiler_params=_params("parallel", "parallel"),
        name="mm_swiglu" if w3 is not None else "mm",
    )(*args)


def _apply_act(v, kind):
    if kind == "sigmoid":
        return jax.nn.sigmoid(v)
    if kind == "tanh":
        return jnp.tanh(v)
    if kind == "logdecay":
        return -jnp.exp(-jax.nn.softplus(-v) - 0.5)
    return v


def _lora_kernel(x_ref, a_ref, b_ref, bias_ref, o_ref, *, mid, fin):
    t = jnp.dot(x_ref[...], a_ref[...], preferred_element_type=F32)
    t = _apply_act(t, mid).astype(BF16)
    y = jnp.dot(t, b_ref[...], preferred_element_type=F32) + bias_ref[...]
    o_ref[...] = _apply_act(y, fin).astype(o_ref.dtype)


def _lora(x, a, b, bias, *, mid, fin, tm=512):
    M, K = x.shape
    N = b.shape[1]
    R = -(-a.shape[1] // LANES) * LANES
    a = jnp.zeros((K, R), a.dtype).at[:, :a.shape[1]].set(a)
    b = jnp.zeros((R, N), b.dtype).at[:b.shape[0]].set(b)
    tm = _tile(M, tm, 8)
    return pl.pallas_call(
        functools.partial(_lora_kernel, mid=mid, fin=fin),
        grid=(M // tm,),
        in_specs=[pl.BlockSpec((tm, K), lambda i: (i, 0)),
                  pl.BlockSpec((K, R), lambda i: (0, 0)),
                  pl.BlockSpec((R, N), lambda i: (0, 0)),
                  pl.BlockSpec((1, N), lambda i: (0, 0))],
        out_specs=pl.BlockSpec((tm, N), lambda i: (i, 0)),
        out_shape=jax.ShapeDtypeStruct((M, N), F32),
        compiler_params=_params("parallel"),
        name="lora",
    )(x, a.astype(BF16), b.astype(BF16), bias.reshape(1, N).astype(F32))


def _ada_kernel(c_ref, w_ref, b_ref, o_ref):
    c = c_ref[...]
    s = (c * jax.nn.sigmoid(c)).astype(BF16)
    o_ref[0] = jnp.dot(s, w_ref[0].astype(BF16), preferred_element_type=F32) + b_ref[0]


def _ada(cond, ada_w, ada_b):
    depth, D, N = ada_w.shape
    R = cond.shape[0]
    tn = _tile(N, 1024)
    return pl.pallas_call(
        _ada_kernel,
        grid=(depth, N // tn),
        in_specs=[pl.BlockSpec((R, D), lambda i, j: (0, 0)),
                  pl.BlockSpec((1, D, tn), lambda i, j: (i, 0, j)),
                  pl.BlockSpec((1, 1, tn), lambda i, j: (i, 0, j))],
        out_specs=pl.BlockSpec((1, R, tn), lambda i, j: (i, 0, j)),
        out_shape=jax.ShapeDtypeStruct((depth, R, N), F32),
        compiler_params=_params("parallel", "parallel"),
        name="ada",
    )(cond, ada_w, ada_b.reshape(depth, 1, N))


def _mod_index(ctx_tiles, ctx_row, off):
    def index(b, t):
        return (jnp.where(t + off < ctx_tiles, ctx_row, b), 0, 0)
    return index


def _modulate_kernel(x_ref, mod_ref, h_ref, *, sh, sc):
    x = x_ref[0]
    h_ref[0] = (x * (1.0 + mod_ref[0, sc:sc + 1, :]) + mod_ref[0, sh:sh + 1, :]).astype(h_ref.dtype)


def _modulate(x, mod, *, sh, sc, ctx_tiles, ctx_row, out_dtype):
    B, T, D = x.shape
    return pl.pallas_call(
        functools.partial(_modulate_kernel, sh=sh, sc=sc),
        grid=(B, T // ROW_TILE),
        in_specs=[pl.BlockSpec((1, ROW_TILE, D), lambda b, t: (b, t, 0)),
                  pl.BlockSpec((1, N_MOD, D), _mod_index(ctx_tiles, ctx_row, 0))],
        out_specs=pl.BlockSpec((1, ROW_TILE, D), lambda b, t: (b, t, 0)),
        out_shape=jax.ShapeDtypeStruct((B, T, D), out_dtype),
        compiler_params=_params("parallel", "parallel"),
        name="modulate",
    )(x, mod)


def _ln_kernel(*refs, alpha, gate, sh, sc, has_h, has_router):
    x_ref, y_ref, mod_ref, modh_ref, w_ref, b_ref = refs[:6]
    pos = 6
    router_ref = None
    if has_router:
        router_ref = refs[pos]
        pos += 1
    xo_ref = refs[pos]
    pos += 1
    z = alpha * x_ref[0] + mod_ref[0, gate:gate + 1, :] * y_ref[0].astype(F32)
    mu = jnp.mean(z, axis=-1, keepdims=True)
    d = z - mu
    var = jnp.mean(d * d, axis=-1, keepdims=True)
    xn = d * lax.rsqrt(var + LN_EPS) * w_ref[...] + b_ref[...]
    xo_ref[0] = xn
    if has_h:
        h = xn * (1.0 + modh_ref[0, sc:sc + 1, :]) + modh_ref[0, sh:sh + 1, :]
        h_ref = refs[pos]
        pos += 1
        h_ref[0] = h.astype(h_ref.dtype)
        if has_router:
            refs[pos][0] = jnp.dot(h, router_ref[...], precision=HIGHEST, preferred_element_type=F32)


def _ln(x, y, mod, ln_w, ln_b, *, alpha, gate, h_mod=None, mod_h=None, h_dtype=BF16, router=None,
        ctx_tiles, ctx_row, drop_tiles=0):
    B, T, D = x.shape
    nt = T // ROW_TILE - drop_tiles
    To = nt * ROW_TILE
    off = drop_tiles
    in_specs = [pl.BlockSpec((1, ROW_TILE, D), lambda b, t: (b, t + off, 0)),
                pl.BlockSpec((1, ROW_TILE, D), lambda b, t: (b, t + off, 0)),
                pl.BlockSpec((1, N_MOD, D), _mod_index(ctx_tiles, ctx_row, off)),
                pl.BlockSpec((1, N_MOD, D), _mod_index(ctx_tiles, ctx_row, off)),
                pl.BlockSpec((1, D), lambda b, t: (0, 0)),
                pl.BlockSpec((1, D), lambda b, t: (0, 0))]
    args = [x, y, mod, mod if mod_h is None else mod_h, ln_w.reshape(1, D), ln_b.reshape(1, D)]
    out_specs = [pl.BlockSpec((1, ROW_TILE, D), lambda b, t: (b, t, 0))]
    out_shape = [jax.ShapeDtypeStruct((B, To, D), F32)]
    sh = sc = 0
    if h_mod is not None:
        sh, sc = h_mod
        out_specs.append(pl.BlockSpec((1, ROW_TILE, D), lambda b, t: (b, t, 0)))
        out_shape.append(jax.ShapeDtypeStruct((B, To, D), h_dtype))
    if router is not None:
        E = router.shape[1]
        router_p = jnp.zeros((D, LANES), F32).at[:, :E].set(router)
        in_specs.append(pl.BlockSpec((D, LANES), lambda b, t: (0, 0)))
        args.append(router_p)
        out_specs.append(pl.BlockSpec((1, ROW_TILE, LANES), lambda b, t: (b, t, 0)))
        out_shape.append(jax.ShapeDtypeStruct((B, To, LANES), F32))
    return pl.pallas_call(
        functools.partial(_ln_kernel, alpha=alpha, gate=gate, sh=sh, sc=sc,
                          has_h=h_mod is not None, has_router=router is not None),
        grid=(B, nt),
        in_specs=in_specs,
        out_specs=out_specs,
        out_shape=out_shape,
        compiler_params=_params("parallel", "parallel"),
        name="ln_residual",
    )(*args)


def _halo_specs(T, C, col):
    r8 = ROW_TILE // 8
    last8 = T // 8 - 1
    return [pl.BlockSpec((1, ROW_TILE, C), lambda b, t, j: (b, t, col(j))),
            pl.BlockSpec((1, 8, C), lambda b, t, j: (b, jnp.maximum(t * r8 - 1, 0), col(j))),
            pl.BlockSpec((1, 8, C), lambda b, t, j: (b, jnp.minimum(t * r8 + r8, last8), col(j)))]


def _neighbours(cur, prev8, next8, t, n_tiles, ctx_tiles):
    rows = lax.broadcasted_iota(jnp.int32, cur.shape, 0)
    has_prev = jnp.logical_and(t != 0, t != ctx_tiles)
    has_next = jnp.logical_and(t != n_tiles - 1, t != ctx_tiles - 1)
    top = jnp.where(has_prev, prev8[7:8, :], 0.0)
    bot = jnp.where(has_next, next8[0:1, :], 0.0)
    up = jnp.where(rows == 0, top, pltpu.roll(cur, 1, axis=0))
    dn = jnp.where(rows == cur.shape[0] - 1, bot, pltpu.roll(cur, cur.shape[0] - 1, axis=0))
    return up, dn


def _hy_gate_kernel(*refs, n_tiles, ctx_tiles):
    zs = refs[0:9]
    cw = refs[9:12]
    cb = refs[12:15]
    x0_ref, vv_ref, vvb_ref = refs[15:18]
    t = pl.program_id(1)
    out = []
    for s in range(3):
        cur = zs[3 * s][0]
        up, dn = _neighbours(cur, zs[3 * s + 1][0], zs[3 * s + 2][0], t, n_tiles, ctx_tiles)
        w = cw[s]
        out.append(up * w[0:1, :] + cur * w[1:2, :] + dn * w[2:3, :] + cb[s][...])
    x0_ref[0] = out[0]
    vv = out[1] * out[2]
    vv_ref[0] = vv
    vvb_ref[0] = vv.astype(BF16)


def _hy_gate(zp, conv_w, conv_b, *, ctx_tiles):
    B, T, D3 = zp.shape
    D = D3 // 3
    tc = _tile(D, 512)
    nj = D // tc
    n_tiles = T // ROW_TILE
    in_specs = []
    for s in range(3):
        in_specs += _halo_specs(T, tc, lambda j, s=s: s * nj + j)
    in_specs += [pl.BlockSpec((3, tc), lambda b, t, j, s=s: (0, s * nj + j)) for s in range(3)]
    in_specs += [pl.BlockSpec((1, tc), lambda b, t, j, s=s: (0, s * nj + j)) for s in range(3)]
    blk = pl.BlockSpec((1, ROW_TILE, tc), lambda b, t, j: (b, t, j))
    return pl.pallas_call(
        functools.partial(_hy_gate_kernel, n_tiles=n_tiles, ctx_tiles=ctx_tiles),
        grid=(B, n_tiles, nj),
        in_specs=in_specs,
        out_specs=[blk, blk, blk],
        out_shape=[jax.ShapeDtypeStruct((B, T, D), F32), jax.ShapeDtypeStruct((B, T, D), F32),
                   jax.ShapeDtypeStruct((B, T, D), BF16)],
        compiler_params=_params("parallel", "parallel", "parallel"),
        name="hy_gate",
    )(*([zp] * 9), conv_w, conv_w, conv_w, *([conv_b.reshape(1, D3)] * 3))


def _hy_filter_kernel(w1_ref, b1_ref, w2_ref, b2_ref, w3_ref, fr_ref, h_ref, s_ref, *, L, D, tl):
    i = pl.program_id(0)
    row = (lax.broadcasted_iota(jnp.int32, (tl, LANES), 0) + i * tl).astype(F32)
    lane = lax.broadcasted_iota(jnp.int32, (tl, LANES), 1)
    band = jnp.where(lane <= HY_BANDS, lane - 1, lane - 1 - HY_BANDS).astype(F32)
    freq = 1e-4 + band * ((HY_BANDS - 1 - 1e-4) / (HY_BANDS - 1))
    ang = freq * (row * (2.0 * math.pi / L))
    z = jnp.where(lane == 0, row / (L - 1),
                  jnp.where(lane <= HY_BANDS, jnp.cos(ang),
                            jnp.where(lane < HY_EMB, -jnp.sin(ang), 0.0)))
    h = jnp.sin(fr_ref[0:1, :] * (jnp.dot(z, w1_ref[...], precision=HIGHEST,
                                           preferred_element_type=F32) + b1_ref[...]))
    h = jnp.sin(fr_ref[1:2, :] * (jnp.dot(h, w2_ref[...], precision=HIGHEST,
                                           preferred_element_type=F32) + b2_ref[...]))
    h = jnp.dot(h, w3_ref[...], precision=HIGHEST, preferred_element_type=F32)
    half = L // 2
    rowd = (lax.broadcasted_iota(jnp.int32, (tl, D), 0) + i * tl).astype(F32)
    dist = jnp.abs(rowd - half) / half
    chan = lax.broadcasted_iota(jnp.int32, (tl, D), 1).astype(F32)
    deltas = HY_MIN_DECAY + chan * ((HY_MAX_DECAY - HY_MIN_DECAY) / (D - 1))
    h = h * jnp.exp(-dist * deltas)
    h_ref[...] = h

    @pl.when(i == 0)
    def _():
        s_ref[...] = jnp.zeros_like(s_ref)

    s_ref[...] += jnp.sum(jnp.abs(h), axis=0, keepdims=True)


def _hy_filter(L, f_w1, f_b1, f_w2, f_b2, f_w3, f_freq):
    D = f_w3.shape[1]
    hid = f_w1.shape[1]
    w1 = jnp.zeros((LANES, LANES), F32).at[:HY_EMB, :hid].set(f_w1)
    b1 = jnp.zeros((1, LANES), F32).at[0, :hid].set(f_b1)
    w2 = jnp.zeros((LANES, LANES), F32).at[:hid, :hid].set(f_w2)
    b2 = jnp.zeros((1, LANES), F32).at[0, :hid].set(f_b2)
    w3 = jnp.zeros((LANES, D), F32).at[:hid].set(f_w3)
    fr = jnp.zeros((2, LANES), F32).at[:, :hid].set(f_freq)
    tl = _tile(L, 256, 8)
    full = lambda shape: pl.BlockSpec(shape, lambda i: (0, 0))
    return pl.pallas_call(
        functools.partial(_hy_filter_kernel, L=L, D=D, tl=tl),
        grid=(L // tl,),
        in_specs=[full((LANES, LANES)), full((1, LANES)), full((LANES, LANES)), full((1, LANES)),
                  full((LANES, D)), full((2, LANES))],
        out_specs=[pl.BlockSpec((tl, D), lambda i: (i, 0)), full((1, D))],
        out_shape=[jax.ShapeDtypeStruct((L, D), F32), jax.ShapeDtypeStruct((1, D), F32)],
        compiler_params=_params("arbitrary"),
        name="hy_filter",
    )(w1, b1, w2, b2, w3, fr)


def _dft_tables(L):
    n = 2 * L
    k = jnp.arange(L, dtype=jnp.int32)
    w = 2.0 * math.pi / n
    ang = ((k[:, None] * k[None, :]) % n).astype(F32) * w
    sign_t = jnp.where(k % 2 == 0, 1.0, -1.0).astype(F32)
    fre = jnp.cos(ang)
    fim = jnp.where(k[:, None] == 0, sign_t[None, :], -jnp.sin(ang))
    fwd = jnp.stack([fre, fim]).astype(BF16)
    m = k + L // 2
    angi = ((m[:, None] * k[None, :]) % n).astype(F32) * w
    sign_m = jnp.where(m % 2 == 0, 1.0, -1.0).astype(F32)
    ire = jnp.where(k[None, :] == 0, 1.0 / n, (2.0 / n) * jnp.cos(angi))
    iim = jnp.where(k[None, :] == 0, sign_m[:, None] / n, (-2.0 / n) * jnp.sin(angi))
    inv = jnp.concatenate([ire, iim], axis=1).astype(BF16)
    return fwd, inv


def _dft_fwd_kernel(a_ref, w_ref, *rest, mode):
    o_ref = rest[-1]
    w = w_ref[0]
    vre = jnp.dot(a_ref[0], w, preferred_element_type=F32)
    vim = jnp.dot(a_ref[1], w, preferred_element_type=F32)
    if mode == "scale":
        inv = 1.0 / (rest[0][...] + 1e-6)
        o_ref[0, 0] = vre * inv
        o_ref[0, 1] = vim * inv
    else:
        hre = rest[0][0]
        him = rest[0][1]
        first = jnp.logical_and(pl.program_id(2) == 0,
                                lax.broadcasted_iota(jnp.int32, vre.shape, 0) == 0)
        zre = jnp.where(first, vre * hre, vre * hre - vim * him)
        zim = jnp.where(first, vim * him, vre * him + vim * hre)
        o_ref[0, 0] = zre.astype(o_ref.dtype)
        o_ref[0, 1] = zim.astype(o_ref.dtype)


def _dft_fwd(fwd, w, extra, *, mode, out_dtype):
    B, L, D = w.shape
    tm = _tile(L, 256, 8)
    tn = _tile(D, 512)
    if mode == "scale":
        extra_spec = pl.BlockSpec((1, tn), lambda b, j, m: (0, j))
    else:
        extra_spec = pl.BlockSpec((2, tm, tn), lambda b, j, m: (0, m, j))
    return pl.pallas_call(
        functools.partial(_dft_fwd_kernel, mode=mode),
        grid=(B, D // tn, L // tm),
        in_specs=[pl.BlockSpec((2, tm, L), lambda b, j, m: (0, m, 0)),
                  pl.BlockSpec((1, L, tn), lambda b, j, m: (b, 0, j)),
                  extra_spec],
        out_specs=pl.BlockSpec((1, 2, tm, tn), lambda b, j, m: (b, 0, m, j)),
        out_shape=jax.ShapeDtypeStruct((B, 2, L, D), out_dtype),
        compiler_params=_params("parallel", "parallel", "parallel"),
        name="dft_fwd",
    )(fwd, w, extra)


def _dft_inv_kernel(b_ref, z_ref, x0_ref, vv_ref, skip_ref, u_ref):
    y = jnp.dot(b_ref[...], z_ref[0], preferred_element_type=F32)
    u_ref[0] = (x0_ref[0] * (y + vv_ref[0] * skip_ref[...])).astype(u_ref.dtype)


def _dft_inv(inv, z, x0, vv, skip, *, row_off):
    B, n, D = z.shape
    L = n // 2
    tm = _tile(L, ROW_TILE, 8)
    tn = _tile(D, 512)
    off = row_off // tm
    return pl.pallas_call(
        _dft_inv_kernel,
        grid=(B, D // tn, L // tm),
        in_specs=[pl.BlockSpec((tm, n), lambda b, j, m: (m, 0)),
                  pl.BlockSpec((1, n, tn), lambda b, j, m: (b, 0, j)),
                  pl.BlockSpec((1, tm, tn), lambda b, j, m: (b, m + off, j)),
                  pl.BlockSpec((1, tm, tn), lambda b, j, m: (b, m + off, j)),
                  pl.BlockSpec((1, tn), lambda b, j, m: (0, j))],
        out_specs=pl.BlockSpec((1, tm, tn), lambda b, j, m: (b, m, j)),
        out_shape=jax.ShapeDtypeStruct((B, L, D), BF16),
        compiler_params=_params("parallel", "parallel", "parallel"),
        name="dft_inv",
    )(inv, z, x0, vv, skip.reshape(1, D))


def _hyena(h, p, tables, *, Lc, has_ctx):
    B, T, D = h.shape
    M = B * T
    zp = _mm(h.reshape(M, D), p["w_in"], bias=p["b_in"]).reshape(B, T, 3 * D)
    ctx_tiles = Lc // ROW_TILE if has_ctx else 0
    x0, vv, vvb = _hy_gate(zp, p["conv_w"], p["conv_b"], ctx_tiles=ctx_tiles)
    segs = [(Lc, T - Lc)] if has_ctx else [(0, T)]
    if has_ctx:
        segs = [(0, Lc)] + segs
    us = []
    for start, L in segs:
        fwd, inv = tables[L]
        filt, asum = _hy_filter(L, *p["filter"])
        hf = _dft_fwd(fwd, filt.astype(BF16)[None], asum, mode="scale", out_dtype=F32)[0]
        seg = vvb if (start == 0 and L == T) else lax.slice_in_dim(vvb, start, start + L, axis=1)
        z = _dft_fwd(fwd, seg, hf, mode="mul", out_dtype=BF16).reshape(B, 2 * L, D)
        us.append(_dft_inv(inv, z, x0, vv, p["skip"], row_off=start))
    u = us[0] if len(us) == 1 else jnp.concatenate(us, axis=1)
    return _mm(u.reshape(M, D), p["w_out"], bias=p["b_out"]).reshape(B, T, D)


def _rw_prep_kernel(cur_ref, prev_ref, next_ref, mu_ref, *o_refs, n_tiles, ctx_tiles):
    t = pl.program_id(1)
    cur = cur_ref[0]
    up, dn = _neighbours(cur, prev_ref[0], next_ref[0], t, n_tiles, ctx_tiles)
    dx = 0.5 * (up + dn) - cur
    for j, o_ref in enumerate(o_refs):
        o_ref[0] = (cur + dx * mu_ref[j:j + 1, :]).astype(o_ref.dtype)


def _rw_prep(hs, mu, *, ctx_tiles):
    B, T, D = hs.shape
    tc = _tile(D, 1024)
    n_tiles = T // ROW_TILE
    blk = pl.BlockSpec((1, ROW_TILE, tc), lambda b, t, j: (b, t, j))
    return pl.pallas_call(
        functools.partial(_rw_prep_kernel, n_tiles=n_tiles, ctx_tiles=ctx_tiles),
        grid=(B, n_tiles, D // tc),
        in_specs=_halo_specs(T, tc, lambda j: j) + [pl.BlockSpec((6, tc), lambda b, t, j: (0, j))],
        out_specs=[blk] * 6,
        out_shape=[jax.ShapeDtypeStruct((B, T, D), BF16)] * 6,
        compiler_params=_params("parallel", "parallel", "parallel"),
        name="rw_prep",
    )(hs, hs, hs, mu)


def _block_diag(x, mask):
    return jnp.where(mask, jnp.concatenate([x] * RW_GROUP, axis=0), 0.0).astype(BF16)


def _rw_chunks(r, k, v, lw, a, k_k, k_a, state, same_head, incl, strict, reverse):
    C = RW_CHUNK
    W = RW_GROUP * HEAD
    G = range(len(r))
    nt = (((1,), (1,)), ((), ()))
    ones = same_head.astype(F32)
    tri = incl[:, :C].astype(F32)
    bd = lambda t: _block_diag(t, same_head)
    mm = lambda x, y: jnp.dot(x.astype(BF16), y, preferred_element_type=F32)

    kkr = [k[g] * k_k[g] for g in G]
    ss = [jnp.dot(kkr[g] * kkr[g], ones, precision=HIGHEST, preferred_element_type=F32) for g in G]
    lp = [jnp.dot(tri, lw[g], precision=HIGHEST, preferred_element_type=F32) for g in G]
    kk = [kkr[g] / jnp.maximum(jnp.sqrt(ss[g]), 1e-12) for g in G]
    kd = [k[g] * (1.0 + (a[g] - 1.0) * k_a[g]) for g in G]
    lp_end = [lp[g][0:1, :] if reverse else lp[g][C - 1:C, :] for g in G]
    e_neg = [jnp.exp(-lp[g]) for g in G]
    lhs = [jnp.concatenate([-kk[g] * jnp.exp(lp[g] - lw[g]), r[g] * jnp.exp(lp[g])], axis=0).astype(BF16)
           for g in G]
    rhs = [jnp.concatenate([bd(kk[g] * a[g] * e_neg[g]), bd(kd[g] * e_neg[g])], axis=0) for g in G]
    cross = [lax.dot_general(lhs[g], rhs[g], nt, preferred_element_type=F32) for g in G]
    from_state = [lax.dot_general(lhs[g], state[g].astype(BF16), nt, preferred_element_type=F32)
                  for g in G]
    v_bd = [bd(v[g]) for g in G]
    p = [jnp.where(strict, cross[g][:C, :W], 0.0) for g in G]
    x = [from_state[g][:C] + mm(jnp.where(strict, cross[g][:C, W:], 0.0), v_bd[g]) for g in G]
    n = 1
    while n < C:
        x = [x[g] + mm(p[g], bd(x[g])) for g in G]
        n *= 2
        if n < C:
            p = [mm(p[g], bd(p[g])) for g in G]
    y = [from_state[g][C:] + mm(jnp.where(incl, cross[g][C:, :W], 0.0), bd(x[g]))
         + mm(jnp.where(incl, cross[g][C:, W:], 0.0), v_bd[g]) for g in G]

    tail = [jnp.exp(lp_end[g] - lp[g]) for g in G]
    uv = [jnp.concatenate([x[g], v[g]], axis=0).astype(BF16) for g in G]
    bk = [jnp.concatenate([kk[g] * a[g] * tail[g], kd[g] * tail[g]], axis=0).astype(BF16) for g in G]
    upd = [lax.dot_general(uv[g], bk[g], (((0,), (0,)), ((), ())), preferred_element_type=F32)
           for g in G]
    new_state = [state[g] * jnp.exp(lp_end[g]) + jnp.where(same_head, upd[g], 0.0) for g in G]
    return y, new_state


def _rw_scan_kernel(r_ref, k_ref, v_ref, lw_ref, a_ref, kk_ref, ka_ref, o_ref, state_ref, *, reverse):
    C = RW_CHUNK
    W = RW_GROUP * HEAD

    @pl.when(pl.program_id(2) == 0)
    def _():
        state_ref[...] = jnp.zeros_like(state_ref)

    ri = lax.broadcasted_iota(jnp.int32, (W, W), 0)
    ci = lax.broadcasted_iota(jnp.int32, (W, W), 1)
    same_head = (ri // HEAD) == (ci // HEAD)
    t_i = lax.broadcasted_iota(jnp.int32, (C, W), 0)
    s_i = lax.broadcasted_iota(jnp.int32, (C, W), 1) % C
    incl = (s_i >= t_i) if reverse else (s_i <= t_i)
    strict = (s_i > t_i) if reverse else (s_i < t_i)

    ng = state_ref.shape[0]
    sl = [slice(g * W, (g + 1) * W) for g in range(ng)]
    load = lambda ref: [ref[0, :, s] for s in sl]
    y, new_state = _rw_chunks(load(r_ref), load(k_ref), load(v_ref), load(lw_ref), load(a_ref),
                              [kk_ref[:, s] for s in sl], [ka_ref[:, s] for s in sl],
                              [state_ref[g] for g in range(ng)], same_head, incl, strict, reverse)
    for g in range(ng):
        o_ref[0, :, sl[g]] = y[g]
        state_ref[g] = new_state[g]


RW_GROUPS_PER_STEP = 4


def _rw_scan(r, k, v, lw, a, k_k, k_a, *, Lc, reverse):
    B, T, D = r.shape
    C = RW_CHUNK
    assert C == HEAD
    W = RW_GROUP * HEAD
    ng = math.gcd(D // W, RW_GROUPS_PER_STEP)
    nc = T // C
    ncc = Lc // C

    def chunk(c):
        if not reverse:
            return c
        return jnp.where(c < ncc, ncc - 1 - c, nc - 1 - (c - ncc))

    blk = pl.BlockSpec((1, C, ng * W), lambda b, g, c: (b, chunk(c), g))
    vec = pl.BlockSpec((1, ng * W), lambda b, g, c: (0, g))
    return pl.pallas_call(
        functools.partial(_rw_scan_kernel, reverse=reverse),
        grid=(B, D // (ng * W), nc),
        in_specs=[blk] * 5 + [vec, vec],
        out_specs=blk,
        out_shape=jax.ShapeDtypeStruct((B, T, D), F32),
        scratch_shapes=[pltpu.VMEM((ng, W, W), F32)],
        compiler_params=_params("parallel", "parallel", "arbitrary"),
        name="rw_scan",
    )(r, k, v, lw, a, k_k.reshape(1, D), k_a.reshape(1, D))


def _head_sums(x, ones_bd):
    W = ones_bd.shape[0]
    hi = x.astype(BF16)
    lo = (x - hi.astype(F32)).astype(BF16)
    cols = []
    for g in range(x.shape[1] // W):
        sl = slice(g * W, (g + 1) * W)
        cols.append(jnp.dot(hi[:, sl], ones_bd, preferred_element_type=F32)
                    + jnp.dot(lo[:, sl], ones_bd, preferred_element_type=F32))
    return jnp.concatenate(cols, axis=1)


def _rw_post_kernel(of_ref, ob_ref, r_ref, k_ref, v_ref, g_ref, rk_ref, gw_ref, gb_ref, o_ref):
    W = RW_GROUP * HEAD
    ri = lax.broadcasted_iota(jnp.int32, (W, W), 0)
    ci = lax.broadcasted_iota(jnp.int32, (W, W), 1)
    ones_bd = jnp.where((ri // HEAD) == (ci // HEAD), 1.0, 0.0).astype(BF16)
    o = of_ref[0] + ob_ref[0]
    m = _head_sums(o, ones_bd) * (1.0 / HEAD)
    d = o - m
    var = _head_sums(d * d, ones_bd) * (1.0 / HEAD)
    on = d * lax.rsqrt(var + RW_GN_EPS) * gw_ref[...] + gb_ref[...]
    bonus = _head_sums(r_ref[0] * k_ref[0] * rk_ref[...], ones_bd) * v_ref[0]
    o_ref[0] = ((on + bonus) * g_ref[0]).astype(o_ref.dtype)


def _rw_post(o_f, o_b, r, k, v, g, r_k, gn_w, gn_b):
    B, T, D = r.shape
    tt = _tile(T, 128, 8)
    blk = pl.BlockSpec((1, tt, D), lambda b, t: (b, t, 0))
    par = pl.BlockSpec((1, D), lambda b, t: (0, 0))
    return pl.pallas_call(
        _rw_post_kernel,
        grid=(B, T // tt),
        in_specs=[blk] * 6 + [par] * 3,
        out_specs=blk,
        out_shape=jax.ShapeDtypeStruct((B, T, D), BF16),
        compiler_params=_params("parallel", "parallel"),
        name="rw_post",
    )(o_f, o_b, r, k, v, g, r_k.reshape(1, D), gn_w.reshape(1, D), gn_b.reshape(1, D))


def _rwkv(hs, p, *, Lc):
    B, T, D = hs.shape
    M = B * T
    ctx_tiles = Lc // ROW_TILE
    xr, xw, xk, xv, xa, xg = [t.reshape(M, D) for t in _rw_prep(hs, p["mu"], ctx_tiles=ctx_tiles)]
    w_rkv = p["w_rkv"]
    r = _mm(xr, w_rkv[0]).reshape(B, T, D)
    k = _mm(xk, w_rkv[1]).reshape(B, T, D)
    v = _mm(xv, w_rkv[2]).reshape(B, T, D)
    g = _lora(xg, p["g1"], p["g2"], jnp.zeros((D,), F32), mid="sigmoid", fin=None).reshape(B, T, D)
    outs = []
    for d in range(2):
        lw = _lora(xw, p["w1"][d], p["w2"][d], p["w0"][d], mid="tanh", fin="logdecay").reshape(B, T, D)
        a = _lora(xa, p["a1"][d], p["a2"][d], p["a0"][d], mid=None, fin="sigmoid").reshape(B, T, D)
        outs.append(_rw_scan(r, k, v, lw, a, p["k_k"], p["k_a"], Lc=Lc, reverse=d == 1))
    y = _rw_post(outs[0], outs[1], r, k, v, g, p["r_k"], p["gn_w"], p["gn_b"])
    return _mm(y.reshape(M, D), p["w_o"]).reshape(B, T, D)


def _rope_tables(T, Lc):
    quarter = HEAD // 4
    pos = jnp.arange(T - Lc, dtype=jnp.int32)
    rows = (pos // GRID_W).astype(F32)
    cols = (pos % GRID_W).astype(F32)
    inv = ROPE_BASE ** (-jnp.arange(quarter, dtype=F32) / quarter)
    ang = jnp.concatenate([rows[:, None] * inv, rows[:, None] * inv,
                           cols[:, None] * inv, cols[:, None] * inv], axis=1)
    ang = jnp.concatenate([jnp.zeros((Lc, HEAD), F32), ang], axis=0)
    ang = jnp.concatenate([ang, ang], axis=1)
    return jnp.cos(ang), jnp.sin(ang)


def _rope_kernel(x_ref, cos_ref, sin_ref, q_ref, k_ref, v_ref, *, n_q, n_kv, scale):
    x = x_ref[0]
    n_rot = (n_q + n_kv) * HEAD
    xr = x[:, :n_rot]
    reps = n_rot // LANES
    cos = jnp.concatenate([cos_ref[...]] * reps, axis=1)
    sin = jnp.concatenate([sin_ref[...]] * reps, axis=1)
    quarter = HEAD // 4
    lane = lax.broadcasted_iota(jnp.int32, xr.shape, 1)
    first = (lane % (2 * quarter)) < quarter
    rot = jnp.where(first, -pltpu.roll(xr, n_rot - quarter, axis=1), pltpu.roll(xr, quarter, axis=1))
    y = xr * cos + rot * sin
    for h in range(n_q):
        q_ref[0, h] = (y[:, h * HEAD:(h + 1) * HEAD] * scale).astype(q_ref.dtype)
    for h in range(n_kv):
        lo = (n_q + h) * HEAD
        k_ref[0, h] = y[:, lo:lo + HEAD].astype(k_ref.dtype)
        lo = (n_q + n_kv + h) * HEAD
        v_ref[0, h] = x[:, lo:lo + HEAD].astype(v_ref.dtype)


def _rope(qkv, cos, sin, *, n_q, n_kv):
    B, T, W = qkv.shape
    tt = _tile(T, 128, 8)
    out = lambda n: pl.BlockSpec((1, n, tt, HEAD), lambda b, t: (b, 0, t, 0))
    return pl.pallas_call(
        functools.partial(_rope_kernel, n_q=n_q, n_kv=n_kv, scale=HEAD ** -0.5),
        grid=(B, T // tt),
        in_specs=[pl.BlockSpec((1, tt, W), lambda b, t: (b, t, 0)),
                  pl.BlockSpec((tt, LANES), lambda b, t: (t, 0)),
                  pl.BlockSpec((tt, LANES), lambda b, t: (t, 0))],
        out_specs=[out(n_q), out(n_kv), out(n_kv)],
        out_shape=[jax.ShapeDtypeStruct((B, n_q, T, HEAD), BF16),
                   jax.ShapeDtypeStruct((B, n_kv, T, HEAD), BF16),
                   jax.ShapeDtypeStruct((B, n_kv, T, HEAD), BF16)],
        compiler_params=_params("parallel", "parallel"),
        name="rope",
    )(qkv, cos, sin)


def _attn_kernel(*refs, n_q, n_kv, local, q_off, n_blocks):
    if local:
        q_ref, kp_ref, kc_ref, kn_ref, vp_ref, vc_ref, vn_ref, kx_ref, vx_ref, sink_ref, o_ref = refs
    else:
        q_ref, kx_ref, vx_ref, sink_ref, o_ref = refs
    G = n_q // n_kv
    R = G * AT_BLOCK
    n = pl.program_id(1)
    nt = (((1,), (1,)), ((), ()))
    if local:
        qi = lax.broadcasted_iota(jnp.int32, (R, 3 * AT_BLOCK), 0) % AT_BLOCK
        kj = lax.broadcasted_iota(jnp.int32, (R, 3 * AT_BLOCK), 1) - AT_BLOCK
        kpos = kj + n * AT_BLOCK
        ok = (jnp.abs(qi - kj) <= AT_WINDOW) & (kpos >= 0) & (kpos < n_blocks * AT_BLOCK)
    for h in range(n_kv):
        q = q_ref[0, h * G:(h + 1) * G].reshape(R, HEAD)
        sink = jnp.concatenate(
            [jnp.broadcast_to(sink_ref[h * G + g:h * G + g + 1, 0:1], (AT_BLOCK, 1)) for g in range(G)],
            axis=0)
        kx = kx_ref[0, h]
        s_ctx = lax.dot_general(q, kx, nt, preferred_element_type=F32)
        m = jnp.maximum(jnp.max(s_ctx, axis=-1, keepdims=True), sink)
        if local:
            k_loc = jnp.concatenate([kp_ref[0, h], kc_ref[0, h], kn_ref[0, h]], axis=0)
            v_loc = jnp.concatenate([vp_ref[0, h], vc_ref[0, h], vn_ref[0, h]], axis=0)
            s_loc = jnp.where(ok, lax.dot_general(q, k_loc, nt, preferred_element_type=F32), NEG_INF)
            m = jnp.maximum(m, jnp.max(s_loc, axis=-1, keepdims=True))
        p_ctx = jnp.exp(s_ctx - m)
        den = jnp.sum(p_ctx, axis=-1, keepdims=True) + jnp.exp(sink - m)
        if local:
            p_loc = jnp.exp(s_loc - m)
            den = den + jnp.sum(p_loc, axis=-1, keepdims=True)
        inv = 1.0 / den
        o = jnp.dot((p_ctx * inv).astype(BF16), vx_ref[0, h], preferred_element_type=F32)
        if local:
            o = o + jnp.dot((p_loc * inv).astype(BF16), v_loc, preferred_element_type=F32)
        for g in range(G):
            hh = h * G + g
            o_ref[0, :, hh * HEAD:(hh + 1) * HEAD] = o[g * AT_BLOCK:(g + 1) * AT_BLOCK].astype(o_ref.dtype)


def _attn(q, k, v, sink, *, Lc, local):
    B, n_q, T, _ = q.shape
    n_kv = k.shape[1]
    cb = Lc // AT_BLOCK
    nb = (T - Lc) // AT_BLOCK if local else cb
    q_off = cb if local else 0
    last = T // AT_BLOCK - 1
    qspec = pl.BlockSpec((1, n_q, AT_BLOCK, HEAD), lambda b, n: (b, 0, n + q_off, 0))
    xspec = pl.BlockSpec((1, n_kv, Lc, HEAD), lambda b, n: (b, 0, 0, 0))
    sspec = pl.BlockSpec((n_q, LANES), lambda b, n: (0, 0))
    sink_b = jnp.broadcast_to(sink.astype(F32)[:, None], (n_q, LANES))
    if local:
        blk = lambda f: pl.BlockSpec((1, n_kv, AT_BLOCK, HEAD), lambda b, n: (b, 0, f(n), 0))
        prev = blk(lambda n: jnp.maximum(n + cb - 1, cb))
        cur = blk(lambda n: n + cb)
        nxt = blk(lambda n: jnp.minimum(n + cb + 1, last))
        in_specs = [qspec, prev, cur, nxt, prev, cur, nxt, xspec, xspec, sspec]
        args = (q, k, k, k, v, v, v, k, v, sink_b)
    else:
        in_specs = [qspec, xspec, xspec, sspec]
        args = (q, k, v, sink_b)
    return pl.pallas_call(
        functools.partial(_attn_kernel, n_q=n_q, n_kv=n_kv, local=local, q_off=q_off, n_blocks=nb),
        grid=(B, nb),
        in_specs=in_specs,
        out_specs=pl.BlockSpec((1, AT_BLOCK, n_q * HEAD), lambda b, n: (b, n, 0)),
        out_shape=jax.ShapeDtypeStruct((B, nb * AT_BLOCK, n_q * HEAD), BF16),
        compiler_params=_params("parallel", "parallel"),
        name="attn_local" if local else "attn_ctx",
    )(*args)


def _attention(h, p, *, Lc):
    B, T, D = h.shape
    M = B * T
    n_q = D // HEAD
    qkv = _mm(h.reshape(M, D), p["w_qkv"], bias=p["b_qkv"]).reshape(B, T, -1)
    cos, sin = _rope_tables(T, Lc)
    q, k, v = _rope(qkv, cos, sin, n_q=n_q, n_kv=AT_KV_HEADS)
    o_ctx = _attn(q, k, v, p["sink"], Lc=Lc, local=False)
    o_lat = _attn(q, k, v, p["sink"], Lc=Lc, local=True)
    o = jnp.concatenate([o_ctx, o_lat], axis=1)
    return _mm(o.reshape(M, D), p["w_o"], bias=p["b_o"]).reshape(B, T, D)


def _swiglu_ffn(h, w1, w3, w2):
    mid = _mm(h, w1, w3=w3, out_dtype=BF16)
    return _mm(mid, w2)


MOE_TILE = 512
ROUTE_G1, ROUTE_G2, ROUTE_I1, ROUTE_I2 = 0, 1, 2, 3


def _gates_kernel(l_ref, route_ref, sel_ref, *, n_experts):
    l = l_ref[...]
    lane = lax.broadcasted_iota(jnp.int32, l.shape, 1)
    l = jnp.where(lane < n_experts, l, -jnp.inf)
    m1 = jnp.max(l, axis=-1, keepdims=True)
    i1 = jnp.min(jnp.where(l == m1, lane, LANES), axis=-1, keepdims=True)
    l2 = jnp.where(lane == i1, -jnp.inf, l)
    m2 = jnp.max(l2, axis=-1, keepdims=True)
    i2 = jnp.min(jnp.where(l2 == m2, lane, LANES), axis=-1, keepdims=True)
    e2 = jnp.exp(m2 - m1)
    den = 1.0 + e2
    route_ref[...] = jnp.where(lane == ROUTE_G1, 1.0 / den,
                               jnp.where(lane == ROUTE_G2, e2 / den,
                                         jnp.where(lane == ROUTE_I1, i1.astype(F32),
                                                   jnp.where(lane == ROUTE_I2, i2.astype(F32), 0.0))))
    sel_ref[...] = jnp.where((lane == i1) | (lane == i2), 1.0, 0.0).astype(sel_ref.dtype)


def _gates(logits, n_experts):
    M = logits.shape[0]
    tm = _tile(M, 1024, 8)
    blk = pl.BlockSpec((tm, LANES), lambda i: (i, 0))
    return pl.pallas_call(
        functools.partial(_gates_kernel, n_experts=n_experts),
        grid=(M // tm,),
        in_specs=[blk],
        out_specs=[blk, blk],
        out_shape=[jax.ShapeDtypeStruct((M, LANES), F32), jax.ShapeDtypeStruct((M, LANES), BF16)],
        compiler_params=_params("parallel"),
        name="moe_gates",
    )(logits)


def _rank_kernel(sel_ref, rank_ref, cnt_ref, carry_ref):
    @pl.when(pl.program_id(0) == 0)
    def _():
        carry_ref[...] = jnp.zeros_like(carry_ref)

    s = sel_ref[...]
    n = s.shape[0]
    earlier = (lax.broadcasted_iota(jnp.int32, (n, n), 1) < lax.broadcasted_iota(jnp.int32, (n, n), 0))
    within = jnp.dot(jnp.where(earlier, 1.0, 0.0).astype(BF16), s, preferred_element_type=F32)
    rank_ref[...] = within + carry_ref[...]
    carry_ref[...] += jnp.sum(s.astype(F32), axis=0, keepdims=True)
    cnt_ref[...] = carry_ref[...]


def _rank(sel):
    M = sel.shape[0]
    tr = _tile(M, 512, 8)
    return pl.pallas_call(
        _rank_kernel,
        grid=(M // tr,),
        in_specs=[pl.BlockSpec((tr, LANES), lambda i: (i, 0))],
        out_specs=[pl.BlockSpec((tr, LANES), lambda i: (i, 0)), pl.BlockSpec((1, LANES), lambda i: (0, 0))],
        out_shape=[jax.ShapeDtypeStruct((M, LANES), F32), jax.ShapeDtypeStruct((1, LANES), F32)],
        scratch_shapes=[pltpu.VMEM((1, LANES), F32)],
        compiler_params=_params("arbitrary"),
        name="moe_rank",
    )(sel)


def _pos_kernel(route_ref, rank_ref, offs_ref, pos_ref):
    lane = lax.broadcasted_iota(jnp.int32, rank_ref.shape, 1)
    lane_f = lane.astype(F32)
    tot = rank_ref[...] + offs_ref[...]
    route = route_ref[...]
    p1 = jnp.sum(jnp.where(lane_f == route[:, ROUTE_I1:ROUTE_I1 + 1], tot, 0.0), axis=-1, keepdims=True)
    p2 = jnp.sum(jnp.where(lane_f == route[:, ROUTE_I2:ROUTE_I2 + 1], tot, 0.0), axis=-1, keepdims=True)
    pos_ref[...] = jnp.where(lane == 0, p1, jnp.where(lane == 1, p2, 0.0)).astype(jnp.int32)


def _positions(route, rank, offs):
    M = route.shape[0]
    tm = _tile(M, 1024, 8)
    blk = pl.BlockSpec((tm, LANES), lambda i: (i, 0))
    return pl.pallas_call(
        _pos_kernel,
        grid=(M // tm,),
        in_specs=[blk, blk, pl.BlockSpec((1, LANES), lambda i: (0, 0))],
        out_specs=blk,
        out_shape=jax.ShapeDtypeStruct((M, LANES), jnp.int32),
        compiler_params=_params("parallel"),
        name="moe_pos",
    )(route, rank, offs)


def _row_copy(src, dst, sem):
    return pltpu.make_async_copy(src, dst, sem)


def _dispatch_kernel(p1_ref, p2_ref, h_ref, xs_in_ref, xs_ref, sem):
    del xs_in_ref
    tt = h_ref.shape[0]
    base = pl.program_id(0) * tt

    def start(r, carry):
        row = h_ref.at[pl.ds(r, 1)]
        _row_copy(row, xs_ref.at[pl.ds(p1_ref[base + r], 1)], sem).start()
        _row_copy(row, xs_ref.at[pl.ds(p2_ref[base + r], 1)], sem).start()
        return carry

    lax.fori_loop(0, tt, start, 0)

    def wait(r, carry):
        _row_copy(h_ref.at[pl.ds(0, 1)], xs_ref.at[pl.ds(0, 1)], sem).wait()
        return carry

    lax.fori_loop(0, 2 * tt, wait, 0)


def _dispatch(h, p1, p2, n_rows):
    M, D = h.shape
    tt = _tile(M, ROW_TILE, 8)
    return pl.pallas_call(
        _dispatch_kernel,
        grid_spec=pltpu.PrefetchScalarGridSpec(
            num_scalar_prefetch=2,
            grid=(M // tt,),
            in_specs=[pl.BlockSpec((tt, D), lambda i, p1, p2: (i, 0)),
                      pl.BlockSpec(memory_space=pl.ANY)],
            out_specs=pl.BlockSpec(memory_space=pl.ANY),
            scratch_shapes=[pltpu.SemaphoreType.DMA(())]),
        out_shape=jax.ShapeDtypeStruct((n_rows, D), h.dtype),
        input_output_aliases={3: 0},
        compiler_params=_params("arbitrary"),
        name="moe_dispatch",
    )(p1, p2, h, jnp.zeros((n_rows, D), h.dtype))


def _gmm_kernel(te_ref, nv_ref, x_ref, *refs, swiglu):
    o_ref = refs[-1]

    @pl.when(pl.program_id(0) < nv_ref[0])
    def _():
        x = x_ref[...].astype(BF16)
        acc = jnp.dot(x, refs[0][0], preferred_element_type=F32)
        if swiglu:
            acc = acc * jax.nn.sigmoid(acc) * jnp.dot(x, refs[1][0], preferred_element_type=F32)
        o_ref[...] = acc.astype(o_ref.dtype)

    @pl.when(pl.program_id(0) >= nv_ref[0])
    def _():
        o_ref[...] = jnp.zeros_like(o_ref)


def _gmm(x, ws, tile_expert, n_valid, *, out_dtype, tn=512):
    P, K = x.shape
    N = ws[0].shape[2]
    tm = MOE_TILE
    tn = _tile(N, tn)
    wspec = pl.BlockSpec((1, K, tn), lambda i, j, te, nv: (te[i], 0, j))
    return pl.pallas_call(
        functools.partial(_gmm_kernel, swiglu=len(ws) == 2),
        grid_spec=pltpu.PrefetchScalarGridSpec(
            num_scalar_prefetch=2,
            grid=(P // tm, N // tn),
            in_specs=[pl.BlockSpec((tm, K), lambda i, j, te, nv: (i, 0))] + [wspec] * len(ws),
            out_specs=pl.BlockSpec((tm, tn), lambda i, j, te, nv: (i, j))),
        out_shape=jax.ShapeDtypeStruct((P, N), out_dtype),
        compiler_params=_params("parallel", "parallel"),
        name="moe_gmm",
    )(tile_expert, n_valid, x, *ws)


def _combine_kernel(p1_ref, p2_ref, route_ref, ys_ref, o_ref, a_ref, b_ref, sem):
    tt = o_ref.shape[0]
    base = pl.program_id(0) * tt

    def start(r, carry):
        _row_copy(ys_ref.at[pl.ds(p1_ref[base + r], 1)], a_ref.at[pl.ds(r, 1)], sem).start()
        _row_copy(ys_ref.at[pl.ds(p2_ref[base + r], 1)], b_ref.at[pl.ds(r, 1)], sem).start()
        return carry

    lax.fori_loop(0, tt, start, 0)

    def wait(r, carry):
        _row_copy(ys_ref.at[pl.ds(0, 1)], a_ref.at[pl.ds(0, 1)], sem).wait()
        return carry

    lax.fori_loop(0, 2 * tt, wait, 0)
    route = route_ref[...]
    o_ref[...] = (route[:, ROUTE_G1:ROUTE_G1 + 1] * a_ref[...]
                  + route[:, ROUTE_G2:ROUTE_G2 + 1] * b_ref[...])


def _combine(ys, route, p1, p2):
    M = route.shape[0]
    D = ys.shape[1]
    tt = _tile(M, ROW_TILE, 8)
    return pl.pallas_call(
        _combine_kernel,
        grid_spec=pltpu.PrefetchScalarGridSpec(
            num_scalar_prefetch=2,
            grid=(M // tt,),
            in_specs=[pl.BlockSpec((tt, LANES), lambda i, p1, p2: (i, 0)),
                      pl.BlockSpec(memory_space=pl.ANY)],
            out_specs=pl.BlockSpec((tt, D), lambda i, p1, p2: (i, 0)),
            scratch_shapes=[pltpu.VMEM((tt, D), F32), pltpu.VMEM((tt, D), F32),
                            pltpu.SemaphoreType.DMA(())]),
        out_shape=jax.ShapeDtypeStruct((M, D), F32),
        compiler_params=_params("arbitrary"),
        name="moe_combine",
    )(p1, p2, route, ys)


def _moe(h, logits, w1, w3, w2):
    M, D = h.shape
    E = w1.shape[0]
    tm = MOE_TILE
    route, sel = _gates(logits, E)
    rank, cnt = _rank(sel)
    counts = cnt[0, :E].astype(jnp.int32)
    padded = (counts + tm - 1) // tm * tm
    ends = jnp.cumsum(padded)
    n_tiles = (TOP_K * M) // tm + E
    tile_expert = jnp.minimum(
        jnp.searchsorted(ends, jnp.arange(n_tiles, dtype=jnp.int32) * tm, side="right"), E - 1
    ).astype(jnp.int32)
    n_valid = (ends[-1:] // tm).astype(jnp.int32)
    offs = jnp.zeros((1, LANES), F32).at[0, :E].set((ends - padded).astype(F32))
    pos = _positions(route, rank, offs)
    p1, p2 = pos[:, 0], pos[:, 1]
    xs = _dispatch(h, p1, p2, n_tiles * tm)
    mid = _gmm(xs, [w1, w3], tile_expert, n_valid, out_dtype=BF16)
    ys = _gmm(mid, [w2], tile_expert, n_valid, out_dtype=F32)
    return _combine(ys, route, p1, p2)


def kernel(x, c, ctx, c_ctx, ada_w, ada_b, ln_w, ln_b, hy_w_in, hy_b_in, hy_conv_w, hy_conv_b, hy_f_w1, hy_f_b1, hy_f_w2, hy_f_b2, hy_f_w3, hy_f_freq, hy_skip, hy_w_out, hy_b_out, rw_mu, rw_w_rkv, rw_w_o, rw_w0, rw_w1, rw_w2, rw_a0, rw_a1, rw_a2, rw_g1, rw_g2, rw_k_k, rw_k_a, rw_r_k, rw_gn_w, rw_gn_b, at_w_qkv, at_b_qkv, at_w_o, at_b_o, at_sink, ff_w1, ff_w3, ff_w2, moe_router, moe_w1, moe_w3, moe_w2):
    B, L, D = x.shape
    Lc = ctx.shape[1]
    depth = ada_w.shape[0]
    alpha = (2 * depth) ** 0.25
    assert Lc % ROW_TILE == 0 and L % ROW_TILE == 0 and D % (RW_GROUP * HEAD) == 0
    bf = lambda t: t.astype(BF16)

    ctx_row = B
    rows = -(-(B + 1) // 8) * 8
    cond = jnp.zeros((rows, D), F32).at[:B].set(c).at[B].set(c_ctx)
    mods = _ada(cond, ada_w, ada_b).reshape(depth, rows, N_MOD, D)

    tables = {}
    if depth > 0:
        tables[L] = _dft_tables(L)
        if depth > 1:
            tables[Lc] = _dft_tables(Lc)

    xs = jnp.concatenate([ctx, x], axis=1)
    ctx_tiles = Lc // ROW_TILE
    h = _modulate(xs, mods[0], sh=0, sc=1, ctx_tiles=ctx_tiles, ctx_row=ctx_row,
                  out_dtype=BF16)
    for i in range(depth):
        last = i == depth - 1
        kind = i % 3
        j = i // 3
        mod = mods[i]
        has_ctx = xs.shape[1] != L
        ct = ctx_tiles if has_ctx else 0
        T = xs.shape[1]
        if kind == 0:
            p = dict(w_in=bf(hy_w_in[j]), b_in=hy_b_in[j], conv_w=hy_conv_w[j], conv_b=hy_conv_b[j],
                     filter=(hy_f_w1[j], hy_f_b1[j], hy_f_w2[j], hy_f_b2[j], hy_f_w3[j], hy_f_freq[j]),
                     skip=hy_skip[j], w_out=bf(hy_w_out[j]), b_out=hy_b_out[j])
            y = _hyena(h, p, tables, Lc=Lc, has_ctx=has_ctx)
        elif kind == 1:
            assert has_ctx and not last
            p = dict(mu=rw_mu[j], w_rkv=bf(rw_w_rkv[j]), w_o=bf(rw_w_o[j]), w0=rw_w0[j], w1=rw_w1[j],
                     w2=rw_w2[j], a0=rw_a0[j], a1=rw_a1[j], a2=rw_a2[j], g1=rw_g1[j], g2=rw_g2[j],
                     k_k=rw_k_k[j], k_a=rw_k_a[j], r_k=rw_r_k[j], gn_w=rw_gn_w[j], gn_b=rw_gn_b[j])
            y = _rwkv(h, p, Lc=Lc)
        else:
            assert has_ctx and not last
            p = dict(w_qkv=bf(at_w_qkv[j]), b_qkv=at_b_qkv[j], w_o=bf(at_w_o[j]), b_o=at_b_o[j],
                     sink=at_sink[j])
            y = _attention(h, p, Lc=Lc)

        drop1 = ct if last else 0
        fj = i // 2
        moe = i % 2 == 1
        res = _ln(xs, y, mod, ln_w[i, 0], ln_b[i, 0], alpha=alpha, gate=2, h_mod=(3, 4),
                  h_dtype=F32 if moe else BF16, router=moe_router[fj] if moe else None,
                  ctx_tiles=ct, ctx_row=ctx_row, drop_tiles=drop1)
        xs, h2 = res[0], res[1]
        ct = ct - drop1
        T = xs.shape[1]
        M = B * T
        if moe:
            f = _moe(h2.reshape(M, D), res[2].reshape(M, LANES), bf(moe_w1[fj]), bf(moe_w3[fj]),
                     bf(moe_w2[fj]))
        else:
            f = _swiglu_ffn(h2.reshape(M, D), bf(ff_w1[fj]), bf(ff_w3[fj]), bf(ff_w2[fj]))
        f = f.reshape(B, T, D)
        if last:
            (xs,) = _ln(xs, f, mod, ln_w[i, 1], ln_b[i, 1], alpha=alpha, gate=5,
                        ctx_tiles=ct, ctx_row=ctx_row)
        else:
            nxt_last = i + 1 == depth - 1
            nkind = (i + 1) % 3
            drop2 = ct if (nxt_last and nkind == 0) else 0
            xs, h = _ln(xs, f, mod, ln_w[i, 1], ln_b[i, 1], alpha=alpha, gate=5, h_mod=(0, 1),
                        mod_h=mods[i + 1], h_dtype=F32 if nkind == 1 else BF16,
                        ctx_tiles=ct, ctx_row=ctx_row, drop_tiles=drop2)
    return xs
```

```python
import functools
import math

import jax
import jax.numpy as jnp
from jax import lax
from jax.experimental import pallas as pl
from jax.experimental.pallas import tpu as pltpu

F32 = jnp.float32
BF16 = jnp.bfloat16
HIGHEST = lax.Precision.HIGHEST

VMEM_LIMIT_BYTES = 56 * 1024 * 1024
LANES = 128
ROW_TILE = 256

LN_EPS = 1e-5
N_MOD = 6
HEAD = 64
RW_GN_EPS = 64e-5
RW_CHUNK = 64
RW_GROUP = 4
AT_KV_HEADS = 4
AT_WINDOW = 128
AT_BLOCK = 128
GRID_W = 64
ROPE_BASE = 10000.0
NEG_INF = -1e30
TOP_K = 2
HY_BANDS = 16
HY_EMB = 2 * HY_BANDS + 1
HY_MIN_DECAY = -math.log(1e-2) / 1.5
HY_MAX_DECAY = -math.log(1e-2) / 0.3


def _params(*sem):
    return pltpu.CompilerParams(dimension_semantics=sem, vmem_limit_bytes=VMEM_LIMIT_BYTES)


def _tile(n, pref, mult=LANES):
    if n <= pref:
        return n
    t = (pref // mult) * mult
    while t >= mult:
        if n % t == 0:
            return t
        t -= mult
    return n


def _mm_kernel(*refs, n_w, has_bias, act):
    x = refs[0][...].astype(BF16)
    o_ref = refs[-1]
    acc = jnp.dot(x, refs[1][...].astype(BF16), preferred_element_type=F32)
    if has_bias:
        acc = acc + refs[1 + n_w][...]
    if act == "swiglu":
        acc3 = jnp.dot(x, refs[2][...].astype(BF16), preferred_element_type=F32)
        acc = acc * jax.nn.sigmoid(acc) * acc3
    o_ref[...] = acc.astype(o_ref.dtype)


def _mm(x, w, *, w3=None, bias=None, out_dtype=F32, tm=1024, tn=512):
    M, K = x.shape
    N = w.shape[1]
    tm = _tile(M, tm, 8)
    tn = _tile(N, tn)
    ws = [w] if w3 is None else [w, w3]
    in_specs = [pl.BlockSpec((tm, K), lambda i, j: (i, 0))]
    in_specs += [pl.BlockSpec((K, tn), lambda i, j: (0, j)) for _ in ws]
    args = [x] + ws
    if bias is not None:
        in_specs.append(pl.BlockSpec((1, tn), lambda i, j: (0, j)))
        args.append(bias.reshape(1, N).astype(F32))
    return pl.pallas_call(
        functools.partial(_mm_kernel, n_w=len(ws), has_bias=bias is not None,
                          act="swiglu" if w3 is not None else None),
        grid=(M // tm, N // tn),
        in_specs=in_specs,
        out_specs=pl.BlockSpec((tm, tn), lambda i, j: (i, j)),
        out_shape=jax.ShapeDtypeStruct((M, N), out_dtype),
        compiler_params=_params("parallel", "parallel"),
        name="mm_swiglu" if w3 is not None else "mm",
    )(*args)


def _apply_act(v, kind):
    if kind == "sigmoid":
        return jax.nn.sigmoid(v)
    if kind == "tanh":
        return jnp.tanh(v)
    if kind == "logdecay":
        return -jnp.exp(-jax.nn.softplus(-v) - 0.5)
    return v


def _lora_kernel(x_ref, a_ref, b_ref, bias_ref, o_ref, *, mid, fin):
    t = jnp.dot(x_ref[...], a_ref[...], preferred_element_type=F32)
    t = _apply_act(t, mid).astype(BF16)
    y = jnp.dot(t, b_ref[...], preferred_element_type=F32) + bias_ref[...]
    o_ref[...] = _apply_act(y, fin).astype(o_ref.dtype)


def _lora(x, a, b, bias, *, mid, fin, tm=512):
    M, K = x.shape
    N = b.shape[1]
    R = -(-a.shape[1] // LANES) * LANES
    a = jnp.zeros((K, R), a.dtype).at[:, :a.shape[1]].set(a)
    b = jnp.zeros((R, N), b.dtype).at[:b.shape[0]].set(b)
    tm = _tile(M, tm, 8)
    return pl.pallas_call(
        functools.partial(_lora_kernel, mid=mid, fin=fin),
        grid=(M // tm,),
        in_specs=[pl.BlockSpec((tm, K), lambda i: (i, 0)),
                  pl.BlockSpec((K, R), lambda i: (0, 0)),
                  pl.BlockSpec((R, N), lambda i: (0, 0)),
                  pl.BlockSpec((1, N), lambda i: (0, 0))],
        out_specs=pl.BlockSpec((tm, N), lambda i: (i, 0)),
        out_shape=jax.ShapeDtypeStruct((M, N), F32),
        compiler_params=_params("parallel"),
        name="lora",
    )(x, a.astype(BF16), b.astype(BF16), bias.reshape(1, N).astype(F32))


def _ada_kernel(c_ref, w_ref, b_ref, o_ref):
    c = c_ref[...]
    s = (c * jax.nn.sigmoid(c)).astype(BF16)
    o_ref[0] = jnp.dot(s, w_ref[0].astype(BF16), preferred_element_type=F32) + b_ref[0]


def _ada(cond, ada_w, ada_b):
    depth, D, N = ada_w.shape
    R = cond.shape[0]
    tn = _tile(N, 1024)
    return pl.pallas_call(
        _ada_kernel,
        grid=(depth, N // tn),
        in_specs=[pl.BlockSpec((R, D), lambda i, j: (0, 0)),
                  pl.BlockSpec((1, D, tn), lambda i, j: (i, 0, j)),
                  pl.BlockSpec((1, 1, tn), lambda i, j: (i, 0, j))],
        out_specs=pl.BlockSpec((1, R, tn), lambda i, j: (i, 0, j)),
        out_shape=jax.ShapeDtypeStruct((depth, R, N), F32),
        compiler_params=_params("parallel", "parallel"),
        name="ada",
    )(cond, ada_w, ada_b.reshape(depth, 1, N))


def _mod_index(ctx_tiles, ctx_row, off):
    def index(b, t):
        return (jnp.where(t + off < ctx_tiles, ctx_row, b), 0, 0)
    return index


def _modulate_kernel(x_ref, mod_ref, h_ref, *, sh, sc):
    x = x_ref[0]
    h_ref[0] = (x * (1.0 + mod_ref[0, sc:sc + 1, :]) + mod_ref[0, sh:sh + 1, :]).astype(h_ref.dtype)


def _modulate(x, mod, *, sh, sc, ctx_tiles, ctx_row, out_dtype):
    B, T, D = x.shape
    return pl.pallas_call(
        functools.partial(_modulate_kernel, sh=sh, sc=sc),
        grid=(B, T // ROW_TILE),
        in_specs=[pl.BlockSpec((1, ROW_TILE, D), lambda b, t: (b, t, 0)),
                  pl.BlockSpec((1, N_MOD, D), _mod_index(ctx_tiles, ctx_row, 0))],
        out_specs=pl.BlockSpec((1, ROW_TILE, D), lambda b, t: (b, t, 0)),
        out_shape=jax.ShapeDtypeStruct((B, T, D), out_dtype),
        compiler_params=_params("parallel", "parallel"),
        name="modulate",
    )(x, mod)


def _ln_kernel(*refs, alpha, gate, sh, sc, has_h, has_router):
    x_ref, y_ref, mod_ref, modh_ref, w_ref, b_ref = refs[:6]
    pos = 6
    router_ref = None
    if has_router:
        router_ref = refs[pos]
        pos += 1
    xo_ref = refs[pos]
    pos += 1
    z = alpha * x_ref[0] + mod_ref[0, gate:gate + 1, :] * y_ref[0].astype(F32)
    mu = jnp.mean(z, axis=-1, keepdims=True)
    d = z - mu
    var = jnp.mean(d * d, axis=-1, keepdims=True)
    xn = d * lax.rsqrt(var + LN_EPS) * w_ref[...] + b_ref[...]
    xo_ref[0] = xn
    if has_h:
        h = xn * (1.0 + modh_ref[0, sc:sc + 1, :]) + modh_ref[0, sh:sh + 1, :]
        h_ref = refs[pos]
        pos += 1
        h_ref[0] = h.astype(h_ref.dtype)
        if has_router:
            refs[pos][0] = jnp.dot(h, router_ref[...], precision=HIGHEST, preferred_element_type=F32)


def _ln(x, y, mod, ln_w, ln_b, *, alpha, gate, h_mod=None, mod_h=None, h_dtype=BF16, router=None,
        ctx_tiles, ctx_row, drop_tiles=0):
    B, T, D = x.shape
    nt = T // ROW_TILE - drop_tiles
    To = nt * ROW_TILE
    off = drop_tiles
    in_specs = [pl.BlockSpec((1, ROW_TILE, D), lambda b, t: (b, t + off, 0)),
                pl.BlockSpec((1, ROW_TILE, D), lambda b, t: (b, t + off, 0)),
                pl.BlockSpec((1, N_MOD, D), _mod_index(ctx_tiles, ctx_row, off)),
                pl.BlockSpec((1, N_MOD, D), _mod_index(ctx_tiles, ctx_row, off)),
                pl.BlockSpec((1, D), lambda b, t: (0, 0)),
                pl.BlockSpec((1, D), lambda b, t: (0, 0))]
    args = [x, y, mod, mod if mod_h is None else mod_h, ln_w.reshape(1, D), ln_b.reshape(1, D)]
    out_specs = [pl.BlockSpec((1, ROW_TILE, D), lambda b, t: (b, t, 0))]
    out_shape = [jax.ShapeDtypeStruct((B, To, D), F32)]
    sh = sc = 0
    if h_mod is not None:
        sh, sc = h_mod
        out_specs.append(pl.BlockSpec((1, ROW_TILE, D), lambda b, t: (b, t, 0)))
        out_shape.append(jax.ShapeDtypeStruct((B, To, D), h_dtype))
    if router is not None:
        E = router.shape[1]
        router_p = jnp.zeros((D, LANES), F32).at[:, :E].set(router)
        in_specs.append(pl.BlockSpec((D, LANES), lambda b, t: (0, 0)))
        args.append(router_p)
        out_specs.append(pl.BlockSpec((1, ROW_TILE, LANES), lambda b, t: (b, t, 0)))
        out_shape.append(jax.ShapeDtypeStruct((B, To, LANES), F32))
    return pl.pallas_call(
        functools.partial(_ln_kernel, alpha=alpha, gate=gate, sh=sh, sc=sc,
                          has_h=h_mod is not None, has_router=router is not None),
        grid=(B, nt),
        in_specs=in_specs,
        out_specs=out_specs,
        out_shape=out_shape,
        compiler_params=_params("parallel", "parallel"),
        name="ln_residual",
    )(*args)


def _halo_specs(T, C, col):
    r8 = ROW_TILE // 8
    last8 = T // 8 - 1
    return [pl.BlockSpec((1, ROW_TILE, C), lambda b, t, j: (b, t, col(j))),
            pl.BlockSpec((1, 8, C), lambda b, t, j: (b, jnp.maximum(t * r8 - 1, 0), col(j))),
            pl.BlockSpec((1, 8, C), lambda b, t, j: (b, jnp.minimum(t * r8 + r8, last8), col(j)))]


def _neighbours(cur, prev8, next8, t, n_tiles, ctx_tiles):
    rows = lax.broadcasted_iota(jnp.int32, cur.shape, 0)
    has_prev = jnp.logical_and(t != 0, t != ctx_tiles)
    has_next = jnp.logical_and(t != n_tiles - 1, t != ctx_tiles - 1)
    top = jnp.where(has_prev, prev8[7:8, :], 0.0)
    bot = jnp.where(has_next, next8[0:1, :], 0.0)
    up = jnp.where(rows == 0, top, pltpu.roll(cur, 1, axis=0))
    dn = jnp.where(rows == cur.shape[0] - 1, bot, pltpu.roll(cur, cur.shape[0] - 1, axis=0))
    return up, dn


def _hy_gate_kernel(*refs, n_tiles, ctx_tiles):
    zs = refs[0:9]
    cw = refs[9:12]
    cb = refs[12:15]
    x0_ref, vv_ref, vvb_ref = refs[15:18]
    t = pl.program_id(1)
    out = []
    for s in range(3):
        cur = zs[3 * s][0]
        up, dn = _neighbours(cur, zs[3 * s + 1][0], zs[3 * s + 2][0], t, n_tiles, ctx_tiles)
        w = cw[s]
        out.append(up * w[0:1, :] + cur * w[1:2, :] + dn * w[2:3, :] + cb[s][...])
    x0_ref[0] = out[0]
    vv = out[1] * out[2]
    vv_ref[0] = vv
    vvb_ref[0] = vv.astype(BF16)


def _hy_gate(zp, conv_w, conv_b, *, ctx_tiles):
    B, T, D3 = zp.shape
    D = D3 // 3
    tc = _tile(D, 512)
    nj = D // tc
    n_tiles = T // ROW_TILE
    in_specs = []
    for s in range(3):
        in_specs += _halo_specs(T, tc, lambda j, s=s: s * nj + j)
    in_specs += [pl.BlockSpec((3, tc), lambda b, t, j, s=s: (0, s * nj + j)) for s in range(3)]
    in_specs += [pl.BlockSpec((1, tc), lambda b, t, j, s=s: (0, s * nj + j)) for s in range(3)]
    blk = pl.BlockSpec((1, ROW_TILE, tc), lambda b, t, j: (b, t, j))
    return pl.pallas_call(
        functools.partial(_hy_gate_kernel, n_tiles=n_tiles, ctx_tiles=ctx_tiles),
        grid=(B, n_tiles, nj),
        in_specs=in_specs,
        out_specs=[blk, blk, blk],
        out_shape=[jax.ShapeDtypeStruct((B, T, D), F32), jax.ShapeDtypeStruct((B, T, D), F32),
                   jax.ShapeDtypeStruct((B, T, D), BF16)],
        compiler_params=_params("parallel", "parallel", "parallel"),
        name="hy_gate",
    )(*([zp] * 9), conv_w, conv_w, conv_w, *([conv_b.reshape(1, D3)] * 3))


def _hy_filter_kernel(w1_ref, b1_ref, w2_ref, b2_ref, w3_ref, fr_ref, h_ref, s_ref, *, L, D, tl):
    i = pl.program_id(0)
    row = (lax.broadcasted_iota(jnp.int32, (tl, LANES), 0) + i * tl).astype(F32)
    lane = lax.broadcasted_iota(jnp.int32, (tl, LANES), 1)
    band = jnp.where(lane <= HY_BANDS, lane - 1, lane - 1 - HY_BANDS).astype(F32)
    freq = 1e-4 + band * ((HY_BANDS - 1 - 1e-4) / (HY_BANDS - 1))
    ang = freq * (row * (2.0 * math.pi / L))
    z = jnp.where(lane == 0, row / (L - 1),
                  jnp.where(lane <= HY_BANDS, jnp.cos(ang),
                            jnp.where(lane < HY_EMB, -jnp.sin(ang), 0.0)))
    h = jnp.sin(fr_ref[0:1, :] * (jnp.dot(z, w1_ref[...], precision=HIGHEST,
                                           preferred_element_type=F32) + b1_ref[...]))
    h = jnp.sin(fr_ref[1:2, :] * (jnp.dot(h, w2_ref[...], precision=HIGHEST,
                                           preferred_element_type=F32) + b2_ref[...]))
    h = jnp.dot(h, w3_ref[...], precision=HIGHEST, preferred_element_type=F32)
    half = L // 2
    rowd = (lax.broadcasted_iota(jnp.int32, (tl, D), 0) + i * tl).astype(F32)
    dist = jnp.abs(rowd - half) / half
    chan = lax.broadcasted_iota(jnp.int32, (tl, D), 1).astype(F32)
    deltas = HY_MIN_DECAY + chan * ((HY_MAX_DECAY - HY_MIN_DECAY) / (D - 1))
    h = h * jnp.exp(-dist * deltas)
    h_ref[...] = h

    @pl.when(i == 0)
    def _():
        s_ref[...] = jnp.zeros_like(s_ref)

    s_ref[...] += jnp.sum(jnp.abs(h), axis=0, keepdims=True)


def _hy_filter(L, f_w1, f_b1, f_w2, f_b2, f_w3, f_freq):
    D = f_w3.shape[1]
    hid = f_w1.shape[1]
    w1 = jnp.zeros((LANES, LANES), F32).at[:HY_EMB, :hid].set(f_w1)
    b1 = jnp.zeros((1, LANES), F32).at[0, :hid].set(f_b1)
    w2 = jnp.zeros((LANES, LANES), F32).at[:hid, :hid].set(f_w2)
    b2 = jnp.zeros((1, LANES), F32).at[0, :hid].set(f_b2)
    w3 = jnp.zeros((LANES, D), F32).at[:hid].set(f_w3)
    fr = jnp.zeros((2, LANES), F32).at[:, :hid].set(f_freq)
    tl = _tile(L, 256, 8)
    full = lambda shape: pl.BlockSpec(shape, lambda i: (0, 0))
    return pl.pallas_call(
        functools.partial(_hy_filter_kernel, L=L, D=D, tl=tl),
        grid=(L // tl,),
        in_specs=[full((LANES, LANES)), full((1, LANES)), full((LANES, LANES)), full((1, LANES)),
                  full((LANES, D)), full((2, LANES))],
        out_specs=[pl.BlockSpec((tl, D), lambda i: (i, 0)), full((1, D))],
        out_shape=[jax.ShapeDtypeStruct((L, D), F32), jax.ShapeDtypeStruct((1, D), F32)],
        compiler_params=_params("arbitrary"),
        name="hy_filter",
    )(w1, b1, w2, b2, w3, fr)


def _dft_tables(L):
    n = 2 * L
    k = jnp.arange(L, dtype=jnp.int32)
    w = 2.0 * math.pi / n
    ang = ((k[:, None] * k[None, :]) % n).astype(F32) * w
    sign_t = jnp.where(k % 2 == 0, 1.0, -1.0).astype(F32)
    fre = jnp.cos(ang)
    fim = jnp.where(k[:, None] == 0, sign_t[None, :], -jnp.sin(ang))
    fwd = jnp.stack([fre, fim]).astype(BF16)
    m = k + L // 2
    angi = ((m[:, None] * k[None, :]) % n).astype(F32) * w
    sign_m = jnp.where(m % 2 == 0, 1.0, -1.0).astype(F32)
    ire = jnp.where(k[None, :] == 0, 1.0 / n, (2.0 / n) * jnp.cos(angi))
    iim = jnp.where(k[None, :] == 0, sign_m[:, None] / n, (-2.0 / n) * jnp.sin(angi))
    inv = jnp.concatenate([ire, iim], axis=1).astype(BF16)
    return fwd, inv


def _dft_fwd_kernel(a_ref, w_ref, *rest, mode):
    o_ref = rest[-1]
    w = w_ref[0]
    vre = jnp.dot(a_ref[0], w, preferred_element_type=F32)
    vim = jnp.dot(a_ref[1], w, preferred_element_type=F32)
    if mode == "scale":
        inv = 1.0 / (rest[0][...] + 1e-6)
        o_ref[0, 0] = vre * inv
        o_ref[0, 1] = vim * inv
    else:
        hre = rest[0][0]
        him = rest[0][1]
        first = jnp.logical_and(pl.program_id(2) == 0,
                                lax.broadcasted_iota(jnp.int32, vre.shape, 0) == 0)
        zre = jnp.where(first, vre * hre, vre * hre - vim * him)
        zim = jnp.where(first, vim * him, vre * him + vim * hre)
        o_ref[0, 0] = zre.astype(o_ref.dtype)
        o_ref[0, 1] = zim.astype(o_ref.dtype)


def _dft_fwd(fwd, w, extra, *, mode, out_dtype):
    B, L, D = w.shape
    tm = _tile(L, 256, 8)
    tn = _tile(D, 1024)
    if mode == "scale":
        extra_spec = pl.BlockSpec((1, tn), lambda b, j, m: (0, j))
    else:
        extra_spec = pl.BlockSpec((2, tm, tn), lambda b, j, m: (0, m, j))
    return pl.pallas_call(
        functools.partial(_dft_fwd_kernel, mode=mode),
        grid=(B, D // tn, L // tm),
        in_specs=[pl.BlockSpec((2, tm, L), lambda b, j, m: (0, m, 0)),
                  pl.BlockSpec((1, L, tn), lambda b, j, m: (b, 0, j)),
                  extra_spec],
        out_specs=pl.BlockSpec((1, 2, tm, tn), lambda b, j, m: (b, 0, m, j)),
        out_shape=jax.ShapeDtypeStruct((B, 2, L, D), out_dtype),
        compiler_params=_params("parallel", "parallel", "parallel"),
        name="dft_fwd",
    )(fwd, w, extra)


def _dft_inv_kernel(b_ref, z_ref, x0_ref, vv_ref, skip_ref, u_ref):
    y = jnp.dot(b_ref[...], z_ref[0], preferred_element_type=F32)
    u_ref[0] = (x0_ref[0] * (y + vv_ref[0] * skip_ref[...])).astype(u_ref.dtype)


def _dft_inv(inv, z, x0, vv, skip, *, row_off):
    B, n, D = z.shape
    L = n // 2
    tm = _tile(L, ROW_TILE, 8)
    tn = _tile(D, 1024)
    off = row_off // tm
    return pl.pallas_call(
        _dft_inv_kernel,
        grid=(B, D // tn, L // tm),
        in_specs=[pl.BlockSpec((tm, n), lambda b, j, m: (m, 0)),
                  pl.BlockSpec((1, n, tn), lambda b, j, m: (b, 0, j)),
                  pl.BlockSpec((1, tm, tn), lambda b, j, m: (b, m + off, j)),
                  pl.BlockSpec((1, tm, tn), lambda b, j, m: (b, m + off, j)),
                  pl.BlockSpec((1, tn), lambda b, j, m: (0, j))],
        out_specs=pl.BlockSpec((1, tm, tn), lambda b, j, m: (b, m, j)),
        out_shape=jax.ShapeDtypeStruct((B, L, D), BF16),
        compiler_params=_params("parallel", "parallel", "parallel"),
        name="dft_inv",
    )(inv, z, x0, vv, skip.reshape(1, D))


def _hyena(h, p, tables, *, Lc, has_ctx):
    B, T, D = h.shape
    M = B * T
    zp = _mm(h.reshape(M, D), p["w_in"], bias=p["b_in"]).reshape(B, T, 3 * D)
    ctx_tiles = Lc // ROW_TILE if has_ctx else 0
    x0, vv, vvb = _hy_gate(zp, p["conv_w"], p["conv_b"], ctx_tiles=ctx_tiles)
    segs = [(Lc, T - Lc)] if has_ctx else [(0, T)]
    if has_ctx:
        segs = [(0, Lc)] + segs
    us = []
    for start, L in segs:
        fwd, inv = tables[L]
        filt, asum = _hy_filter(L, *p["filter"])
        hf = _dft_fwd(fwd, filt.astype(BF16)[None], asum, mode="scale", out_dtype=F32)[0]
        seg = vvb if (start == 0 and L == T) else lax.slice_in_dim(vvb, start, start + L, axis=1)
        z = _dft_fwd(fwd, seg, hf, mode="mul", out_dtype=BF16).reshape(B, 2 * L, D)
        us.append(_dft_inv(inv, z, x0, vv, p["skip"], row_off=start))
    u = us[0] if len(us) == 1 else jnp.concatenate(us, axis=1)
    return _mm(u.reshape(M, D), p["w_out"], bias=p["b_out"]).reshape(B, T, D)


def _rw_prep_kernel(cur_ref, prev_ref, next_ref, mu_ref, *o_refs, n_tiles, ctx_tiles):
    t = pl.program_id(1)
    cur = cur_ref[0]
    up, dn = _neighbours(cur, prev_ref[0], next_ref[0], t, n_tiles, ctx_tiles)
    dx = 0.5 * (up + dn) - cur
    for j, o_ref in enumerate(o_refs):
        o_ref[0] = (cur + dx * mu_ref[j:j + 1, :]).astype(o_ref.dtype)


def _rw_prep(hs, mu, *, ctx_tiles):
    B, T, D = hs.shape
    tc = _tile(D, 1024)
    n_tiles = T // ROW_TILE
    blk = pl.BlockSpec((1, ROW_TILE, tc), lambda b, t, j: (b, t, j))
    return pl.pallas_call(
        functools.partial(_rw_prep_kernel, n_tiles=n_tiles, ctx_tiles=ctx_tiles),
        grid=(B, n_tiles, D // tc),
        in_specs=_halo_specs(T, tc, lambda j: j) + [pl.BlockSpec((6, tc), lambda b, t, j: (0, j))],
        out_specs=[blk] * 6,
        out_shape=[jax.ShapeDtypeStruct((B, T, D), BF16)] * 6,
        compiler_params=_params("parallel", "parallel", "parallel"),
        name="rw_prep",
    )(hs, hs, hs, mu)


def _block_diag(x, mask):
    return jnp.where(mask, jnp.concatenate([x] * RW_GROUP, axis=0), 0.0).astype(BF16)


def _rw_chunks(r, k, v, lw, a, k_k, k_a, state, same_head, incl, strict, reverse):
    C = RW_CHUNK
    W = RW_GROUP * HEAD
    G = range(len(r))
    nt = (((1,), (1,)), ((), ()))
    ones = jnp.where(same_head, 1.0, 0.0).astype(BF16)
    tri = jnp.where(incl[:, :C], 1.0, 0.0).astype(BF16)
    bd = lambda t: _block_diag(t, same_head)
    mm = lambda x, y: jnp.dot(x.astype(BF16), y, preferred_element_type=F32)

    def split(t):
        hi = t.astype(BF16)
        return hi, (t - hi.astype(F32)).astype(BF16)

    kkr = [k[g] * k_k[g] for g in G]
    sq = [split(kkr[g] * kkr[g]) for g in G]
    ss = [jnp.dot(sq[g][0], ones, preferred_element_type=F32)
          + jnp.dot(sq[g][1], ones, preferred_element_type=F32) for g in G]
    lws = [split(lw[g]) for g in G]
    lp = [jnp.dot(tri, lws[g][0], preferred_element_type=F32)
          + jnp.dot(tri, lws[g][1], preferred_element_type=F32) for g in G]
    kk = [kkr[g] / jnp.maximum(jnp.sqrt(ss[g]), 1e-12) for g in G]
    kd = [k[g] * (1.0 + (a[g] - 1.0) * k_a[g]) for g in G]
    lp_end = [lp[g][0:1, :] if reverse else lp[g][C - 1:C, :] for g in G]
    e_neg = [jnp.exp(-lp[g]) for g in G]
    lhs = [jnp.concatenate([-kk[g] * jnp.exp(lp[g] - lw[g]), r[g] * jnp.exp(lp[g])], axis=0).astype(BF16)
           for g in G]
    rhs = [jnp.concatenate([bd(kk[g] * a[g] * e_neg[g]), bd(kd[g] * e_neg[g])], axis=0) for g in G]
    cross = [lax.dot_general(lhs[g], rhs[g], nt, preferred_element_type=F32) for g in G]
    from_state = [lax.dot_general(lhs[g], state[g].astype(BF16), nt, preferred_element_type=F32)
                  for g in G]
    v_bd = [bd(v[g]) for g in G]
    p = [jnp.where(strict, cross[g][:C, :W], 0.0) for g in G]
    x = [from_state[g][:C] + mm(jnp.where(strict, cross[g][:C, W:], 0.0), v_bd[g]) for g in G]
    n = 1
    while n < C:
        x = [x[g] + mm(p[g], bd(x[g])) for g in G]
        n *= 2
        if n < C:
            p = [mm(p[g], bd(p[g])) for g in G]
    y = [from_state[g][C:] + mm(jnp.where(incl, cross[g][C:, :W], 0.0), bd(x[g]))
         + mm(jnp.where(incl, cross[g][C:, W:], 0.0), v_bd[g]) for g in G]

    tail = [jnp.exp(lp_end[g] - lp[g]) for g in G]
    uv = [jnp.concatenate([x[g], v[g]], axis=0).astype(BF16) for g in G]
    bk = [jnp.concatenate([kk[g] * a[g] * tail[g], kd[g] * tail[g]], axis=0).astype(BF16) for g in G]
    upd = [lax.dot_general(uv[g], bk[g], (((0,), (0,)), ((), ())), preferred_element_type=F32)
           for g in G]
    new_state = [state[g] * jnp.exp(lp_end[g]) + jnp.where(same_head, upd[g], 0.0) for g in G]
    return y, new_state


def _rw_scan_kernel(r_ref, k_ref, v_ref, lw_ref, a_ref, kk_ref, ka_ref, o_ref, state_ref, *, reverse):
    C = RW_CHUNK
    W = RW_GROUP * HEAD

    @pl.when(pl.program_id(2) == 0)
    def _():
        state_ref[...] = jnp.zeros_like(state_ref)

    ri = lax.broadcasted_iota(jnp.int32, (W, W), 0)
    ci = lax.broadcasted_iota(jnp.int32, (W, W), 1)
    same_head = (ri // HEAD) == (ci // HEAD)
    t_i = lax.broadcasted_iota(jnp.int32, (C, W), 0)
    s_i = lax.broadcasted_iota(jnp.int32, (C, W), 1) % C
    incl = (s_i >= t_i) if reverse else (s_i <= t_i)
    strict = (s_i > t_i) if reverse else (s_i < t_i)

    ng = state_ref.shape[0]
    sl = [slice(g * W, (g + 1) * W) for g in range(ng)]
    load = lambda ref: [ref[0, :, s] for s in sl]
    y, new_state = _rw_chunks(load(r_ref), load(k_ref), load(v_ref), load(lw_ref), load(a_ref),
                              [kk_ref[:, s] for s in sl], [ka_ref[:, s] for s in sl],
                              [state_ref[g] for g in range(ng)], same_head, incl, strict, reverse)
    for g in range(ng):
        o_ref[0, :, sl[g]] = y[g]
        state_ref[g] = new_state[g]


RW_GROUPS_PER_STEP = 8


def _rw_scan(r, k, v, lw, a, k_k, k_a, *, Lc, reverse):
    B, T, D = r.shape
    C = RW_CHUNK
    assert C == HEAD
    W = RW_GROUP * HEAD
    ng = math.gcd(D // W, RW_GROUPS_PER_STEP)
    nc = T // C
    ncc = Lc // C

    def chunk(c):
        if not reverse:
            return c
        return jnp.where(c < ncc, ncc - 1 - c, nc - 1 - (c - ncc))

    blk = pl.BlockSpec((1, C, ng * W), lambda b, g, c: (b, chunk(c), g))
    vec = pl.BlockSpec((1, ng * W), lambda b, g, c: (0, g))
    return pl.pallas_call(
        functools.partial(_rw_scan_kernel, reverse=reverse),
        grid=(B, D // (ng * W), nc),
        in_specs=[blk] * 5 + [vec, vec],
        out_specs=blk,
        out_shape=jax.ShapeDtypeStruct((B, T, D), F32),
        scratch_shapes=[pltpu.VMEM((ng, W, W), F32)],
        compiler_params=_params("parallel", "parallel", "arbitrary"),
        name="rw_scan",
    )(r, k, v, lw, a, k_k.reshape(1, D), k_a.reshape(1, D))


def _head_sums(x, ones_bd):
    W = ones_bd.shape[0]
    hi = x.astype(BF16)
    lo = (x - hi.astype(F32)).astype(BF16)
    cols = []
    for g in range(x.shape[1] // W):
        sl = slice(g * W, (g + 1) * W)
        cols.append(jnp.dot(hi[:, sl], ones_bd, preferred_element_type=F32)
                    + jnp.dot(lo[:, sl], ones_bd, preferred_element_type=F32))
    return jnp.concatenate(cols, axis=1)


def _rw_post_kernel(of_ref, ob_ref, r_ref, k_ref, v_ref, g_ref, rk_ref, gw_ref, gb_ref, o_ref):
    W = RW_GROUP * HEAD
    ri = lax.broadcasted_iota(jnp.int32, (W, W), 0)
    ci = lax.broadcasted_iota(jnp.int32, (W, W), 1)
    ones_bd = jnp.where((ri // HEAD) == (ci // HEAD), 1.0, 0.0).astype(BF16)
    o = of_ref[0] + ob_ref[0]
    m = _head_sums(o, ones_bd) * (1.0 / HEAD)
    d = o - m
    var = _head_sums(d * d, ones_bd) * (1.0 / HEAD)
    on = d * lax.rsqrt(var + RW_GN_EPS) * gw_ref[...] + gb_ref[...]
    bonus = _head_sums(r_ref[0] * k_ref[0] * rk_ref[...], ones_bd) * v_ref[0]
    o_ref[0] = ((on + bonus) * g_ref[0]).astype(o_ref.dtype)


def _rw_post(o_f, o_b, r, k, v, g, r_k, gn_w, gn_b):
    B, T, D = r.shape
    tt = _tile(T, 128, 8)
    blk = pl.BlockSpec((1, tt, D), lambda b, t: (b, t, 0))
    par = pl.BlockSpec((1, D), lambda b, t: (0, 0))
    return pl.pallas_call(
        _rw_post_kernel,
        grid=(B, T // tt),
        in_specs=[blk] * 6 + [par] * 3,
        out_specs=blk,
        out_shape=jax.ShapeDtypeStruct((B, T, D), BF16),
        compiler_params=_params("parallel", "parallel"),
        name="rw_post",
    )(o_f, o_b, r, k, v, g, r_k.reshape(1, D), gn_w.reshape(1, D), gn_b.reshape(1, D))


def _rwkv(hs, p, *, Lc):
    B, T, D = hs.shape
    M = B * T
    ctx_tiles = Lc // ROW_TILE
    xr, xw, xk, xv, xa, xg = [t.reshape(M, D) for t in _rw_prep(hs, p["mu"], ctx_tiles=ctx_tiles)]
    w_rkv = p["w_rkv"]
    r = _mm(xr, w_rkv[0]).reshape(B, T, D)
    k = _mm(xk, w_rkv[1]).reshape(B, T, D)
    v = _mm(xv, w_rkv[2]).reshape(B, T, D)
    g = _lora(xg, p["g1"], p["g2"], jnp.zeros((D,), F32), mid="sigmoid", fin=None).reshape(B, T, D)
    outs = []
    for d in range(2):
        lw = _lora(xw, p["w1"][d], p["w2"][d], p["w0"][d], mid="tanh", fin="logdecay").reshape(B, T, D)
        a = _lora(xa, p["a1"][d], p["a2"][d], p["a0"][d], mid=None, fin="sigmoid").reshape(B, T, D)
        outs.append(_rw_scan(r, k, v, lw, a, p["k_k"], p["k_a"], Lc=Lc, reverse=d == 1))
    y = _rw_post(outs[0], outs[1], r, k, v, g, p["r_k"], p["gn_w"], p["gn_b"])
    return _mm(y.reshape(M, D), p["w_o"]).reshape(B, T, D)


def _rope_tables(T, Lc):
    quarter = HEAD // 4
    pos = jnp.arange(T - Lc, dtype=jnp.int32)
    rows = (pos // GRID_W).astype(F32)
    cols = (pos % GRID_W).astype(F32)
    inv = ROPE_BASE ** (-jnp.arange(quarter, dtype=F32) / quarter)
    ang = jnp.concatenate([rows[:, None] * inv, rows[:, None] * inv,
                           cols[:, None] * inv, cols[:, None] * inv], axis=1)
    ang = jnp.concatenate([jnp.zeros((Lc, HEAD), F32), ang], axis=0)
    ang = jnp.concatenate([ang, ang], axis=1)
    return jnp.cos(ang), jnp.sin(ang)


def _rope_kernel(x_ref, cos_ref, sin_ref, q_ref, k_ref, v_ref, *, n_q, n_kv, scale):
    x = x_ref[0]
    n_rot = (n_q + n_kv) * HEAD
    xr = x[:, :n_rot]
    reps = n_rot // LANES
    cos = jnp.concatenate([cos_ref[...]] * reps, axis=1)
    sin = jnp.concatenate([sin_ref[...]] * reps, axis=1)
    quarter = HEAD // 4
    lane = lax.broadcasted_iota(jnp.int32, xr.shape, 1)
    first = (lane % (2 * quarter)) < quarter
    rot = jnp.where(first, -pltpu.roll(xr, n_rot - quarter, axis=1), pltpu.roll(xr, quarter, axis=1))
    y = xr * cos + rot * sin
    for h in range(n_q):
        q_ref[0, h] = (y[:, h * HEAD:(h + 1) * HEAD] * scale).astype(q_ref.dtype)
    for h in range(n_kv):
        lo = (n_q + h) * HEAD
        k_ref[0, h] = y[:, lo:lo + HEAD].astype(k_ref.dtype)
        lo = (n_q + n_kv + h) * HEAD
        v_ref[0, h] = x[:, lo:lo + HEAD].astype(v_ref.dtype)


def _rope(qkv, cos, sin, *, n_q, n_kv):
    B, T, W = qkv.shape
    tt = _tile(T, 128, 8)
    out = lambda n: pl.BlockSpec((1, n, tt, HEAD), lambda b, t: (b, 0, t, 0))
    return pl.pallas_call(
        functools.partial(_rope_kernel, n_q=n_q, n_kv=n_kv, scale=HEAD ** -0.5),
        grid=(B, T // tt),
        in_specs=[pl.BlockSpec((1, tt, W), lambda b, t: (b, t, 0)),
                  pl.BlockSpec((tt, LANES), lambda b, t: (t, 0)),
                  pl.BlockSpec((tt, LANES), lambda b, t: (t, 0))],
        out_specs=[out(n_q), out(n_kv), out(n_kv)],
        out_shape=[jax.ShapeDtypeStruct((B, n_q, T, HEAD), BF16),
                   jax.ShapeDtypeStruct((B, n_kv, T, HEAD), BF16),
                   jax.ShapeDtypeStruct((B, n_kv, T, HEAD), BF16)],
        compiler_params=_params("parallel", "parallel"),
        name="rope",
    )(qkv, cos, sin)


def _attn_kernel(*refs, n_q, n_kv, local, q_off, n_blocks):
    if local:
        q_ref, kp_ref, kc_ref, kn_ref, vp_ref, vc_ref, vn_ref, kx_ref, vx_ref, sink_ref, o_ref = refs
    else:
        q_ref, kx_ref, vx_ref, sink_ref, o_ref = refs
    G = n_q // n_kv
    R = G * AT_BLOCK
    n = pl.program_id(1)
    nt = (((1,), (1,)), ((), ()))
    if local:
        qi = lax.broadcasted_iota(jnp.int32, (R, 3 * AT_BLOCK), 0) % AT_BLOCK
        kj = lax.broadcasted_iota(jnp.int32, (R, 3 * AT_BLOCK), 1) - AT_BLOCK
        kpos = kj + n * AT_BLOCK
        ok = (jnp.abs(qi - kj) <= AT_WINDOW) & (kpos >= 0) & (kpos < n_blocks * AT_BLOCK)
    H = range(n_kv)
    q = [q_ref[0, h * G:(h + 1) * G].reshape(R, HEAD) for h in H]
    sink = [jnp.concatenate(
        [jnp.broadcast_to(sink_ref[h * G + g:h * G + g + 1, 0:1], (AT_BLOCK, 1)) for g in range(G)],
        axis=0) for h in H]
    s_ctx = [lax.dot_general(q[h], kx_ref[0, h], nt, preferred_element_type=F32) for h in H]
    m = [jnp.maximum(jnp.max(s_ctx[h], axis=-1, keepdims=True), sink[h]) for h in H]
    if local:
        k_loc = [jnp.concatenate([kp_ref[0, h], kc_ref[0, h], kn_ref[0, h]], axis=0) for h in H]
        v_loc = [jnp.concatenate([vp_ref[0, h], vc_ref[0, h], vn_ref[0, h]], axis=0) for h in H]
        s_loc = [jnp.where(ok, lax.dot_general(q[h], k_loc[h], nt, preferred_element_type=F32), NEG_INF)
                 for h in H]
        m = [jnp.maximum(m[h], jnp.max(s_loc[h], axis=-1, keepdims=True)) for h in H]
    p_ctx = [jnp.exp(s_ctx[h] - m[h]) for h in H]
    den = [jnp.sum(p_ctx[h], axis=-1, keepdims=True) + jnp.exp(sink[h] - m[h]) for h in H]
    o = [jnp.dot(p_ctx[h].astype(BF16), vx_ref[0, h], preferred_element_type=F32) for h in H]
    if local:
        p_loc = [jnp.exp(s_loc[h] - m[h]) for h in H]
        den = [den[h] + jnp.sum(p_loc[h], axis=-1, keepdims=True) for h in H]
        o = [o[h] + jnp.dot(p_loc[h].astype(BF16), v_loc[h], preferred_element_type=F32) for h in H]
    o = [o[h] * (1.0 / den[h]) for h in H]
    for h in H:
        for g in range(G):
            hh = h * G + g
            o_ref[0, :, hh * HEAD:(hh + 1) * HEAD] = o[h][g * AT_BLOCK:(g + 1) * AT_BLOCK].astype(o_ref.dtype)


def _attn(q, k, v, sink, *, Lc, local):
    B, n_q, T, _ = q.shape
    n_kv = k.shape[1]
    cb = Lc // AT_BLOCK
    nb = (T - Lc) // AT_BLOCK if local else cb
    q_off = cb if local else 0
    last = T // AT_BLOCK - 1
    qspec = pl.BlockSpec((1, n_q, AT_BLOCK, HEAD), lambda b, n: (b, 0, n + q_off, 0))
    xspec = pl.BlockSpec((1, n_kv, Lc, HEAD), lambda b, n: (b, 0, 0, 0))
    sspec = pl.BlockSpec((n_q, LANES), lambda b, n: (0, 0))
    sink_b = jnp.broadcast_to(sink.astype(F32)[:, None], (n_q, LANES))
    if local:
        blk = lambda f: pl.BlockSpec((1, n_kv, AT_BLOCK, HEAD), lambda b, n: (b, 0, f(n), 0))
        prev = blk(lambda n: jnp.maximum(n + cb - 1, cb))
        cur = blk(lambda n: n + cb)
        nxt = blk(lambda n: jnp.minimum(n + cb + 1, last))
        in_specs = [qspec, prev, cur, nxt, prev, cur, nxt, xspec, xspec, sspec]
        args = (q, k, k, k, v, v, v, k, v, sink_b)
    else:
        in_specs = [qspec, xspec, xspec, sspec]
        args = (q, k, v, sink_b)
    return pl.pallas_call(
        functools.partial(_attn_kernel, n_q=n_q, n_kv=n_kv, local=local, q_off=q_off, n_blocks=nb),
        grid=(B, nb),
        in_specs=in_specs,
        out_specs=pl.BlockSpec((1, AT_BLOCK, n_q * HEAD), lambda b, n: (b, n, 0)),
        out_shape=jax.ShapeDtypeStruct((B, nb * AT_BLOCK, n_q * HEAD), BF16),
        compiler_params=_params("parallel", "parallel"),
        name="attn_local" if local else "attn_ctx",
    )(*args)


def _attention(h, p, *, Lc):
    B, T, D = h.shape
    M = B * T
    n_q = D // HEAD
    qkv = _mm(h.reshape(M, D), p["w_qkv"], bias=p["b_qkv"]).reshape(B, T, -1)
    cos, sin = _rope_tables(T, Lc)
    q, k, v = _rope(qkv, cos, sin, n_q=n_q, n_kv=AT_KV_HEADS)
    o_ctx = _attn(q, k, v, p["sink"], Lc=Lc, local=False)
    o_lat = _attn(q, k, v, p["sink"], Lc=Lc, local=True)
    o = jnp.concatenate([o_ctx, o_lat], axis=1)
    return _mm(o.reshape(M, D), p["w_o"], bias=p["b_o"]).reshape(B, T, D)


def _swiglu_ffn(h, w1, w3, w2):
    mid = _mm(h, w1, w3=w3, out_dtype=BF16)
    return _mm(mid, w2)


MOE_TILE = 1024
ROUTE_G1, ROUTE_G2, ROUTE_I1, ROUTE_I2 = 0, 1, 2, 3


def _gates_kernel(l_ref, route_ref, sel_ref, *, n_experts):
    l = l_ref[...]
    lane = lax.broadcasted_iota(jnp.int32, l.shape, 1)
    l = jnp.where(lane < n_experts, l, -jnp.inf)
    m1 = jnp.max(l, axis=-1, keepdims=True)
    i1 = jnp.min(jnp.where(l == m1, lane, LANES), axis=-1, keepdims=True)
    l2 = jnp.where(lane == i1, -jnp.inf, l)
    m2 = jnp.max(l2, axis=-1, keepdims=True)
    i2 = jnp.min(jnp.where(l2 == m2, lane, LANES), axis=-1, keepdims=True)
    e2 = jnp.exp(m2 - m1)
    den = 1.0 + e2
    route_ref[...] = jnp.where(lane == ROUTE_G1, 1.0 / den,
                               jnp.where(lane == ROUTE_G2, e2 / den,
                                         jnp.where(lane == ROUTE_I1, i1.astype(F32),
                                                   jnp.where(lane == ROUTE_I2, i2.astype(F32), 0.0))))
    sel_ref[...] = jnp.where((lane == i1) | (lane == i2), 1.0, 0.0).astype(sel_ref.dtype)


def _gates(logits, n_experts):
    M = logits.shape[0]
    tm = _tile(M, 1024, 8)
    blk = pl.BlockSpec((tm, LANES), lambda i: (i, 0))
    return pl.pallas_call(
        functools.partial(_gates_kernel, n_experts=n_experts),
        grid=(M // tm,),
        in_specs=[blk],
        out_specs=[blk, blk],
        out_shape=[jax.ShapeDtypeStruct((M, LANES), F32), jax.ShapeDtypeStruct((M, LANES), BF16)],
        compiler_params=_params("parallel"),
        name="moe_gates",
    )(logits)


def _rank_kernel(sel_ref, rank_ref, cnt_ref, carry_ref):
    @pl.when(pl.program_id(0) == 0)
    def _():
        carry_ref[...] = jnp.zeros_like(carry_ref)

    s = sel_ref[...]
    n = s.shape[0]
    earlier = (lax.broadcasted_iota(jnp.int32, (n, n), 1) < lax.broadcasted_iota(jnp.int32, (n, n), 0))
    within = jnp.dot(jnp.where(earlier, 1.0, 0.0).astype(BF16), s, preferred_element_type=F32)
    rank_ref[...] = within + carry_ref[...]
    carry_ref[...] += jnp.sum(s.astype(F32), axis=0, keepdims=True)
    cnt_ref[...] = carry_ref[...]


def _rank(sel):
    M = sel.shape[0]
    tr = _tile(M, 512, 8)
    return pl.pallas_call(
        _rank_kernel,
        grid=(M // tr,),
        in_specs=[pl.BlockSpec((tr, LANES), lambda i: (i, 0))],
        out_specs=[pl.BlockSpec((tr, LANES), lambda i: (i, 0)), pl.BlockSpec((1, LANES), lambda i: (0, 0))],
        out_shape=[jax.ShapeDtypeStruct((M, LANES), F32), jax.ShapeDtypeStruct((1, LANES), F32)],
        scratch_shapes=[pltpu.VMEM((1, LANES), F32)],
        compiler_params=_params("arbitrary"),
        name="moe_rank",
    )(sel)


def _pos_kernel(route_ref, rank_ref, offs_ref, pos_ref):
    lane = lax.broadcasted_iota(jnp.int32, rank_ref.shape, 1)
    lane_f = lane.astype(F32)
    tot = rank_ref[...] + offs_ref[...]
    route = route_ref[...]
    p1 = jnp.sum(jnp.where(lane_f == route[:, ROUTE_I1:ROUTE_I1 + 1], tot, 0.0), axis=-1, keepdims=True)
    p2 = jnp.sum(jnp.where(lane_f == route[:, ROUTE_I2:ROUTE_I2 + 1], tot, 0.0), axis=-1, keepdims=True)
    pos_ref[...] = jnp.where(lane == 0, p1, jnp.where(lane == 1, p2, 0.0)).astype(jnp.int32)


def _positions(route, rank, offs):
    M = route.shape[0]
    tm = _tile(M, 1024, 8)
    blk = pl.BlockSpec((tm, LANES), lambda i: (i, 0))
    return pl.pallas_call(
        _pos_kernel,
        grid=(M // tm,),
        in_specs=[blk, blk, pl.BlockSpec((1, LANES), lambda i: (0, 0))],
        out_specs=blk,
        out_shape=jax.ShapeDtypeStruct((M, LANES), jnp.int32),
        compiler_params=_params("parallel"),
        name="moe_pos",
    )(route, rank, offs)


def _row_copy(src, dst, sem):
    return pltpu.make_async_copy(src, dst, sem)


def _dispatch_kernel(p1_ref, p2_ref, h_ref, xs_in_ref, xs_ref, sem):
    del xs_in_ref
    tt = h_ref.shape[0]
    base = pl.program_id(0) * tt

    def start(r, carry):
        row = h_ref.at[pl.ds(r, 1)]
        _row_copy(row, xs_ref.at[pl.ds(p1_ref[base + r], 1)], sem).start()
        _row_copy(row, xs_ref.at[pl.ds(p2_ref[base + r], 1)], sem).start()
        return carry

    lax.fori_loop(0, tt, start, 0, unroll=8)
    for _ in range(2):
        _row_copy(h_ref, xs_ref.at[pl.ds(0, tt)], sem).wait()


def _dispatch(h, p1, p2, n_rows):
    M, D = h.shape
    tt = _tile(M, ROW_TILE, 8)
    return pl.pallas_call(
        _dispatch_kernel,
        grid_spec=pltpu.PrefetchScalarGridSpec(
            num_scalar_prefetch=2,
            grid=(M // tt,),
            in_specs=[pl.BlockSpec((tt, D), lambda i, p1, p2: (i, 0)),
                      pl.BlockSpec(memory_space=pl.ANY)],
            out_specs=pl.BlockSpec(memory_space=pl.ANY),
            scratch_shapes=[pltpu.SemaphoreType.DMA(())]),
        out_shape=jax.ShapeDtypeStruct((n_rows, D), h.dtype),
        input_output_aliases={3: 0},
        compiler_params=_params("arbitrary"),
        name="moe_dispatch",
    )(p1, p2, h, jnp.zeros((n_rows, D), h.dtype))


def _gmm_kernel(te_ref, nv_ref, x_ref, *refs, swiglu):
    o_ref = refs[-1]

    @pl.when(pl.program_id(0) < nv_ref[0])
    def _():
        x = x_ref[...].astype(BF16)
        acc = jnp.dot(x, refs[0][0], preferred_element_type=F32)
        if swiglu:
            acc = acc * jax.nn.sigmoid(acc) * jnp.dot(x, refs[1][0], preferred_element_type=F32)
        o_ref[...] = acc.astype(o_ref.dtype)

    @pl.when(pl.program_id(0) >= nv_ref[0])
    def _():
        o_ref[...] = jnp.zeros_like(o_ref)


def _gmm(x, ws, tile_expert, n_valid, *, out_dtype, tn=512):
    P, K = x.shape
    N = ws[0].shape[2]
    tm = MOE_TILE
    tn = _tile(N, tn)
    wspec = pl.BlockSpec((1, K, tn), lambda i, j, te, nv: (te[i], 0, j))
    return pl.pallas_call(
        functools.partial(_gmm_kernel, swiglu=len(ws) == 2),
        grid_spec=pltpu.PrefetchScalarGridSpec(
            num_scalar_prefetch=2,
            grid=(P // tm, N // tn),
            in_specs=[pl.BlockSpec((tm, K), lambda i, j, te, nv: (i, 0))] + [wspec] * len(ws),
            out_specs=pl.BlockSpec((tm, tn), lambda i, j, te, nv: (i, j))),
        out_shape=jax.ShapeDtypeStruct((P, N), out_dtype),
        compiler_params=_params("parallel", "parallel"),
        name="moe_gmm",
    )(tile_expert, n_valid, x, *ws)


def _combine_kernel(p1_ref, p2_ref, route_ref, ys_ref, o_ref, a_ref, b_ref, sem):
    tt = o_ref.shape[0]
    base = pl.program_id(0) * tt

    def start(r, carry):
        _row_copy(ys_ref.at[pl.ds(p1_ref[base + r], 1)], a_ref.at[pl.ds(r, 1)], sem).start()
        _row_copy(ys_ref.at[pl.ds(p2_ref[base + r], 1)], b_ref.at[pl.ds(r, 1)], sem).start()
        return carry

    lax.fori_loop(0, tt, start, 0, unroll=8)
    _row_copy(ys_ref.at[pl.ds(0, tt)], a_ref, sem).wait()
    _row_copy(ys_ref.at[pl.ds(0, tt)], b_ref, sem).wait()
    route = route_ref[...]
    o_ref[...] = (route[:, ROUTE_G1:ROUTE_G1 + 1] * a_ref[...]
                  + route[:, ROUTE_G2:ROUTE_G2 + 1] * b_ref[...])


def _combine(ys, route, p1, p2):
    M = route.shape[0]
    D = ys.shape[1]
    tt = _tile(M, ROW_TILE, 8)
    return pl.pallas_call(
        _combine_kernel,
        grid_spec=pltpu.PrefetchScalarGridSpec(
            num_scalar_prefetch=2,
            grid=(M // tt,),
            in_specs=[pl.BlockSpec((tt, LANES), lambda i, p1, p2: (i, 0)),
                      pl.BlockSpec(memory_space=pl.ANY)],
            out_specs=pl.BlockSpec((tt, D), lambda i, p1, p2: (i, 0)),
            scratch_shapes=[pltpu.VMEM((tt, D), F32), pltpu.VMEM((tt, D), F32),
                            pltpu.SemaphoreType.DMA(())]),
        out_shape=jax.ShapeDtypeStruct((M, D), F32),
        compiler_params=_params("arbitrary"),
        name="moe_combine",
    )(p1, p2, route, ys)


def _moe(h, logits, w1, w3, w2):
    M, D = h.shape
    E = w1.shape[0]
    tm = MOE_TILE
    route, sel = _gates(logits, E)
    rank, cnt = _rank(sel)
    counts = cnt[0, :E].astype(jnp.int32)
    padded = (counts + tm - 1) // tm * tm
    ends = jnp.cumsum(padded)
    n_tiles = (TOP_K * M) // tm + E
    tile_expert = jnp.minimum(
        jnp.searchsorted(ends, jnp.arange(n_tiles, dtype=jnp.int32) * tm, side="right"), E - 1
    ).astype(jnp.int32)
    n_valid = (ends[-1:] // tm).astype(jnp.int32)
    offs = jnp.zeros((1, LANES), F32).at[0, :E].set((ends - padded).astype(F32))
    pos = _positions(route, rank, offs)
    p1, p2 = pos[:, 0], pos[:, 1]
    xs = _dispatch(h, p1, p2, n_tiles * tm)
    mid = _gmm(xs, [w1, w3], tile_expert, n_valid, out_dtype=BF16)
    ys = _gmm(mid, [w2], tile_expert, n_valid, out_dtype=F32)
    return _combine(ys, route, p1, p2)


def kernel(x, c, ctx, c_ctx, ada_w, ada_b, ln_w, ln_b, hy_w_in, hy_b_in, hy_conv_w, hy_conv_b, hy_f_w1, hy_f_b1, hy_f_w2, hy_f_b2, hy_f_w3, hy_f_freq, hy_skip, hy_w_out, hy_b_out, rw_mu, rw_w_rkv, rw_w_o, rw_w0, rw_w1, rw_w2, rw_a0, rw_a1, rw_a2, rw_g1, rw_g2, rw_k_k, rw_k_a, rw_r_k, rw_gn_w, rw_gn_b, at_w_qkv, at_b_qkv, at_w_o, at_b_o, at_sink, ff_w1, ff_w3, ff_w2, moe_router, moe_w1, moe_w3, moe_w2):
    B, L, D = x.shape
    Lc = ctx.shape[1]
    depth = ada_w.shape[0]
    alpha = (2 * depth) ** 0.25
    assert Lc % ROW_TILE == 0 and L % ROW_TILE == 0 and D % (RW_GROUP * HEAD) == 0
    bf = lambda t: t.astype(BF16)

    ctx_row = B
    rows = -(-(B + 1) // 8) * 8
    cond = jnp.zeros((rows, D), F32).at[:B].set(c).at[B].set(c_ctx)
    mods = _ada(cond, ada_w, ada_b).reshape(depth, rows, N_MOD, D)

    tables = {}
    if depth > 0:
        tables[L] = _dft_tables(L)
        if depth > 1:
            tables[Lc] = _dft_tables(Lc)

    xs = jnp.concatenate([ctx, x], axis=1)
    ctx_tiles = Lc // ROW_TILE
    h = _modulate(xs, mods[0], sh=0, sc=1, ctx_tiles=ctx_tiles, ctx_row=ctx_row,
                  out_dtype=BF16)
    for i in range(depth):
        last = i == depth - 1
        kind = i % 3
        j = i // 3
        mod = mods[i]
        has_ctx = xs.shape[1] != L
        ct = ctx_tiles if has_ctx else 0
        T = xs.shape[1]
        if kind == 0:
            p = dict(w_in=bf(hy_w_in[j]), b_in=hy_b_in[j], conv_w=hy_conv_w[j], conv_b=hy_conv_b[j],
                     filter=(hy_f_w1[j], hy_f_b1[j], hy_f_w2[j], hy_f_b2[j], hy_f_w3[j], hy_f_freq[j]),
                     skip=hy_skip[j], w_out=bf(hy_w_out[j]), b_out=hy_b_out[j])
            y = _hyena(h, p, tables, Lc=Lc, has_ctx=has_ctx)
        elif kind == 1:
            assert has_ctx and not last
            p = dict(mu=rw_mu[j], w_rkv=bf(rw_w_rkv[j]), w_o=bf(rw_w_o[j]), w0=rw_w0[j], w1=rw_w1[j],
                     w2=rw_w2[j], a0=rw_a0[j], a1=rw_a1[j], a2=rw_a2[j], g1=rw_g1[j], g2=rw_g2[j],
                     k_k=rw_k_k[j], k_a=rw_k_a[j], r_k=rw_r_k[j], gn_w=rw_gn_w[j], gn_b=rw_gn_b[j])
            y = _rwkv(h, p, Lc=Lc)
        else:
            assert has_ctx and not last
            p = dict(w_qkv=bf(at_w_qkv[j]), b_qkv=at_b_qkv[j], w_o=bf(at_w_o[j]), b_o=at_b_o[j],
                     sink=at_sink[j])
            y = _attention(h, p, Lc=Lc)

        drop1 = ct if last else 0
        fj = i // 2
        moe = i % 2 == 1
        res = _ln(xs, y, mod, ln_w[i, 0], ln_b[i, 0], alpha=alpha, gate=2, h_mod=(3, 4),
                  h_dtype=F32 if moe else BF16, router=moe_router[fj] if moe else None,
                  ctx_tiles=ct, ctx_row=ctx_row, drop_tiles=drop1)
        xs, h2 = res[0], res[1]
        ct = ct - drop1
        T = xs.shape[1]
        M = B * T
        if moe:
            f = _moe(h2.reshape(M, D), res[2].reshape(M, LANES), bf(moe_w1[fj]), bf(moe_w3[fj]),
                     bf(moe_w2[fj]))
        else:
            f = _swiglu_ffn(h2.reshape(M, D), bf(ff_w1[fj]), bf(ff_w3[fj]), bf(ff_w2[fj]))
        f = f.reshape(B, T, D)
        if last:
            (xs,) = _ln(xs, f, mod, ln_w[i, 1], ln_b[i, 1], alpha=alpha, gate=5,
                        ctx_tiles=ct, ctx_row=ctx_row)
        else:
            nxt_last = i + 1 == depth - 1
            nkind = (i + 1) % 3
            drop2 = ct if (nxt_last and nkind == 0) else 0
            xs, h = _ln(xs, f, mod, ln_w[i, 1], ln_b[i, 1], alpha=alpha, gate=5, h_mod=(0, 1),
                        mod_h=mods[i + 1], h_dtype=F32 if nkind == 1 else BF16,
                        ctx_tiles=ct, ctx_row=ctx_row, drop_tiles=drop2)
    return xs
```

```python
import functools
import math

import jax
import jax.numpy as jnp
from jax import lax
from jax.experimental import pallas as pl
from jax.experimental.pallas import tpu as pltpu

F32 = jnp.float32
BF16 = jnp.bfloat16
HIGHEST = lax.Precision.HIGHEST

VMEM_LIMIT_BYTES = 56 * 1024 * 1024
LANES = 128
ROW_TILE = 256
BF16_ROWS = 16

LN_EPS = 1e-5
N_MOD = 6
HEAD = 64
RW_GN_EPS = 64e-5
RW_CHUNK = 64
RW_GROUP = 4
AT_KV_HEADS = 4
AT_WINDOW = 128
AT_BLOCK = 128
GRID_W = 64
ROPE_BASE = 10000.0
NEG_INF = -1e30
TOP_K = 2
HY_BANDS = 16
HY_EMB = 2 * HY_BANDS + 1
HY_MIN_DECAY = -math.log(1e-2) / 1.5
HY_MAX_DECAY = -math.log(1e-2) / 0.3


def _params(*sem):
    return pltpu.CompilerParams(dimension_semantics=sem, vmem_limit_bytes=VMEM_LIMIT_BYTES)


def _tile(n, pref, mult=LANES):
    if n <= pref:
        return n
    t = (pref // mult) * mult
    while t >= mult:
        if n % t == 0:
            return t
        t -= mult
    return n


def _mm_kernel(*refs, n_w, has_bias, act):
    x = refs[0][...].astype(BF16)
    o_ref = refs[-1]
    acc = jnp.dot(x, refs[1][...].astype(BF16), preferred_element_type=F32)
    if has_bias:
        acc = acc + refs[1 + n_w][...]
    if act == "swiglu":
        acc3 = jnp.dot(x, refs[2][...].astype(BF16), preferred_element_type=F32)
        acc = acc * jax.nn.sigmoid(acc) * acc3
    o_ref[...] = acc.astype(o_ref.dtype)


def _mm(x, w, *, w3=None, bias=None, out_dtype=F32, tm=1024, tn=512):
    M, K = x.shape
    N = w.shape[1]
    tm = _tile(M, tm, 8)
    tn = _tile(N, tn)
    ws = [w] if w3 is None else [w, w3]
    in_specs = [pl.BlockSpec((tm, K), lambda i, j: (i, 0))]
    in_specs += [pl.BlockSpec((K, tn), lambda i, j: (0, j)) for _ in ws]
    args = [x] + ws
    if bias is not None:
        in_specs.append(pl.BlockSpec((1, tn), lambda i, j: (0, j)))
        args.append(bias.reshape(1, N).astype(F32))
    return pl.pallas_call(
        functools.partial(_mm_kernel, n_w=len(ws), has_bias=bias is not None,
                          act="swiglu" if w3 is not None else None),
        grid=(M // tm, N // tn),
        in_specs=in_specs,
        out_specs=pl.BlockSpec((tm, tn), lambda i, j: (i, j)),
        out_shape=jax.ShapeDtypeStruct((M, N), out_dtype),
        compiler_params=_params("parallel", "parallel"),
        name="mm_swiglu" if w3 is not None else "mm",
    )(*args)


def _apply_act(v, kind):
    if kind == "sigmoid":
        return jax.nn.sigmoid(v)
    if kind == "tanh":
        return jnp.tanh(v)
    if kind == "logdecay":
        return -jnp.exp(-jax.nn.softplus(-v) - 0.5)
    return v


def _lora_kernel(x_ref, a_ref, b_ref, bias_ref, o_ref, *, mid, fin):
    t = jnp.dot(x_ref[...], a_ref[...], preferred_element_type=F32)
    t = _apply_act(t, mid).astype(BF16)
    y = jnp.dot(t, b_ref[...], preferred_element_type=F32) + bias_ref[...]
    o_ref[...] = _apply_act(y, fin).astype(o_ref.dtype)


def _lora(x, a, b, bias, *, mid, fin, tm=512):
    M, K = x.shape
    N = b.shape[1]
    R = -(-a.shape[1] // LANES) * LANES
    a = jnp.zeros((K, R), a.dtype).at[:, :a.shape[1]].set(a)
    b = jnp.zeros((R, N), b.dtype).at[:b.shape[0]].set(b)
    tm = _tile(M, tm, 8)
    return pl.pallas_call(
        functools.partial(_lora_kernel, mid=mid, fin=fin),
        grid=(M // tm,),
        in_specs=[pl.BlockSpec((tm, K), lambda i: (i, 0)),
                  pl.BlockSpec((K, R), lambda i: (0, 0)),
                  pl.BlockSpec((R, N), lambda i: (0, 0)),
                  pl.BlockSpec((1, N), lambda i: (0, 0))],
        out_specs=pl.BlockSpec((tm, N), lambda i: (i, 0)),
        out_shape=jax.ShapeDtypeStruct((M, N), F32),
        compiler_params=_params("parallel"),
        name="lora",
    )(x, a.astype(BF16), b.astype(BF16), bias.reshape(1, N).astype(F32))


def _ada_kernel(c_ref, w_ref, b_ref, o_ref):
    c = c_ref[...]
    s = (c * jax.nn.sigmoid(c)).astype(BF16)
    o_ref[0] = jnp.dot(s, w_ref[0].astype(BF16), preferred_element_type=F32) + b_ref[0]


def _ada(cond, ada_w, ada_b):
    depth, D, N = ada_w.shape
    R = cond.shape[0]
    tn = _tile(N, 1024)
    return pl.pallas_call(
        _ada_kernel,
        grid=(depth, N // tn),
        in_specs=[pl.BlockSpec((R, D), lambda i, j: (0, 0)),
                  pl.BlockSpec((1, D, tn), lambda i, j: (i, 0, j)),
                  pl.BlockSpec((1, 1, tn), lambda i, j: (i, 0, j))],
        out_specs=pl.BlockSpec((1, R, tn), lambda i, j: (i, 0, j)),
        out_shape=jax.ShapeDtypeStruct((depth, R, N), F32),
        compiler_params=_params("parallel", "parallel"),
        name="ada",
    )(cond, ada_w, ada_b.reshape(depth, 1, N))


def _mod_index(ctx_tiles, ctx_row, off):
    def index(b, t):
        return (jnp.where(t + off < ctx_tiles, ctx_row, b), 0, 0)
    return index


def _modulate_kernel(x_ref, mod_ref, h_ref, *, sh, sc):
    x = x_ref[0]
    h_ref[0] = (x * (1.0 + mod_ref[0, sc:sc + 1, :]) + mod_ref[0, sh:sh + 1, :]).astype(h_ref.dtype)


def _modulate(x, mod, *, sh, sc, ctx_tiles, ctx_row, out_dtype):
    B, T, D = x.shape
    return pl.pallas_call(
        functools.partial(_modulate_kernel, sh=sh, sc=sc),
        grid=(B, T // ROW_TILE),
        in_specs=[pl.BlockSpec((1, ROW_TILE, D), lambda b, t: (b, t, 0)),
                  pl.BlockSpec((1, N_MOD, D), _mod_index(ctx_tiles, ctx_row, 0))],
        out_specs=pl.BlockSpec((1, ROW_TILE, D), lambda b, t: (b, t, 0)),
        out_shape=jax.ShapeDtypeStruct((B, T, D), out_dtype),
        compiler_params=_params("parallel", "parallel"),
        name="modulate",
    )(x, mod)


def _ln_kernel(*refs, alpha, gate, sh, sc, has_h, has_router):
    x_ref, y_ref, mod_ref, modh_ref, w_ref, b_ref = refs[:6]
    pos = 6
    router_ref = None
    if has_router:
        router_ref = refs[pos]
        pos += 1
    xo_ref = refs[pos]
    pos += 1
    z = alpha * x_ref[0] + mod_ref[0, gate:gate + 1, :] * y_ref[0].astype(F32)
    mu = jnp.mean(z, axis=-1, keepdims=True)
    d = z - mu
    var = jnp.mean(d * d, axis=-1, keepdims=True)
    xn = d * lax.rsqrt(var + LN_EPS) * w_ref[...] + b_ref[...]
    xo_ref[0] = xn
    if has_h:
        h = xn * (1.0 + modh_ref[0, sc:sc + 1, :]) + modh_ref[0, sh:sh + 1, :]
        h_ref = refs[pos]
        pos += 1
        h_ref[0] = h.astype(h_ref.dtype)
        if has_router:
            refs[pos][0] = jnp.dot(h, router_ref[...], precision=HIGHEST, preferred_element_type=F32)


def _ln(x, y, mod, ln_w, ln_b, *, alpha, gate, h_mod=None, mod_h=None, h_dtype=BF16, router=None,
        ctx_tiles, ctx_row, drop_tiles=0):
    B, T, D = x.shape
    nt = T // ROW_TILE - drop_tiles
    To = nt * ROW_TILE
    off = drop_tiles
    in_specs = [pl.BlockSpec((1, ROW_TILE, D), lambda b, t: (b, t + off, 0)),
                pl.BlockSpec((1, ROW_TILE, D), lambda b, t: (b, t + off, 0)),
                pl.BlockSpec((1, N_MOD, D), _mod_index(ctx_tiles, ctx_row, off)),
                pl.BlockSpec((1, N_MOD, D), _mod_index(ctx_tiles, ctx_row, off)),
                pl.BlockSpec((1, D), lambda b, t: (0, 0)),
                pl.BlockSpec((1, D), lambda b, t: (0, 0))]
    args = [x, y, mod, mod if mod_h is None else mod_h, ln_w.reshape(1, D), ln_b.reshape(1, D)]
    out_specs = [pl.BlockSpec((1, ROW_TILE, D), lambda b, t: (b, t, 0))]
    out_shape = [jax.ShapeDtypeStruct((B, To, D), F32)]
    sh = sc = 0
    if h_mod is not None:
        sh, sc = h_mod
        out_specs.append(pl.BlockSpec((1, ROW_TILE, D), lambda b, t: (b, t, 0)))
        out_shape.append(jax.ShapeDtypeStruct((B, To, D), h_dtype))
    if router is not None:
        E = router.shape[1]
        router_p = jnp.zeros((D, LANES), F32).at[:, :E].set(router)
        in_specs.append(pl.BlockSpec((D, LANES), lambda b, t: (0, 0)))
        args.append(router_p)
        out_specs.append(pl.BlockSpec((1, ROW_TILE, LANES), lambda b, t: (b, t, 0)))
        out_shape.append(jax.ShapeDtypeStruct((B, To, LANES), F32))
    return pl.pallas_call(
        functools.partial(_ln_kernel, alpha=alpha, gate=gate, sh=sh, sc=sc,
                          has_h=h_mod is not None, has_router=router is not None),
        grid=(B, nt),
        in_specs=in_specs,
        out_specs=out_specs,
        out_shape=out_shape,
        compiler_params=_params("parallel", "parallel"),
        name="ln_residual",
    )(*args)


def _halo_specs(T, C, col):
    r8 = ROW_TILE // 8
    last8 = T // 8 - 1
    return [pl.BlockSpec((1, ROW_TILE, C), lambda b, t, j: (b, t, col(j))),
            pl.BlockSpec((1, 8, C), lambda b, t, j: (b, jnp.maximum(t * r8 - 1, 0), col(j))),
            pl.BlockSpec((1, 8, C), lambda b, t, j: (b, jnp.minimum(t * r8 + r8, last8), col(j)))]


def _neighbours(cur, prev8, next8, t, n_tiles, ctx_tiles):
    rows = lax.broadcasted_iota(jnp.int32, cur.shape, 0)
    has_prev = jnp.logical_and(t != 0, t != ctx_tiles)
    has_next = jnp.logical_and(t != n_tiles - 1, t != ctx_tiles - 1)
    top = jnp.where(has_prev, prev8[7:8, :], 0.0)
    bot = jnp.where(has_next, next8[0:1, :], 0.0)
    up = jnp.where(rows == 0, top, pltpu.roll(cur, 1, axis=0))
    dn = jnp.where(rows == cur.shape[0] - 1, bot, pltpu.roll(cur, cur.shape[0] - 1, axis=0))
    return up, dn


def _hy_in_kernel(x_ref, xp_ref, xn_ref, *refs, tm, T, Lc):
    ws, bs, cws, cbs = refs[0:3], refs[3:6], refs[6:9], refs[9:12]
    x0_ref, vv_ref, vvb_ref = refs[12:15]
    H = BF16_ROWS
    x = jnp.concatenate([xp_ref[0], x_ref[0], xn_ref[0]], axis=0)
    row = lax.broadcasted_iota(jnp.int32, (tm, 1), 0) + pl.program_id(1) * tm
    no_prev = (row == 0) | (row == Lc)
    no_next = (row == Lc - 1) | (row == T - 1)
    out = []
    for s in range(3):
        acc = jnp.dot(x, ws[s][...], preferred_element_type=F32) + bs[s][...]
        n = acc.shape[0]
        cur = acc[H:H + tm]
        up = jnp.where(no_prev, 0.0, pltpu.roll(acc, 1, axis=0)[H:H + tm])
        dn = jnp.where(no_next, 0.0, pltpu.roll(acc, n - 1, axis=0)[H:H + tm])
        cw = cws[s]
        out.append(up * cw[0:1, :] + cur * cw[1:2, :] + dn * cw[2:3, :] + cbs[s][...])
    x0_ref[0] = out[0]
    vv = out[1] * out[2]
    vv_ref[0] = vv
    vvb_ref[0] = vv.astype(BF16)


def _hy_in(h, w_in, b_in, conv_w, conv_b, *, Lc):
    B, T, K = h.shape
    D = w_in.shape[1] // 3
    H = BF16_ROWS
    tm = _tile(T, 1280, 64)
    tc = _tile(D, 512)
    nj = D // tc
    r16 = tm // H
    last16 = T // H - 1
    sec = lambda shape: [pl.BlockSpec(shape, lambda b, t, j, s=s: (0, s * nj + j)) for s in range(3)]
    in_specs = [pl.BlockSpec((1, tm, K), lambda b, t, j: (b, t, 0)),
                pl.BlockSpec((1, H, K), lambda b, t, j: (b, jnp.maximum(t * r16 - 1, 0), 0)),
                pl.BlockSpec((1, H, K), lambda b, t, j: (b, jnp.minimum(t * r16 + r16, last16), 0))]
    in_specs += sec((K, tc)) + sec((1, tc)) + sec((3, tc)) + sec((1, tc))
    blk = pl.BlockSpec((1, tm, tc), lambda b, t, j: (b, t, j))
    b2 = b_in.reshape(1, 3 * D)
    cb2 = conv_b.reshape(1, 3 * D)
    return pl.pallas_call(
        functools.partial(_hy_in_kernel, tm=tm, T=T, Lc=Lc),
        grid=(B, T // tm, nj),
        in_specs=in_specs,
        out_specs=[blk, blk, blk],
        out_shape=[jax.ShapeDtypeStruct((B, T, D), F32), jax.ShapeDtypeStruct((B, T, D), F32),
                   jax.ShapeDtypeStruct((B, T, D), BF16)],
        compiler_params=_params("parallel", "parallel", "parallel"),
        name="hy_in",
    )(h, h, h, w_in, w_in, w_in, b2, b2, b2, conv_w, conv_w, conv_w, cb2, cb2, cb2)


def _hy_filter_kernel(w1_ref, b1_ref, w2_ref, b2_ref, w3_ref, fr_ref, h_ref, s_ref, *, L, D, tl):
    i = pl.program_id(0)
    row = (lax.broadcasted_iota(jnp.int32, (tl, LANES), 0) + i * tl).astype(F32)
    lane = lax.broadcasted_iota(jnp.int32, (tl, LANES), 1)
    band = jnp.where(lane <= HY_BANDS, lane - 1, lane - 1 - HY_BANDS).astype(F32)
    freq = 1e-4 + band * ((HY_BANDS - 1 - 1e-4) / (HY_BANDS - 1))
    ang = freq * (row * (2.0 * math.pi / L))
    z = jnp.where(lane == 0, row / (L - 1),
                  jnp.where(lane <= HY_BANDS, jnp.cos(ang),
                            jnp.where(lane < HY_EMB, -jnp.sin(ang), 0.0)))
    h = jnp.sin(fr_ref[0:1, :] * (jnp.dot(z, w1_ref[...], precision=HIGHEST,
                                           preferred_element_type=F32) + b1_ref[...]))
    h = jnp.sin(fr_ref[1:2, :] * (jnp.dot(h, w2_ref[...], precision=HIGHEST,
                                           preferred_element_type=F32) + b2_ref[...]))
    h = jnp.dot(h, w3_ref[...], precision=HIGHEST, preferred_element_type=F32)
    half = L // 2
    rowd = (lax.broadcasted_iota(jnp.int32, (tl, D), 0) + i * tl).astype(F32)
    dist = jnp.abs(rowd - half) / half
    chan = lax.broadcasted_iota(jnp.int32, (tl, D), 1).astype(F32)
    deltas = HY_MIN_DECAY + chan * ((HY_MAX_DECAY - HY_MIN_DECAY) / (D - 1))
    h = h * jnp.exp(-dist * deltas)
    h_ref[...] = h

    @pl.when(i == 0)
    def _():
        s_ref[...] = jnp.zeros_like(s_ref)

    s_ref[...] += jnp.sum(jnp.abs(h), axis=0, keepdims=True)


def _hy_filter(L, f_w1, f_b1, f_w2, f_b2, f_w3, f_freq):
    D = f_w3.shape[1]
    hid = f_w1.shape[1]
    w1 = jnp.zeros((LANES, LANES), F32).at[:HY_EMB, :hid].set(f_w1)
    b1 = jnp.zeros((1, LANES), F32).at[0, :hid].set(f_b1)
    w2 = jnp.zeros((LANES, LANES), F32).at[:hid, :hid].set(f_w2)
    b2 = jnp.zeros((1, LANES), F32).at[0, :hid].set(f_b2)
    w3 = jnp.zeros((LANES, D), F32).at[:hid].set(f_w3)
    fr = jnp.zeros((2, LANES), F32).at[:, :hid].set(f_freq)
    tl = _tile(L, 256, 8)
    full = lambda shape: pl.BlockSpec(shape, lambda i: (0, 0))
    return pl.pallas_call(
        functools.partial(_hy_filter_kernel, L=L, D=D, tl=tl),
        grid=(L // tl,),
        in_specs=[full((LANES, LANES)), full((1, LANES)), full((LANES, LANES)), full((1, LANES)),
                  full((LANES, D)), full((2, LANES))],
        out_specs=[pl.BlockSpec((tl, D), lambda i: (i, 0)), full((1, D))],
        out_shape=[jax.ShapeDtypeStruct((L, D), F32), jax.ShapeDtypeStruct((1, D), F32)],
        compiler_params=_params("arbitrary"),
        name="hy_filter",
    )(w1, b1, w2, b2, w3, fr)


DFT_ROWS = 64


def _dft_tables_kernel(fa_ref, ia_ref, b_ref, fwd_ref, inv_ref, *, L):
    n = 2 * L
    i = pl.program_id(0)
    cb, sb = b_ref[0], b_ref[1]
    q = lax.broadcasted_iota(jnp.int32, (DFT_ROWS, L), 0)
    col = lax.broadcasted_iota(jnp.int32, (DFT_ROWS, L), 1)
    sign_c = jnp.where(col % 2 == 0, 1.0, -1.0)
    sign_r = jnp.where(q % 2 == 0, 1.0, -1.0)
    ca, sa = fa_ref[0, 0:1, :], fa_ref[0, 1:2, :]
    cos_f = ca * cb - sa * sb
    sin_f = sa * cb + ca * sb
    first_row = (q == 0) & (i == 0)
    fwd_ref[0] = cos_f.astype(fwd_ref.dtype)
    fwd_ref[1] = jnp.where(first_row, sign_c, -sin_f).astype(fwd_ref.dtype)
    ca, sa = ia_ref[0, 0:1, :], ia_ref[0, 1:2, :]
    cos_i = ca * cb - sa * sb
    sin_i = sa * cb + ca * sb
    inv_ref[:, :L] = jnp.where(col == 0, 1.0 / n, (2.0 / n) * cos_i).astype(inv_ref.dtype)
    inv_ref[:, L:] = jnp.where(col == 0, sign_r / n, (-2.0 / n) * sin_i).astype(inv_ref.dtype)


def _dft_tables(L):
    n = 2 * L
    R = DFT_ROWS
    assert L % (2 * R) == 0
    w = 2.0 * math.pi / n
    c = jnp.arange(L, dtype=jnp.int32)
    hi = jnp.arange(L // R, dtype=jnp.int32) * R

    def cos_sin(rows):
        ang = ((rows[:, None] * c[None, :]) % n).astype(F32) * w
        return jnp.stack([jnp.cos(ang), jnp.sin(ang)], axis=1)

    fa = cos_sin(hi)
    ia = cos_sin(hi + L // 2)
    b = jnp.swapaxes(cos_sin(jnp.arange(R, dtype=jnp.int32)), 0, 1)
    return pl.pallas_call(
        functools.partial(_dft_tables_kernel, L=L),
        grid=(L // R,),
        in_specs=[pl.BlockSpec((1, 2, L), lambda i: (i, 0, 0)),
                  pl.BlockSpec((1, 2, L), lambda i: (i, 0, 0)),
                  pl.BlockSpec((2, R, L), lambda i: (0, 0, 0))],
        out_specs=[pl.BlockSpec((2, R, L), lambda i: (0, i, 0)),
                   pl.BlockSpec((R, 2 * L), lambda i: (i, 0))],
        out_shape=[jax.ShapeDtypeStruct((2, L, L), BF16), jax.ShapeDtypeStruct((L, 2 * L), BF16)],
        compiler_params=_params("parallel"),
        name="dft_tables",
    )(fa, ia, b)


def _dft_fwd_kernel(a_ref, w_ref, *rest, mode):
    o_ref = rest[-1]
    w = w_ref[0]
    vre = jnp.dot(a_ref[0], w, preferred_element_type=F32)
    vim = jnp.dot(a_ref[1], w, preferred_element_type=F32)
    if mode == "scale":
        inv = 1.0 / (rest[0][...] + 1e-6)
        o_ref[0, 0] = vre * inv
        o_ref[0, 1] = vim * inv
    else:
        hre = rest[0][0]
        him = rest[0][1]
        first = jnp.logical_and(pl.program_id(2) == 0,
                                lax.broadcasted_iota(jnp.int32, vre.shape, 0) == 0)
        zre = jnp.where(first, vre * hre, vre * hre - vim * him)
        zim = jnp.where(first, vim * him, vre * him + vim * hre)
        o_ref[0, 0] = zre.astype(o_ref.dtype)
        o_ref[0, 1] = zim.astype(o_ref.dtype)


def _dft_fwd(fwd, w, extra, *, mode, out_dtype):
    B, L, D = w.shape
    tm = _tile(L, 256, 8)
    tn = _tile(D, 1024)
    if mode == "scale":
        extra_spec = pl.BlockSpec((1, tn), lambda b, j, m: (0, j))
    else:
        extra_spec = pl.BlockSpec((2, tm, tn), lambda b, j, m: (0, m, j))
    return pl.pallas_call(
        functools.partial(_dft_fwd_kernel, mode=mode),
        grid=(B, D // tn, L // tm),
        in_specs=[pl.BlockSpec((2, tm, L), lambda b, j, m: (0, m, 0)),
                  pl.BlockSpec((1, L, tn), lambda b, j, m: (b, 0, j)),
                  extra_spec],
        out_specs=pl.BlockSpec((1, 2, tm, tn), lambda b, j, m: (b, 0, m, j)),
        out_shape=jax.ShapeDtypeStruct((B, 2, L, D), out_dtype),
        compiler_params=_params("parallel", "parallel", "parallel"),
        name="dft_fwd",
    )(fwd, w, extra)


def _dft_inv_kernel(b_ref, z_ref, x0_ref, vv_ref, skip_ref, u_ref):
    y = jnp.dot(b_ref[...], z_ref[0], preferred_element_type=F32)
    u_ref[0] = (x0_ref[0] * (y + vv_ref[0] * skip_ref[...])).astype(u_ref.dtype)


def _dft_inv(inv, z, x0, vv, skip, *, row_off):
    B, n, D = z.shape
    L = n // 2
    tm = _tile(L, ROW_TILE, 8)
    tn = _tile(D, 1024)
    off = row_off // tm
    return pl.pallas_call(
        _dft_inv_kernel,
        grid=(B, D // tn, L // tm),
        in_specs=[pl.BlockSpec((tm, n), lambda b, j, m: (m, 0)),
                  pl.BlockSpec((1, n, tn), lambda b, j, m: (b, 0, j)),
                  pl.BlockSpec((1, tm, tn), lambda b, j, m: (b, m + off, j)),
                  pl.BlockSpec((1, tm, tn), lambda b, j, m: (b, m + off, j)),
                  pl.BlockSpec((1, tn), lambda b, j, m: (0, j))],
        out_specs=pl.BlockSpec((1, tm, tn), lambda b, j, m: (b, m, j)),
        out_shape=jax.ShapeDtypeStruct((B, L, D), BF16),
        compiler_params=_params("parallel", "parallel", "parallel"),
        name="dft_inv",
    )(inv, z, x0, vv, skip.reshape(1, D))


def _hyena(h, p, tables, *, Lc, has_ctx):
    B, T, D = h.shape
    M = B * T
    x0, vv, vvb = _hy_in(h, p["w_in"], p["b_in"], p["conv_w"], p["conv_b"], Lc=Lc if has_ctx else -1)
    segs = [(Lc, T - Lc)] if has_ctx else [(0, T)]
    if has_ctx:
        segs = [(0, Lc)] + segs
    us = []
    for start, L in segs:
        fwd, inv = tables[L]
        filt, asum = _hy_filter(L, *p["filter"])
        hf = _dft_fwd(fwd, filt.astype(BF16)[None], asum, mode="scale", out_dtype=F32)[0]
        seg = vvb if (start == 0 and L == T) else lax.slice_in_dim(vvb, start, start + L, axis=1)
        z = _dft_fwd(fwd, seg, hf, mode="mul", out_dtype=BF16).reshape(B, 2 * L, D)
        us.append(_dft_inv(inv, z, x0, vv, p["skip"], row_off=start))
    u = us[0] if len(us) == 1 else jnp.concatenate(us, axis=1)
    return _mm(u.reshape(M, D), p["w_out"], bias=p["b_out"]).reshape(B, T, D)


def _rw_prep_kernel(cur_ref, prev_ref, next_ref, mu_ref, *o_refs, n_tiles, ctx_tiles):
    t = pl.program_id(1)
    cur = cur_ref[0]
    up, dn = _neighbours(cur, prev_ref[0], next_ref[0], t, n_tiles, ctx_tiles)
    dx = 0.5 * (up + dn) - cur
    for j, o_ref in enumerate(o_refs):
        o_ref[0] = (cur + dx * mu_ref[j:j + 1, :]).astype(o_ref.dtype)


def _rw_prep(hs, mu, *, ctx_tiles):
    B, T, D = hs.shape
    tc = _tile(D, 1024)
    n_tiles = T // ROW_TILE
    blk = pl.BlockSpec((1, ROW_TILE, tc), lambda b, t, j: (b, t, j))
    return pl.pallas_call(
        functools.partial(_rw_prep_kernel, n_tiles=n_tiles, ctx_tiles=ctx_tiles),
        grid=(B, n_tiles, D // tc),
        in_specs=_halo_specs(T, tc, lambda j: j) + [pl.BlockSpec((6, tc), lambda b, t, j: (0, j))],
        out_specs=[blk] * 6,
        out_shape=[jax.ShapeDtypeStruct((B, T, D), BF16)] * 6,
        compiler_params=_params("parallel", "parallel", "parallel"),
        name="rw_prep",
    )(hs, hs, hs, mu)


def _block_diag(x, mask):
    return jnp.where(mask, jnp.concatenate([x] * RW_GROUP, axis=0), 0.0).astype(BF16)


def _rw_chunks(r, k, v, lw, a, k_k, k_a, state, same_head, incl, strict, reverse):
    C = RW_CHUNK
    W = RW_GROUP * HEAD
    G = range(len(r))
    nt = (((1,), (1,)), ((), ()))
    ones = jnp.where(same_head, 1.0, 0.0).astype(BF16)
    tri = jnp.where(incl[:, :C], 1.0, 0.0).astype(BF16)
    bd = lambda t: _block_diag(t, same_head)
    mm = lambda x, y: jnp.dot(x.astype(BF16), y, preferred_element_type=F32)

    def split(t):
        hi = t.astype(BF16)
        return hi, (t - hi.astype(F32)).astype(BF16)

    kkr = [k[g] * k_k[g] for g in G]
    sq = [split(kkr[g] * kkr[g]) for g in G]
    ss = [jnp.dot(sq[g][0], ones, preferred_element_type=F32)
          + jnp.dot(sq[g][1], ones, preferred_element_type=F32) for g in G]
    lws = [split(lw[g]) for g in G]
    lp = [jnp.dot(tri, lws[g][0], preferred_element_type=F32)
          + jnp.dot(tri, lws[g][1], preferred_element_type=F32) for g in G]
    kk = [kkr[g] / jnp.maximum(jnp.sqrt(ss[g]), 1e-12) for g in G]
    kd = [k[g] * (1.0 + (a[g] - 1.0) * k_a[g]) for g in G]
    lp_end = [lp[g][0:1, :] if reverse else lp[g][C - 1:C, :] for g in G]
    e_neg = [jnp.exp(-lp[g]) for g in G]
    lhs = [jnp.concatenate([-kk[g] * jnp.exp(lp[g] - lw[g]), r[g] * jnp.exp(lp[g])], axis=0).astype(BF16)
           for g in G]
    rhs = [jnp.concatenate([bd(kk[g] * a[g] * e_neg[g]), bd(kd[g] * e_neg[g])], axis=0) for g in G]
    cross = [lax.dot_general(lhs[g], rhs[g], nt, preferred_element_type=F32) for g in G]
    from_state = [lax.dot_general(lhs[g], state[g].astype(BF16), nt, preferred_element_type=F32)
                  for g in G]
    v_bd = [bd(v[g]) for g in G]
    p = [jnp.where(strict, cross[g][:C, :W], 0.0) for g in G]
    x = [from_state[g][:C] + mm(jnp.where(strict, cross[g][:C, W:], 0.0), v_bd[g]) for g in G]
    n = 1
    while n < C:
        x = [x[g] + mm(p[g], bd(x[g])) for g in G]
        n *= 2
        if n < C:
            p = [mm(p[g], bd(p[g])) for g in G]
    y = [from_state[g][C:] + mm(jnp.where(incl, cross[g][C:, :W], 0.0), bd(x[g]))
         + mm(jnp.where(incl, cross[g][C:, W:], 0.0), v_bd[g]) for g in G]

    tail = [jnp.exp(lp_end[g] - lp[g]) for g in G]
    uv = [jnp.concatenate([x[g], v[g]], axis=0).astype(BF16) for g in G]
    bk = [jnp.concatenate([kk[g] * a[g] * tail[g], kd[g] * tail[g]], axis=0).astype(BF16) for g in G]
    upd = [lax.dot_general(uv[g], bk[g], (((0,), (0,)), ((), ())), preferred_element_type=F32)
           for g in G]
    new_state = [state[g] * jnp.exp(lp_end[g]) + jnp.where(same_head, upd[g], 0.0) for g in G]
    return y, new_state


RW_GROUPS_PER_STEP = 8
RW_CHUNKS_PER_STEP = 2


def _rw_scan_kernel(r_ref, k_ref, v_ref, lw_ref, a_ref, kk_ref, ka_ref, o_ref, state_ref, *, reverse):
    C = RW_CHUNK
    W = RW_GROUP * HEAD

    @pl.when(pl.program_id(2) == 0)
    def _():
        state_ref[...] = jnp.zeros_like(state_ref)

    ri = lax.broadcasted_iota(jnp.int32, (W, W), 0)
    ci = lax.broadcasted_iota(jnp.int32, (W, W), 1)
    same_head = (ri // HEAD) == (ci // HEAD)
    t_i = lax.broadcasted_iota(jnp.int32, (C, W), 0)
    s_i = lax.broadcasted_iota(jnp.int32, (C, W), 1) % C
    incl = (s_i >= t_i) if reverse else (s_i <= t_i)
    strict = (s_i > t_i) if reverse else (s_i < t_i)

    ng = state_ref.shape[0]
    sl = [slice(g * W, (g + 1) * W) for g in range(ng)]
    state = [state_ref[g] for g in range(ng)]
    order = range(RW_CHUNKS_PER_STEP)
    for c in (reversed(order) if reverse else order):
        rows = slice(c * C, (c + 1) * C)
        load = lambda ref: [ref[0, rows, s] for s in sl]
        y, state = _rw_chunks(load(r_ref), load(k_ref), load(v_ref), load(lw_ref), load(a_ref),
                              [kk_ref[:, s] for s in sl], [ka_ref[:, s] for s in sl],
                              state, same_head, incl, strict, reverse)
        for g in range(ng):
            o_ref[0, rows, sl[g]] = y[g]
    for g in range(ng):
        state_ref[g] = state[g]


def _rw_scan(r, k, v, lw, a, k_k, k_a, *, Lc, reverse):
    B, T, D = r.shape
    assert RW_CHUNK == HEAD
    rows = RW_CHUNK * RW_CHUNKS_PER_STEP
    assert Lc % rows == 0 and T % rows == 0
    W = RW_GROUP * HEAD
    ng = math.gcd(D // W, RW_GROUPS_PER_STEP)
    nb = T // rows
    ncb = Lc // rows

    def block(c):
        if not reverse:
            return c
        return jnp.where(c < ncb, ncb - 1 - c, nb - 1 - (c - ncb))

    blk = pl.BlockSpec((1, rows, ng * W), lambda b, g, c: (b, block(c), g))
    vec = pl.BlockSpec((1, ng * W), lambda b, g, c: (0, g))
    return pl.pallas_call(
        functools.partial(_rw_scan_kernel, reverse=reverse),
        grid=(B, D // (ng * W), nb),
        in_specs=[blk] * 5 + [vec, vec],
        out_specs=blk,
        out_shape=jax.ShapeDtypeStruct((B, T, D), F32),
        scratch_shapes=[pltpu.VMEM((ng, W, W), F32)],
        compiler_params=_params("parallel", "parallel", "arbitrary"),
        name="rw_scan",
    )(r, k, v, lw, a, k_k.reshape(1, D), k_a.reshape(1, D))


def _head_sums(x, ones_bd):
    W = ones_bd.shape[0]
    hi = x.astype(BF16)
    lo = (x - hi.astype(F32)).astype(BF16)
    cols = []
    for g in range(x.shape[1] // W):
        sl = slice(g * W, (g + 1) * W)
        cols.append(jnp.dot(hi[:, sl], ones_bd, preferred_element_type=F32)
                    + jnp.dot(lo[:, sl], ones_bd, preferred_element_type=F32))
    return jnp.concatenate(cols, axis=1)


def _rw_post_kernel(of_ref, ob_ref, r_ref, k_ref, v_ref, g_ref, rk_ref, gw_ref, gb_ref, o_ref):
    W = RW_GROUP * HEAD
    ri = lax.broadcasted_iota(jnp.int32, (W, W), 0)
    ci = lax.broadcasted_iota(jnp.int32, (W, W), 1)
    ones_bd = jnp.where((ri // HEAD) == (ci // HEAD), 1.0, 0.0).astype(BF16)
    o = of_ref[0] + ob_ref[0]
    m = _head_sums(o, ones_bd) * (1.0 / HEAD)
    d = o - m
    var = _head_sums(d * d, ones_bd) * (1.0 / HEAD)
    on = d * lax.rsqrt(var + RW_GN_EPS) * gw_ref[...] + gb_ref[...]
    bonus = _head_sums(r_ref[0] * k_ref[0] * rk_ref[...], ones_bd) * v_ref[0]
    o_ref[0] = ((on + bonus) * g_ref[0]).astype(o_ref.dtype)


def _rw_post(o_f, o_b, r, k, v, g, r_k, gn_w, gn_b):
    B, T, D = r.shape
    tt = _tile(T, 128, 8)
    blk = pl.BlockSpec((1, tt, D), lambda b, t: (b, t, 0))
    par = pl.BlockSpec((1, D), lambda b, t: (0, 0))
    return pl.pallas_call(
        _rw_post_kernel,
        grid=(B, T // tt),
        in_specs=[blk] * 6 + [par] * 3,
        out_specs=blk,
        out_shape=jax.ShapeDtypeStruct((B, T, D), BF16),
        compiler_params=_params("parallel", "parallel"),
        name="rw_post",
    )(o_f, o_b, r, k, v, g, r_k.reshape(1, D), gn_w.reshape(1, D), gn_b.reshape(1, D))


def _rwkv(hs, p, *, Lc):
    B, T, D = hs.shape
    M = B * T
    ctx_tiles = Lc // ROW_TILE
    xr, xw, xk, xv, xa, xg = [t.reshape(M, D) for t in _rw_prep(hs, p["mu"], ctx_tiles=ctx_tiles)]
    w_rkv = p["w_rkv"]
    r = _mm(xr, w_rkv[0]).reshape(B, T, D)
    k = _mm(xk, w_rkv[1]).reshape(B, T, D)
    v = _mm(xv, w_rkv[2]).reshape(B, T, D)
    g = _lora(xg, p["g1"], p["g2"], jnp.zeros((D,), F32), mid="sigmoid", fin=None).reshape(B, T, D)
    outs = []
    for d in range(2):
        lw = _lora(xw, p["w1"][d], p["w2"][d], p["w0"][d], mid="tanh", fin="logdecay").reshape(B, T, D)
        a = _lora(xa, p["a1"][d], p["a2"][d], p["a0"][d], mid=None, fin="sigmoid").reshape(B, T, D)
        outs.append(_rw_scan(r, k, v, lw, a, p["k_k"], p["k_a"], Lc=Lc, reverse=d == 1))
    y = _rw_post(outs[0], outs[1], r, k, v, g, p["r_k"], p["gn_w"], p["gn_b"])
    return _mm(y.reshape(M, D), p["w_o"]).reshape(B, T, D)


def _rope_tables(T, Lc):
    quarter = HEAD // 4
    pos = jnp.arange(T - Lc, dtype=jnp.int32)
    rows = (pos // GRID_W).astype(F32)
    cols = (pos % GRID_W).astype(F32)
    inv = ROPE_BASE ** (-jnp.arange(quarter, dtype=F32) / quarter)
    ang = jnp.concatenate([rows[:, None] * inv, rows[:, None] * inv,
                           cols[:, None] * inv, cols[:, None] * inv], axis=1)
    ang = jnp.concatenate([jnp.zeros((Lc, HEAD), F32), ang], axis=0)
    ang = jnp.concatenate([ang, ang], axis=1)
    return jnp.cos(ang), jnp.sin(ang)


def _rope_kernel(x_ref, cos_ref, sin_ref, q_ref, k_ref, v_ref, *, n_q, n_kv, scale):
    x = x_ref[0]
    n_rot = (n_q + n_kv) * HEAD
    xr = x[:, :n_rot]
    reps = n_rot // LANES
    cos = jnp.concatenate([cos_ref[...]] * reps, axis=1)
    sin = jnp.concatenate([sin_ref[...]] * reps, axis=1)
    quarter = HEAD // 4
    lane = lax.broadcasted_iota(jnp.int32, xr.shape, 1)
    first = (lane % (2 * quarter)) < quarter
    rot = jnp.where(first, -pltpu.roll(xr, n_rot - quarter, axis=1), pltpu.roll(xr, quarter, axis=1))
    y = xr * cos + rot * sin
    for h in range(n_q):
        q_ref[0, h] = (y[:, h * HEAD:(h + 1) * HEAD] * scale).astype(q_ref.dtype)
    for h in range(n_kv):
        lo = (n_q + h) * HEAD
        k_ref[0, h] = y[:, lo:lo + HEAD].astype(k_ref.dtype)
        lo = (n_q + n_kv + h) * HEAD
        v_ref[0, h] = x[:, lo:lo + HEAD].astype(v_ref.dtype)


def _rope(qkv, cos, sin, *, n_q, n_kv):
    B, T, W = qkv.shape
    tt = _tile(T, 128, 8)
    out = lambda n: pl.BlockSpec((1, n, tt, HEAD), lambda b, t: (b, 0, t, 0))
    return pl.pallas_call(
        functools.partial(_rope_kernel, n_q=n_q, n_kv=n_kv, scale=HEAD ** -0.5),
        grid=(B, T // tt),
        in_specs=[pl.BlockSpec((1, tt, W), lambda b, t: (b, t, 0)),
                  pl.BlockSpec((tt, LANES), lambda b, t: (t, 0)),
                  pl.BlockSpec((tt, LANES), lambda b, t: (t, 0))],
        out_specs=[out(n_q), out(n_kv), out(n_kv)],
        out_shape=[jax.ShapeDtypeStruct((B, n_q, T, HEAD), BF16),
                   jax.ShapeDtypeStruct((B, n_kv, T, HEAD), BF16),
                   jax.ShapeDtypeStruct((B, n_kv, T, HEAD), BF16)],
        compiler_params=_params("parallel", "parallel"),
        name="rope",
    )(qkv, cos, sin)


def _attn_kernel(*refs, n_q, n_kv, local, q_off, n_blocks):
    if local:
        q_ref, kp_ref, kc_ref, kn_ref, vp_ref, vc_ref, vn_ref, kx_ref, vx_ref, sink_ref, o_ref = refs
    else:
        q_ref, kx_ref, vx_ref, sink_ref, o_ref = refs
    G = n_q // n_kv
    R = G * AT_BLOCK
    n = pl.program_id(1)
    nt = (((1,), (1,)), ((), ()))
    if local:
        qi = lax.broadcasted_iota(jnp.int32, (R, 3 * AT_BLOCK), 0) % AT_BLOCK
        kj = lax.broadcasted_iota(jnp.int32, (R, 3 * AT_BLOCK), 1) - AT_BLOCK
        kpos = kj + n * AT_BLOCK
        ok = (jnp.abs(qi - kj) <= AT_WINDOW) & (kpos >= 0) & (kpos < n_blocks * AT_BLOCK)
    H = range(n_kv)
    q = [q_ref[0, h * G:(h + 1) * G].reshape(R, HEAD) for h in H]
    sink = [jnp.concatenate(
        [jnp.broadcast_to(sink_ref[h * G + g:h * G + g + 1, 0:1], (AT_BLOCK, 1)) for g in range(G)],
        axis=0) for h in H]
    s_ctx = [lax.dot_general(q[h], kx_ref[0, h], nt, preferred_element_type=F32) for h in H]
    m = [jnp.maximum(jnp.max(s_ctx[h], axis=-1, keepdims=True), sink[h]) for h in H]
    if local:
        k_loc = [jnp.concatenate([kp_ref[0, h], kc_ref[0, h], kn_ref[0, h]], axis=0) for h in H]
        v_loc = [jnp.concatenate([vp_ref[0, h], vc_ref[0, h], vn_ref[0, h]], axis=0) for h in H]
        s_loc = [jnp.where(ok, lax.dot_general(q[h], k_loc[h], nt, preferred_element_type=F32), NEG_INF)
                 for h in H]
        m = [jnp.maximum(m[h], jnp.max(s_loc[h], axis=-1, keepdims=True)) for h in H]
    p_ctx = [jnp.exp(s_ctx[h] - m[h]) for h in H]
    den = [jnp.sum(p_ctx[h], axis=-1, keepdims=True) + jnp.exp(sink[h] - m[h]) for h in H]
    o = [jnp.dot(p_ctx[h].astype(BF16), vx_ref[0, h], preferred_element_type=F32) for h in H]
    if local:
        p_loc = [jnp.exp(s_loc[h] - m[h]) for h in H]
        den = [den[h] + jnp.sum(p_loc[h], axis=-1, keepdims=True) for h in H]
        o = [o[h] + jnp.dot(p_loc[h].astype(BF16), v_loc[h], preferred_element_type=F32) for h in H]
    o = [o[h] * (1.0 / den[h]) for h in H]
    for h in H:
        for g in range(G):
            hh = h * G + g
            o_ref[0, :, hh * HEAD:(hh + 1) * HEAD] = o[h][g * AT_BLOCK:(g + 1) * AT_BLOCK].astype(o_ref.dtype)


def _attn(q, k, v, sink, *, Lc, local):
    B, n_q, T, _ = q.shape
    n_kv = k.shape[1]
    cb = Lc // AT_BLOCK
    nb = (T - Lc) // AT_BLOCK if local else cb
    q_off = cb if local else 0
    last = T // AT_BLOCK - 1
    qspec = pl.BlockSpec((1, n_q, AT_BLOCK, HEAD), lambda b, n: (b, 0, n + q_off, 0))
    xspec = pl.BlockSpec((1, n_kv, Lc, HEAD), lambda b, n: (b, 0, 0, 0))
    sspec = pl.BlockSpec((n_q, LANES), lambda b, n: (0, 0))
    sink_b = jnp.broadcast_to(sink.astype(F32)[:, None], (n_q, LANES))
    if local:
        blk = lambda f: pl.BlockSpec((1, n_kv, AT_BLOCK, HEAD), lambda b, n: (b, 0, f(n), 0))
        prev = blk(lambda n: jnp.maximum(n + cb - 1, cb))
        cur = blk(lambda n: n + cb)
        nxt = blk(lambda n: jnp.minimum(n + cb + 1, last))
        in_specs = [qspec, prev, cur, nxt, prev, cur, nxt, xspec, xspec, sspec]
        args = (q, k, k, k, v, v, v, k, v, sink_b)
    else:
        in_specs = [qspec, xspec, xspec, sspec]
        args = (q, k, v, sink_b)
    return pl.pallas_call(
        functools.partial(_attn_kernel, n_q=n_q, n_kv=n_kv, local=local, q_off=q_off, n_blocks=nb),
        grid=(B, nb),
        in_specs=in_specs,
        out_specs=pl.BlockSpec((1, AT_BLOCK, n_q * HEAD), lambda b, n: (b, n, 0)),
        out_shape=jax.ShapeDtypeStruct((B, nb * AT_BLOCK, n_q * HEAD), BF16),
        compiler_params=_params("parallel", "parallel"),
        name="attn_local" if local else "attn_ctx",
    )(*args)


def _attention(h, p, *, Lc):
    B, T, D = h.shape
    M = B * T
    n_q = D // HEAD
    qkv = _mm(h.reshape(M, D), p["w_qkv"], bias=p["b_qkv"]).reshape(B, T, -1)
    cos, sin = _rope_tables(T, Lc)
    q, k, v = _rope(qkv, cos, sin, n_q=n_q, n_kv=AT_KV_HEADS)
    o_ctx = _attn(q, k, v, p["sink"], Lc=Lc, local=False)
    o_lat = _attn(q, k, v, p["sink"], Lc=Lc, local=True)
    o = jnp.concatenate([o_ctx, o_lat], axis=1)
    return _mm(o.reshape(M, D), p["w_o"], bias=p["b_o"]).reshape(B, T, D)


def _swiglu_ffn(h, w1, w3, w2):
    mid = _mm(h, w1, w3=w3, out_dtype=BF16)
    return _mm(mid, w2)


MOE_TILE = 1024
ROUTE_G1, ROUTE_G2, ROUTE_I1, ROUTE_I2 = 0, 1, 2, 3


def _gates_kernel(l_ref, route_ref, sel_ref, *, n_experts):
    l = l_ref[...]
    lane = lax.broadcasted_iota(jnp.int32, l.shape, 1)
    l = jnp.where(lane < n_experts, l, -jnp.inf)
    m1 = jnp.max(l, axis=-1, keepdims=True)
    i1 = jnp.min(jnp.where(l == m1, lane, LANES), axis=-1, keepdims=True)
    l2 = jnp.where(lane == i1, -jnp.inf, l)
    m2 = jnp.max(l2, axis=-1, keepdims=True)
    i2 = jnp.min(jnp.where(l2 == m2, lane, LANES), axis=-1, keepdims=True)
    e2 = jnp.exp(m2 - m1)
    den = 1.0 + e2
    route_ref[...] = jnp.where(lane == ROUTE_G1, 1.0 / den,
                               jnp.where(lane == ROUTE_G2, e2 / den,
                                         jnp.where(lane == ROUTE_I1, i1.astype(F32),
                                                   jnp.where(lane == ROUTE_I2, i2.astype(F32), 0.0))))
    sel_ref[...] = jnp.where((lane == i1) | (lane == i2), 1.0, 0.0).astype(sel_ref.dtype)


def _gates(logits, n_experts):
    M = logits.shape[0]
    tm = _tile(M, 1024, 8)
    blk = pl.BlockSpec((tm, LANES), lambda i: (i, 0))
    return pl.pallas_call(
        functools.partial(_gates_kernel, n_experts=n_experts),
        grid=(M // tm,),
        in_specs=[blk],
        out_specs=[blk, blk],
        out_shape=[jax.ShapeDtypeStruct((M, LANES), F32), jax.ShapeDtypeStruct((M, LANES), BF16)],
        compiler_params=_params("parallel"),
        name="moe_gates",
    )(logits)


def _rank_kernel(sel_ref, rank_ref, cnt_ref, carry_ref):
    @pl.when(pl.program_id(0) == 0)
    def _():
        carry_ref[...] = jnp.zeros_like(carry_ref)

    s = sel_ref[...]
    n = s.shape[0]
    earlier = (lax.broadcasted_iota(jnp.int32, (n, n), 1) < lax.broadcasted_iota(jnp.int32, (n, n), 0))
    within = jnp.dot(jnp.where(earlier, 1.0, 0.0).astype(BF16), s, preferred_element_type=F32)
    rank_ref[...] = within + carry_ref[...]
    carry_ref[...] += jnp.sum(s.astype(F32), axis=0, keepdims=True)
    cnt_ref[...] = carry_ref[...]


def _rank(sel):
    M = sel.shape[0]
    tr = _tile(M, 512, 8)
    return pl.pallas_call(
        _rank_kernel,
        grid=(M // tr,),
        in_specs=[pl.BlockSpec((tr, LANES), lambda i: (i, 0))],
        out_specs=[pl.BlockSpec((tr, LANES), lambda i: (i, 0)), pl.BlockSpec((1, LANES), lambda i: (0, 0))],
        out_shape=[jax.ShapeDtypeStruct((M, LANES), F32), jax.ShapeDtypeStruct((1, LANES), F32)],
        scratch_shapes=[pltpu.VMEM((1, LANES), F32)],
        compiler_params=_params("arbitrary"),
        name="moe_rank",
    )(sel)


def _pos_kernel(route_ref, rank_ref, offs_ref, pos_ref):
    lane = lax.broadcasted_iota(jnp.int32, rank_ref.shape, 1)
    lane_f = lane.astype(F32)
    tot = rank_ref[...] + offs_ref[...]
    route = route_ref[...]
    p1 = jnp.sum(jnp.where(lane_f == route[:, ROUTE_I1:ROUTE_I1 + 1], tot, 0.0), axis=-1, keepdims=True)
    p2 = jnp.sum(jnp.where(lane_f == route[:, ROUTE_I2:ROUTE_I2 + 1], tot, 0.0), axis=-1, keepdims=True)
    pos_ref[...] = jnp.where(lane == 0, p1, jnp.where(lane == 1, p2, 0.0)).astype(jnp.int32)


def _positions(route, rank, offs):
    M = route.shape[0]
    tm = _tile(M, 1024, 8)
    blk = pl.BlockSpec((tm, LANES), lambda i: (i, 0))
    return pl.pallas_call(
        _pos_kernel,
        grid=(M // tm,),
        in_specs=[blk, blk, pl.BlockSpec((1, LANES), lambda i: (0, 0))],
        out_specs=blk,
        out_shape=jax.ShapeDtypeStruct((M, LANES), jnp.int32),
        compiler_params=_params("parallel"),
        name="moe_pos",
    )(route, rank, offs)


def _row_copy(src, dst, sem):
    return pltpu.make_async_copy(src, dst, sem)


def _dispatch_kernel(p1_ref, p2_ref, h_ref, xs_in_ref, xs_ref, sem):
    del xs_in_ref
    tt = h_ref.shape[0]
    base = pl.program_id(0) * tt

    def start(r, carry):
        row = h_ref.at[pl.ds(r, 1)]
        _row_copy(row, xs_ref.at[pl.ds(p1_ref[base + r], 1)], sem).start()
        _row_copy(row, xs_ref.at[pl.ds(p2_ref[base + r], 1)], sem).start()
        return carry

    lax.fori_loop(0, tt, start, 0, unroll=8)
    for _ in range(2):
        _row_copy(h_ref, xs_ref.at[pl.ds(0, tt)], sem).wait()


def _dispatch(h, p1, p2, n_rows):
    M, D = h.shape
    tt = _tile(M, ROW_TILE, 8)
    return pl.pallas_call(
        _dispatch_kernel,
        grid_spec=pltpu.PrefetchScalarGridSpec(
            num_scalar_prefetch=2,
            grid=(M // tt,),
            in_specs=[pl.BlockSpec((tt, D), lambda i, p1, p2: (i, 0)),
                      pl.BlockSpec(memory_space=pl.ANY)],
            out_specs=pl.BlockSpec(memory_space=pl.ANY),
            scratch_shapes=[pltpu.SemaphoreType.DMA(())]),
        out_shape=jax.ShapeDtypeStruct((n_rows, D), h.dtype),
        input_output_aliases={3: 0},
        compiler_params=_params("arbitrary"),
        name="moe_dispatch",
    )(p1, p2, h, jnp.zeros((n_rows, D), h.dtype))


def _gmm_kernel(te_ref, nv_ref, x_ref, *refs, swiglu):
    o_ref = refs[-1]

    @pl.when(pl.program_id(0) < nv_ref[0])
    def _():
        x = x_ref[...].astype(BF16)
        acc = jnp.dot(x, refs[0][0], preferred_element_type=F32)
        if swiglu:
            acc = acc * jax.nn.sigmoid(acc) * jnp.dot(x, refs[1][0], preferred_element_type=F32)
        o_ref[...] = acc.astype(o_ref.dtype)

    @pl.when(pl.program_id(0) >= nv_ref[0])
    def _():
        o_ref[...] = jnp.zeros_like(o_ref)


def _gmm(x, ws, tile_expert, n_valid, *, out_dtype, tn=512):
    P, K = x.shape
    N = ws[0].shape[2]
    tm = MOE_TILE
    tn = _tile(N, tn)
    wspec = pl.BlockSpec((1, K, tn), lambda i, j, te, nv: (te[i], 0, j))
    return pl.pallas_call(
        functools.partial(_gmm_kernel, swiglu=len(ws) == 2),
        grid_spec=pltpu.PrefetchScalarGridSpec(
            num_scalar_prefetch=2,
            grid=(P // tm, N // tn),
            in_specs=[pl.BlockSpec((tm, K), lambda i, j, te, nv: (i, 0))] + [wspec] * len(ws),
            out_specs=pl.BlockSpec((tm, tn), lambda i, j, te, nv: (i, j))),
        out_shape=jax.ShapeDtypeStruct((P, N), out_dtype),
        compiler_params=_params("parallel", "parallel"),
        name="moe_gmm",
    )(tile_expert, n_valid, x, *ws)


def _combine_kernel(p1_ref, p2_ref, route_ref, ys_ref, o_ref, a_ref, b_ref, sem):
    tt = o_ref.shape[0]
    base = pl.program_id(0) * tt

    def start(r, carry):
        _row_copy(ys_ref.at[pl.ds(p1_ref[base + r], 1)], a_ref.at[pl.ds(r, 1)], sem).start()
        _row_copy(ys_ref.at[pl.ds(p2_ref[base + r], 1)], b_ref.at[pl.ds(r, 1)], sem).start()
        return carry

    lax.fori_loop(0, tt, start, 0, unroll=8)
    _row_copy(ys_ref.at[pl.ds(0, tt)], a_ref, sem).wait()
    _row_copy(ys_ref.at[pl.ds(0, tt)], b_ref, sem).wait()
    route = route_ref[...]
    o_ref[...] = (route[:, ROUTE_G1:ROUTE_G1 + 1] * a_ref[...]
                  + route[:, ROUTE_G2:ROUTE_G2 + 1] * b_ref[...])


def _combine(ys, route, p1, p2):
    M = route.shape[0]
    D = ys.shape[1]
    tt = _tile(M, ROW_TILE, 8)
    return pl.pallas_call(
        _combine_kernel,
        grid_spec=pltpu.PrefetchScalarGridSpec(
            num_scalar_prefetch=2,
            grid=(M // tt,),
            in_specs=[pl.BlockSpec((tt, LANES), lambda i, p1, p2: (i, 0)),
                      pl.BlockSpec(memory_space=pl.ANY)],
            out_specs=pl.BlockSpec((tt, D), lambda i, p1, p2: (i, 0)),
            scratch_shapes=[pltpu.VMEM((tt, D), F32), pltpu.VMEM((tt, D), F32),
                            pltpu.SemaphoreType.DMA(())]),
        out_shape=jax.ShapeDtypeStruct((M, D), F32),
        compiler_params=_params("arbitrary"),
        name="moe_combine",
    )(p1, p2, route, ys)


def _moe(h, logits, w1, w3, w2):
    M, D = h.shape
    E = w1.shape[0]
    tm = MOE_TILE
    route, sel = _gates(logits, E)
    rank, cnt = _rank(sel)
    counts = cnt[0, :E].astype(jnp.int32)
    padded = (counts + tm - 1) // tm * tm
    ends = jnp.cumsum(padded)
    n_tiles = (TOP_K * M) // tm + E
    tile_expert = jnp.minimum(
        jnp.searchsorted(ends, jnp.arange(n_tiles, dtype=jnp.int32) * tm, side="right"), E - 1
    ).astype(jnp.int32)
    n_valid = (ends[-1:] // tm).astype(jnp.int32)
    offs = jnp.zeros((1, LANES), F32).at[0, :E].set((ends - padded).astype(F32))
    pos = _positions(route, rank, offs)
    p1, p2 = pos[:, 0], pos[:, 1]
    xs = _dispatch(h, p1, p2, n_tiles * tm)
    mid = _gmm(xs, [w1, w3], tile_expert, n_valid, out_dtype=BF16)
    ys = _gmm(mid, [w2], tile_expert, n_valid, out_dtype=F32)
    return _combine(ys, route, p1, p2)


def kernel(x, c, ctx, c_ctx, ada_w, ada_b, ln_w, ln_b, hy_w_in, hy_b_in, hy_conv_w, hy_conv_b, hy_f_w1, hy_f_b1, hy_f_w2, hy_f_b2, hy_f_w3, hy_f_freq, hy_skip, hy_w_out, hy_b_out, rw_mu, rw_w_rkv, rw_w_o, rw_w0, rw_w1, rw_w2, rw_a0, rw_a1, rw_a2, rw_g1, rw_g2, rw_k_k, rw_k_a, rw_r_k, rw_gn_w, rw_gn_b, at_w_qkv, at_b_qkv, at_w_o, at_b_o, at_sink, ff_w1, ff_w3, ff_w2, moe_router, moe_w1, moe_w3, moe_w2):
    B, L, D = x.shape
    Lc = ctx.shape[1]
    depth = ada_w.shape[0]
    alpha = (2 * depth) ** 0.25
    assert Lc % ROW_TILE == 0 and L % ROW_TILE == 0 and D % (RW_GROUP * HEAD) == 0
    bf = lambda t: t.astype(BF16)

    ctx_row = B
    rows = -(-(B + 1) // 8) * 8
    cond = jnp.zeros((rows, D), F32).at[:B].set(c).at[B].set(c_ctx)
    mods = _ada(cond, ada_w, ada_b).reshape(depth, rows, N_MOD, D)

    tables = {}
    if depth > 0:
        tables[L] = _dft_tables(L)
        if depth > 1:
            tables[Lc] = _dft_tables(Lc)

    xs = jnp.concatenate([ctx, x], axis=1)
    ctx_tiles = Lc // ROW_TILE
    h = _modulate(xs, mods[0], sh=0, sc=1, ctx_tiles=ctx_tiles, ctx_row=ctx_row,
                  out_dtype=BF16)
    for i in range(depth):
        last = i == depth - 1
        kind = i % 3
        j = i // 3
        mod = mods[i]
        has_ctx = xs.shape[1] != L
        ct = ctx_tiles if has_ctx else 0
        T = xs.shape[1]
        if kind == 0:
            p = dict(w_in=bf(hy_w_in[j]), b_in=hy_b_in[j], conv_w=hy_conv_w[j], conv_b=hy_conv_b[j],
                     filter=(hy_f_w1[j], hy_f_b1[j], hy_f_w2[j], hy_f_b2[j], hy_f_w3[j], hy_f_freq[j]),
                     skip=hy_skip[j], w_out=bf(hy_w_out[j]), b_out=hy_b_out[j])
            y = _hyena(h, p, tables, Lc=Lc, has_ctx=has_ctx)
        elif kind == 1:
            assert has_ctx and not last
            p = dict(mu=rw_mu[j], w_rkv=bf(rw_w_rkv[j]), w_o=bf(rw_w_o[j]), w0=rw_w0[j], w1=rw_w1[j],
                     w2=rw_w2[j], a0=rw_a0[j], a1=rw_a1[j], a2=rw_a2[j], g1=rw_g1[j], g2=rw_g2[j],
                     k_k=rw_k_k[j], k_a=rw_k_a[j], r_k=rw_r_k[j], gn_w=rw_gn_w[j], gn_b=rw_gn_b[j])
            y = _rwkv(h, p, Lc=Lc)
        else:
            assert has_ctx and not last
            p = dict(w_qkv=bf(at_w_qkv[j]), b_qkv=at_b_qkv[j], w_o=bf(at_w_o[j]), b_o=at_b_o[j],
                     sink=at_sink[j])
            y = _attention(h, p, Lc=Lc)

        drop1 = ct if last else 0
        fj = i // 2
        moe = i % 2 == 1
        res = _ln(xs, y, mod, ln_w[i, 0], ln_b[i, 0], alpha=alpha, gate=2, h_mod=(3, 4),
                  h_dtype=F32 if moe else BF16, router=moe_router[fj] if moe else None,
                  ctx_tiles=ct, ctx_row=ctx_row, drop_tiles=drop1)
        xs, h2 = res[0], res[1]
        ct = ct - drop1
        T = xs.shape[1]
        M = B * T
        if moe:
            f = _moe(h2.reshape(M, D), res[2].reshape(M, LANES), bf(moe_w1[fj]), bf(moe_w3[fj]),
                     bf(moe_w2[fj]))
        else:
            f = _swiglu_ffn(h2.reshape(M, D), bf(ff_w1[fj]), bf(ff_w3[fj]), bf(ff_w2[fj]))
        f = f.reshape(B, T, D)
        if last:
            (xs,) = _ln(xs, f, mod, ln_w[i, 1], ln_b[i, 1], alpha=alpha, gate=5,
                        ctx_tiles=ct, ctx_row=ctx_row)
        else:
            nxt_last = i + 1 == depth - 1
            nkind = (i + 1) % 3
            drop2 = ct if (nxt_last and nkind == 0) else 0
            xs, h = _ln(xs, f, mod, ln_w[i, 1], ln_b[i, 1], alpha=alpha, gate=5, h_mod=(0, 1),
                        mod_h=mods[i + 1], h_dtype=F32 if nkind == 1 else BF16,
                        ctx_tiles=ct, ctx_row=ctx_row, drop_tiles=drop2)
    return xs
```

```python
import functools
import math

import jax
import jax.numpy as jnp
from jax import lax
from jax.experimental import pallas as pl
from jax.experimental.pallas import tpu as pltpu

F32 = jnp.float32
BF16 = jnp.bfloat16
HIGHEST = lax.Precision.HIGHEST

VMEM_LIMIT_BYTES = 56 * 1024 * 1024
LANES = 128
ROW_TILE = 256
BF16_ROWS = 16

LN_EPS = 1e-5
N_MOD = 6
HEAD = 64
RW_GN_EPS = 64e-5
RW_CHUNK = 64
RW_GROUP = 4
AT_KV_HEADS = 4
AT_WINDOW = 128
AT_BLOCK = 128
GRID_W = 64
ROPE_BASE = 10000.0
NEG_INF = -1e30
TOP_K = 2
HY_BANDS = 16
HY_EMB = 2 * HY_BANDS + 1
HY_MIN_DECAY = -math.log(1e-2) / 1.5
HY_MAX_DECAY = -math.log(1e-2) / 0.3


def _params(*sem):
    return pltpu.CompilerParams(dimension_semantics=sem, vmem_limit_bytes=VMEM_LIMIT_BYTES)


def _tile(n, pref, mult=LANES):
    if n <= pref:
        return n
    t = (pref // mult) * mult
    while t >= mult:
        if n % t == 0:
            return t
        t -= mult
    return n


def _mm_kernel(*refs, n_w, has_bias, act):
    x = refs[0][...].astype(BF16)
    o_ref = refs[-1]
    acc = jnp.dot(x, refs[1][...].astype(BF16), preferred_element_type=F32)
    if has_bias:
        acc = acc + refs[1 + n_w][...]
    if act == "swiglu":
        acc3 = jnp.dot(x, refs[2][...].astype(BF16), preferred_element_type=F32)
        acc = acc * jax.nn.sigmoid(acc) * acc3
    o_ref[...] = acc.astype(o_ref.dtype)


def _mm(x, w, *, w3=None, bias=None, out_dtype=F32, tm=1024, tn=512):
    M, K = x.shape
    N = w.shape[1]
    tm = _tile(M, tm, 8)
    tn = _tile(N, tn)
    ws = [w] if w3 is None else [w, w3]
    in_specs = [pl.BlockSpec((tm, K), lambda i, j: (i, 0))]
    in_specs += [pl.BlockSpec((K, tn), lambda i, j: (0, j)) for _ in ws]
    args = [x] + ws
    if bias is not None:
        in_specs.append(pl.BlockSpec((1, tn), lambda i, j: (0, j)))
        args.append(bias.reshape(1, N).astype(F32))
    return pl.pallas_call(
        functools.partial(_mm_kernel, n_w=len(ws), has_bias=bias is not None,
                          act="swiglu" if w3 is not None else None),
        grid=(M // tm, N // tn),
        in_specs=in_specs,
        out_specs=pl.BlockSpec((tm, tn), lambda i, j: (i, j)),
        out_shape=jax.ShapeDtypeStruct((M, N), out_dtype),
        compiler_params=_params("parallel", "parallel"),
        name="mm_swiglu" if w3 is not None else "mm",
    )(*args)


def _ada_kernel(c_ref, w_ref, b_ref, o_ref):
    c = c_ref[...]
    s = (c * jax.nn.sigmoid(c)).astype(BF16)
    o_ref[0] = jnp.dot(s, w_ref[0].astype(BF16), preferred_element_type=F32) + b_ref[0]


def _ada(cond, ada_w, ada_b):
    depth, D, N = ada_w.shape
    R = cond.shape[0]
    tn = _tile(N, 1024)
    return pl.pallas_call(
        _ada_kernel,
        grid=(depth, N // tn),
        in_specs=[pl.BlockSpec((R, D), lambda i, j: (0, 0)),
                  pl.BlockSpec((1, D, tn), lambda i, j: (i, 0, j)),
                  pl.BlockSpec((1, 1, tn), lambda i, j: (i, 0, j))],
        out_specs=pl.BlockSpec((1, R, tn), lambda i, j: (i, 0, j)),
        out_shape=jax.ShapeDtypeStruct((depth, R, N), F32),
        compiler_params=_params("parallel", "parallel"),
        name="ada",
    )(cond, ada_w, ada_b.reshape(depth, 1, N))


def _mod_index(ctx_tiles, ctx_row, off):
    def index(b, t):
        return (jnp.where(t + off < ctx_tiles, ctx_row, b), 0, 0)
    return index


def _modulate_kernel(x_ref, mod_ref, h_ref, *, sh, sc):
    x = x_ref[0]
    h_ref[0] = (x * (1.0 + mod_ref[0, sc:sc + 1, :]) + mod_ref[0, sh:sh + 1, :]).astype(h_ref.dtype)


def _modulate(x, mod, *, sh, sc, ctx_tiles, ctx_row, out_dtype):
    B, T, D = x.shape
    return pl.pallas_call(
        functools.partial(_modulate_kernel, sh=sh, sc=sc),
        grid=(B, T // ROW_TILE),
        in_specs=[pl.BlockSpec((1, ROW_TILE, D), lambda b, t: (b, t, 0)),
                  pl.BlockSpec((1, N_MOD, D), _mod_index(ctx_tiles, ctx_row, 0))],
        out_specs=pl.BlockSpec((1, ROW_TILE, D), lambda b, t: (b, t, 0)),
        out_shape=jax.ShapeDtypeStruct((B, T, D), out_dtype),
        compiler_params=_params("parallel", "parallel"),
        name="modulate",
    )(x, mod)


def _ln_kernel(*refs, alpha, gate, sh, sc, has_h, has_router, has_proj):
    x_ref, y_ref, mod_ref, modh_ref, w_ref, b_ref = refs[:6]
    pos = 6
    if has_proj:
        pw_ref, pb_ref = refs[pos:pos + 2]
        pos += 2
    router_ref = None
    if has_router:
        router_ref = refs[pos]
        pos += 1
    xo_ref = refs[pos]
    pos += 1
    if has_proj:
        y = jnp.dot(y_ref[0], pw_ref[...], preferred_element_type=F32) + pb_ref[...]
    else:
        y = y_ref[0].astype(F32)
    z = alpha * x_ref[0] + mod_ref[0, gate:gate + 1, :] * y
    mu = jnp.mean(z, axis=-1, keepdims=True)
    d = z - mu
    var = jnp.mean(d * d, axis=-1, keepdims=True)
    xn = d * lax.rsqrt(var + LN_EPS) * w_ref[...] + b_ref[...]
    xo_ref[0] = xn
    if has_h:
        h = xn * (1.0 + modh_ref[0, sc:sc + 1, :]) + modh_ref[0, sh:sh + 1, :]
        h_ref = refs[pos]
        pos += 1
        h_ref[0] = h.astype(h_ref.dtype)
        if has_router:
            refs[pos][0] = jnp.dot(h, router_ref[...], precision=HIGHEST, preferred_element_type=F32)


def _ln(x, y, mod, ln_w, ln_b, *, alpha, gate, proj=None, h_mod=None, mod_h=None, h_dtype=BF16,
        router=None, ctx_tiles, ctx_row, drop_tiles=0):
    B, T, D = x.shape
    K = y.shape[2]
    nt = T // ROW_TILE - drop_tiles
    To = nt * ROW_TILE
    off = drop_tiles
    in_specs = [pl.BlockSpec((1, ROW_TILE, D), lambda b, t: (b, t + off, 0)),
                pl.BlockSpec((1, ROW_TILE, K), lambda b, t: (b, t + off, 0)),
                pl.BlockSpec((1, N_MOD, D), _mod_index(ctx_tiles, ctx_row, off)),
                pl.BlockSpec((1, N_MOD, D), _mod_index(ctx_tiles, ctx_row, off)),
                pl.BlockSpec((1, D), lambda b, t: (0, 0)),
                pl.BlockSpec((1, D), lambda b, t: (0, 0))]
    args = [x, y, mod, mod if mod_h is None else mod_h, ln_w.reshape(1, D), ln_b.reshape(1, D)]
    if proj is not None:
        pw, pb = proj
        in_specs += [pl.BlockSpec((K, D), lambda b, t: (0, 0), pipeline_mode=pl.Buffered(1)),
                     pl.BlockSpec((1, D), lambda b, t: (0, 0))]
        args += [pw, jnp.zeros((1, D), F32) if pb is None else pb.reshape(1, D).astype(F32)]
    out_specs = [pl.BlockSpec((1, ROW_TILE, D), lambda b, t: (b, t, 0))]
    out_shape = [jax.ShapeDtypeStruct((B, To, D), F32)]
    sh = sc = 0
    if h_mod is not None:
        sh, sc = h_mod
        out_specs.append(pl.BlockSpec((1, ROW_TILE, D), lambda b, t: (b, t, 0)))
        out_shape.append(jax.ShapeDtypeStruct((B, To, D), h_dtype))
    if router is not None:
        E = router.shape[1]
        router_p = jnp.zeros((D, LANES), F32).at[:, :E].set(router)
        in_specs.append(pl.BlockSpec((D, LANES), lambda b, t: (0, 0)))
        args.append(router_p)
        out_specs.append(pl.BlockSpec((1, ROW_TILE, LANES), lambda b, t: (b, t, 0)))
        out_shape.append(jax.ShapeDtypeStruct((B, To, LANES), F32))
    return pl.pallas_call(
        functools.partial(_ln_kernel, alpha=alpha, gate=gate, sh=sh, sc=sc, has_h=h_mod is not None,
                          has_router=router is not None, has_proj=proj is not None),
        grid=(B, nt),
        in_specs=in_specs,
        out_specs=out_specs,
        out_shape=out_shape,
        compiler_params=_params("parallel", "parallel"),
        name="ln_residual",
    )(*args)


def _halo_specs(T, C, col):
    r8 = ROW_TILE // 8
    last8 = T // 8 - 1
    return [pl.BlockSpec((1, ROW_TILE, C), lambda b, t, j: (b, t, col(j))),
            pl.BlockSpec((1, 8, C), lambda b, t, j: (b, jnp.maximum(t * r8 - 1, 0), col(j))),
            pl.BlockSpec((1, 8, C), lambda b, t, j: (b, jnp.minimum(t * r8 + r8, last8), col(j)))]


def _neighbours(cur, prev8, next8, t, n_tiles, ctx_tiles):
    rows = lax.broadcasted_iota(jnp.int32, cur.shape, 0)
    has_prev = jnp.logical_and(t != 0, t != ctx_tiles)
    has_next = jnp.logical_and(t != n_tiles - 1, t != ctx_tiles - 1)
    top = jnp.where(has_prev, prev8[7:8, :], 0.0)
    bot = jnp.where(has_next, next8[0:1, :], 0.0)
    up = jnp.where(rows == 0, top, pltpu.roll(cur, 1, axis=0))
    dn = jnp.where(rows == cur.shape[0] - 1, bot, pltpu.roll(cur, cur.shape[0] - 1, axis=0))
    return up, dn


def _hy_in_kernel(x_ref, xp_ref, xn_ref, *refs, tm, T, Lc):
    ws, bs, cws, cbs = refs[0:3], refs[3:6], refs[6:9], refs[9:12]
    x0_ref, vv_ref, vvb_ref = refs[12:15]
    H = BF16_ROWS
    x = jnp.concatenate([xp_ref[0], x_ref[0], xn_ref[0]], axis=0)
    row = lax.broadcasted_iota(jnp.int32, (tm, 1), 0) + pl.program_id(1) * tm
    no_prev = (row == 0) | (row == Lc)
    no_next = (row == Lc - 1) | (row == T - 1)
    out = []
    for s in range(3):
        acc = jnp.dot(x, ws[s][...], preferred_element_type=F32) + bs[s][...]
        n = acc.shape[0]
        cur = acc[H:H + tm]
        up = jnp.where(no_prev, 0.0, pltpu.roll(acc, 1, axis=0)[H:H + tm])
        dn = jnp.where(no_next, 0.0, pltpu.roll(acc, n - 1, axis=0)[H:H + tm])
        cw = cws[s]
        out.append(up * cw[0:1, :] + cur * cw[1:2, :] + dn * cw[2:3, :] + cbs[s][...])
    x0_ref[0] = out[0]
    vv = out[1] * out[2]
    vv_ref[0] = vv
    vvb_ref[0] = vv.astype(BF16)


def _hy_in(h, w_in, b_in, conv_w, conv_b, *, Lc):
    B, T, K = h.shape
    D = w_in.shape[1] // 3
    H = BF16_ROWS
    tm = _tile(T, 1280, 64)
    tc = _tile(D, 512)
    nj = D // tc
    r16 = tm // H
    last16 = T // H - 1
    sec = lambda shape: [pl.BlockSpec(shape, lambda b, t, j, s=s: (0, s * nj + j)) for s in range(3)]
    in_specs = [pl.BlockSpec((1, tm, K), lambda b, t, j: (b, t, 0)),
                pl.BlockSpec((1, H, K), lambda b, t, j: (b, jnp.maximum(t * r16 - 1, 0), 0)),
                pl.BlockSpec((1, H, K), lambda b, t, j: (b, jnp.minimum(t * r16 + r16, last16), 0))]
    in_specs += sec((K, tc)) + sec((1, tc)) + sec((3, tc)) + sec((1, tc))
    blk = pl.BlockSpec((1, tm, tc), lambda b, t, j: (b, t, j))
    b2 = b_in.reshape(1, 3 * D)
    cb2 = conv_b.reshape(1, 3 * D)
    return pl.pallas_call(
        functools.partial(_hy_in_kernel, tm=tm, T=T, Lc=Lc),
        grid=(B, T // tm, nj),
        in_specs=in_specs,
        out_specs=[blk, blk, blk],
        out_shape=[jax.ShapeDtypeStruct((B, T, D), F32), jax.ShapeDtypeStruct((B, T, D), F32),
                   jax.ShapeDtypeStruct((B, T, D), BF16)],
        compiler_params=_params("parallel", "parallel", "parallel"),
        name="hy_in",
    )(h, h, h, w_in, w_in, w_in, b2, b2, b2, conv_w, conv_w, conv_w, cb2, cb2, cb2)


def _hy_filter_kernel(w1_ref, b1_ref, w2_ref, b2_ref, w3_ref, fr_ref, h_ref, s_ref, *, L, D, tl):
    i = pl.program_id(0)
    row = (lax.broadcasted_iota(jnp.int32, (tl, LANES), 0) + i * tl).astype(F32)
    lane = lax.broadcasted_iota(jnp.int32, (tl, LANES), 1)
    band = jnp.where(lane <= HY_BANDS, lane - 1, lane - 1 - HY_BANDS).astype(F32)
    freq = 1e-4 + band * ((HY_BANDS - 1 - 1e-4) / (HY_BANDS - 1))
    ang = freq * (row * (2.0 * math.pi / L))
    z = jnp.where(lane == 0, row / (L - 1),
                  jnp.where(lane <= HY_BANDS, jnp.cos(ang),
                            jnp.where(lane < HY_EMB, -jnp.sin(ang), 0.0)))
    h = jnp.sin(fr_ref[0:1, :] * (jnp.dot(z, w1_ref[...], precision=HIGHEST,
                                           preferred_element_type=F32) + b1_ref[...]))
    h = jnp.sin(fr_ref[1:2, :] * (jnp.dot(h, w2_ref[...], precision=HIGHEST,
                                           preferred_element_type=F32) + b2_ref[...]))
    h = jnp.dot(h, w3_ref[...], precision=HIGHEST, preferred_element_type=F32)
    half = L // 2
    rowd = (lax.broadcasted_iota(jnp.int32, (tl, D), 0) + i * tl).astype(F32)
    dist = jnp.abs(rowd - half) / half
    chan = lax.broadcasted_iota(jnp.int32, (tl, D), 1).astype(F32)
    deltas = HY_MIN_DECAY + chan * ((HY_MAX_DECAY - HY_MIN_DECAY) / (D - 1))
    h = h * jnp.exp(-dist * deltas)
    h_ref[...] = h

    @pl.when(i == 0)
    def _():
        s_ref[...] = jnp.zeros_like(s_ref)

    s_ref[...] += jnp.sum(jnp.abs(h), axis=0, keepdims=True)


def _hy_filter(L, f_w1, f_b1, f_w2, f_b2, f_w3, f_freq):
    D = f_w3.shape[1]
    hid = f_w1.shape[1]
    w1 = jnp.zeros((LANES, LANES), F32).at[:HY_EMB, :hid].set(f_w1)
    b1 = jnp.zeros((1, LANES), F32).at[0, :hid].set(f_b1)
    w2 = jnp.zeros((LANES, LANES), F32).at[:hid, :hid].set(f_w2)
    b2 = jnp.zeros((1, LANES), F32).at[0, :hid].set(f_b2)
    w3 = jnp.zeros((LANES, D), F32).at[:hid].set(f_w3)
    fr = jnp.zeros((2, LANES), F32).at[:, :hid].set(f_freq)
    tl = _tile(L, 256, 8)
    full = lambda shape: pl.BlockSpec(shape, lambda i: (0, 0))
    return pl.pallas_call(
        functools.partial(_hy_filter_kernel, L=L, D=D, tl=tl),
        grid=(L // tl,),
        in_specs=[full((LANES, LANES)), full((1, LANES)), full((LANES, LANES)), full((1, LANES)),
                  full((LANES, D)), full((2, LANES))],
        out_specs=[pl.BlockSpec((tl, D), lambda i: (i, 0)), full((1, D))],
        out_shape=[jax.ShapeDtypeStruct((L, D), F32), jax.ShapeDtypeStruct((1, D), F32)],
        compiler_params=_params("arbitrary"),
        name="hy_filter",
    )(w1, b1, w2, b2, w3, fr)


DFT_ROWS = 64


def _dft_tables_kernel(fa_ref, ia_ref, b_ref, fwd_ref, inv_ref, *, L):
    n = 2 * L
    i = pl.program_id(0)
    cb, sb = b_ref[0], b_ref[1]
    q = lax.broadcasted_iota(jnp.int32, (DFT_ROWS, L), 0)
    col = lax.broadcasted_iota(jnp.int32, (DFT_ROWS, L), 1)
    sign_c = jnp.where(col % 2 == 0, 1.0, -1.0)
    sign_r = jnp.where(q % 2 == 0, 1.0, -1.0)
    ca, sa = fa_ref[0, 0:1, :], fa_ref[0, 1:2, :]
    cos_f = ca * cb - sa * sb
    sin_f = sa * cb + ca * sb
    first_row = (q == 0) & (i == 0)
    fwd_ref[0] = cos_f.astype(fwd_ref.dtype)
    fwd_ref[1] = jnp.where(first_row, sign_c, -sin_f).astype(fwd_ref.dtype)
    ca, sa = ia_ref[0, 0:1, :], ia_ref[0, 1:2, :]
    cos_i = ca * cb - sa * sb
    sin_i = sa * cb + ca * sb
    inv_ref[:, :L] = jnp.where(col == 0, 1.0 / n, (2.0 / n) * cos_i).astype(inv_ref.dtype)
    inv_ref[:, L:] = jnp.where(col == 0, sign_r / n, (-2.0 / n) * sin_i).astype(inv_ref.dtype)


def _dft_tables(L):
    n = 2 * L
    R = DFT_ROWS
    assert L % (2 * R) == 0
    w = 2.0 * math.pi / n
    c = jnp.arange(L, dtype=jnp.int32)
    hi = jnp.arange(L // R, dtype=jnp.int32) * R

    def cos_sin(rows):
        ang = ((rows[:, None] * c[None, :]) % n).astype(F32) * w
        return jnp.stack([jnp.cos(ang), jnp.sin(ang)], axis=1)

    fa = cos_sin(hi)
    ia = cos_sin(hi + L // 2)
    b = jnp.swapaxes(cos_sin(jnp.arange(R, dtype=jnp.int32)), 0, 1)
    return pl.pallas_call(
        functools.partial(_dft_tables_kernel, L=L),
        grid=(L // R,),
        in_specs=[pl.BlockSpec((1, 2, L), lambda i: (i, 0, 0)),
                  pl.BlockSpec((1, 2, L), lambda i: (i, 0, 0)),
                  pl.BlockSpec((2, R, L), lambda i: (0, 0, 0))],
        out_specs=[pl.BlockSpec((2, R, L), lambda i: (0, i, 0)),
                   pl.BlockSpec((R, 2 * L), lambda i: (i, 0))],
        out_shape=[jax.ShapeDtypeStruct((2, L, L), BF16), jax.ShapeDtypeStruct((L, 2 * L), BF16)],
        compiler_params=_params("parallel"),
        name="dft_tables",
    )(fa, ia, b)


def _dft_fwd_kernel(a_ref, w_ref, *rest, mode):
    o_ref = rest[-1]
    w = w_ref[0]
    vre = jnp.dot(a_ref[0], w, preferred_element_type=F32)
    vim = jnp.dot(a_ref[1], w, preferred_element_type=F32)
    if mode == "scale":
        inv = 1.0 / (rest[0][...] + 1e-6)
        o_ref[0, 0] = vre * inv
        o_ref[0, 1] = vim * inv
    else:
        hre = rest[0][0]
        him = rest[0][1]
        first = jnp.logical_and(pl.program_id(2) == 0,
                                lax.broadcasted_iota(jnp.int32, vre.shape, 0) == 0)
        zre = jnp.where(first, vre * hre, vre * hre - vim * him)
        zim = jnp.where(first, vim * him, vre * him + vim * hre)
        o_ref[0, 0] = zre.astype(o_ref.dtype)
        o_ref[0, 1] = zim.astype(o_ref.dtype)


def _dft_fwd(fwd, w, extra, *, mode, out_dtype):
    B, L, D = w.shape
    tm = _tile(L, 256, 8)
    tn = _tile(D, 1024)
    if mode == "scale":
        extra_spec = pl.BlockSpec((1, tn), lambda b, j, m: (0, j))
    else:
        extra_spec = pl.BlockSpec((2, tm, tn), lambda b, j, m: (0, m, j))
    return pl.pallas_call(
        functools.partial(_dft_fwd_kernel, mode=mode),
        grid=(B, D // tn, L // tm),
        in_specs=[pl.BlockSpec((2, tm, L), lambda b, j, m: (0, m, 0)),
                  pl.BlockSpec((1, L, tn), lambda b, j, m: (b, 0, j)),
                  extra_spec],
        out_specs=pl.BlockSpec((1, 2, tm, tn), lambda b, j, m: (b, 0, m, j)),
        out_shape=jax.ShapeDtypeStruct((B, 2, L, D), out_dtype),
        compiler_params=_params("parallel", "parallel", "parallel"),
        name="dft_fwd",
    )(fwd, w, extra)


def _dft_inv_kernel(b_ref, z_ref, x0_ref, vv_ref, skip_ref, u_ref):
    y = jnp.dot(b_ref[...], z_ref[0], preferred_element_type=F32)
    u_ref[0] = (x0_ref[0] * (y + vv_ref[0] * skip_ref[...])).astype(u_ref.dtype)


def _dft_inv(inv, z, x0, vv, skip, *, row_off):
    B, n, D = z.shape
    L = n // 2
    tm = _tile(L, ROW_TILE, 8)
    tn = _tile(D, 1024)
    off = row_off // tm
    return pl.pallas_call(
        _dft_inv_kernel,
        grid=(B, D // tn, L // tm),
        in_specs=[pl.BlockSpec((tm, n), lambda b, j, m: (m, 0)),
                  pl.BlockSpec((1, n, tn), lambda b, j, m: (b, 0, j)),
                  pl.BlockSpec((1, tm, tn), lambda b, j, m: (b, m + off, j)),
                  pl.BlockSpec((1, tm, tn), lambda b, j, m: (b, m + off, j)),
                  pl.BlockSpec((1, tn), lambda b, j, m: (0, j))],
        out_specs=pl.BlockSpec((1, tm, tn), lambda b, j, m: (b, m, j)),
        out_shape=jax.ShapeDtypeStruct((B, L, D), BF16),
        compiler_params=_params("parallel", "parallel", "parallel"),
        name="dft_inv",
    )(inv, z, x0, vv, skip.reshape(1, D))


def _hyena(h, p, tables, *, Lc, has_ctx):
    B, T, D = h.shape
    x0, vv, vvb = _hy_in(h, p["w_in"], p["b_in"], p["conv_w"], p["conv_b"], Lc=Lc if has_ctx else -1)
    segs = [(Lc, T - Lc)] if has_ctx else [(0, T)]
    if has_ctx:
        segs = [(0, Lc)] + segs
    us = []
    for start, L in segs:
        fwd, inv = tables[L]
        filt, asum = _hy_filter(L, *p["filter"])
        hf = _dft_fwd(fwd, filt.astype(BF16)[None], asum, mode="scale", out_dtype=F32)[0]
        seg = vvb if (start == 0 and L == T) else lax.slice_in_dim(vvb, start, start + L, axis=1)
        z = _dft_fwd(fwd, seg, hf, mode="mul", out_dtype=BF16).reshape(B, 2 * L, D)
        us.append(_dft_inv(inv, z, x0, vv, p["skip"], row_off=start))
    return us[0] if len(us) == 1 else jnp.concatenate(us, axis=1)


MU_R, MU_W, MU_K, MU_V, MU_A, MU_G = range(6)


def _rw_mix(cur_ref, prev_ref, next_ref, mu_ref, which, *, n_tiles, ctx_tiles):
    cur = cur_ref[0]
    up, dn = _neighbours(cur, prev_ref[0], next_ref[0], pl.program_id(1), n_tiles, ctx_tiles)
    dx = 0.5 * (up + dn) - cur
    return [(cur + dx * mu_ref[j:j + 1, :]).astype(BF16) for j in which]


def _rw_rkvg_kernel(cur_ref, prev_ref, next_ref, mu_ref, wr_ref, wk_ref, wv_ref, g1_ref, g2_ref,
                    r_ref, k_ref, v_ref, g_ref, **tiles):
    xr, xk, xv, xg = _rw_mix(cur_ref, prev_ref, next_ref, mu_ref, (MU_R, MU_K, MU_V, MU_G), **tiles)
    r_ref[0] = jnp.dot(xr, wr_ref[...], preferred_element_type=F32)
    k_ref[0] = jnp.dot(xk, wk_ref[...], preferred_element_type=F32)
    v_ref[0] = jnp.dot(xv, wv_ref[...], preferred_element_type=F32)
    t = jax.nn.sigmoid(jnp.dot(xg, g1_ref[...], preferred_element_type=F32)).astype(BF16)
    g_ref[0] = jnp.dot(t, g2_ref[...], preferred_element_type=F32)


def _rw_decay_kernel(cur_ref, prev_ref, next_ref, mu_ref, w1_ref, w2_ref, w0_ref, a1_ref, a2_ref, a0_ref,
                     lwf_ref, af_ref, lwb_ref, ab_ref, **tiles):
    xw, xa = _rw_mix(cur_ref, prev_ref, next_ref, mu_ref, (MU_W, MU_A), **tiles)
    R = w2_ref.shape[1]
    tw = jnp.tanh(jnp.dot(xw, w1_ref[...], preferred_element_type=F32)).astype(BF16)
    ta = jnp.dot(xa, a1_ref[...], preferred_element_type=F32).astype(BF16)
    for d, (lw_ref, a_ref) in enumerate(((lwf_ref, af_ref), (lwb_ref, ab_ref))):
        wl = jnp.dot(tw[:, d * R:(d + 1) * R], w2_ref[d], preferred_element_type=F32) + w0_ref[d:d + 1, :]
        lw_ref[0] = -jnp.exp(-jax.nn.softplus(-wl) - 0.5)
        al = jnp.dot(ta[:, d * R:(d + 1) * R], a2_ref[d], preferred_element_type=F32) + a0_ref[d:d + 1, :]
        a_ref[0] = jax.nn.sigmoid(al)


def _pad_rank(first, second):
    _, D, r = first.shape
    R = -(-r // LANES) * LANES
    f = jnp.zeros((D, 2 * R), BF16)
    s = jnp.zeros((2, R, D), BF16)
    for d in range(2):
        f = f.at[:, d * R:d * R + r].set(first[d].astype(BF16))
        s = s.at[d, :r].set(second[d].astype(BF16))
    return f, s


def _rw_project(hs, p, *, ctx_tiles):
    B, T, D = hs.shape
    n_tiles = T // ROW_TILE
    tiles = dict(n_tiles=n_tiles, ctx_tiles=ctx_tiles)
    halo = _halo_specs(T, D, lambda j: 0)
    once = pl.Buffered(1)
    full = lambda a: pl.BlockSpec(a.shape, lambda b, t, j: (0,) * a.ndim, pipeline_mode=once)
    blk = pl.BlockSpec((1, ROW_TILE, D), lambda b, t, j: (b, t, 0))
    out = jax.ShapeDtypeStruct((B, T, D), F32)
    mu = p["mu"]
    wr, wk, wv = p["w_rkv"][0], p["w_rkv"][1], p["w_rkv"][2]
    g1, g2 = p["g1"].astype(BF16), p["g2"].astype(BF16)
    r, k, v, g = pl.pallas_call(
        functools.partial(_rw_rkvg_kernel, **tiles),
        grid=(B, n_tiles, 1),
        in_specs=halo + [full(a) for a in (mu, wr, wk, wv, g1, g2)],
        out_specs=[blk] * 4,
        out_shape=[out] * 4,
        compiler_params=_params("parallel", "parallel", "parallel"),
        name="rw_rkvg",
    )(hs, hs, hs, mu, wr, wk, wv, g1, g2)
    w1, w2 = _pad_rank(p["w1"], p["w2"])
    a1, a2 = _pad_rank(p["a1"], p["a2"])
    consts = (mu, w1, w2, p["w0"], a1, a2, p["a0"])
    lwf, af, lwb, ab = pl.pallas_call(
        functools.partial(_rw_decay_kernel, **tiles),
        grid=(B, n_tiles, 1),
        in_specs=halo + [full(a) for a in consts],
        out_specs=[blk] * 4,
        out_shape=[out] * 4,
        compiler_params=_params("parallel", "parallel", "parallel"),
        name="rw_decay",
    )(hs, hs, hs, *consts)
    return r, k, v, g, ((lwf, af), (lwb, ab))


def _block_diag(x, mask):
    return jnp.where(mask, jnp.concatenate([x] * RW_GROUP, axis=0), 0.0).astype(BF16)


def _rw_chunks(r, k, v, lw, a, k_k, k_a, state, same_head, incl, strict, reverse):
    C = RW_CHUNK
    W = RW_GROUP * HEAD
    G = range(len(r))
    nt = (((1,), (1,)), ((), ()))
    ones = jnp.where(same_head, 1.0, 0.0).astype(BF16)
    tri = jnp.where(incl[:, :C], 1.0, 0.0).astype(BF16)
    bd = lambda t: _block_diag(t, same_head)
    mm = lambda x, y: jnp.dot(x.astype(BF16), y, preferred_element_type=F32)

    def split(t):
        hi = t.astype(BF16)
        return hi, (t - hi.astype(F32)).astype(BF16)

    kkr = [k[g] * k_k[g] for g in G]
    sq = [split(kkr[g] * kkr[g]) for g in G]
    ss = [jnp.dot(sq[g][0], ones, preferred_element_type=F32)
          + jnp.dot(sq[g][1], ones, preferred_element_type=F32) for g in G]
    lws = [split(lw[g]) for g in G]
    lp = [jnp.dot(tri, lws[g][0], preferred_element_type=F32)
          + jnp.dot(tri, lws[g][1], preferred_element_type=F32) for g in G]
    kk = [kkr[g] / jnp.maximum(jnp.sqrt(ss[g]), 1e-12) for g in G]
    kd = [k[g] * (1.0 + (a[g] - 1.0) * k_a[g]) for g in G]
    lp_end = [lp[g][0:1, :] if reverse else lp[g][C - 1:C, :] for g in G]
    e_neg = [jnp.exp(-lp[g]) for g in G]
    lhs = [jnp.concatenate([-kk[g] * jnp.exp(lp[g] - lw[g]), r[g] * jnp.exp(lp[g])], axis=0).astype(BF16)
           for g in G]
    rhs = [jnp.concatenate([bd(kk[g] * a[g] * e_neg[g]), bd(kd[g] * e_neg[g])], axis=0) for g in G]
    cross = [lax.dot_general(lhs[g], rhs[g], nt, preferred_element_type=F32) for g in G]
    from_state = [lax.dot_general(lhs[g], state[g].astype(BF16), nt, preferred_element_type=F32)
                  for g in G]
    v_bd = [bd(v[g]) for g in G]
    p = [jnp.where(strict, cross[g][:C, :W], 0.0) for g in G]
    x = [from_state[g][:C] + mm(jnp.where(strict, cross[g][:C, W:], 0.0), v_bd[g]) for g in G]
    n = 1
    while n < C:
        x = [x[g] + mm(p[g], bd(x[g])) for g in G]
        n *= 2
        if n < C:
            p = [mm(p[g], bd(p[g])) for g in G]
    y = [from_state[g][C:] + mm(jnp.where(incl, cross[g][C:, :W], 0.0), bd(x[g]))
         + mm(jnp.where(incl, cross[g][C:, W:], 0.0), v_bd[g]) for g in G]

    tail = [jnp.exp(lp_end[g] - lp[g]) for g in G]
    uv = [jnp.concatenate([x[g], v[g]], axis=0).astype(BF16) for g in G]
    bk = [jnp.concatenate([kk[g] * a[g] * tail[g], kd[g] * tail[g]], axis=0).astype(BF16) for g in G]
    upd = [lax.dot_general(uv[g], bk[g], (((0,), (0,)), ((), ())), preferred_element_type=F32)
           for g in G]
    new_state = [state[g] * jnp.exp(lp_end[g]) + jnp.where(same_head, upd[g], 0.0) for g in G]
    return y, new_state


RW_GROUPS_PER_STEP = 8
RW_CHUNKS_PER_STEP = 2


def _rw_scan_kernel(r_ref, k_ref, v_ref, lw_ref, a_ref, kk_ref, ka_ref, o_ref, state_ref, *, reverse):
    C = RW_CHUNK
    W = RW_GROUP * HEAD

    @pl.when(pl.program_id(2) == 0)
    def _():
        state_ref[...] = jnp.zeros_like(state_ref)

    ri = lax.broadcasted_iota(jnp.int32, (W, W), 0)
    ci = lax.broadcasted_iota(jnp.int32, (W, W), 1)
    same_head = (ri // HEAD) == (ci // HEAD)
    t_i = lax.broadcasted_iota(jnp.int32, (C, W), 0)
    s_i = lax.broadcasted_iota(jnp.int32, (C, W), 1) % C
    incl = (s_i >= t_i) if reverse else (s_i <= t_i)
    strict = (s_i > t_i) if reverse else (s_i < t_i)

    ng = state_ref.shape[0]
    sl = [slice(g * W, (g + 1) * W) for g in range(ng)]
    state = [state_ref[g] for g in range(ng)]
    order = range(RW_CHUNKS_PER_STEP)
    for c in (reversed(order) if reverse else order):
        rows = slice(c * C, (c + 1) * C)
        load = lambda ref: [ref[0, rows, s] for s in sl]
        y, state = _rw_chunks(load(r_ref), load(k_ref), load(v_ref), load(lw_ref), load(a_ref),
                              [kk_ref[:, s] for s in sl], [ka_ref[:, s] for s in sl],
                              state, same_head, incl, strict, reverse)
        for g in range(ng):
            o_ref[0, rows, sl[g]] = y[g]
    for g in range(ng):
        state_ref[g] = state[g]


def _rw_scan(r, k, v, lw, a, k_k, k_a, *, Lc, reverse):
    B, T, D = r.shape
    assert RW_CHUNK == HEAD
    rows = RW_CHUNK * RW_CHUNKS_PER_STEP
    assert Lc % rows == 0 and T % rows == 0
    W = RW_GROUP * HEAD
    ng = math.gcd(D // W, RW_GROUPS_PER_STEP)
    nb = T // rows
    ncb = Lc // rows

    def block(c):
        if not reverse:
            return c
        return jnp.where(c < ncb, ncb - 1 - c, nb - 1 - (c - ncb))

    blk = pl.BlockSpec((1, rows, ng * W), lambda b, g, c: (b, block(c), g))
    vec = pl.BlockSpec((1, ng * W), lambda b, g, c: (0, g))
    return pl.pallas_call(
        functools.partial(_rw_scan_kernel, reverse=reverse),
        grid=(B, D // (ng * W), nb),
        in_specs=[blk] * 5 + [vec, vec],
        out_specs=blk,
        out_shape=jax.ShapeDtypeStruct((B, T, D), F32),
        scratch_shapes=[pltpu.VMEM((ng, W, W), F32)],
        compiler_params=_params("parallel", "parallel", "arbitrary"),
        name="rw_scan",
    )(r, k, v, lw, a, k_k.reshape(1, D), k_a.reshape(1, D))


def _head_sums(x, ones_bd):
    W = ones_bd.shape[0]
    hi = x.astype(BF16)
    lo = (x - hi.astype(F32)).astype(BF16)
    cols = []
    for g in range(x.shape[1] // W):
        sl = slice(g * W, (g + 1) * W)
        cols.append(jnp.dot(hi[:, sl], ones_bd, preferred_element_type=F32)
                    + jnp.dot(lo[:, sl], ones_bd, preferred_element_type=F32))
    return jnp.concatenate(cols, axis=1)


def _rw_post_kernel(of_ref, ob_ref, r_ref, k_ref, v_ref, g_ref, rk_ref, gw_ref, gb_ref, o_ref):
    W = RW_GROUP * HEAD
    ri = lax.broadcasted_iota(jnp.int32, (W, W), 0)
    ci = lax.broadcasted_iota(jnp.int32, (W, W), 1)
    ones_bd = jnp.where((ri // HEAD) == (ci // HEAD), 1.0, 0.0).astype(BF16)
    o = of_ref[0] + ob_ref[0]
    m = _head_sums(o, ones_bd) * (1.0 / HEAD)
    d = o - m
    var = _head_sums(d * d, ones_bd) * (1.0 / HEAD)
    on = d * lax.rsqrt(var + RW_GN_EPS) * gw_ref[...] + gb_ref[...]
    bonus = _head_sums(r_ref[0] * k_ref[0] * rk_ref[...], ones_bd) * v_ref[0]
    o_ref[0] = ((on + bonus) * g_ref[0]).astype(o_ref.dtype)


def _rw_post(o_f, o_b, r, k, v, g, r_k, gn_w, gn_b):
    B, T, D = r.shape
    tt = _tile(T, 128, 8)
    blk = pl.BlockSpec((1, tt, D), lambda b, t: (b, t, 0))
    par = pl.BlockSpec((1, D), lambda b, t: (0, 0))
    return pl.pallas_call(
        _rw_post_kernel,
        grid=(B, T // tt),
        in_specs=[blk] * 6 + [par] * 3,
        out_specs=blk,
        out_shape=jax.ShapeDtypeStruct((B, T, D), BF16),
        compiler_params=_params("parallel", "parallel"),
        name="rw_post",
    )(o_f, o_b, r, k, v, g, r_k.reshape(1, D), gn_w.reshape(1, D), gn_b.reshape(1, D))


def _rwkv(hs, p, *, Lc):
    r, k, v, g, dirs = _rw_project(hs, p, ctx_tiles=Lc // ROW_TILE)
    outs = [_rw_scan(r, k, v, lw, a, p["k_k"], p["k_a"], Lc=Lc, reverse=d == 1)
            for d, (lw, a) in enumerate(dirs)]
    return _rw_post(outs[0], outs[1], r, k, v, g, p["r_k"], p["gn_w"], p["gn_b"])


def _rope_tables(T, Lc):
    quarter = HEAD // 4
    pos = jnp.arange(T - Lc, dtype=jnp.int32)
    rows = (pos // GRID_W).astype(F32)
    cols = (pos % GRID_W).astype(F32)
    inv = ROPE_BASE ** (-jnp.arange(quarter, dtype=F32) / quarter)
    ang = jnp.concatenate([rows[:, None] * inv, rows[:, None] * inv,
                           cols[:, None] * inv, cols[:, None] * inv], axis=1)
    ang = jnp.concatenate([jnp.zeros((Lc, HEAD), F32), ang], axis=0)
    ang = jnp.concatenate([ang, ang], axis=1)
    return jnp.cos(ang), jnp.sin(ang)


def _rope_kernel(x_ref, cos_ref, sin_ref, q_ref, k_ref, v_ref, *, n_q, n_kv, scale):
    x = x_ref[0]
    n_rot = (n_q + n_kv) * HEAD
    xr = x[:, :n_rot]
    reps = n_rot // LANES
    cos = jnp.concatenate([cos_ref[...]] * reps, axis=1)
    sin = jnp.concatenate([sin_ref[...]] * reps, axis=1)
    quarter = HEAD // 4
    lane = lax.broadcasted_iota(jnp.int32, xr.shape, 1)
    first = (lane % (2 * quarter)) < quarter
    rot = jnp.where(first, -pltpu.roll(xr, n_rot - quarter, axis=1), pltpu.roll(xr, quarter, axis=1))
    y = xr * cos + rot * sin
    for h in range(n_q):
        q_ref[0, h] = (y[:, h * HEAD:(h + 1) * HEAD] * scale).astype(q_ref.dtype)
    for h in range(n_kv):
        lo = (n_q + h) * HEAD
        k_ref[0, h] = y[:, lo:lo + HEAD].astype(k_ref.dtype)
        lo = (n_q + n_kv + h) * HEAD
        v_ref[0, h] = x[:, lo:lo + HEAD].astype(v_ref.dtype)


def _rope(qkv, cos, sin, *, n_q, n_kv):
    B, T, W = qkv.shape
    tt = _tile(T, 128, 8)
    out = lambda n: pl.BlockSpec((1, n, tt, HEAD), lambda b, t: (b, 0, t, 0))
    return pl.pallas_call(
        functools.partial(_rope_kernel, n_q=n_q, n_kv=n_kv, scale=HEAD ** -0.5),
        grid=(B, T // tt),
        in_specs=[pl.BlockSpec((1, tt, W), lambda b, t: (b, t, 0)),
                  pl.BlockSpec((tt, LANES), lambda b, t: (t, 0)),
                  pl.BlockSpec((tt, LANES), lambda b, t: (t, 0))],
        out_specs=[out(n_q), out(n_kv), out(n_kv)],
        out_shape=[jax.ShapeDtypeStruct((B, n_q, T, HEAD), BF16),
                   jax.ShapeDtypeStruct((B, n_kv, T, HEAD), BF16),
                   jax.ShapeDtypeStruct((B, n_kv, T, HEAD), BF16)],
        compiler_params=_params("parallel", "parallel"),
        name="rope",
    )(qkv, cos, sin)


def _attn_kernel(*refs, n_q, n_kv, local, q_off, n_blocks):
    if local:
        q_ref, kp_ref, kc_ref, kn_ref, vp_ref, vc_ref, vn_ref, kx_ref, vx_ref, sink_ref, o_ref = refs
    else:
        q_ref, kx_ref, vx_ref, sink_ref, o_ref = refs
    G = n_q // n_kv
    R = G * AT_BLOCK
    n = pl.program_id(1)
    nt = (((1,), (1,)), ((), ()))
    if local:
        qi = lax.broadcasted_iota(jnp.int32, (R, 3 * AT_BLOCK), 0) % AT_BLOCK
        kj = lax.broadcasted_iota(jnp.int32, (R, 3 * AT_BLOCK), 1) - AT_BLOCK
        kpos = kj + n * AT_BLOCK
        ok = (jnp.abs(qi - kj) <= AT_WINDOW) & (kpos >= 0) & (kpos < n_blocks * AT_BLOCK)
    H = range(n_kv)
    q = [q_ref[0, h * G:(h + 1) * G].reshape(R, HEAD) for h in H]
    sink = [jnp.concatenate(
        [jnp.broadcast_to(sink_ref[h * G + g:h * G + g + 1, 0:1], (AT_BLOCK, 1)) for g in range(G)],
        axis=0) for h in H]
    s_ctx = [lax.dot_general(q[h], kx_ref[0, h], nt, preferred_element_type=F32) for h in H]
    m = [jnp.maximum(jnp.max(s_ctx[h], axis=-1, keepdims=True), sink[h]) for h in H]
    if local:
        k_loc = [jnp.concatenate([kp_ref[0, h], kc_ref[0, h], kn_ref[0, h]], axis=0) for h in H]
        v_loc = [jnp.concatenate([vp_ref[0, h], vc_ref[0, h], vn_ref[0, h]], axis=0) for h in H]
        s_loc = [jnp.where(ok, lax.dot_general(q[h], k_loc[h], nt, preferred_element_type=F32), NEG_INF)
                 for h in H]
        m = [jnp.maximum(m[h], jnp.max(s_loc[h], axis=-1, keepdims=True)) for h in H]
    p_ctx = [jnp.exp(s_ctx[h] - m[h]) for h in H]
    den = [jnp.sum(p_ctx[h], axis=-1, keepdims=True) + jnp.exp(sink[h] - m[h]) for h in H]
    o = [jnp.dot(p_ctx[h].astype(BF16), vx_ref[0, h], preferred_element_type=F32) for h in H]
    if local:
        p_loc = [jnp.exp(s_loc[h] - m[h]) for h in H]
        den = [den[h] + jnp.sum(p_loc[h], axis=-1, keepdims=True) for h in H]
        o = [o[h] + jnp.dot(p_loc[h].astype(BF16), v_loc[h], preferred_element_type=F32) for h in H]
    o = [o[h] * (1.0 / den[h]) for h in H]
    for h in H:
        for g in range(G):
            hh = h * G + g
            o_ref[0, :, hh * HEAD:(hh + 1) * HEAD] = o[h][g * AT_BLOCK:(g + 1) * AT_BLOCK].astype(o_ref.dtype)


def _attn(q, k, v, sink, *, Lc, local):
    B, n_q, T, _ = q.shape
    n_kv = k.shape[1]
    cb = Lc // AT_BLOCK
    nb = (T - Lc) // AT_BLOCK if local else cb
    q_off = cb if local else 0
    last = T // AT_BLOCK - 1
    qspec = pl.BlockSpec((1, n_q, AT_BLOCK, HEAD), lambda b, n: (b, 0, n + q_off, 0))
    xspec = pl.BlockSpec((1, n_kv, Lc, HEAD), lambda b, n: (b, 0, 0, 0))
    sspec = pl.BlockSpec((n_q, LANES), lambda b, n: (0, 0))
    sink_b = jnp.broadcast_to(sink.astype(F32)[:, None], (n_q, LANES))
    if local:
        blk = lambda f: pl.BlockSpec((1, n_kv, AT_BLOCK, HEAD), lambda b, n: (b, 0, f(n), 0))
        prev = blk(lambda n: jnp.maximum(n + cb - 1, cb))
        cur = blk(lambda n: n + cb)
        nxt = blk(lambda n: jnp.minimum(n + cb + 1, last))
        in_specs = [qspec, prev, cur, nxt, prev, cur, nxt, xspec, xspec, sspec]
        args = (q, k, k, k, v, v, v, k, v, sink_b)
    else:
        in_specs = [qspec, xspec, xspec, sspec]
        args = (q, k, v, sink_b)
    return pl.pallas_call(
        functools.partial(_attn_kernel, n_q=n_q, n_kv=n_kv, local=local, q_off=q_off, n_blocks=nb),
        grid=(B, nb),
        in_specs=in_specs,
        out_specs=pl.BlockSpec((1, AT_BLOCK, n_q * HEAD), lambda b, n: (b, n, 0)),
        out_shape=jax.ShapeDtypeStruct((B, nb * AT_BLOCK, n_q * HEAD), BF16),
        compiler_params=_params("parallel", "parallel"),
        name="attn_local" if local else "attn_ctx",
    )(*args)


def _attention(h, p, *, Lc):
    B, T, D = h.shape
    M = B * T
    n_q = D // HEAD
    qkv = _mm(h.reshape(M, D), p["w_qkv"], bias=p["b_qkv"]).reshape(B, T, -1)
    cos, sin = _rope_tables(T, Lc)
    q, k, v = _rope(qkv, cos, sin, n_q=n_q, n_kv=AT_KV_HEADS)
    o_ctx = _attn(q, k, v, p["sink"], Lc=Lc, local=False)
    o_lat = _attn(q, k, v, p["sink"], Lc=Lc, local=True)
    return jnp.concatenate([o_ctx, o_lat], axis=1)


def _swiglu_ffn(h, w1, w3, w2):
    mid = _mm(h, w1, w3=w3, out_dtype=BF16)
    return _mm(mid, w2)


MOE_TILE = 1024
ROUTE_G1, ROUTE_G2, ROUTE_I1, ROUTE_I2 = 0, 1, 2, 3


def _gates_kernel(l_ref, route_ref, sel_ref, *, n_experts):
    l = l_ref[...]
    lane = lax.broadcasted_iota(jnp.int32, l.shape, 1)
    l = jnp.where(lane < n_experts, l, -jnp.inf)
    m1 = jnp.max(l, axis=-1, keepdims=True)
    i1 = jnp.min(jnp.where(l == m1, lane, LANES), axis=-1, keepdims=True)
    l2 = jnp.where(lane == i1, -jnp.inf, l)
    m2 = jnp.max(l2, axis=-1, keepdims=True)
    i2 = jnp.min(jnp.where(l2 == m2, lane, LANES), axis=-1, keepdims=True)
    e2 = jnp.exp(m2 - m1)
    den = 1.0 + e2
    route_ref[...] = jnp.where(lane == ROUTE_G1, 1.0 / den,
                               jnp.where(lane == ROUTE_G2, e2 / den,
                                         jnp.where(lane == ROUTE_I1, i1.astype(F32),
                                                   jnp.where(lane == ROUTE_I2, i2.astype(F32), 0.0))))
    sel_ref[...] = jnp.where((lane == i1) | (lane == i2), 1.0, 0.0).astype(sel_ref.dtype)


def _gates(logits, n_experts):
    M = logits.shape[0]
    tm = _tile(M, 1024, 8)
    blk = pl.BlockSpec((tm, LANES), lambda i: (i, 0))
    return pl.pallas_call(
        functools.partial(_gates_kernel, n_experts=n_experts),
        grid=(M // tm,),
        in_specs=[blk],
        out_specs=[blk, blk],
        out_shape=[jax.ShapeDtypeStruct((M, LANES), F32), jax.ShapeDtypeStruct((M, LANES), BF16)],
        compiler_params=_params("parallel"),
        name="moe_gates",
    )(logits)


def _rank_kernel(sel_ref, rank_ref, cnt_ref, carry_ref):
    @pl.when(pl.program_id(0) == 0)
    def _():
        carry_ref[...] = jnp.zeros_like(carry_ref)

    s = sel_ref[...]
    n = s.shape[0]
    earlier = (lax.broadcasted_iota(jnp.int32, (n, n), 1) < lax.broadcasted_iota(jnp.int32, (n, n), 0))
    within = jnp.dot(jnp.where(earlier, 1.0, 0.0).astype(BF16), s, preferred_element_type=F32)
    rank_ref[...] = within + carry_ref[...]
    carry_ref[...] += jnp.sum(s.astype(F32), axis=0, keepdims=True)
    cnt_ref[...] = carry_ref[...]


def _rank(sel):
    M = sel.shape[0]
    tr = _tile(M, 512, 8)
    return pl.pallas_call(
        _rank_kernel,
        grid=(M // tr,),
        in_specs=[pl.BlockSpec((tr, LANES), lambda i: (i, 0))],
        out_specs=[pl.BlockSpec((tr, LANES), lambda i: (i, 0)), pl.BlockSpec((1, LANES), lambda i: (0, 0))],
        out_shape=[jax.ShapeDtypeStruct((M, LANES), F32), jax.ShapeDtypeStruct((1, LANES), F32)],
        scratch_shapes=[pltpu.VMEM((1, LANES), F32)],
        compiler_params=_params("arbitrary"),
        name="moe_rank",
    )(sel)


def _pos_kernel(route_ref, rank_ref, offs_ref, pos_ref):
    lane = lax.broadcasted_iota(jnp.int32, rank_ref.shape, 1)
    lane_f = lane.astype(F32)
    tot = rank_ref[...] + offs_ref[...]
    route = route_ref[...]
    p1 = jnp.sum(jnp.where(lane_f == route[:, ROUTE_I1:ROUTE_I1 + 1], tot, 0.0), axis=-1, keepdims=True)
    p2 = jnp.sum(jnp.where(lane_f == route[:, ROUTE_I2:ROUTE_I2 + 1], tot, 0.0), axis=-1, keepdims=True)
    pos_ref[...] = jnp.where(lane == 0, p1, jnp.where(lane == 1, p2, 0.0)).astype(jnp.int32)


def _positions(route, rank, offs):
    M = route.shape[0]
    tm = _tile(M, 1024, 8)
    blk = pl.BlockSpec((tm, LANES), lambda i: (i, 0))
    return pl.pallas_call(
        _pos_kernel,
        grid=(M // tm,),
        in_specs=[blk, blk, pl.BlockSpec((1, LANES), lambda i: (0, 0))],
        out_specs=blk,
        out_shape=jax.ShapeDtypeStruct((M, LANES), jnp.int32),
        compiler_params=_params("parallel"),
        name="moe_pos",
    )(route, rank, offs)


def _row_copy(src, dst, sem):
    return pltpu.make_async_copy(src, dst, sem)


def _dispatch_kernel(p1_ref, p2_ref, h_ref, xs_in_ref, xs_ref, sem):
    del xs_in_ref
    tt = h_ref.shape[0]
    base = pl.program_id(0) * tt

    def start(r, carry):
        row = h_ref.at[pl.ds(r, 1)]
        _row_copy(row, xs_ref.at[pl.ds(p1_ref[base + r], 1)], sem).start()
        _row_copy(row, xs_ref.at[pl.ds(p2_ref[base + r], 1)], sem).start()
        return carry

    lax.fori_loop(0, tt, start, 0, unroll=8)
    for _ in range(2):
        _row_copy(h_ref, xs_ref.at[pl.ds(0, tt)], sem).wait()


def _dispatch(h, p1, p2, n_rows):
    M, D = h.shape
    tt = _tile(M, ROW_TILE, 8)
    return pl.pallas_call(
        _dispatch_kernel,
        grid_spec=pltpu.PrefetchScalarGridSpec(
            num_scalar_prefetch=2,
            grid=(M // tt,),
            in_specs=[pl.BlockSpec((tt, D), lambda i, p1, p2: (i, 0)),
                      pl.BlockSpec(memory_space=pl.ANY)],
            out_specs=pl.BlockSpec(memory_space=pl.ANY),
            scratch_shapes=[pltpu.SemaphoreType.DMA(())]),
        out_shape=jax.ShapeDtypeStruct((n_rows, D), h.dtype),
        input_output_aliases={3: 0},
        compiler_params=_params("arbitrary"),
        name="moe_dispatch",
    )(p1, p2, h, jnp.zeros((n_rows, D), h.dtype))


def _gmm_kernel(te_ref, nv_ref, x_ref, *refs, swiglu):
    o_ref = refs[-1]

    @pl.when(pl.program_id(0) < nv_ref[0])
    def _():
        x = x_ref[...].astype(BF16)
        acc = jnp.dot(x, refs[0][0], preferred_element_type=F32)
        if swiglu:
            acc = acc * jax.nn.sigmoid(acc) * jnp.dot(x, refs[1][0], preferred_element_type=F32)
        o_ref[...] = acc.astype(o_ref.dtype)

    @pl.when(pl.program_id(0) >= nv_ref[0])
    def _():
        o_ref[...] = jnp.zeros_like(o_ref)


def _gmm(x, ws, tile_expert, n_valid, *, out_dtype, tn=512):
    P, K = x.shape
    N = ws[0].shape[2]
    tm = MOE_TILE
    tn = _tile(N, tn)
    wspec = pl.BlockSpec((1, K, tn), lambda i, j, te, nv: (te[i], 0, j))
    return pl.pallas_call(
        functools.partial(_gmm_kernel, swiglu=len(ws) == 2),
        grid_spec=pltpu.PrefetchScalarGridSpec(
            num_scalar_prefetch=2,
            grid=(P // tm, N // tn),
            in_specs=[pl.BlockSpec((tm, K), lambda i, j, te, nv: (i, 0))] + [wspec] * len(ws),
            out_specs=pl.BlockSpec((tm, tn), lambda i, j, te, nv: (i, j))),
        out_shape=jax.ShapeDtypeStruct((P, N), out_dtype),
        compiler_params=_params("parallel", "parallel"),
        name="moe_gmm",
    )(tile_expert, n_valid, x, *ws)


def _combine_kernel(p1_ref, p2_ref, route_ref, ys_ref, o_ref, a_ref, b_ref, sem):
    tt = o_ref.shape[0]
    base = pl.program_id(0) * tt

    def start(r, carry):
        _row_copy(ys_ref.at[pl.ds(p1_ref[base + r], 1)], a_ref.at[pl.ds(r, 1)], sem).start()
        _row_copy(ys_ref.at[pl.ds(p2_ref[base + r], 1)], b_ref.at[pl.ds(r, 1)], sem).start()
        return carry

    lax.fori_loop(0, tt, start, 0, unroll=8)
    _row_copy(ys_ref.at[pl.ds(0, tt)], a_ref, sem).wait()
    _row_copy(ys_ref.at[pl.ds(0, tt)], b_ref, sem).wait()
    route = route_ref[...]
    o_ref[...] = (route[:, ROUTE_G1:ROUTE_G1 + 1] * a_ref[...]
                  + route[:, ROUTE_G2:ROUTE_G2 + 1] * b_ref[...])


def _combine(ys, route, p1, p2):
    M = route.shape[0]
    D = ys.shape[1]
    tt = _tile(M, ROW_TILE, 8)
    return pl.pallas_call(
        _combine_kernel,
        grid_spec=pltpu.PrefetchScalarGridSpec(
            num_scalar_prefetch=2,
            grid=(M // tt,),
            in_specs=[pl.BlockSpec((tt, LANES), lambda i, p1, p2: (i, 0)),
                      pl.BlockSpec(memory_space=pl.ANY)],
            out_specs=pl.BlockSpec((tt, D), lambda i, p1, p2: (i, 0)),
            scratch_shapes=[pltpu.VMEM((tt, D), F32), pltpu.VMEM((tt, D), F32),
                            pltpu.SemaphoreType.DMA(())]),
        out_shape=jax.ShapeDtypeStruct((M, D), F32),
        compiler_params=_params("arbitrary"),
        name="moe_combine",
    )(p1, p2, route, ys)


def _moe(h, logits, w1, w3, w2):
    M, D = h.shape
    E = w1.shape[0]
    tm = MOE_TILE
    route, sel = _gates(logits, E)
    rank, cnt = _rank(sel)
    counts = cnt[0, :E].astype(jnp.int32)
    padded = (counts + tm - 1) // tm * tm
    ends = jnp.cumsum(padded)
    n_tiles = (TOP_K * M) // tm + E
    tile_expert = jnp.minimum(
        jnp.searchsorted(ends, jnp.arange(n_tiles, dtype=jnp.int32) * tm, side="right"), E - 1
    ).astype(jnp.int32)
    n_valid = (ends[-1:] // tm).astype(jnp.int32)
    offs = jnp.zeros((1, LANES), F32).at[0, :E].set((ends - padded).astype(F32))
    pos = _positions(route, rank, offs)
    p1, p2 = pos[:, 0], pos[:, 1]
    xs = _dispatch(h, p1, p2, n_tiles * tm)
    mid = _gmm(xs, [w1, w3], tile_expert, n_valid, out_dtype=BF16)
    ys = _gmm(mid, [w2], tile_expert, n_valid, out_dtype=F32)
    return _combine(ys, route, p1, p2)


def kernel(x, c, ctx, c_ctx, ada_w, ada_b, ln_w, ln_b, hy_w_in, hy_b_in, hy_conv_w, hy_conv_b, hy_f_w1, hy_f_b1, hy_f_w2, hy_f_b2, hy_f_w3, hy_f_freq, hy_skip, hy_w_out, hy_b_out, rw_mu, rw_w_rkv, rw_w_o, rw_w0, rw_w1, rw_w2, rw_a0, rw_a1, rw_a2, rw_g1, rw_g2, rw_k_k, rw_k_a, rw_r_k, rw_gn_w, rw_gn_b, at_w_qkv, at_b_qkv, at_w_o, at_b_o, at_sink, ff_w1, ff_w3, ff_w2, moe_router, moe_w1, moe_w3, moe_w2):
    B, L, D = x.shape
    Lc = ctx.shape[1]
    depth = ada_w.shape[0]
    alpha = (2 * depth) ** 0.25
    assert Lc % ROW_TILE == 0 and L % ROW_TILE == 0 and D % (RW_GROUP * HEAD) == 0
    bf = lambda t: t.astype(BF16)

    ctx_row = B
    rows = -(-(B + 1) // 8) * 8
    cond = jnp.zeros((rows, D), F32).at[:B].set(c).at[B].set(c_ctx)
    mods = _ada(cond, ada_w, ada_b).reshape(depth, rows, N_MOD, D)

    tables = {}
    if depth > 0:
        tables[L] = _dft_tables(L)
        if depth > 1:
            tables[Lc] = _dft_tables(Lc)

    xs = jnp.concatenate([ctx, x], axis=1)
    ctx_tiles = Lc // ROW_TILE
    h = _modulate(xs, mods[0], sh=0, sc=1, ctx_tiles=ctx_tiles, ctx_row=ctx_row,
                  out_dtype=BF16)
    for i in range(depth):
        last = i == depth - 1
        kind = i % 3
        j = i // 3
        mod = mods[i]
        has_ctx = xs.shape[1] != L
        ct = ctx_tiles if has_ctx else 0
        T = xs.shape[1]
        if kind == 0:
            p = dict(w_in=bf(hy_w_in[j]), b_in=hy_b_in[j], conv_w=hy_conv_w[j], conv_b=hy_conv_b[j],
                     filter=(hy_f_w1[j], hy_f_b1[j], hy_f_w2[j], hy_f_b2[j], hy_f_w3[j], hy_f_freq[j]),
                     skip=hy_skip[j], w_out=bf(hy_w_out[j]), b_out=hy_b_out[j])
            y = _hyena(h, p, tables, Lc=Lc, has_ctx=has_ctx)
            proj = (p["w_out"], p["b_out"])
        elif kind == 1:
            assert has_ctx and not last
            p = dict(mu=rw_mu[j], w_rkv=bf(rw_w_rkv[j]), w_o=bf(rw_w_o[j]), w0=rw_w0[j], w1=rw_w1[j],
                     w2=rw_w2[j], a0=rw_a0[j], a1=rw_a1[j], a2=rw_a2[j], g1=rw_g1[j], g2=rw_g2[j],
                     k_k=rw_k_k[j], k_a=rw_k_a[j], r_k=rw_r_k[j], gn_w=rw_gn_w[j], gn_b=rw_gn_b[j])
            y = _rwkv(h, p, Lc=Lc)
            proj = (p["w_o"], None)
        else:
            assert has_ctx and not last
            p = dict(w_qkv=bf(at_w_qkv[j]), b_qkv=at_b_qkv[j], w_o=bf(at_w_o[j]), b_o=at_b_o[j],
                     sink=at_sink[j])
            y = _attention(h, p, Lc=Lc)
            proj = (p["w_o"], p["b_o"])

        drop1 = ct if last else 0
        fj = i // 2
        moe = i % 2 == 1
        res = _ln(xs, y, mod, ln_w[i, 0], ln_b[i, 0], alpha=alpha, gate=2, proj=proj, h_mod=(3, 4),
                  h_dtype=F32 if moe else BF16, router=moe_router[fj] if moe else None,
                  ctx_tiles=ct, ctx_row=ctx_row, drop_tiles=drop1)
        xs, h2 = res[0], res[1]
        ct = ct - drop1
        T = xs.shape[1]
        M = B * T
        if moe:
            f = _moe(h2.reshape(M, D), res[2].reshape(M, LANES), bf(moe_w1[fj]), bf(moe_w3[fj]),
                     bf(moe_w2[fj]))
        else:
            f = _swiglu_ffn(h2.reshape(M, D), bf(ff_w1[fj]), bf(ff_w3[fj]), bf(ff_w2[fj]))
        f = f.reshape(B, T, D)
        if last:
            (xs,) = _ln(xs, f, mod, ln_w[i, 1], ln_b[i, 1], alpha=alpha, gate=5,
                        ctx_tiles=ct, ctx_row=ctx_row)
        else:
            nxt_last = i + 1 == depth - 1
            nkind = (i + 1) % 3
            drop2 = ct if (nxt_last and nkind == 0) else 0
            xs, h = _ln(xs, f, mod, ln_w[i, 1], ln_b[i, 1], alpha=alpha, gate=5, h_mod=(0, 1),
                        mod_h=mods[i + 1], h_dtype=F32 if nkind == 1 else BF16,
                        ctx_tiles=ct, ctx_row=ctx_row, drop_tiles=drop2)
    return xs
```

```python
import functools
import math

import jax
import jax.numpy as jnp
from jax import lax
from jax.experimental import pallas as pl
from jax.experimental.pallas import tpu as pltpu

F32 = jnp.float32
BF16 = jnp.bfloat16
HIGHEST = lax.Precision.HIGHEST

VMEM_LIMIT_BYTES = 56 * 1024 * 1024
LANES = 128
ROW_TILE = 256
BF16_ROWS = 16

LN_EPS = 1e-5
N_MOD = 6
HEAD = 64
RW_GN_EPS = 64e-5
RW_CHUNK = 64
RW_GROUP = 4
AT_KV_HEADS = 4
AT_WINDOW = 128
AT_BLOCK = 128
GRID_W = 64
ROPE_BASE = 10000.0
NEG_INF = -1e30
TOP_K = 2
HY_BANDS = 16
HY_EMB = 2 * HY_BANDS + 1
HY_MIN_DECAY = -math.log(1e-2) / 1.5
HY_MAX_DECAY = -math.log(1e-2) / 0.3


def _params(*sem):
    return pltpu.CompilerParams(dimension_semantics=sem, vmem_limit_bytes=VMEM_LIMIT_BYTES)


def _tile(n, pref, mult=LANES):
    if n <= pref:
        return n
    t = (pref // mult) * mult
    while t >= mult:
        if n % t == 0:
            return t
        t -= mult
    return n


def _mm_kernel(*refs, n_w, has_bias, act):
    x = refs[0][...].astype(BF16)
    o_ref = refs[-1]
    acc = jnp.dot(x, refs[1][0].astype(BF16), preferred_element_type=F32)
    if has_bias:
        acc = acc + refs[1 + n_w][...]
    if act == "swiglu":
        acc3 = jnp.dot(x, refs[2][0].astype(BF16), preferred_element_type=F32)
        acc = acc * jax.nn.sigmoid(acc) * acc3
    o_ref[...] = acc.astype(o_ref.dtype)


def _mm(x, w, *, w3=None, wi=0, bias=None, out_dtype=F32, tm=1024, tn=512):
    M, K = x.shape
    N = w.shape[2]
    tm = _tile(M, tm, 8)
    tn = _tile(N, tn)
    ws = [w] if w3 is None else [w, w3]
    in_specs = [pl.BlockSpec((tm, K), lambda i, j: (i, 0))]
    in_specs += [pl.BlockSpec((1, K, tn), lambda i, j: (wi, 0, j)) for _ in ws]
    args = [x] + ws
    if bias is not None:
        in_specs.append(pl.BlockSpec((1, tn), lambda i, j: (0, j)))
        args.append(bias.reshape(1, N).astype(F32))
    return pl.pallas_call(
        functools.partial(_mm_kernel, n_w=len(ws), has_bias=bias is not None,
                          act="swiglu" if w3 is not None else None),
        grid=(M // tm, N // tn),
        in_specs=in_specs,
        out_specs=pl.BlockSpec((tm, tn), lambda i, j: (i, j)),
        out_shape=jax.ShapeDtypeStruct((M, N), out_dtype),
        compiler_params=_params("parallel", "parallel"),
        name="mm_swiglu" if w3 is not None else "mm",
    )(*args)


def _ada_kernel(c_ref, w_ref, b_ref, o_ref):
    c = c_ref[...]
    s = (c * jax.nn.sigmoid(c)).astype(BF16)
    o_ref[0] = jnp.dot(s, w_ref[0].astype(BF16), preferred_element_type=F32) + b_ref[0]


def _ada(cond, ada_w, ada_b):
    depth, D, N = ada_w.shape
    R = cond.shape[0]
    tn = _tile(N, 1024)
    return pl.pallas_call(
        _ada_kernel,
        grid=(depth, N // tn),
        in_specs=[pl.BlockSpec((R, D), lambda i, j: (0, 0)),
                  pl.BlockSpec((1, D, tn), lambda i, j: (i, 0, j)),
                  pl.BlockSpec((1, 1, tn), lambda i, j: (i, 0, j))],
        out_specs=pl.BlockSpec((1, R, tn), lambda i, j: (i, 0, j)),
        out_shape=jax.ShapeDtypeStruct((depth, R, N), F32),
        compiler_params=_params("parallel", "parallel"),
        name="ada",
    )(cond, ada_w, ada_b.reshape(depth, 1, N))


def _mod_index(ctx_tiles, ctx_row, off):
    def index(b, t):
        return (jnp.where(t + off < ctx_tiles, ctx_row, b), 0, 0)
    return index


def _modulate_kernel(x_ref, mod_ref, h_ref, *, sh, sc):
    x = x_ref[0]
    h_ref[0] = (x * (1.0 + mod_ref[0, sc:sc + 1, :]) + mod_ref[0, sh:sh + 1, :]).astype(h_ref.dtype)


def _modulate(x, mod, *, sh, sc, ctx_tiles, ctx_row, out_dtype):
    B, T, D = x.shape
    return pl.pallas_call(
        functools.partial(_modulate_kernel, sh=sh, sc=sc),
        grid=(B, T // ROW_TILE),
        in_specs=[pl.BlockSpec((1, ROW_TILE, D), lambda b, t: (b, t, 0)),
                  pl.BlockSpec((1, N_MOD, D), _mod_index(ctx_tiles, ctx_row, 0))],
        out_specs=pl.BlockSpec((1, ROW_TILE, D), lambda b, t: (b, t, 0)),
        out_shape=jax.ShapeDtypeStruct((B, T, D), out_dtype),
        compiler_params=_params("parallel", "parallel"),
        name="modulate",
    )(x, mod)


def _ln_kernel(*refs, alpha, gate, sh, sc, has_h, has_router, has_proj):
    x_ref, y_ref, mod_ref, modh_ref, w_ref, b_ref = refs[:6]
    pos = 6
    if has_proj:
        pw_ref, pb_ref = refs[pos:pos + 2]
        pos += 2
    router_ref = None
    if has_router:
        router_ref = refs[pos]
        pos += 1
    xo_ref = refs[pos]
    pos += 1
    if has_proj:
        y = jnp.dot(y_ref[0], pw_ref[0], preferred_element_type=F32) + pb_ref[...]
    else:
        y = y_ref[0].astype(F32)
    z = alpha * x_ref[0] + mod_ref[0, gate:gate + 1, :] * y
    mu = jnp.mean(z, axis=-1, keepdims=True)
    d = z - mu
    var = jnp.mean(d * d, axis=-1, keepdims=True)
    xn = d * lax.rsqrt(var + LN_EPS) * w_ref[...] + b_ref[...]
    xo_ref[0] = xn
    if has_h:
        h = xn * (1.0 + modh_ref[0, sc:sc + 1, :]) + modh_ref[0, sh:sh + 1, :]
        h_ref = refs[pos]
        pos += 1
        h_ref[0] = h.astype(h_ref.dtype)
        if has_router:
            refs[pos][0] = jnp.dot(h, router_ref[...], precision=HIGHEST, preferred_element_type=F32)


def _ln(x, y, mod, ln_w, ln_b, *, alpha, gate, proj=None, h_mod=None, mod_h=None, h_dtype=BF16,
        router=None, ctx_tiles, ctx_row, drop_tiles=0):
    B, T, D = x.shape
    K = y.shape[2]
    nt = T // ROW_TILE - drop_tiles
    To = nt * ROW_TILE
    off = drop_tiles
    in_specs = [pl.BlockSpec((1, ROW_TILE, D), lambda b, t: (b, t + off, 0)),
                pl.BlockSpec((1, ROW_TILE, K), lambda b, t: (b, t + off, 0)),
                pl.BlockSpec((1, N_MOD, D), _mod_index(ctx_tiles, ctx_row, off)),
                pl.BlockSpec((1, N_MOD, D), _mod_index(ctx_tiles, ctx_row, off)),
                pl.BlockSpec((1, D), lambda b, t: (0, 0)),
                pl.BlockSpec((1, D), lambda b, t: (0, 0))]
    args = [x, y, mod, mod if mod_h is None else mod_h, ln_w.reshape(1, D), ln_b.reshape(1, D)]
    if proj is not None:
        pw, pi, pb = proj
        in_specs += [pl.BlockSpec((1, K, D), lambda b, t: (pi, 0, 0), pipeline_mode=pl.Buffered(1)),
                     pl.BlockSpec((1, D), lambda b, t: (0, 0))]
        args += [pw, jnp.zeros((1, D), F32) if pb is None else pb.reshape(1, D).astype(F32)]
    out_specs = [pl.BlockSpec((1, ROW_TILE, D), lambda b, t: (b, t, 0))]
    out_shape = [jax.ShapeDtypeStruct((B, To, D), F32)]
    sh = sc = 0
    if h_mod is not None:
        sh, sc = h_mod
        out_specs.append(pl.BlockSpec((1, ROW_TILE, D), lambda b, t: (b, t, 0)))
        out_shape.append(jax.ShapeDtypeStruct((B, To, D), h_dtype))
    if router is not None:
        E = router.shape[1]
        router_p = jnp.zeros((D, LANES), F32).at[:, :E].set(router)
        in_specs.append(pl.BlockSpec((D, LANES), lambda b, t: (0, 0)))
        args.append(router_p)
        out_specs.append(pl.BlockSpec((1, ROW_TILE, LANES), lambda b, t: (b, t, 0)))
        out_shape.append(jax.ShapeDtypeStruct((B, To, LANES), F32))
    return pl.pallas_call(
        functools.partial(_ln_kernel, alpha=alpha, gate=gate, sh=sh, sc=sc, has_h=h_mod is not None,
                          has_router=router is not None, has_proj=proj is not None),
        grid=(B, nt),
        in_specs=in_specs,
        out_specs=out_specs,
        out_shape=out_shape,
        compiler_params=_params("parallel", "parallel"),
        name="ln_residual",
    )(*args)


def _halo_specs(T, C, col):
    r8 = ROW_TILE // 8
    last8 = T // 8 - 1
    return [pl.BlockSpec((1, ROW_TILE, C), lambda b, t, j: (b, t, col(j))),
            pl.BlockSpec((1, 8, C), lambda b, t, j: (b, jnp.maximum(t * r8 - 1, 0), col(j))),
            pl.BlockSpec((1, 8, C), lambda b, t, j: (b, jnp.minimum(t * r8 + r8, last8), col(j)))]


def _neighbours(cur, prev8, next8, t, n_tiles, ctx_tiles):
    rows = lax.broadcasted_iota(jnp.int32, cur.shape, 0)
    has_prev = jnp.logical_and(t != 0, t != ctx_tiles)
    has_next = jnp.logical_and(t != n_tiles - 1, t != ctx_tiles - 1)
    top = jnp.where(has_prev, prev8[7:8, :], 0.0)
    bot = jnp.where(has_next, next8[0:1, :], 0.0)
    up = jnp.where(rows == 0, top, pltpu.roll(cur, 1, axis=0))
    dn = jnp.where(rows == cur.shape[0] - 1, bot, pltpu.roll(cur, cur.shape[0] - 1, axis=0))
    return up, dn


def _hy_in_kernel(x_ref, xp_ref, xn_ref, *refs, tm, T, Lc):
    ws, bs, cws, cbs = refs[0:3], refs[3:6], refs[6:9], refs[9:12]
    x0_ref, vv_ref, vvb_ref = refs[12:15]
    H = BF16_ROWS
    x = jnp.concatenate([xp_ref[0], x_ref[0], xn_ref[0]], axis=0)
    row = lax.broadcasted_iota(jnp.int32, (tm, 1), 0) + pl.program_id(1) * tm
    no_prev = (row == 0) | (row == Lc)
    no_next = (row == Lc - 1) | (row == T - 1)
    out = []
    for s in range(3):
        acc = jnp.dot(x, ws[s][...], preferred_element_type=F32) + bs[s][...]
        n = acc.shape[0]
        cur = acc[H:H + tm]
        up = jnp.where(no_prev, 0.0, pltpu.roll(acc, 1, axis=0)[H:H + tm])
        dn = jnp.where(no_next, 0.0, pltpu.roll(acc, n - 1, axis=0)[H:H + tm])
        cw = cws[s]
        out.append(up * cw[0:1, :] + cur * cw[1:2, :] + dn * cw[2:3, :] + cbs[s][...])
    x0_ref[0] = out[0]
    vv = out[1] * out[2]
    vv_ref[0] = vv
    vvb_ref[0] = vv.astype(BF16)


def _hy_in(h, w_in, b_in, conv_w, conv_b, *, Lc):
    B, T, K = h.shape
    D = w_in.shape[1] // 3
    H = BF16_ROWS
    tm = _tile(T, 1280, 64)
    tc = _tile(D, 512)
    nj = D // tc
    r16 = tm // H
    last16 = T // H - 1
    sec = lambda shape: [pl.BlockSpec(shape, lambda b, t, j, s=s: (0, s * nj + j)) for s in range(3)]
    in_specs = [pl.BlockSpec((1, tm, K), lambda b, t, j: (b, t, 0)),
                pl.BlockSpec((1, H, K), lambda b, t, j: (b, jnp.maximum(t * r16 - 1, 0), 0)),
                pl.BlockSpec((1, H, K), lambda b, t, j: (b, jnp.minimum(t * r16 + r16, last16), 0))]
    in_specs += sec((K, tc)) + sec((1, tc)) + sec((3, tc)) + sec((1, tc))
    blk = pl.BlockSpec((1, tm, tc), lambda b, t, j: (b, t, j))
    b2 = b_in.reshape(1, 3 * D)
    cb2 = conv_b.reshape(1, 3 * D)
    return pl.pallas_call(
        functools.partial(_hy_in_kernel, tm=tm, T=T, Lc=Lc),
        grid=(B, T // tm, nj),
        in_specs=in_specs,
        out_specs=[blk, blk, blk],
        out_shape=[jax.ShapeDtypeStruct((B, T, D), F32), jax.ShapeDtypeStruct((B, T, D), F32),
                   jax.ShapeDtypeStruct((B, T, D), BF16)],
        compiler_params=_params("parallel", "parallel", "parallel"),
        name="hy_in",
    )(h, h, h, w_in, w_in, w_in, b2, b2, b2, conv_w, conv_w, conv_w, cb2, cb2, cb2)


def _hy_filter_kernel(w1_ref, b1_ref, w2_ref, b2_ref, w3_ref, fr_ref, h_ref, s_ref, *, L, D, tl):
    i = pl.program_id(0)
    row = (lax.broadcasted_iota(jnp.int32, (tl, LANES), 0) + i * tl).astype(F32)
    lane = lax.broadcasted_iota(jnp.int32, (tl, LANES), 1)
    band = jnp.where(lane <= HY_BANDS, lane - 1, lane - 1 - HY_BANDS).astype(F32)
    freq = 1e-4 + band * ((HY_BANDS - 1 - 1e-4) / (HY_BANDS - 1))
    ang = freq * (row * (2.0 * math.pi / L))
    z = jnp.where(lane == 0, row / (L - 1),
                  jnp.where(lane <= HY_BANDS, jnp.cos(ang),
                            jnp.where(lane < HY_EMB, -jnp.sin(ang), 0.0)))
    h = jnp.sin(fr_ref[0:1, :] * (jnp.dot(z, w1_ref[...], precision=HIGHEST,
                                           preferred_element_type=F32) + b1_ref[...]))
    h = jnp.sin(fr_ref[1:2, :] * (jnp.dot(h, w2_ref[...], precision=HIGHEST,
                                           preferred_element_type=F32) + b2_ref[...]))
    h = jnp.dot(h, w3_ref[...], precision=HIGHEST, preferred_element_type=F32)
    half = L // 2
    rowd = (lax.broadcasted_iota(jnp.int32, (tl, D), 0) + i * tl).astype(F32)
    dist = jnp.abs(rowd - half) / half
    chan = lax.broadcasted_iota(jnp.int32, (tl, D), 1).astype(F32)
    deltas = HY_MIN_DECAY + chan * ((HY_MAX_DECAY - HY_MIN_DECAY) / (D - 1))
    h = h * jnp.exp(-dist * deltas)
    h_ref[...] = h

    @pl.when(i == 0)
    def _():
        s_ref[...] = jnp.zeros_like(s_ref)

    s_ref[...] += jnp.sum(jnp.abs(h), axis=0, keepdims=True)


def _hy_filter(L, f_w1, f_b1, f_w2, f_b2, f_w3, f_freq):
    D = f_w3.shape[1]
    hid = f_w1.shape[1]
    w1 = jnp.zeros((LANES, LANES), F32).at[:HY_EMB, :hid].set(f_w1)
    b1 = jnp.zeros((1, LANES), F32).at[0, :hid].set(f_b1)
    w2 = jnp.zeros((LANES, LANES), F32).at[:hid, :hid].set(f_w2)
    b2 = jnp.zeros((1, LANES), F32).at[0, :hid].set(f_b2)
    w3 = jnp.zeros((LANES, D), F32).at[:hid].set(f_w3)
    fr = jnp.zeros((2, LANES), F32).at[:, :hid].set(f_freq)
    tl = _tile(L, 256, 8)
    full = lambda shape: pl.BlockSpec(shape, lambda i: (0, 0))
    return pl.pallas_call(
        functools.partial(_hy_filter_kernel, L=L, D=D, tl=tl),
        grid=(L // tl,),
        in_specs=[full((LANES, LANES)), full((1, LANES)), full((LANES, LANES)), full((1, LANES)),
                  full((LANES, D)), full((2, LANES))],
        out_specs=[pl.BlockSpec((tl, D), lambda i: (i, 0)), full((1, D))],
        out_shape=[jax.ShapeDtypeStruct((L, D), F32), jax.ShapeDtypeStruct((1, D), F32)],
        compiler_params=_params("arbitrary"),
        name="hy_filter",
    )(w1, b1, w2, b2, w3, fr)


DFT_ROWS = 64


def _dft_tables_kernel(fa_ref, ia_ref, b_ref, fwd_ref, inv_ref, *, L):
    n = 2 * L
    i = pl.program_id(0)
    cb, sb = b_ref[0], b_ref[1]
    q = lax.broadcasted_iota(jnp.int32, (DFT_ROWS, L), 0)
    col = lax.broadcasted_iota(jnp.int32, (DFT_ROWS, L), 1)
    sign_c = jnp.where(col % 2 == 0, 1.0, -1.0)
    sign_r = jnp.where(q % 2 == 0, 1.0, -1.0)
    ca, sa = fa_ref[0, 0:1, :], fa_ref[0, 1:2, :]
    cos_f = ca * cb - sa * sb
    sin_f = sa * cb + ca * sb
    first_row = (q == 0) & (i == 0)
    fwd_ref[0] = cos_f.astype(fwd_ref.dtype)
    fwd_ref[1] = jnp.where(first_row, sign_c, -sin_f).astype(fwd_ref.dtype)
    ca, sa = ia_ref[0, 0:1, :], ia_ref[0, 1:2, :]
    cos_i = ca * cb - sa * sb
    sin_i = sa * cb + ca * sb
    inv_ref[:, :L] = jnp.where(col == 0, 1.0 / n, (2.0 / n) * cos_i).astype(inv_ref.dtype)
    inv_ref[:, L:] = jnp.where(col == 0, sign_r / n, (-2.0 / n) * sin_i).astype(inv_ref.dtype)


def _dft_tables(L):
    n = 2 * L
    R = DFT_ROWS
    assert L % (2 * R) == 0
    w = 2.0 * math.pi / n
    c = jnp.arange(L, dtype=jnp.int32)
    hi = jnp.arange(L // R, dtype=jnp.int32) * R

    def cos_sin(rows):
        ang = ((rows[:, None] * c[None, :]) % n).astype(F32) * w
        return jnp.stack([jnp.cos(ang), jnp.sin(ang)], axis=1)

    fa = cos_sin(hi)
    ia = cos_sin(hi + L // 2)
    b = jnp.swapaxes(cos_sin(jnp.arange(R, dtype=jnp.int32)), 0, 1)
    return pl.pallas_call(
        functools.partial(_dft_tables_kernel, L=L),
        grid=(L // R,),
        in_specs=[pl.BlockSpec((1, 2, L), lambda i: (i, 0, 0)),
                  pl.BlockSpec((1, 2, L), lambda i: (i, 0, 0)),
                  pl.BlockSpec((2, R, L), lambda i: (0, 0, 0))],
        out_specs=[pl.BlockSpec((2, R, L), lambda i: (0, i, 0)),
                   pl.BlockSpec((R, 2 * L), lambda i: (i, 0))],
        out_shape=[jax.ShapeDtypeStruct((2, L, L), BF16), jax.ShapeDtypeStruct((L, 2 * L), BF16)],
        compiler_params=_params("parallel"),
        name="dft_tables",
    )(fa, ia, b)


def _dft_fwd_kernel(a_ref, w_ref, *rest, mode):
    o_ref = rest[-1]
    w = w_ref[0]
    vre = jnp.dot(a_ref[0], w, preferred_element_type=F32)
    vim = jnp.dot(a_ref[1], w, preferred_element_type=F32)
    if mode == "scale":
        inv = 1.0 / (rest[0][...] + 1e-6)
        o_ref[0, 0] = vre * inv
        o_ref[0, 1] = vim * inv
    else:
        hre = rest[0][0]
        him = rest[0][1]
        first = jnp.logical_and(pl.program_id(2) == 0,
                                lax.broadcasted_iota(jnp.int32, vre.shape, 0) == 0)
        zre = jnp.where(first, vre * hre, vre * hre - vim * him)
        zim = jnp.where(first, vim * him, vre * him + vim * hre)
        o_ref[0, 0] = zre.astype(o_ref.dtype)
        o_ref[0, 1] = zim.astype(o_ref.dtype)


def _dft_fwd(fwd, w, extra, *, mode, out_dtype):
    B, L, D = w.shape
    tm = _tile(L, 256, 8)
    tn = _tile(D, 1024)
    if mode == "scale":
        extra_spec = pl.BlockSpec((1, tn), lambda b, j, m: (0, j))
    else:
        extra_spec = pl.BlockSpec((2, tm, tn), lambda b, j, m: (0, m, j))
    return pl.pallas_call(
        functools.partial(_dft_fwd_kernel, mode=mode),
        grid=(B, D // tn, L // tm),
        in_specs=[pl.BlockSpec((2, tm, L), lambda b, j, m: (0, m, 0)),
                  pl.BlockSpec((1, L, tn), lambda b, j, m: (b, 0, j)),
                  extra_spec],
        out_specs=pl.BlockSpec((1, 2, tm, tn), lambda b, j, m: (b, 0, m, j)),
        out_shape=jax.ShapeDtypeStruct((B, 2, L, D), out_dtype),
        compiler_params=_params("parallel", "parallel", "parallel"),
        name="dft_fwd",
    )(fwd, w, extra)


def _dft_inv_kernel(b_ref, z_ref, x0_ref, vv_ref, skip_ref, u_ref):
    y = jnp.dot(b_ref[...], z_ref[0], preferred_element_type=F32)
    u_ref[0] = (x0_ref[0] * (y + vv_ref[0] * skip_ref[...])).astype(u_ref.dtype)


def _dft_inv(inv, z, x0, vv, skip, *, row_off):
    B, n, D = z.shape
    L = n // 2
    tm = _tile(L, ROW_TILE, 8)
    tn = _tile(D, 1024)
    off = row_off // tm
    return pl.pallas_call(
        _dft_inv_kernel,
        grid=(B, D // tn, L // tm),
        in_specs=[pl.BlockSpec((tm, n), lambda b, j, m: (m, 0)),
                  pl.BlockSpec((1, n, tn), lambda b, j, m: (b, 0, j)),
                  pl.BlockSpec((1, tm, tn), lambda b, j, m: (b, m + off, j)),
                  pl.BlockSpec((1, tm, tn), lambda b, j, m: (b, m + off, j)),
                  pl.BlockSpec((1, tn), lambda b, j, m: (0, j))],
        out_specs=pl.BlockSpec((1, tm, tn), lambda b, j, m: (b, m, j)),
        out_shape=jax.ShapeDtypeStruct((B, L, D), BF16),
        compiler_params=_params("parallel", "parallel", "parallel"),
        name="dft_inv",
    )(inv, z, x0, vv, skip.reshape(1, D))


def _hyena(h, p, tables, *, Lc, has_ctx):
    B, T, D = h.shape
    x0, vv, vvb = _hy_in(h, p["w_in"], p["b_in"], p["conv_w"], p["conv_b"], Lc=Lc if has_ctx else -1)
    segs = [(Lc, T - Lc)] if has_ctx else [(0, T)]
    if has_ctx:
        segs = [(0, Lc)] + segs
    us = []
    for start, L in segs:
        fwd, inv = tables[L]
        filt, asum = _hy_filter(L, *p["filter"])
        hf = _dft_fwd(fwd, filt.astype(BF16)[None], asum, mode="scale", out_dtype=F32)[0]
        seg = vvb if (start == 0 and L == T) else lax.slice_in_dim(vvb, start, start + L, axis=1)
        z = _dft_fwd(fwd, seg, hf, mode="mul", out_dtype=BF16).reshape(B, 2 * L, D)
        us.append(_dft_inv(inv, z, x0, vv, p["skip"], row_off=start))
    return us[0] if len(us) == 1 else jnp.concatenate(us, axis=1)


MU_R, MU_W, MU_K, MU_V, MU_A, MU_G = range(6)


def _rw_mix(cur_ref, prev_ref, next_ref, mu_ref, which, *, n_tiles, ctx_tiles):
    cur = cur_ref[0]
    up, dn = _neighbours(cur, prev_ref[0], next_ref[0], pl.program_id(1), n_tiles, ctx_tiles)
    dx = 0.5 * (up + dn) - cur
    return [(cur + dx * mu_ref[j:j + 1, :]).astype(BF16) for j in which]


def _rw_rkvg_kernel(cur_ref, prev_ref, next_ref, mu_ref, wr_ref, wk_ref, wv_ref, g1_ref, g2_ref,
                    r_ref, k_ref, v_ref, g_ref, **tiles):
    xr, xk, xv, xg = _rw_mix(cur_ref, prev_ref, next_ref, mu_ref, (MU_R, MU_K, MU_V, MU_G), **tiles)
    r_ref[0] = jnp.dot(xr, wr_ref[...], preferred_element_type=F32)
    k_ref[0] = jnp.dot(xk, wk_ref[...], preferred_element_type=F32)
    v_ref[0] = jnp.dot(xv, wv_ref[...], preferred_element_type=F32)
    t = jax.nn.sigmoid(jnp.dot(xg, g1_ref[...], preferred_element_type=F32)).astype(BF16)
    g_ref[0] = jnp.dot(t, g2_ref[...], preferred_element_type=F32)


def _rw_decay_kernel(cur_ref, prev_ref, next_ref, mu_ref, w1_ref, w2_ref, w0_ref, a1_ref, a2_ref, a0_ref,
                     lwf_ref, af_ref, lwb_ref, ab_ref, **tiles):
    xw, xa = _rw_mix(cur_ref, prev_ref, next_ref, mu_ref, (MU_W, MU_A), **tiles)
    R = w2_ref.shape[1]
    tw = jnp.tanh(jnp.dot(xw, w1_ref[...], preferred_element_type=F32)).astype(BF16)
    ta = jnp.dot(xa, a1_ref[...], preferred_element_type=F32).astype(BF16)
    for d, (lw_ref, a_ref) in enumerate(((lwf_ref, af_ref), (lwb_ref, ab_ref))):
        wl = jnp.dot(tw[:, d * R:(d + 1) * R], w2_ref[d], preferred_element_type=F32) + w0_ref[d:d + 1, :]
        lw_ref[0] = -jnp.exp(-jax.nn.softplus(-wl) - 0.5)
        al = jnp.dot(ta[:, d * R:(d + 1) * R], a2_ref[d], preferred_element_type=F32) + a0_ref[d:d + 1, :]
        a_ref[0] = jax.nn.sigmoid(al)


def _pad_rank(first, second):
    _, D, r = first.shape
    R = -(-r // LANES) * LANES
    f = jnp.zeros((D, 2 * R), BF16)
    s = jnp.zeros((2, R, D), BF16)
    for d in range(2):
        f = f.at[:, d * R:d * R + r].set(first[d].astype(BF16))
        s = s.at[d, :r].set(second[d].astype(BF16))
    return f, s


def _rw_project(hs, p, *, ctx_tiles):
    B, T, D = hs.shape
    n_tiles = T // ROW_TILE
    tiles = dict(n_tiles=n_tiles, ctx_tiles=ctx_tiles)
    halo = _halo_specs(T, D, lambda j: 0)
    once = pl.Buffered(1)
    full = lambda a: pl.BlockSpec(a.shape, lambda b, t, j: (0,) * a.ndim, pipeline_mode=once)
    blk = pl.BlockSpec((1, ROW_TILE, D), lambda b, t, j: (b, t, 0))
    out = jax.ShapeDtypeStruct((B, T, D), F32)
    mu = p["mu"]
    wr, wk, wv = p["w_rkv"][0], p["w_rkv"][1], p["w_rkv"][2]
    g1, g2 = p["g1"].astype(BF16), p["g2"].astype(BF16)
    r, k, v, g = pl.pallas_call(
        functools.partial(_rw_rkvg_kernel, **tiles),
        grid=(B, n_tiles, 1),
        in_specs=halo + [full(a) for a in (mu, wr, wk, wv, g1, g2)],
        out_specs=[blk] * 4,
        out_shape=[out] * 4,
        compiler_params=_params("parallel", "parallel", "parallel"),
        name="rw_rkvg",
    )(hs, hs, hs, mu, wr, wk, wv, g1, g2)
    w1, w2 = _pad_rank(p["w1"], p["w2"])
    a1, a2 = _pad_rank(p["a1"], p["a2"])
    consts = (mu, w1, w2, p["w0"], a1, a2, p["a0"])
    lwf, af, lwb, ab = pl.pallas_call(
        functools.partial(_rw_decay_kernel, **tiles),
        grid=(B, n_tiles, 1),
        in_specs=halo + [full(a) for a in consts],
        out_specs=[blk] * 4,
        out_shape=[out] * 4,
        compiler_params=_params("parallel", "parallel", "parallel"),
        name="rw_decay",
    )(hs, hs, hs, *consts)
    return r, k, v, g, ((lwf, af), (lwb, ab))


def _block_diag(x, mask):
    return jnp.where(mask, jnp.concatenate([x] * RW_GROUP, axis=0), 0.0).astype(BF16)


def _rw_chunks(r, k, v, lw, a, k_k, k_a, state, same_head, incl, strict, reverse):
    C = RW_CHUNK
    W = RW_GROUP * HEAD
    G = range(len(r))
    nt = (((1,), (1,)), ((), ()))
    ones = jnp.where(same_head, 1.0, 0.0).astype(BF16)
    tri = jnp.where(incl[:, :C], 1.0, 0.0).astype(BF16)
    bd = lambda t: _block_diag(t, same_head)
    mm = lambda x, y: jnp.dot(x.astype(BF16), y, preferred_element_type=F32)

    def split(t):
        hi = t.astype(BF16)
        return hi, (t - hi.astype(F32)).astype(BF16)

    kkr = [k[g] * k_k[g] for g in G]
    sq = [split(kkr[g] * kkr[g]) for g in G]
    ss = [jnp.dot(sq[g][0], ones, preferred_element_type=F32)
          + jnp.dot(sq[g][1], ones, preferred_element_type=F32) for g in G]
    lws = [split(lw[g]) for g in G]
    lp = [jnp.dot(tri, lws[g][0], preferred_element_type=F32)
          + jnp.dot(tri, lws[g][1], preferred_element_type=F32) for g in G]
    kk = [kkr[g] / jnp.maximum(jnp.sqrt(ss[g]), 1e-12) for g in G]
    kd = [k[g] * (1.0 + (a[g] - 1.0) * k_a[g]) for g in G]
    lp_end = [lp[g][0:1, :] if reverse else lp[g][C - 1:C, :] for g in G]
    e_neg = [jnp.exp(-lp[g]) for g in G]
    lhs = [jnp.concatenate([-kk[g] * jnp.exp(lp[g] - lw[g]), r[g] * jnp.exp(lp[g])], axis=0).astype(BF16)
           for g in G]
    rhs = [jnp.concatenate([bd(kk[g] * a[g] * e_neg[g]), bd(kd[g] * e_neg[g])], axis=0) for g in G]
    cross = [lax.dot_general(lhs[g], rhs[g], nt, preferred_element_type=F32) for g in G]
    from_state = [lax.dot_general(lhs[g], state[g].astype(BF16), nt, preferred_element_type=F32)
                  for g in G]
    v_bd = [bd(v[g]) for g in G]
    p = [jnp.where(strict, cross[g][:C, :W], 0.0) for g in G]
    x = [from_state[g][:C] + mm(jnp.where(strict, cross[g][:C, W:], 0.0), v_bd[g]) for g in G]
    n = 1
    while n < C:
        x = [x[g] + mm(p[g], bd(x[g])) for g in G]
        n *= 2
        if n < C:
            p = [mm(p[g], bd(p[g])) for g in G]
    y = [from_state[g][C:] + mm(jnp.where(incl, cross[g][C:, :W], 0.0), bd(x[g]))
         + mm(jnp.where(incl, cross[g][C:, W:], 0.0), v_bd[g]) for g in G]

    tail = [jnp.exp(lp_end[g] - lp[g]) for g in G]
    uv = [jnp.concatenate([x[g], v[g]], axis=0).astype(BF16) for g in G]
    bk = [jnp.concatenate([kk[g] * a[g] * tail[g], kd[g] * tail[g]], axis=0).astype(BF16) for g in G]
    upd = [lax.dot_general(uv[g], bk[g], (((0,), (0,)), ((), ())), preferred_element_type=F32)
           for g in G]
    new_state = [state[g] * jnp.exp(lp_end[g]) + jnp.where(same_head, upd[g], 0.0) for g in G]
    return y, new_state


RW_GROUPS_PER_STEP = 8
RW_CHUNKS_PER_STEP = 2


def _rw_scan_kernel(r_ref, k_ref, v_ref, lw_ref, a_ref, kk_ref, ka_ref, o_ref, state_ref, *, reverse):
    C = RW_CHUNK
    W = RW_GROUP * HEAD

    @pl.when(pl.program_id(2) == 0)
    def _():
        state_ref[...] = jnp.zeros_like(state_ref)

    ri = lax.broadcasted_iota(jnp.int32, (W, W), 0)
    ci = lax.broadcasted_iota(jnp.int32, (W, W), 1)
    same_head = (ri // HEAD) == (ci // HEAD)
    t_i = lax.broadcasted_iota(jnp.int32, (C, W), 0)
    s_i = lax.broadcasted_iota(jnp.int32, (C, W), 1) % C
    incl = (s_i >= t_i) if reverse else (s_i <= t_i)
    strict = (s_i > t_i) if reverse else (s_i < t_i)

    ng = state_ref.shape[0]
    sl = [slice(g * W, (g + 1) * W) for g in range(ng)]
    state = [state_ref[g] for g in range(ng)]
    order = range(RW_CHUNKS_PER_STEP)
    for c in (reversed(order) if reverse else order):
        rows = slice(c * C, (c + 1) * C)
        load = lambda ref: [ref[0, rows, s] for s in sl]
        y, state = _rw_chunks(load(r_ref), load(k_ref), load(v_ref), load(lw_ref), load(a_ref),
                              [kk_ref[:, s] for s in sl], [ka_ref[:, s] for s in sl],
                              state, same_head, incl, strict, reverse)
        for g in range(ng):
            o_ref[0, rows, sl[g]] = y[g]
    for g in range(ng):
        state_ref[g] = state[g]


def _rw_scan(r, k, v, lw, a, k_k, k_a, *, Lc, reverse):
    B, T, D = r.shape
    assert RW_CHUNK == HEAD
    rows = RW_CHUNK * RW_CHUNKS_PER_STEP
    assert Lc % rows == 0 and T % rows == 0
    W = RW_GROUP * HEAD
    ng = math.gcd(D // W, RW_GROUPS_PER_STEP)
    nb = T // rows
    ncb = Lc // rows

    def block(c):
        if not reverse:
            return c
        return jnp.where(c < ncb, ncb - 1 - c, nb - 1 - (c - ncb))

    blk = pl.BlockSpec((1, rows, ng * W), lambda b, g, c: (b, block(c), g))
    vec = pl.BlockSpec((1, ng * W), lambda b, g, c: (0, g))
    return pl.pallas_call(
        functools.partial(_rw_scan_kernel, reverse=reverse),
        grid=(B, D // (ng * W), nb),
        in_specs=[blk] * 5 + [vec, vec],
        out_specs=blk,
        out_shape=jax.ShapeDtypeStruct((B, T, D), F32),
        scratch_shapes=[pltpu.VMEM((ng, W, W), F32)],
        compiler_params=_params("parallel", "parallel", "arbitrary"),
        name="rw_scan",
    )(r, k, v, lw, a, k_k.reshape(1, D), k_a.reshape(1, D))


def _head_sums(x, ones_bd):
    W = ones_bd.shape[0]
    hi = x.astype(BF16)
    lo = (x - hi.astype(F32)).astype(BF16)
    cols = []
    for g in range(x.shape[1] // W):
        sl = slice(g * W, (g + 1) * W)
        cols.append(jnp.dot(hi[:, sl], ones_bd, preferred_element_type=F32)
                    + jnp.dot(lo[:, sl], ones_bd, preferred_element_type=F32))
    return jnp.concatenate(cols, axis=1)


def _rw_post_kernel(of_ref, ob_ref, r_ref, k_ref, v_ref, g_ref, rk_ref, gw_ref, gb_ref, o_ref):
    W = RW_GROUP * HEAD
    ri = lax.broadcasted_iota(jnp.int32, (W, W), 0)
    ci = lax.broadcasted_iota(jnp.int32, (W, W), 1)
    ones_bd = jnp.where((ri // HEAD) == (ci // HEAD), 1.0, 0.0).astype(BF16)
    o = of_ref[0] + ob_ref[0]
    m = _head_sums(o, ones_bd) * (1.0 / HEAD)
    d = o - m
    var = _head_sums(d * d, ones_bd) * (1.0 / HEAD)
    on = d * lax.rsqrt(var + RW_GN_EPS) * gw_ref[...] + gb_ref[...]
    bonus = _head_sums(r_ref[0] * k_ref[0] * rk_ref[...], ones_bd) * v_ref[0]
    o_ref[0] = ((on + bonus) * g_ref[0]).astype(o_ref.dtype)


def _rw_post(o_f, o_b, r, k, v, g, r_k, gn_w, gn_b):
    B, T, D = r.shape
    tt = _tile(T, 128, 8)
    blk = pl.BlockSpec((1, tt, D), lambda b, t: (b, t, 0))
    par = pl.BlockSpec((1, D), lambda b, t: (0, 0))
    return pl.pallas_call(
        _rw_post_kernel,
        grid=(B, T // tt),
        in_specs=[blk] * 6 + [par] * 3,
        out_specs=blk,
        out_shape=jax.ShapeDtypeStruct((B, T, D), BF16),
        compiler_params=_params("parallel", "parallel"),
        name="rw_post",
    )(o_f, o_b, r, k, v, g, r_k.reshape(1, D), gn_w.reshape(1, D), gn_b.reshape(1, D))


def _rwkv(hs, p, *, Lc):
    r, k, v, g, dirs = _rw_project(hs, p, ctx_tiles=Lc // ROW_TILE)
    outs = [_rw_scan(r, k, v, lw, a, p["k_k"], p["k_a"], Lc=Lc, reverse=d == 1)
            for d, (lw, a) in enumerate(dirs)]
    return _rw_post(outs[0], outs[1], r, k, v, g, p["r_k"], p["gn_w"], p["gn_b"])


def _rope_tables(T, Lc):
    quarter = HEAD // 4
    pos = jnp.arange(T - Lc, dtype=jnp.int32)
    rows = (pos // GRID_W).astype(F32)
    cols = (pos % GRID_W).astype(F32)
    inv = ROPE_BASE ** (-jnp.arange(quarter, dtype=F32) / quarter)
    ang = jnp.concatenate([rows[:, None] * inv, rows[:, None] * inv,
                           cols[:, None] * inv, cols[:, None] * inv], axis=1)
    ang = jnp.concatenate([jnp.zeros((Lc, HEAD), F32), ang], axis=0)
    ang = jnp.concatenate([ang, ang], axis=1)
    return jnp.cos(ang), jnp.sin(ang)


def _qkv_rope_kernel(h_ref, w_ref, b_ref, cos_ref, sin_ref, q_ref, k_ref, v_ref, *, n_q, n_kv, scale):
    x = jnp.dot(h_ref[0], w_ref[0], preferred_element_type=F32) + b_ref[...]
    n_rot = (n_q + n_kv) * HEAD
    xr = x[:, :n_rot]
    reps = n_rot // LANES
    cos = jnp.concatenate([cos_ref[...]] * reps, axis=1)
    sin = jnp.concatenate([sin_ref[...]] * reps, axis=1)
    quarter = HEAD // 4
    lane = lax.broadcasted_iota(jnp.int32, xr.shape, 1)
    first = (lane % (2 * quarter)) < quarter
    rot = jnp.where(first, -pltpu.roll(xr, n_rot - quarter, axis=1), pltpu.roll(xr, quarter, axis=1))
    y = xr * cos + rot * sin
    for h in range(n_q):
        q_ref[0, h] = (y[:, h * HEAD:(h + 1) * HEAD] * scale).astype(q_ref.dtype)
    for h in range(n_kv):
        lo = (n_q + h) * HEAD
        k_ref[0, h] = y[:, lo:lo + HEAD].astype(k_ref.dtype)
        lo = (n_q + n_kv + h) * HEAD
        v_ref[0, h] = x[:, lo:lo + HEAD].astype(v_ref.dtype)


def _qkv_rope(h, w, wi, bias, cos, sin, *, n_q, n_kv):
    B, T, K = h.shape
    W = w.shape[2]
    tt = _tile(T, ROW_TILE, BF16_ROWS)
    out = lambda n: pl.BlockSpec((1, n, tt, HEAD), lambda b, t: (b, 0, t, 0))
    return pl.pallas_call(
        functools.partial(_qkv_rope_kernel, n_q=n_q, n_kv=n_kv, scale=HEAD ** -0.5),
        grid=(B, T // tt),
        in_specs=[pl.BlockSpec((1, tt, K), lambda b, t: (b, t, 0)),
                  pl.BlockSpec((1, K, W), lambda b, t: (wi, 0, 0), pipeline_mode=pl.Buffered(1)),
                  pl.BlockSpec((1, W), lambda b, t: (0, 0)),
                  pl.BlockSpec((tt, LANES), lambda b, t: (t, 0)),
                  pl.BlockSpec((tt, LANES), lambda b, t: (t, 0))],
        out_specs=[out(n_q), out(n_kv), out(n_kv)],
        out_shape=[jax.ShapeDtypeStruct((B, n_q, T, HEAD), BF16),
                   jax.ShapeDtypeStruct((B, n_kv, T, HEAD), BF16),
                   jax.ShapeDtypeStruct((B, n_kv, T, HEAD), BF16)],
        compiler_params=_params("parallel", "parallel"),
        name="qkv_rope",
    )(h, w, bias.reshape(1, W).astype(F32), cos, sin)


def _attn_kernel(*refs, n_q, n_kv, local, q_off, n_blocks):
    if local:
        q_ref, kp_ref, kc_ref, kn_ref, vp_ref, vc_ref, vn_ref, kx_ref, vx_ref, sink_ref, o_ref = refs
    else:
        q_ref, kx_ref, vx_ref, sink_ref, o_ref = refs
    G = n_q // n_kv
    R = G * AT_BLOCK
    n = pl.program_id(1)
    nt = (((1,), (1,)), ((), ()))
    if local:
        qi = lax.broadcasted_iota(jnp.int32, (R, 3 * AT_BLOCK), 0) % AT_BLOCK
        kj = lax.broadcasted_iota(jnp.int32, (R, 3 * AT_BLOCK), 1) - AT_BLOCK
        kpos = kj + n * AT_BLOCK
        ok = (jnp.abs(qi - kj) <= AT_WINDOW) & (kpos >= 0) & (kpos < n_blocks * AT_BLOCK)
    H = range(n_kv)
    q = [q_ref[0, h * G:(h + 1) * G].reshape(R, HEAD) for h in H]
    sink = [jnp.concatenate(
        [jnp.broadcast_to(sink_ref[h * G + g:h * G + g + 1, 0:1], (AT_BLOCK, 1)) for g in range(G)],
        axis=0) for h in H]
    s_ctx = [lax.dot_general(q[h], kx_ref[0, h], nt, preferred_element_type=F32) for h in H]
    m = [jnp.maximum(jnp.max(s_ctx[h], axis=-1, keepdims=True), sink[h]) for h in H]
    if local:
        k_loc = [jnp.concatenate([kp_ref[0, h], kc_ref[0, h], kn_ref[0, h]], axis=0) for h in H]
        v_loc = [jnp.concatenate([vp_ref[0, h], vc_ref[0, h], vn_ref[0, h]], axis=0) for h in H]
        s_loc = [jnp.where(ok, lax.dot_general(q[h], k_loc[h], nt, preferred_element_type=F32), NEG_INF)
                 for h in H]
        m = [jnp.maximum(m[h], jnp.max(s_loc[h], axis=-1, keepdims=True)) for h in H]
    p_ctx = [jnp.exp(s_ctx[h] - m[h]) for h in H]
    den = [jnp.sum(p_ctx[h], axis=-1, keepdims=True) + jnp.exp(sink[h] - m[h]) for h in H]
    o = [jnp.dot(p_ctx[h].astype(BF16), vx_ref[0, h], preferred_element_type=F32) for h in H]
    if local:
        p_loc = [jnp.exp(s_loc[h] - m[h]) for h in H]
        den = [den[h] + jnp.sum(p_loc[h], axis=-1, keepdims=True) for h in H]
        o = [o[h] + jnp.dot(p_loc[h].astype(BF16), v_loc[h], preferred_element_type=F32) for h in H]
    o = [o[h] * (1.0 / den[h]) for h in H]
    for h in H:
        for g in range(G):
            hh = h * G + g
            o_ref[0, :, hh * HEAD:(hh + 1) * HEAD] = o[h][g * AT_BLOCK:(g + 1) * AT_BLOCK].astype(o_ref.dtype)


def _attn(q, k, v, sink, *, Lc, local):
    B, n_q, T, _ = q.shape
    n_kv = k.shape[1]
    cb = Lc // AT_BLOCK
    nb = (T - Lc) // AT_BLOCK if local else cb
    q_off = cb if local else 0
    last = T // AT_BLOCK - 1
    qspec = pl.BlockSpec((1, n_q, AT_BLOCK, HEAD), lambda b, n: (b, 0, n + q_off, 0))
    xspec = pl.BlockSpec((1, n_kv, Lc, HEAD), lambda b, n: (b, 0, 0, 0))
    sspec = pl.BlockSpec((n_q, LANES), lambda b, n: (0, 0))
    sink_b = jnp.broadcast_to(sink.astype(F32)[:, None], (n_q, LANES))
    if local:
        blk = lambda f: pl.BlockSpec((1, n_kv, AT_BLOCK, HEAD), lambda b, n: (b, 0, f(n), 0))
        prev = blk(lambda n: jnp.maximum(n + cb - 1, cb))
        cur = blk(lambda n: n + cb)
        nxt = blk(lambda n: jnp.minimum(n + cb + 1, last))
        in_specs = [qspec, prev, cur, nxt, prev, cur, nxt, xspec, xspec, sspec]
        args = (q, k, k, k, v, v, v, k, v, sink_b)
    else:
        in_specs = [qspec, xspec, xspec, sspec]
        args = (q, k, v, sink_b)
    return pl.pallas_call(
        functools.partial(_attn_kernel, n_q=n_q, n_kv=n_kv, local=local, q_off=q_off, n_blocks=nb),
        grid=(B, nb),
        in_specs=in_specs,
        out_specs=pl.BlockSpec((1, AT_BLOCK, n_q * HEAD), lambda b, n: (b, n, 0)),
        out_shape=jax.ShapeDtypeStruct((B, nb * AT_BLOCK, n_q * HEAD), BF16),
        compiler_params=_params("parallel", "parallel"),
        name="attn_local" if local else "attn_ctx",
    )(*args)


def _attention(h, p, *, Lc):
    B, T, D = h.shape
    n_q = D // HEAD
    cos, sin = _rope_tables(T, Lc)
    q, k, v = _qkv_rope(h, p["w_qkv"], p["index"], p["b_qkv"], cos, sin, n_q=n_q, n_kv=AT_KV_HEADS)
    o_ctx = _attn(q, k, v, p["sink"], Lc=Lc, local=False)
    o_lat = _attn(q, k, v, p["sink"], Lc=Lc, local=True)
    return jnp.concatenate([o_ctx, o_lat], axis=1)


MOE_TILE = 1024
ROUTE_G1, ROUTE_G2, ROUTE_I1, ROUTE_I2 = 0, 1, 2, 3


def _gates_kernel(l_ref, route_ref, sel_ref, *, n_experts):
    l = l_ref[...]
    lane = lax.broadcasted_iota(jnp.int32, l.shape, 1)
    l = jnp.where(lane < n_experts, l, -jnp.inf)
    m1 = jnp.max(l, axis=-1, keepdims=True)
    i1 = jnp.min(jnp.where(l == m1, lane, LANES), axis=-1, keepdims=True)
    l2 = jnp.where(lane == i1, -jnp.inf, l)
    m2 = jnp.max(l2, axis=-1, keepdims=True)
    i2 = jnp.min(jnp.where(l2 == m2, lane, LANES), axis=-1, keepdims=True)
    e2 = jnp.exp(m2 - m1)
    den = 1.0 + e2
    route_ref[...] = jnp.where(lane == ROUTE_G1, 1.0 / den,
                               jnp.where(lane == ROUTE_G2, e2 / den,
                                         jnp.where(lane == ROUTE_I1, i1.astype(F32),
                                                   jnp.where(lane == ROUTE_I2, i2.astype(F32), 0.0))))
    sel_ref[...] = jnp.where((lane == i1) | (lane == i2), 1.0, 0.0).astype(sel_ref.dtype)


def _gates(logits, n_experts):
    M = logits.shape[0]
    tm = _tile(M, 1024, 8)
    blk = pl.BlockSpec((tm, LANES), lambda i: (i, 0))
    return pl.pallas_call(
        functools.partial(_gates_kernel, n_experts=n_experts),
        grid=(M // tm,),
        in_specs=[blk],
        out_specs=[blk, blk],
        out_shape=[jax.ShapeDtypeStruct((M, LANES), F32), jax.ShapeDtypeStruct((M, LANES), BF16)],
        compiler_params=_params("parallel"),
        name="moe_gates",
    )(logits)


def _rank_kernel(sel_ref, rank_ref, cnt_ref, carry_ref):
    @pl.when(pl.program_id(0) == 0)
    def _():
        carry_ref[...] = jnp.zeros_like(carry_ref)

    s = sel_ref[...]
    n = s.shape[0]
    earlier = (lax.broadcasted_iota(jnp.int32, (n, n), 1) < lax.broadcasted_iota(jnp.int32, (n, n), 0))
    within = jnp.dot(jnp.where(earlier, 1.0, 0.0).astype(BF16), s, preferred_element_type=F32)
    rank_ref[...] = within + carry_ref[...]
    carry_ref[...] += jnp.sum(s.astype(F32), axis=0, keepdims=True)
    cnt_ref[...] = carry_ref[...]


def _rank(sel):
    M = sel.shape[0]
    tr = _tile(M, 512, 8)
    return pl.pallas_call(
        _rank_kernel,
        grid=(M // tr,),
        in_specs=[pl.BlockSpec((tr, LANES), lambda i: (i, 0))],
        out_specs=[pl.BlockSpec((tr, LANES), lambda i: (i, 0)), pl.BlockSpec((1, LANES), lambda i: (0, 0))],
        out_shape=[jax.ShapeDtypeStruct((M, LANES), F32), jax.ShapeDtypeStruct((1, LANES), F32)],
        scratch_shapes=[pltpu.VMEM((1, LANES), F32)],
        compiler_params=_params("arbitrary"),
        name="moe_rank",
    )(sel)


def _pos_kernel(route_ref, rank_ref, offs_ref, pos_ref):
    lane = lax.broadcasted_iota(jnp.int32, rank_ref.shape, 1)
    lane_f = lane.astype(F32)
    tot = rank_ref[...] + offs_ref[...]
    route = route_ref[...]
    p1 = jnp.sum(jnp.where(lane_f == route[:, ROUTE_I1:ROUTE_I1 + 1], tot, 0.0), axis=-1, keepdims=True)
    p2 = jnp.sum(jnp.where(lane_f == route[:, ROUTE_I2:ROUTE_I2 + 1], tot, 0.0), axis=-1, keepdims=True)
    pos_ref[...] = jnp.where(lane == 0, p1, jnp.where(lane == 1, p2, 0.0)).astype(jnp.int32)


def _positions(route, rank, offs):
    M = route.shape[0]
    tm = _tile(M, 1024, 8)
    blk = pl.BlockSpec((tm, LANES), lambda i: (i, 0))
    return pl.pallas_call(
        _pos_kernel,
        grid=(M // tm,),
        in_specs=[blk, blk, pl.BlockSpec((1, LANES), lambda i: (0, 0))],
        out_specs=blk,
        out_shape=jax.ShapeDtypeStruct((M, LANES), jnp.int32),
        compiler_params=_params("parallel"),
        name="moe_pos",
    )(route, rank, offs)


def _row_copy(src, dst, sem):
    return pltpu.make_async_copy(src, dst, sem)


def _dispatch_kernel(p1_ref, p2_ref, h_ref, xs_in_ref, xs_ref, sem):
    del xs_in_ref
    tt = h_ref.shape[0]
    base = pl.program_id(0) * tt

    def start(r, carry):
        row = h_ref.at[pl.ds(r, 1)]
        _row_copy(row, xs_ref.at[pl.ds(p1_ref[base + r], 1)], sem).start()
        _row_copy(row, xs_ref.at[pl.ds(p2_ref[base + r], 1)], sem).start()
        return carry

    lax.fori_loop(0, tt, start, 0, unroll=8)
    for _ in range(2):
        _row_copy(h_ref, xs_ref.at[pl.ds(0, tt)], sem).wait()


def _dispatch(h, p1, p2, n_rows):
    M, D = h.shape
    tt = _tile(M, ROW_TILE, 8)
    return pl.pallas_call(
        _dispatch_kernel,
        grid_spec=pltpu.PrefetchScalarGridSpec(
            num_scalar_prefetch=2,
            grid=(M // tt,),
            in_specs=[pl.BlockSpec((tt, D), lambda i, p1, p2: (i, 0)),
                      pl.BlockSpec(memory_space=pl.ANY)],
            out_specs=pl.BlockSpec(memory_space=pl.ANY),
            scratch_shapes=[pltpu.SemaphoreType.DMA(())]),
        out_shape=jax.ShapeDtypeStruct((n_rows, D), h.dtype),
        input_output_aliases={3: 0},
        compiler_params=_params("arbitrary"),
        name="moe_dispatch",
    )(p1, p2, h, jnp.zeros((n_rows, D), h.dtype))


def _gmm_kernel(te_ref, nv_ref, x_ref, *refs, swiglu):
    o_ref = refs[-1]

    @pl.when(pl.program_id(0) < nv_ref[0])
    def _():
        x = x_ref[...].astype(BF16)
        acc = jnp.dot(x, refs[0][0, 0], preferred_element_type=F32)
        if swiglu:
            acc = acc * jax.nn.sigmoid(acc) * jnp.dot(x, refs[1][0, 0], preferred_element_type=F32)
        o_ref[...] = acc.astype(o_ref.dtype)

    @pl.when(pl.program_id(0) >= nv_ref[0])
    def _():
        o_ref[...] = jnp.zeros_like(o_ref)


def _gmm(x, ws, layer, tile_expert, n_valid, *, out_dtype, tn=512):
    P, K = x.shape
    N = ws[0].shape[3]
    tm = MOE_TILE
    tn = _tile(N, tn)
    wspec = pl.BlockSpec((1, 1, K, tn), lambda i, j, te, nv: (layer, te[i], 0, j))
    return pl.pallas_call(
        functools.partial(_gmm_kernel, swiglu=len(ws) == 2),
        grid_spec=pltpu.PrefetchScalarGridSpec(
            num_scalar_prefetch=2,
            grid=(P // tm, N // tn),
            in_specs=[pl.BlockSpec((tm, K), lambda i, j, te, nv: (i, 0))] + [wspec] * len(ws),
            out_specs=pl.BlockSpec((tm, tn), lambda i, j, te, nv: (i, j))),
        out_shape=jax.ShapeDtypeStruct((P, N), out_dtype),
        compiler_params=_params("parallel", "parallel"),
        name="moe_gmm",
    )(tile_expert, n_valid, x, *ws)


def _combine_kernel(p1_ref, p2_ref, route_ref, ys_ref, o_ref, a_ref, b_ref, sem):
    tt = o_ref.shape[0]
    base = pl.program_id(0) * tt

    def start(r, carry):
        _row_copy(ys_ref.at[pl.ds(p1_ref[base + r], 1)], a_ref.at[pl.ds(r, 1)], sem).start()
        _row_copy(ys_ref.at[pl.ds(p2_ref[base + r], 1)], b_ref.at[pl.ds(r, 1)], sem).start()
        return carry

    lax.fori_loop(0, tt, start, 0, unroll=8)
    _row_copy(ys_ref.at[pl.ds(0, tt)], a_ref, sem).wait()
    _row_copy(ys_ref.at[pl.ds(0, tt)], b_ref, sem).wait()
    route = route_ref[...]
    o_ref[...] = (route[:, ROUTE_G1:ROUTE_G1 + 1] * a_ref[...]
                  + route[:, ROUTE_G2:ROUTE_G2 + 1] * b_ref[...])


def _combine(ys, route, p1, p2):
    M = route.shape[0]
    D = ys.shape[1]
    tt = _tile(M, ROW_TILE, 8)
    return pl.pallas_call(
        _combine_kernel,
        grid_spec=pltpu.PrefetchScalarGridSpec(
            num_scalar_prefetch=2,
            grid=(M // tt,),
            in_specs=[pl.BlockSpec((tt, LANES), lambda i, p1, p2: (i, 0)),
                      pl.BlockSpec(memory_space=pl.ANY)],
            out_specs=pl.BlockSpec((tt, D), lambda i, p1, p2: (i, 0)),
            scratch_shapes=[pltpu.VMEM((tt, D), F32), pltpu.VMEM((tt, D), F32),
                            pltpu.SemaphoreType.DMA(())]),
        out_shape=jax.ShapeDtypeStruct((M, D), F32),
        compiler_params=_params("arbitrary"),
        name="moe_combine",
    )(p1, p2, route, ys)


def _moe(h, logits, w1, w3, w2, layer):
    M, D = h.shape
    E = w1.shape[1]
    tm = MOE_TILE
    route, sel = _gates(logits, E)
    rank, cnt = _rank(sel)
    counts = cnt[0, :E].astype(jnp.int32)
    padded = (counts + tm - 1) // tm * tm
    ends = jnp.cumsum(padded)
    n_tiles = (TOP_K * M) // tm + E
    tile_expert = jnp.minimum(
        jnp.searchsorted(ends, jnp.arange(n_tiles, dtype=jnp.int32) * tm, side="right"), E - 1
    ).astype(jnp.int32)
    n_valid = (ends[-1:] // tm).astype(jnp.int32)
    offs = jnp.zeros((1, LANES), F32).at[0, :E].set((ends - padded).astype(F32))
    pos = _positions(route, rank, offs)
    p1, p2 = pos[:, 0], pos[:, 1]
    xs = _dispatch(h, p1, p2, n_tiles * tm)
    mid = _gmm(xs, [w1, w3], layer, tile_expert, n_valid, out_dtype=BF16)
    ys = _gmm(mid, [w2], layer, tile_expert, n_valid, out_dtype=F32)
    return _combine(ys, route, p1, p2)


def kernel(x, c, ctx, c_ctx, ada_w, ada_b, ln_w, ln_b, hy_w_in, hy_b_in, hy_conv_w, hy_conv_b, hy_f_w1, hy_f_b1, hy_f_w2, hy_f_b2, hy_f_w3, hy_f_freq, hy_skip, hy_w_out, hy_b_out, rw_mu, rw_w_rkv, rw_w_o, rw_w0, rw_w1, rw_w2, rw_a0, rw_a1, rw_a2, rw_g1, rw_g2, rw_k_k, rw_k_a, rw_r_k, rw_gn_w, rw_gn_b, at_w_qkv, at_b_qkv, at_w_o, at_b_o, at_sink, ff_w1, ff_w3, ff_w2, moe_router, moe_w1, moe_w3, moe_w2):
    B, L, D = x.shape
    Lc = ctx.shape[1]
    depth = ada_w.shape[0]
    alpha = (2 * depth) ** 0.25
    assert Lc % ROW_TILE == 0 and L % ROW_TILE == 0 and D % (RW_GROUP * HEAD) == 0
    bf = lambda t: t.astype(BF16)
    w_bf = dict(hy_out=bf(hy_w_out), rw_o=bf(rw_w_o), at_qkv=bf(at_w_qkv), at_o=bf(at_w_o),
                ff1=bf(ff_w1), ff3=bf(ff_w3), ff2=bf(ff_w2),
                moe1=bf(moe_w1), moe3=bf(moe_w3), moe2=bf(moe_w2))

    ctx_row = B
    rows = -(-(B + 1) // 8) * 8
    cond = jnp.zeros((rows, D), F32).at[:B].set(c).at[B].set(c_ctx)
    mods = _ada(cond, ada_w, ada_b).reshape(depth, rows, N_MOD, D)

    tables = {}
    if depth > 0:
        tables[L] = _dft_tables(L)
        if depth > 1:
            tables[Lc] = _dft_tables(Lc)

    xs = jnp.concatenate([ctx, x], axis=1)
    ctx_tiles = Lc // ROW_TILE
    h = _modulate(xs, mods[0], sh=0, sc=1, ctx_tiles=ctx_tiles, ctx_row=ctx_row,
                  out_dtype=BF16)
    for i in range(depth):
        last = i == depth - 1
        kind = i % 3
        j = i // 3
        mod = mods[i]
        has_ctx = xs.shape[1] != L
        ct = ctx_tiles if has_ctx else 0
        T = xs.shape[1]
        if kind == 0:
            p = dict(w_in=bf(hy_w_in[j]), b_in=hy_b_in[j], conv_w=hy_conv_w[j], conv_b=hy_conv_b[j],
                     filter=(hy_f_w1[j], hy_f_b1[j], hy_f_w2[j], hy_f_b2[j], hy_f_w3[j], hy_f_freq[j]),
                     skip=hy_skip[j])
            y = _hyena(h, p, tables, Lc=Lc, has_ctx=has_ctx)
            proj = (w_bf["hy_out"], j, hy_b_out[j])
        elif kind == 1:
            assert has_ctx and not last
            p = dict(mu=rw_mu[j], w_rkv=bf(rw_w_rkv[j]), w0=rw_w0[j], w1=rw_w1[j],
                     w2=rw_w2[j], a0=rw_a0[j], a1=rw_a1[j], a2=rw_a2[j], g1=rw_g1[j], g2=rw_g2[j],
                     k_k=rw_k_k[j], k_a=rw_k_a[j], r_k=rw_r_k[j], gn_w=rw_gn_w[j], gn_b=rw_gn_b[j])
            y = _rwkv(h, p, Lc=Lc)
            proj = (w_bf["rw_o"], j, None)
        else:
            assert has_ctx and not last
            p = dict(w_qkv=w_bf["at_qkv"], index=j, b_qkv=at_b_qkv[j], sink=at_sink[j])
            y = _attention(h, p, Lc=Lc)
            proj = (w_bf["at_o"], j, at_b_o[j])

        drop1 = ct if last else 0
        fj = i // 2
        moe = i % 2 == 1
        if moe:
            pw, pi, pb = proj
            y = _mm(y.reshape(B * T, D), pw, wi=pi, bias=pb).reshape(B, T, D)
            proj = None
        res = _ln(xs, y, mod, ln_w[i, 0], ln_b[i, 0], alpha=alpha, gate=2, proj=proj, h_mod=(3, 4),
                  h_dtype=F32 if moe else BF16, router=moe_router[fj] if moe else None,
                  ctx_tiles=ct, ctx_row=ctx_row, drop_tiles=drop1)
        xs, h2 = res[0], res[1]
        ct = ct - drop1
        T = xs.shape[1]
        M = B * T
        if moe:
            f = _moe(h2.reshape(M, D), res[2].reshape(M, LANES), w_bf["moe1"], w_bf["moe3"], w_bf["moe2"],
                     fj).reshape(B, T, D)
            proj = None
        else:
            f = _mm(h2.reshape(M, D), w_bf["ff1"], w3=w_bf["ff3"], wi=fj, out_dtype=BF16).reshape(B, T, -1)
            proj = (w_bf["ff2"], fj, None)
        if last:
            (xs,) = _ln(xs, f, mod, ln_w[i, 1], ln_b[i, 1], alpha=alpha, gate=5, proj=proj,
                        ctx_tiles=ct, ctx_row=ctx_row)
        else:
            nxt_last = i + 1 == depth - 1
            nkind = (i + 1) % 3
            drop2 = ct if (nxt_last and nkind == 0) else 0
            xs, h = _ln(xs, f, mod, ln_w[i, 1], ln_b[i, 1], alpha=alpha, gate=5, proj=proj, h_mod=(0, 1),
                        mod_h=mods[i + 1], h_dtype=F32 if nkind == 1 else BF16,
                        ctx_tiles=ct, ctx_row=ctx_row, drop_tiles=drop2)
    return xs
```

```python
import functools
import math

import jax
import jax.numpy as jnp
from jax import lax
from jax.experimental import pallas as pl
from jax.experimental.pallas import tpu as pltpu

F32 = jnp.float32
BF16 = jnp.bfloat16
HIGHEST = lax.Precision.HIGHEST

VMEM_LIMIT_BYTES = 56 * 1024 * 1024
LANES = 128
ROW_TILE = 256
BF16_ROWS = 16

LN_EPS = 1e-5
N_MOD = 6
HEAD = 64
RW_GN_EPS = 64e-5
RW_CHUNK = 64
RW_GROUP = 4
AT_KV_HEADS = 4
AT_WINDOW = 128
AT_BLOCK = 128
GRID_W = 64
ROPE_BASE = 10000.0
NEG_INF = -1e30
TOP_K = 2
HY_BANDS = 16
HY_EMB = 2 * HY_BANDS + 1
HY_MIN_DECAY = -math.log(1e-2) / 1.5
HY_MAX_DECAY = -math.log(1e-2) / 0.3


def _params(*sem):
    return pltpu.CompilerParams(dimension_semantics=sem, vmem_limit_bytes=VMEM_LIMIT_BYTES)


def _tile(n, pref, mult=LANES):
    if n <= pref:
        return n
    t = (pref // mult) * mult
    while t >= mult:
        if n % t == 0:
            return t
        t -= mult
    return n


def _mm_kernel(*refs, n_w, has_bias, act):
    x = refs[0][...].astype(BF16)
    o_ref = refs[-1]
    acc = jnp.dot(x, refs[1][0].astype(BF16), preferred_element_type=F32)
    if has_bias:
        acc = acc + refs[1 + n_w][...]
    if act == "swiglu":
        acc3 = jnp.dot(x, refs[2][0].astype(BF16), preferred_element_type=F32)
        acc = acc * jax.nn.sigmoid(acc) * acc3
    o_ref[...] = acc.astype(o_ref.dtype)


def _mm(x, w, *, w3=None, wi=0, bias=None, out_dtype=F32, tm=1024, tn=512):
    M, K = x.shape
    N = w.shape[2]
    tm = _tile(M, tm, 8)
    tn = _tile(N, tn)
    ws = [w] if w3 is None else [w, w3]
    in_specs = [pl.BlockSpec((tm, K), lambda i, j: (i, 0))]
    in_specs += [pl.BlockSpec((1, K, tn), lambda i, j: (wi, 0, j)) for _ in ws]
    args = [x] + ws
    if bias is not None:
        in_specs.append(pl.BlockSpec((1, tn), lambda i, j: (0, j)))
        args.append(bias.reshape(1, N).astype(F32))
    return pl.pallas_call(
        functools.partial(_mm_kernel, n_w=len(ws), has_bias=bias is not None,
                          act="swiglu" if w3 is not None else None),
        grid=(M // tm, N // tn),
        in_specs=in_specs,
        out_specs=pl.BlockSpec((tm, tn), lambda i, j: (i, j)),
        out_shape=jax.ShapeDtypeStruct((M, N), out_dtype),
        compiler_params=_params("parallel", "parallel"),
        name="mm_swiglu" if w3 is not None else "mm",
    )(*args)


def _ada_kernel(c_ref, w_ref, b_ref, o_ref):
    c = c_ref[...]
    s = (c * jax.nn.sigmoid(c)).astype(BF16)
    o_ref[0] = jnp.dot(s, w_ref[0].astype(BF16), preferred_element_type=F32) + b_ref[0]


def _ada(cond, ada_w, ada_b):
    depth, D, N = ada_w.shape
    R = cond.shape[0]
    tn = _tile(N, 1024)
    return pl.pallas_call(
        _ada_kernel,
        grid=(depth, N // tn),
        in_specs=[pl.BlockSpec((R, D), lambda i, j: (0, 0)),
                  pl.BlockSpec((1, D, tn), lambda i, j: (i, 0, j)),
                  pl.BlockSpec((1, 1, tn), lambda i, j: (i, 0, j))],
        out_specs=pl.BlockSpec((1, R, tn), lambda i, j: (i, 0, j)),
        out_shape=jax.ShapeDtypeStruct((depth, R, N), F32),
        compiler_params=_params("parallel", "parallel"),
        name="ada",
    )(cond, ada_w, ada_b.reshape(depth, 1, N))


def _mod_index(ctx_tiles, ctx_row, off):
    def index(b, t):
        return (jnp.where(t + off < ctx_tiles, ctx_row, b), 0, 0)
    return index


def _modulate_kernel(x_ref, mod_ref, h_ref, *, sh, sc):
    x = x_ref[0]
    h_ref[0] = (x * (1.0 + mod_ref[0, sc:sc + 1, :]) + mod_ref[0, sh:sh + 1, :]).astype(h_ref.dtype)


def _modulate(x, mod, *, sh, sc, ctx_tiles, ctx_row, out_dtype):
    B, T, D = x.shape
    return pl.pallas_call(
        functools.partial(_modulate_kernel, sh=sh, sc=sc),
        grid=(B, T // ROW_TILE),
        in_specs=[pl.BlockSpec((1, ROW_TILE, D), lambda b, t: (b, t, 0)),
                  pl.BlockSpec((1, N_MOD, D), _mod_index(ctx_tiles, ctx_row, 0))],
        out_specs=pl.BlockSpec((1, ROW_TILE, D), lambda b, t: (b, t, 0)),
        out_shape=jax.ShapeDtypeStruct((B, T, D), out_dtype),
        compiler_params=_params("parallel", "parallel"),
        name="modulate",
    )(x, mod)


def _post_norm(x, y, mod_ref, w_ref, b_ref, *, alpha, gate):
    z = alpha * x + mod_ref[0, gate:gate + 1, :] * y
    mu = jnp.mean(z, axis=-1, keepdims=True)
    d = z - mu
    var = jnp.mean(d * d, axis=-1, keepdims=True)
    return d * lax.rsqrt(var + LN_EPS) * w_ref[...] + b_ref[...]


def _ln_kernel(*refs, alpha, gate, sh, sc, has_h, has_router, has_proj):
    x_ref, y_ref, mod_ref, modh_ref, w_ref, b_ref = refs[:6]
    pos = 6
    if has_proj:
        pw_ref, pb_ref = refs[pos:pos + 2]
        pos += 2
    router_ref = None
    if has_router:
        router_ref = refs[pos]
        pos += 1
    xo_ref = refs[pos]
    pos += 1
    if has_proj:
        y = jnp.dot(y_ref[0], pw_ref[0], preferred_element_type=F32) + pb_ref[...]
    else:
        y = y_ref[0].astype(F32)
    xn = _post_norm(x_ref[0], y, mod_ref, w_ref, b_ref, alpha=alpha, gate=gate)
    xo_ref[0] = xn
    if has_h:
        h = xn * (1.0 + modh_ref[0, sc:sc + 1, :]) + modh_ref[0, sh:sh + 1, :]
        h_ref = refs[pos]
        pos += 1
        h_ref[0] = h.astype(h_ref.dtype)
        if has_router:
            refs[pos][0] = jnp.dot(h, router_ref[...], precision=HIGHEST, preferred_element_type=F32)


def _ln(x, y, mod, ln_w, ln_b, *, alpha, gate, proj=None, h_mod=None, mod_h=None, h_dtype=BF16,
        router=None, ctx_tiles, ctx_row, drop_tiles=0):
    B, T, D = x.shape
    K = y.shape[2]
    nt = T // ROW_TILE - drop_tiles
    To = nt * ROW_TILE
    off = drop_tiles
    in_specs = [pl.BlockSpec((1, ROW_TILE, D), lambda b, t: (b, t + off, 0)),
                pl.BlockSpec((1, ROW_TILE, K), lambda b, t: (b, t + off, 0)),
                pl.BlockSpec((1, N_MOD, D), _mod_index(ctx_tiles, ctx_row, off)),
                pl.BlockSpec((1, N_MOD, D), _mod_index(ctx_tiles, ctx_row, off)),
                pl.BlockSpec((1, D), lambda b, t: (0, 0)),
                pl.BlockSpec((1, D), lambda b, t: (0, 0))]
    args = [x, y, mod, mod if mod_h is None else mod_h, ln_w.reshape(1, D), ln_b.reshape(1, D)]
    if proj is not None:
        pw, pi, pb = proj
        in_specs += [pl.BlockSpec((1, K, D), lambda b, t: (pi, 0, 0), pipeline_mode=pl.Buffered(1)),
                     pl.BlockSpec((1, D), lambda b, t: (0, 0))]
        args += [pw, jnp.zeros((1, D), F32) if pb is None else pb.reshape(1, D).astype(F32)]
    out_specs = [pl.BlockSpec((1, ROW_TILE, D), lambda b, t: (b, t, 0))]
    out_shape = [jax.ShapeDtypeStruct((B, To, D), F32)]
    sh = sc = 0
    if h_mod is not None:
        sh, sc = h_mod
        out_specs.append(pl.BlockSpec((1, ROW_TILE, D), lambda b, t: (b, t, 0)))
        out_shape.append(jax.ShapeDtypeStruct((B, To, D), h_dtype))
    if router is not None:
        E = router.shape[1]
        router_p = jnp.zeros((D, LANES), F32).at[:, :E].set(router)
        in_specs.append(pl.BlockSpec((D, LANES), lambda b, t: (0, 0)))
        args.append(router_p)
        out_specs.append(pl.BlockSpec((1, ROW_TILE, LANES), lambda b, t: (b, t, 0)))
        out_shape.append(jax.ShapeDtypeStruct((B, To, LANES), F32))
    return pl.pallas_call(
        functools.partial(_ln_kernel, alpha=alpha, gate=gate, sh=sh, sc=sc, has_h=h_mod is not None,
                          has_router=router is not None, has_proj=proj is not None),
        grid=(B, nt),
        in_specs=in_specs,
        out_specs=out_specs,
        out_shape=out_shape,
        compiler_params=_params("parallel", "parallel"),
        name="ln_residual",
    )(*args)


def _halo_specs(T, C, col):
    r8 = ROW_TILE // 8
    last8 = T // 8 - 1
    return [pl.BlockSpec((1, ROW_TILE, C), lambda b, t, j: (b, t, col(j))),
            pl.BlockSpec((1, 8, C), lambda b, t, j: (b, jnp.maximum(t * r8 - 1, 0), col(j))),
            pl.BlockSpec((1, 8, C), lambda b, t, j: (b, jnp.minimum(t * r8 + r8, last8), col(j)))]


def _neighbours(cur, prev8, next8, t, n_tiles, ctx_tiles):
    rows = lax.broadcasted_iota(jnp.int32, cur.shape, 0)
    has_prev = jnp.logical_and(t != 0, t != ctx_tiles)
    has_next = jnp.logical_and(t != n_tiles - 1, t != ctx_tiles - 1)
    top = jnp.where(has_prev, prev8[7:8, :], 0.0)
    bot = jnp.where(has_next, next8[0:1, :], 0.0)
    up = jnp.where(rows == 0, top, pltpu.roll(cur, 1, axis=0))
    dn = jnp.where(rows == cur.shape[0] - 1, bot, pltpu.roll(cur, cur.shape[0] - 1, axis=0))
    return up, dn


def _hy_in_kernel(x_ref, xp_ref, xn_ref, *refs, tm, T, Lc):
    ws, bs, cws, cbs = refs[0:3], refs[3:6], refs[6:9], refs[9:12]
    x0_ref, vv_ref, vvb_ref = refs[12:15]
    H = BF16_ROWS
    x = jnp.concatenate([xp_ref[0], x_ref[0], xn_ref[0]], axis=0)
    row = lax.broadcasted_iota(jnp.int32, (tm, 1), 0) + pl.program_id(1) * tm
    no_prev = (row == 0) | (row == Lc)
    no_next = (row == Lc - 1) | (row == T - 1)
    out = []
    for s in range(3):
        acc = jnp.dot(x, ws[s][...], preferred_element_type=F32) + bs[s][...]
        n = acc.shape[0]
        cur = acc[H:H + tm]
        up = jnp.where(no_prev, 0.0, pltpu.roll(acc, 1, axis=0)[H:H + tm])
        dn = jnp.where(no_next, 0.0, pltpu.roll(acc, n - 1, axis=0)[H:H + tm])
        cw = cws[s]
        out.append(up * cw[0:1, :] + cur * cw[1:2, :] + dn * cw[2:3, :] + cbs[s][...])
    x0_ref[0] = out[0]
    vv = out[1] * out[2]
    vv_ref[0] = vv
    vvb_ref[0] = vv.astype(BF16)


def _hy_in(h, w_in, b_in, conv_w, conv_b, *, Lc):
    B, T, K = h.shape
    D = w_in.shape[1] // 3
    H = BF16_ROWS
    tm = _tile(T, 1280, 64)
    tc = _tile(D, 512)
    nj = D // tc
    r16 = tm // H
    last16 = T // H - 1
    sec = lambda shape: [pl.BlockSpec(shape, lambda b, t, j, s=s: (0, s * nj + j)) for s in range(3)]
    in_specs = [pl.BlockSpec((1, tm, K), lambda b, t, j: (b, t, 0)),
                pl.BlockSpec((1, H, K), lambda b, t, j: (b, jnp.maximum(t * r16 - 1, 0), 0)),
                pl.BlockSpec((1, H, K), lambda b, t, j: (b, jnp.minimum(t * r16 + r16, last16), 0))]
    in_specs += sec((K, tc)) + sec((1, tc)) + sec((3, tc)) + sec((1, tc))
    blk = pl.BlockSpec((1, tm, tc), lambda b, t, j: (b, t, j))
    b2 = b_in.reshape(1, 3 * D)
    cb2 = conv_b.reshape(1, 3 * D)
    return pl.pallas_call(
        functools.partial(_hy_in_kernel, tm=tm, T=T, Lc=Lc),
        grid=(B, T // tm, nj),
        in_specs=in_specs,
        out_specs=[blk, blk, blk],
        out_shape=[jax.ShapeDtypeStruct((B, T, D), F32), jax.ShapeDtypeStruct((B, T, D), F32),
                   jax.ShapeDtypeStruct((B, T, D), BF16)],
        compiler_params=_params("parallel", "parallel", "parallel"),
        name="hy_in",
    )(h, h, h, w_in, w_in, w_in, b2, b2, b2, conv_w, conv_w, conv_w, cb2, cb2, cb2)


def _hy_filter_kernel(w1_ref, b1_ref, w2_ref, b2_ref, w3_ref, fr_ref, h_ref, s_ref, *, L, D, tl):
    i = pl.program_id(0)
    row = (lax.broadcasted_iota(jnp.int32, (tl, LANES), 0) + i * tl).astype(F32)
    lane = lax.broadcasted_iota(jnp.int32, (tl, LANES), 1)
    band = jnp.where(lane <= HY_BANDS, lane - 1, lane - 1 - HY_BANDS).astype(F32)
    freq = 1e-4 + band * ((HY_BANDS - 1 - 1e-4) / (HY_BANDS - 1))
    ang = freq * (row * (2.0 * math.pi / L))
    z = jnp.where(lane == 0, row / (L - 1),
                  jnp.where(lane <= HY_BANDS, jnp.cos(ang),
                            jnp.where(lane < HY_EMB, -jnp.sin(ang), 0.0)))
    h = jnp.sin(fr_ref[0:1, :] * (jnp.dot(z, w1_ref[...], precision=HIGHEST,
                                           preferred_element_type=F32) + b1_ref[...]))
    h = jnp.sin(fr_ref[1:2, :] * (jnp.dot(h, w2_ref[...], precision=HIGHEST,
                                           preferred_element_type=F32) + b2_ref[...]))
    h = jnp.dot(h, w3_ref[...], precision=HIGHEST, preferred_element_type=F32)
    half = L // 2
    rowd = (lax.broadcasted_iota(jnp.int32, (tl, D), 0) + i * tl).astype(F32)
    dist = jnp.abs(rowd - half) / half
    chan = lax.broadcasted_iota(jnp.int32, (tl, D), 1).astype(F32)
    deltas = HY_MIN_DECAY + chan * ((HY_MAX_DECAY - HY_MIN_DECAY) / (D - 1))
    h = h * jnp.exp(-dist * deltas)
    h_ref[...] = h

    @pl.when(i == 0)
    def _():
        s_ref[...] = jnp.zeros_like(s_ref)

    s_ref[...] += jnp.sum(jnp.abs(h), axis=0, keepdims=True)


def _hy_filter(L, f_w1, f_b1, f_w2, f_b2, f_w3, f_freq):
    D = f_w3.shape[1]
    hid = f_w1.shape[1]
    w1 = jnp.zeros((LANES, LANES), F32).at[:HY_EMB, :hid].set(f_w1)
    b1 = jnp.zeros((1, LANES), F32).at[0, :hid].set(f_b1)
    w2 = jnp.zeros((LANES, LANES), F32).at[:hid, :hid].set(f_w2)
    b2 = jnp.zeros((1, LANES), F32).at[0, :hid].set(f_b2)
    w3 = jnp.zeros((LANES, D), F32).at[:hid].set(f_w3)
    fr = jnp.zeros((2, LANES), F32).at[:, :hid].set(f_freq)
    tl = _tile(L, 256, 8)
    full = lambda shape: pl.BlockSpec(shape, lambda i: (0, 0))
    return pl.pallas_call(
        functools.partial(_hy_filter_kernel, L=L, D=D, tl=tl),
        grid=(L // tl,),
        in_specs=[full((LANES, LANES)), full((1, LANES)), full((LANES, LANES)), full((1, LANES)),
                  full((LANES, D)), full((2, LANES))],
        out_specs=[pl.BlockSpec((tl, D), lambda i: (i, 0)), full((1, D))],
        out_shape=[jax.ShapeDtypeStruct((L, D), F32), jax.ShapeDtypeStruct((1, D), F32)],
        compiler_params=_params("arbitrary"),
        name="hy_filter",
    )(w1, b1, w2, b2, w3, fr)


DFT_ROWS = 64


def _dft_tables_kernel(fa_ref, ia_ref, b_ref, fwd_ref, inv_ref, *, L):
    n = 2 * L
    i = pl.program_id(0)
    cb, sb = b_ref[0], b_ref[1]
    q = lax.broadcasted_iota(jnp.int32, (DFT_ROWS, L), 0)
    col = lax.broadcasted_iota(jnp.int32, (DFT_ROWS, L), 1)
    sign_c = jnp.where(col % 2 == 0, 1.0, -1.0)
    sign_r = jnp.where(q % 2 == 0, 1.0, -1.0)
    ca, sa = fa_ref[0, 0:1, :], fa_ref[0, 1:2, :]
    cos_f = ca * cb - sa * sb
    sin_f = sa * cb + ca * sb
    first_row = (q == 0) & (i == 0)
    fwd_ref[0] = cos_f.astype(fwd_ref.dtype)
    fwd_ref[1] = jnp.where(first_row, sign_c, -sin_f).astype(fwd_ref.dtype)
    ca, sa = ia_ref[0, 0:1, :], ia_ref[0, 1:2, :]
    cos_i = ca * cb - sa * sb
    sin_i = sa * cb + ca * sb
    inv_ref[:, :L] = jnp.where(col == 0, 1.0 / n, (2.0 / n) * cos_i).astype(inv_ref.dtype)
    inv_ref[:, L:] = jnp.where(col == 0, sign_r / n, (-2.0 / n) * sin_i).astype(inv_ref.dtype)


def _dft_tables(L):
    n = 2 * L
    R = DFT_ROWS
    assert L % (2 * R) == 0
    w = 2.0 * math.pi / n
    c = jnp.arange(L, dtype=jnp.int32)
    hi = jnp.arange(L // R, dtype=jnp.int32) * R

    def cos_sin(rows):
        ang = ((rows[:, None] * c[None, :]) % n).astype(F32) * w
        return jnp.stack([jnp.cos(ang), jnp.sin(ang)], axis=1)

    fa = cos_sin(hi)
    ia = cos_sin(hi + L // 2)
    b = jnp.swapaxes(cos_sin(jnp.arange(R, dtype=jnp.int32)), 0, 1)
    return pl.pallas_call(
        functools.partial(_dft_tables_kernel, L=L),
        grid=(L // R,),
        in_specs=[pl.BlockSpec((1, 2, L), lambda i: (i, 0, 0)),
                  pl.BlockSpec((1, 2, L), lambda i: (i, 0, 0)),
                  pl.BlockSpec((2, R, L), lambda i: (0, 0, 0))],
        out_specs=[pl.BlockSpec((2, R, L), lambda i: (0, i, 0)),
                   pl.BlockSpec((R, 2 * L), lambda i: (i, 0))],
        out_shape=[jax.ShapeDtypeStruct((2, L, L), BF16), jax.ShapeDtypeStruct((L, 2 * L), BF16)],
        compiler_params=_params("parallel"),
        name="dft_tables",
    )(fa, ia, b)


def _dft_fwd_kernel(a_ref, w_ref, *rest, mode):
    o_ref = rest[-1]
    w = w_ref[0]
    vre = jnp.dot(a_ref[0], w, preferred_element_type=F32)
    vim = jnp.dot(a_ref[1], w, preferred_element_type=F32)
    if mode == "scale":
        inv = 1.0 / (rest[0][...] + 1e-6)
        o_ref[0, 0] = vre * inv
        o_ref[0, 1] = vim * inv
    else:
        hre = rest[0][0]
        him = rest[0][1]
        first = jnp.logical_and(pl.program_id(2) == 0,
                                lax.broadcasted_iota(jnp.int32, vre.shape, 0) == 0)
        zre = jnp.where(first, vre * hre, vre * hre - vim * him)
        zim = jnp.where(first, vim * him, vre * him + vim * hre)
        o_ref[0, 0] = zre.astype(o_ref.dtype)
        o_ref[0, 1] = zim.astype(o_ref.dtype)


def _dft_fwd(fwd, w, extra, *, mode, out_dtype):
    B, L, D = w.shape
    tm = _tile(L, 256, 8)
    tn = _tile(D, 1024)
    if mode == "scale":
        extra_spec = pl.BlockSpec((1, tn), lambda b, j, m: (0, j))
    else:
        extra_spec = pl.BlockSpec((2, tm, tn), lambda b, j, m: (0, m, j))
    return pl.pallas_call(
        functools.partial(_dft_fwd_kernel, mode=mode),
        grid=(B, D // tn, L // tm),
        in_specs=[pl.BlockSpec((2, tm, L), lambda b, j, m: (0, m, 0)),
                  pl.BlockSpec((1, L, tn), lambda b, j, m: (b, 0, j)),
                  extra_spec],
        out_specs=pl.BlockSpec((1, 2, tm, tn), lambda b, j, m: (b, 0, m, j)),
        out_shape=jax.ShapeDtypeStruct((B, 2, L, D), out_dtype),
        compiler_params=_params("parallel", "parallel", "parallel"),
        name="dft_fwd",
    )(fwd, w, extra)


def _dft_inv_kernel(b_ref, z_ref, x0_ref, vv_ref, skip_ref, u_ref):
    y = jnp.dot(b_ref[...], z_ref[0], preferred_element_type=F32)
    u_ref[0] = (x0_ref[0] * (y + vv_ref[0] * skip_ref[...])).astype(u_ref.dtype)


def _dft_inv(inv, z, x0, vv, skip, *, row_off):
    B, n, D = z.shape
    L = n // 2
    tm = _tile(L, ROW_TILE, 8)
    tn = _tile(D, 1024)
    off = row_off // tm
    return pl.pallas_call(
        _dft_inv_kernel,
        grid=(B, D // tn, L // tm),
        in_specs=[pl.BlockSpec((tm, n), lambda b, j, m: (m, 0)),
                  pl.BlockSpec((1, n, tn), lambda b, j, m: (b, 0, j)),
                  pl.BlockSpec((1, tm, tn), lambda b, j, m: (b, m + off, j)),
                  pl.BlockSpec((1, tm, tn), lambda b, j, m: (b, m + off, j)),
                  pl.BlockSpec((1, tn), lambda b, j, m: (0, j))],
        out_specs=pl.BlockSpec((1, tm, tn), lambda b, j, m: (b, m, j)),
        out_shape=jax.ShapeDtypeStruct((B, L, D), BF16),
        compiler_params=_params("parallel", "parallel", "parallel"),
        name="dft_inv",
    )(inv, z, x0, vv, skip.reshape(1, D))


def _hyena(h, p, tables, *, Lc, has_ctx):
    B, T, D = h.shape
    x0, vv, vvb = _hy_in(h, p["w_in"], p["b_in"], p["conv_w"], p["conv_b"], Lc=Lc if has_ctx else -1)
    segs = [(Lc, T - Lc)] if has_ctx else [(0, T)]
    if has_ctx:
        segs = [(0, Lc)] + segs
    us = []
    for start, L in segs:
        fwd, inv = tables[L]
        filt, asum = _hy_filter(L, *p["filter"])
        hf = _dft_fwd(fwd, filt.astype(BF16)[None], asum, mode="scale", out_dtype=F32)[0]
        seg = vvb if (start == 0 and L == T) else lax.slice_in_dim(vvb, start, start + L, axis=1)
        z = _dft_fwd(fwd, seg, hf, mode="mul", out_dtype=BF16).reshape(B, 2 * L, D)
        us.append(_dft_inv(inv, z, x0, vv, p["skip"], row_off=start))
    return us[0] if len(us) == 1 else jnp.concatenate(us, axis=1)


MU_R, MU_W, MU_K, MU_V, MU_A, MU_G = range(6)


def _rw_mix(cur_ref, prev_ref, next_ref, mu_ref, which, *, n_tiles, ctx_tiles):
    cur = cur_ref[0]
    up, dn = _neighbours(cur, prev_ref[0], next_ref[0], pl.program_id(1), n_tiles, ctx_tiles)
    dx = 0.5 * (up + dn) - cur
    return [(cur + dx * mu_ref[j:j + 1, :]).astype(BF16) for j in which]


def _rw_rkvg_kernel(cur_ref, prev_ref, next_ref, mu_ref, wr_ref, wk_ref, wv_ref, g1_ref, g2_ref,
                    r_ref, k_ref, v_ref, g_ref, **tiles):
    xr, xk, xv, xg = _rw_mix(cur_ref, prev_ref, next_ref, mu_ref, (MU_R, MU_K, MU_V, MU_G), **tiles)
    r_ref[0] = jnp.dot(xr, wr_ref[...], preferred_element_type=F32)
    k_ref[0] = jnp.dot(xk, wk_ref[...], preferred_element_type=F32)
    v_ref[0] = jnp.dot(xv, wv_ref[...], preferred_element_type=F32)
    t = jax.nn.sigmoid(jnp.dot(xg, g1_ref[...], preferred_element_type=F32)).astype(BF16)
    g_ref[0] = jnp.dot(t, g2_ref[...], preferred_element_type=F32)


def _rw_decay_kernel(cur_ref, prev_ref, next_ref, mu_ref, w1_ref, w2_ref, w0_ref, a1_ref, a2_ref, a0_ref,
                     lwf_ref, af_ref, lwb_ref, ab_ref, **tiles):
    xw, xa = _rw_mix(cur_ref, prev_ref, next_ref, mu_ref, (MU_W, MU_A), **tiles)
    R = w2_ref.shape[1]
    tw = jnp.tanh(jnp.dot(xw, w1_ref[...], preferred_element_type=F32)).astype(BF16)
    ta = jnp.dot(xa, a1_ref[...], preferred_element_type=F32).astype(BF16)
    for d, (lw_ref, a_ref) in enumerate(((lwf_ref, af_ref), (lwb_ref, ab_ref))):
        wl = jnp.dot(tw[:, d * R:(d + 1) * R], w2_ref[d], preferred_element_type=F32) + w0_ref[d:d + 1, :]
        lw_ref[0] = -math.exp(-0.5) * jax.nn.sigmoid(wl)
        al = jnp.dot(ta[:, d * R:(d + 1) * R], a2_ref[d], preferred_element_type=F32) + a0_ref[d:d + 1, :]
        a_ref[0] = jax.nn.sigmoid(al)


def _pad_rank(first, second):
    _, D, r = first.shape
    R = -(-r // LANES) * LANES
    f = jnp.zeros((D, 2 * R), BF16)
    s = jnp.zeros((2, R, D), BF16)
    for d in range(2):
        f = f.at[:, d * R:d * R + r].set(first[d].astype(BF16))
        s = s.at[d, :r].set(second[d].astype(BF16))
    return f, s


def _rw_project(hs, p, *, ctx_tiles):
    B, T, D = hs.shape
    n_tiles = T // ROW_TILE
    tiles = dict(n_tiles=n_tiles, ctx_tiles=ctx_tiles)
    halo = _halo_specs(T, D, lambda j: 0)
    once = pl.Buffered(1)
    full = lambda a: pl.BlockSpec(a.shape, lambda b, t, j: (0,) * a.ndim, pipeline_mode=once)
    blk = pl.BlockSpec((1, ROW_TILE, D), lambda b, t, j: (b, t, 0))
    out = jax.ShapeDtypeStruct((B, T, D), F32)
    mu = p["mu"]
    wr, wk, wv = p["w_rkv"][0], p["w_rkv"][1], p["w_rkv"][2]
    g1, g2 = p["g1"].astype(BF16), p["g2"].astype(BF16)
    r, k, v, g = pl.pallas_call(
        functools.partial(_rw_rkvg_kernel, **tiles),
        grid=(B, n_tiles, 1),
        in_specs=halo + [full(a) for a in (mu, wr, wk, wv, g1, g2)],
        out_specs=[blk] * 4,
        out_shape=[out] * 4,
        compiler_params=_params("parallel", "parallel", "parallel"),
        name="rw_rkvg",
    )(hs, hs, hs, mu, wr, wk, wv, g1, g2)
    w1, w2 = _pad_rank(p["w1"], p["w2"])
    a1, a2 = _pad_rank(p["a1"], p["a2"])
    consts = (mu, w1, w2, p["w0"], a1, a2, p["a0"])
    lwf, af, lwb, ab = pl.pallas_call(
        functools.partial(_rw_decay_kernel, **tiles),
        grid=(B, n_tiles, 1),
        in_specs=halo + [full(a) for a in consts],
        out_specs=[blk] * 4,
        out_shape=[out] * 4,
        compiler_params=_params("parallel", "parallel", "parallel"),
        name="rw_decay",
    )(hs, hs, hs, *consts)
    return r, k, v, g, ((lwf, af), (lwb, ab))


def _block_diag(x, mask):
    return jnp.where(mask, jnp.concatenate([x] * RW_GROUP, axis=0), 0.0).astype(BF16)


def _rw_chunks(r, k, v, lw, a, k_k, k_a, state, same_head, incl, strict, reverse):
    C = RW_CHUNK
    W = RW_GROUP * HEAD
    G = range(len(r))
    nt = (((1,), (1,)), ((), ()))
    ones = jnp.where(same_head, 1.0, 0.0).astype(BF16)
    tri = jnp.where(incl[:, :C], 1.0, 0.0).astype(BF16)
    bd = lambda t: _block_diag(t, same_head)
    mm = lambda x, y: jnp.dot(x.astype(BF16), y, preferred_element_type=F32)

    def split(t):
        hi = t.astype(BF16)
        return hi, (t - hi.astype(F32)).astype(BF16)

    kkr = [k[g] * k_k[g] for g in G]
    sq = [split(kkr[g] * kkr[g]) for g in G]
    ng = len(r)
    sums = jnp.dot(jnp.concatenate([sq[g][0] for g in G] + [sq[g][1] for g in G], axis=0), ones,
                   preferred_element_type=F32)
    ss = [sums[g * C:(g + 1) * C] + sums[(ng + g) * C:(ng + g + 1) * C] for g in G]
    lws = [split(lw[g]) for g in G]
    lp = [jnp.dot(tri, lws[g][0], preferred_element_type=F32)
          + jnp.dot(tri, lws[g][1], preferred_element_type=F32) for g in G]
    kk = [kkr[g] / jnp.maximum(jnp.sqrt(ss[g]), 1e-12) for g in G]
    kd = [k[g] * (1.0 + (a[g] - 1.0) * k_a[g]) for g in G]
    lp_end = [lp[g][0:1, :] if reverse else lp[g][C - 1:C, :] for g in G]
    e_neg = [jnp.exp(-lp[g]) for g in G]
    lhs = [jnp.concatenate([-kk[g] * jnp.exp(lp[g] - lw[g]), r[g] * jnp.exp(lp[g])], axis=0).astype(BF16)
           for g in G]
    rhs = [jnp.concatenate([bd(kk[g] * a[g] * e_neg[g]), bd(kd[g] * e_neg[g])], axis=0) for g in G]
    cross = [lax.dot_general(lhs[g], rhs[g], nt, preferred_element_type=F32) for g in G]
    from_state = [lax.dot_general(lhs[g], state[g].astype(BF16), nt, preferred_element_type=F32)
                  for g in G]
    v_bd = [bd(v[g]) for g in G]
    p = [jnp.where(strict, cross[g][:C, :W], 0.0) for g in G]
    x = [from_state[g][:C] + mm(jnp.where(strict, cross[g][:C, W:], 0.0), v_bd[g]) for g in G]
    n = 1
    while n < C:
        x = [x[g] + mm(p[g], bd(x[g])) for g in G]
        n *= 2
        if n < C:
            p = [mm(p[g], bd(p[g])) for g in G]
    y = [from_state[g][C:] + mm(jnp.where(incl, cross[g][C:, :W], 0.0), bd(x[g]))
         + mm(jnp.where(incl, cross[g][C:, W:], 0.0), v_bd[g]) for g in G]

    tail = [jnp.exp(lp_end[g] - lp[g]) for g in G]
    uv = [jnp.concatenate([x[g], v[g]], axis=0).astype(BF16) for g in G]
    bk = [jnp.concatenate([kk[g] * a[g] * tail[g], kd[g] * tail[g]], axis=0).astype(BF16) for g in G]
    upd = [lax.dot_general(uv[g], bk[g], (((0,), (0,)), ((), ())), preferred_element_type=F32)
           for g in G]
    new_state = [state[g] * jnp.exp(lp_end[g]) + jnp.where(same_head, upd[g], 0.0) for g in G]
    return y, new_state


RW_GROUPS_PER_STEP = 8
RW_CHUNKS_PER_STEP = 2


def _rw_scan_kernel(r_ref, k_ref, v_ref, lw_ref, a_ref, kk_ref, ka_ref, o_ref, state_ref, *, reverse):
    C = RW_CHUNK
    W = RW_GROUP * HEAD

    @pl.when(pl.program_id(2) == 0)
    def _():
        state_ref[...] = jnp.zeros_like(state_ref)

    ri = lax.broadcasted_iota(jnp.int32, (W, W), 0)
    ci = lax.broadcasted_iota(jnp.int32, (W, W), 1)
    same_head = (ri // HEAD) == (ci // HEAD)
    t_i = lax.broadcasted_iota(jnp.int32, (C, W), 0)
    s_i = lax.broadcasted_iota(jnp.int32, (C, W), 1) % C
    incl = (s_i >= t_i) if reverse else (s_i <= t_i)
    strict = (s_i > t_i) if reverse else (s_i < t_i)

    ng = state_ref.shape[0]
    sl = [slice(g * W, (g + 1) * W) for g in range(ng)]
    state = [state_ref[g] for g in range(ng)]
    order = range(RW_CHUNKS_PER_STEP)
    for c in (reversed(order) if reverse else order):
        rows = slice(c * C, (c + 1) * C)
        load = lambda ref: [ref[0, rows, s] for s in sl]
        y, state = _rw_chunks(load(r_ref), load(k_ref), load(v_ref), load(lw_ref), load(a_ref),
                              [kk_ref[:, s] for s in sl], [ka_ref[:, s] for s in sl],
                              state, same_head, incl, strict, reverse)
        for g in range(ng):
            o_ref[0, rows, sl[g]] = y[g]
    for g in range(ng):
        state_ref[g] = state[g]


def _rw_scan(r, k, v, lw, a, k_k, k_a, *, Lc, reverse):
    B, T, D = r.shape
    assert RW_CHUNK == HEAD
    rows = RW_CHUNK * RW_CHUNKS_PER_STEP
    assert Lc % rows == 0 and T % rows == 0
    W = RW_GROUP * HEAD
    ng = math.gcd(D // W, RW_GROUPS_PER_STEP)
    nb = T // rows
    ncb = Lc // rows

    def block(c):
        if not reverse:
            return c
        return jnp.where(c < ncb, ncb - 1 - c, nb - 1 - (c - ncb))

    blk = pl.BlockSpec((1, rows, ng * W), lambda b, g, c: (b, block(c), g))
    vec = pl.BlockSpec((1, ng * W), lambda b, g, c: (0, g))
    return pl.pallas_call(
        functools.partial(_rw_scan_kernel, reverse=reverse),
        grid=(B, D // (ng * W), nb),
        in_specs=[blk] * 5 + [vec, vec],
        out_specs=blk,
        out_shape=jax.ShapeDtypeStruct((B, T, D), F32),
        scratch_shapes=[pltpu.VMEM((ng, W, W), F32)],
        compiler_params=_params("parallel", "parallel", "arbitrary"),
        name="rw_scan",
    )(r, k, v, lw, a, k_k.reshape(1, D), k_a.reshape(1, D))


def _head_sums(x, ones_bd):
    W = ones_bd.shape[0]
    hi = x.astype(BF16)
    lo = (x - hi.astype(F32)).astype(BF16)
    cols = []
    for g in range(x.shape[1] // W):
        sl = slice(g * W, (g + 1) * W)
        cols.append(jnp.dot(hi[:, sl], ones_bd, preferred_element_type=F32)
                    + jnp.dot(lo[:, sl], ones_bd, preferred_element_type=F32))
    return jnp.concatenate(cols, axis=1)


def _rw_post_kernel(of_ref, ob_ref, r_ref, k_ref, v_ref, g_ref, rk_ref, gw_ref, gb_ref, o_ref):
    W = RW_GROUP * HEAD
    ri = lax.broadcasted_iota(jnp.int32, (W, W), 0)
    ci = lax.broadcasted_iota(jnp.int32, (W, W), 1)
    ones_bd = jnp.where((ri // HEAD) == (ci // HEAD), 1.0, 0.0).astype(BF16)
    o = of_ref[0] + ob_ref[0]
    m = _head_sums(o, ones_bd) * (1.0 / HEAD)
    d = o - m
    var = _head_sums(d * d, ones_bd) * (1.0 / HEAD)
    on = d * lax.rsqrt(var + RW_GN_EPS) * gw_ref[...] + gb_ref[...]
    bonus = _head_sums(r_ref[0] * k_ref[0] * rk_ref[...], ones_bd) * v_ref[0]
    o_ref[0] = ((on + bonus) * g_ref[0]).astype(o_ref.dtype)


def _rw_post(o_f, o_b, r, k, v, g, r_k, gn_w, gn_b):
    B, T, D = r.shape
    tt = _tile(T, 128, 8)
    blk = pl.BlockSpec((1, tt, D), lambda b, t: (b, t, 0))
    par = pl.BlockSpec((1, D), lambda b, t: (0, 0))
    return pl.pallas_call(
        _rw_post_kernel,
        grid=(B, T // tt),
        in_specs=[blk] * 6 + [par] * 3,
        out_specs=blk,
        out_shape=jax.ShapeDtypeStruct((B, T, D), BF16),
        compiler_params=_params("parallel", "parallel"),
        name="rw_post",
    )(o_f, o_b, r, k, v, g, r_k.reshape(1, D), gn_w.reshape(1, D), gn_b.reshape(1, D))


def _rwkv(hs, p, *, Lc):
    r, k, v, g, dirs = _rw_project(hs, p, ctx_tiles=Lc // ROW_TILE)
    outs = [_rw_scan(r, k, v, lw, a, p["k_k"], p["k_a"], Lc=Lc, reverse=d == 1)
            for d, (lw, a) in enumerate(dirs)]
    return _rw_post(outs[0], outs[1], r, k, v, g, p["r_k"], p["gn_w"], p["gn_b"])


def _rope_tables(T, Lc):
    quarter = HEAD // 4
    pos = jnp.arange(T - Lc, dtype=jnp.int32)
    rows = (pos // GRID_W).astype(F32)
    cols = (pos % GRID_W).astype(F32)
    inv = ROPE_BASE ** (-jnp.arange(quarter, dtype=F32) / quarter)
    ang = jnp.concatenate([rows[:, None] * inv, rows[:, None] * inv,
                           cols[:, None] * inv, cols[:, None] * inv], axis=1)
    ang = jnp.concatenate([jnp.zeros((Lc, HEAD), F32), ang], axis=0)
    ang = jnp.concatenate([ang, ang], axis=1)
    return jnp.cos(ang), jnp.sin(ang)


def _qkv_rope_kernel(h_ref, w_ref, b_ref, cos_ref, sin_ref, q_ref, k_ref, v_ref, *, n_q, n_kv, scale):
    x = jnp.dot(h_ref[0], w_ref[0], preferred_element_type=F32) + b_ref[...]
    n_rot = (n_q + n_kv) * HEAD
    xr = x[:, :n_rot]
    reps = n_rot // LANES
    cos = jnp.concatenate([cos_ref[...]] * reps, axis=1)
    sin = jnp.concatenate([sin_ref[...]] * reps, axis=1)
    quarter = HEAD // 4
    lane = lax.broadcasted_iota(jnp.int32, xr.shape, 1)
    first = (lane % (2 * quarter)) < quarter
    rot = jnp.where(first, -pltpu.roll(xr, n_rot - quarter, axis=1), pltpu.roll(xr, quarter, axis=1))
    y = xr * cos + rot * sin
    for h in range(n_q):
        q_ref[0, h] = (y[:, h * HEAD:(h + 1) * HEAD] * scale).astype(q_ref.dtype)
    for h in range(n_kv):
        lo = (n_q + h) * HEAD
        k_ref[0, h] = y[:, lo:lo + HEAD].astype(k_ref.dtype)
        lo = (n_q + n_kv + h) * HEAD
        v_ref[0, h, :, :HEAD] = x[:, lo:lo + HEAD].astype(v_ref.dtype)
        v_ref[0, h, :, HEAD:] = jnp.ones((x.shape[0], HEAD), v_ref.dtype)


def _qkv_rope(h, w, wi, bias, cos, sin, *, n_q, n_kv):
    B, T, K = h.shape
    W = w.shape[2]
    tt = _tile(T, ROW_TILE, BF16_ROWS)
    out = lambda n, width=HEAD: pl.BlockSpec((1, n, tt, width), lambda b, t: (b, 0, t, 0))
    return pl.pallas_call(
        functools.partial(_qkv_rope_kernel, n_q=n_q, n_kv=n_kv, scale=HEAD ** -0.5),
        grid=(B, T // tt),
        in_specs=[pl.BlockSpec((1, tt, K), lambda b, t: (b, t, 0)),
                  pl.BlockSpec((1, K, W), lambda b, t: (wi, 0, 0), pipeline_mode=pl.Buffered(1)),
                  pl.BlockSpec((1, W), lambda b, t: (0, 0)),
                  pl.BlockSpec((tt, LANES), lambda b, t: (t, 0)),
                  pl.BlockSpec((tt, LANES), lambda b, t: (t, 0))],
        out_specs=[out(n_q), out(n_kv), out(n_kv, 2 * HEAD)],
        out_shape=[jax.ShapeDtypeStruct((B, n_q, T, HEAD), BF16),
                   jax.ShapeDtypeStruct((B, n_kv, T, HEAD), BF16),
                   jax.ShapeDtypeStruct((B, n_kv, T, 2 * HEAD), BF16)],
        compiler_params=_params("parallel", "parallel"),
        name="qkv_rope",
    )(h, w, bias.reshape(1, W).astype(F32), cos, sin)


def _attn_kernel(*refs, n_q, n_kv, local, q_off, n_blocks):
    if local:
        q_ref, kp_ref, kc_ref, kn_ref, vp_ref, vc_ref, vn_ref, kx_ref, vx_ref, sink_ref, o_ref = refs
    else:
        q_ref, kx_ref, vx_ref, sink_ref, o_ref = refs
    G = n_q // n_kv
    R = G * AT_BLOCK
    n = pl.program_id(1)
    nt = (((1,), (1,)), ((), ()))
    if local:
        qi = lax.broadcasted_iota(jnp.int32, (R, 3 * AT_BLOCK), 0) % AT_BLOCK
        kj = lax.broadcasted_iota(jnp.int32, (R, 3 * AT_BLOCK), 1) - AT_BLOCK
        kpos = kj + n * AT_BLOCK
        ok = (jnp.abs(qi - kj) <= AT_WINDOW) & (kpos >= 0) & (kpos < n_blocks * AT_BLOCK)
    H = range(n_kv)
    q = [q_ref[0, h * G:(h + 1) * G].reshape(R, HEAD) for h in H]
    sink = [jnp.concatenate(
        [jnp.broadcast_to(sink_ref[h * G + g:h * G + g + 1, 0:1], (AT_BLOCK, 1)) for g in range(G)],
        axis=0) for h in H]
    s_ctx = [lax.dot_general(q[h], kx_ref[0, h], nt, preferred_element_type=F32) for h in H]
    m = [jnp.maximum(jnp.max(s_ctx[h], axis=-1, keepdims=True), sink[h]) for h in H]
    if local:
        k_loc = [jnp.concatenate([kp_ref[0, h], kc_ref[0, h], kn_ref[0, h]], axis=0) for h in H]
        v_loc = [jnp.concatenate([vp_ref[0, h], vc_ref[0, h], vn_ref[0, h]], axis=0) for h in H]
        s_loc = [jnp.where(ok, lax.dot_general(q[h], k_loc[h], nt, preferred_element_type=F32), NEG_INF)
                 for h in H]
        m = [jnp.maximum(m[h], jnp.max(s_loc[h], axis=-1, keepdims=True)) for h in H]
    o = [jnp.dot(jnp.exp(s_ctx[h] - m[h]).astype(BF16), vx_ref[0, h], preferred_element_type=F32)
         for h in H]
    if local:
        o = [o[h] + jnp.dot(jnp.exp(s_loc[h] - m[h]).astype(BF16), v_loc[h], preferred_element_type=F32)
             for h in H]
    den = [o[h][:, HEAD:HEAD + 1] + jnp.exp(sink[h] - m[h]) for h in H]
    o = [o[h][:, :HEAD] * (1.0 / den[h]) for h in H]
    for h in H:
        for g in range(G):
            hh = h * G + g
            o_ref[0, :, hh * HEAD:(hh + 1) * HEAD] = o[h][g * AT_BLOCK:(g + 1) * AT_BLOCK].astype(o_ref.dtype)


def _attn(q, k, v, sink, *, Lc, local):
    B, n_q, T, _ = q.shape
    n_kv = k.shape[1]
    cb = Lc // AT_BLOCK
    nb = (T - Lc) // AT_BLOCK if local else cb
    q_off = cb if local else 0
    last = T // AT_BLOCK - 1
    qspec = pl.BlockSpec((1, n_q, AT_BLOCK, HEAD), lambda b, n: (b, 0, n + q_off, 0))
    xspec = lambda a: pl.BlockSpec((1, n_kv, Lc, a.shape[3]), lambda b, n: (b, 0, 0, 0))
    sspec = pl.BlockSpec((n_q, LANES), lambda b, n: (0, 0))
    sink_b = jnp.broadcast_to(sink.astype(F32)[:, None], (n_q, LANES))
    if local:
        blk = lambda a, f: pl.BlockSpec((1, n_kv, AT_BLOCK, a.shape[3]), lambda b, n: (b, 0, f(n), 0))
        prev = lambda n: jnp.maximum(n + cb - 1, cb)
        cur = lambda n: n + cb
        nxt = lambda n: jnp.minimum(n + cb + 1, last)
        in_specs = ([qspec] + [blk(k, f) for f in (prev, cur, nxt)] + [blk(v, f) for f in (prev, cur, nxt)]
                    + [xspec(k), xspec(v), sspec])
        args = (q, k, k, k, v, v, v, k, v, sink_b)
    else:
        in_specs = [qspec, xspec(k), xspec(v), sspec]
        args = (q, k, v, sink_b)
    return pl.pallas_call(
        functools.partial(_attn_kernel, n_q=n_q, n_kv=n_kv, local=local, q_off=q_off, n_blocks=nb),
        grid=(B, nb),
        in_specs=in_specs,
        out_specs=pl.BlockSpec((1, AT_BLOCK, n_q * HEAD), lambda b, n: (b, n, 0)),
        out_shape=jax.ShapeDtypeStruct((B, nb * AT_BLOCK, n_q * HEAD), BF16),
        compiler_params=_params("parallel", "parallel"),
        name="attn_local" if local else "attn_ctx",
    )(*args)


def _attention(h, p, *, Lc):
    B, T, D = h.shape
    n_q = D // HEAD
    cos, sin = _rope_tables(T, Lc)
    q, k, v = _qkv_rope(h, p["w_qkv"], p["index"], p["b_qkv"], cos, sin, n_q=n_q, n_kv=AT_KV_HEADS)
    o_ctx = _attn(q, k, v, p["sink"], Lc=Lc, local=False)
    o_lat = _attn(q, k, v, p["sink"], Lc=Lc, local=True)
    return jnp.concatenate([o_ctx, o_lat], axis=1)


MOE_TILE = 1024
ROUTE_G1, ROUTE_G2, ROUTE_I1, ROUTE_I2 = 0, 1, 2, 3


def _gates_kernel(l_ref, route_ref, sel_ref, *, n_experts):
    l = l_ref[...]
    lane = lax.broadcasted_iota(jnp.int32, l.shape, 1)
    l = jnp.where(lane < n_experts, l, -jnp.inf)
    m1 = jnp.max(l, axis=-1, keepdims=True)
    i1 = jnp.min(jnp.where(l == m1, lane, LANES), axis=-1, keepdims=True)
    l2 = jnp.where(lane == i1, -jnp.inf, l)
    m2 = jnp.max(l2, axis=-1, keepdims=True)
    i2 = jnp.min(jnp.where(l2 == m2, lane, LANES), axis=-1, keepdims=True)
    e2 = jnp.exp(m2 - m1)
    den = 1.0 + e2
    route_ref[...] = jnp.where(lane == ROUTE_G1, 1.0 / den,
                               jnp.where(lane == ROUTE_G2, e2 / den,
                                         jnp.where(lane == ROUTE_I1, i1.astype(F32),
                                                   jnp.where(lane == ROUTE_I2, i2.astype(F32), 0.0))))
    sel_ref[...] = jnp.where((lane == i1) | (lane == i2), 1.0, 0.0).astype(sel_ref.dtype)


def _gates(logits, n_experts):
    M = logits.shape[0]
    tm = _tile(M, 1024, 8)
    blk = pl.BlockSpec((tm, LANES), lambda i: (i, 0))
    return pl.pallas_call(
        functools.partial(_gates_kernel, n_experts=n_experts),
        grid=(M // tm,),
        in_specs=[blk],
        out_specs=[blk, blk],
        out_shape=[jax.ShapeDtypeStruct((M, LANES), F32), jax.ShapeDtypeStruct((M, LANES), BF16)],
        compiler_params=_params("parallel"),
        name="moe_gates",
    )(logits)


def _rank_kernel(sel_ref, rank_ref, cnt_ref, carry_ref):
    @pl.when(pl.program_id(0) == 0)
    def _():
        carry_ref[...] = jnp.zeros_like(carry_ref)

    s = sel_ref[...]
    n = s.shape[0]
    earlier = (lax.broadcasted_iota(jnp.int32, (n, n), 1) < lax.broadcasted_iota(jnp.int32, (n, n), 0))
    within = jnp.dot(jnp.where(earlier, 1.0, 0.0).astype(BF16), s, preferred_element_type=F32)
    rank_ref[...] = within + carry_ref[...]
    carry_ref[...] += jnp.sum(s.astype(F32), axis=0, keepdims=True)
    cnt_ref[...] = carry_ref[...]


def _rank(sel):
    M = sel.shape[0]
    tr = _tile(M, 512, 8)
    return pl.pallas_call(
        _rank_kernel,
        grid=(M // tr,),
        in_specs=[pl.BlockSpec((tr, LANES), lambda i: (i, 0))],
        out_specs=[pl.BlockSpec((tr, LANES), lambda i: (i, 0)), pl.BlockSpec((1, LANES), lambda i: (0, 0))],
        out_shape=[jax.ShapeDtypeStruct((M, LANES), F32), jax.ShapeDtypeStruct((1, LANES), F32)],
        scratch_shapes=[pltpu.VMEM((1, LANES), F32)],
        compiler_params=_params("arbitrary"),
        name="moe_rank",
    )(sel)


def _pos_kernel(route_ref, rank_ref, offs_ref, pos_ref):
    lane = lax.broadcasted_iota(jnp.int32, rank_ref.shape, 1)
    lane_f = lane.astype(F32)
    tot = rank_ref[...] + offs_ref[...]
    route = route_ref[...]
    p1 = jnp.sum(jnp.where(lane_f == route[:, ROUTE_I1:ROUTE_I1 + 1], tot, 0.0), axis=-1, keepdims=True)
    p2 = jnp.sum(jnp.where(lane_f == route[:, ROUTE_I2:ROUTE_I2 + 1], tot, 0.0), axis=-1, keepdims=True)
    pos_ref[...] = jnp.where(lane == 0, p1, jnp.where(lane == 1, p2, 0.0)).astype(jnp.int32)


def _positions(route, rank, offs):
    M = route.shape[0]
    tm = _tile(M, 1024, 8)
    blk = pl.BlockSpec((tm, LANES), lambda i: (i, 0))
    return pl.pallas_call(
        _pos_kernel,
        grid=(M // tm,),
        in_specs=[blk, blk, pl.BlockSpec((1, LANES), lambda i: (0, 0))],
        out_specs=blk,
        out_shape=jax.ShapeDtypeStruct((M, LANES), jnp.int32),
        compiler_params=_params("parallel"),
        name="moe_pos",
    )(route, rank, offs)


def _row_copy(src, dst, sem):
    return pltpu.make_async_copy(src, dst, sem)


def _dispatch_kernel(p1_ref, p2_ref, h_ref, xs_in_ref, xs_ref, sem):
    del xs_in_ref
    tt = h_ref.shape[0]
    base = pl.program_id(0) * tt

    def start(r, carry):
        row = h_ref.at[pl.ds(r, 1)]
        _row_copy(row, xs_ref.at[pl.ds(p1_ref[base + r], 1)], sem).start()
        _row_copy(row, xs_ref.at[pl.ds(p2_ref[base + r], 1)], sem).start()
        return carry

    lax.fori_loop(0, tt, start, 0, unroll=8)
    for _ in range(2):
        _row_copy(h_ref, xs_ref.at[pl.ds(0, tt)], sem).wait()


def _dispatch(h, p1, p2, n_rows):
    M, D = h.shape
    tt = _tile(M, ROW_TILE, 8)
    return pl.pallas_call(
        _dispatch_kernel,
        grid_spec=pltpu.PrefetchScalarGridSpec(
            num_scalar_prefetch=2,
            grid=(M // tt,),
            in_specs=[pl.BlockSpec((tt, D), lambda i, p1, p2: (i, 0)),
                      pl.BlockSpec(memory_space=pl.ANY)],
            out_specs=pl.BlockSpec(memory_space=pl.ANY),
            scratch_shapes=[pltpu.SemaphoreType.DMA(())]),
        out_shape=jax.ShapeDtypeStruct((n_rows, D), h.dtype),
        input_output_aliases={3: 0},
        compiler_params=_params("arbitrary"),
        name="moe_dispatch",
    )(p1, p2, h, jnp.zeros((n_rows, D), h.dtype))


def _gmm_kernel(te_ref, nv_ref, x_ref, *refs, swiglu):
    o_ref = refs[-1]

    @pl.when(pl.program_id(0) < nv_ref[0])
    def _():
        x = x_ref[...].astype(BF16)
        acc = jnp.dot(x, refs[0][0, 0], preferred_element_type=F32)
        if swiglu:
            acc = acc * jax.nn.sigmoid(acc) * jnp.dot(x, refs[1][0, 0], preferred_element_type=F32)
        o_ref[...] = acc.astype(o_ref.dtype)

    @pl.when(pl.program_id(0) >= nv_ref[0])
    def _():
        o_ref[...] = jnp.zeros_like(o_ref)


def _gmm(x, ws, layer, tile_expert, n_valid, *, out_dtype, tn=512):
    K = x.shape[1]
    N = ws[0].shape[3]
    tm = MOE_TILE
    P = tile_expert.shape[0] * tm
    tn = _tile(N, tn)
    wspec = pl.BlockSpec((1, 1, K, tn), lambda i, j, te, nv: (layer, te[i], 0, j))
    return pl.pallas_call(
        functools.partial(_gmm_kernel, swiglu=len(ws) == 2),
        grid_spec=pltpu.PrefetchScalarGridSpec(
            num_scalar_prefetch=2,
            grid=(P // tm, N // tn),
            in_specs=[pl.BlockSpec((tm, K), lambda i, j, te, nv: (jnp.minimum(i, nv[0] - 1), 0))]
            + [wspec] * len(ws),
            out_specs=pl.BlockSpec((tm, tn), lambda i, j, te, nv: (i, j))),
        out_shape=jax.ShapeDtypeStruct((P, N), out_dtype),
        compiler_params=_params("parallel", "parallel"),
        name="moe_gmm",
    )(tile_expert, n_valid, x, *ws)


def _combine_ln_kernel(p1_ref, p2_ref, route_ref, ys_ref, x_ref, mod_ref, modh_ref, w_ref, b_ref, *rest,
                       alpha, gate, sh, sc, has_h):
    a_ref, b2_ref, sem = rest[-3:]
    xo_ref = rest[0]
    tt = xo_ref.shape[1]
    base = (pl.program_id(0) * pl.num_programs(1) + pl.program_id(1)) * tt

    def start(r, carry):
        _row_copy(ys_ref.at[pl.ds(p1_ref[base + r], 1)], a_ref.at[pl.ds(r, 1)], sem).start()
        _row_copy(ys_ref.at[pl.ds(p2_ref[base + r], 1)], b2_ref.at[pl.ds(r, 1)], sem).start()
        return carry

    lax.fori_loop(0, tt, start, 0, unroll=8)
    _row_copy(ys_ref.at[pl.ds(0, tt)], a_ref, sem).wait()
    _row_copy(ys_ref.at[pl.ds(0, tt)], b2_ref, sem).wait()
    route = route_ref[0]
    f = route[:, ROUTE_G1:ROUTE_G1 + 1] * a_ref[...] + route[:, ROUTE_G2:ROUTE_G2 + 1] * b2_ref[...]
    xn = _post_norm(x_ref[0], f, mod_ref, w_ref, b_ref, alpha=alpha, gate=gate)
    xo_ref[0] = xn
    if has_h:
        h = xn * (1.0 + modh_ref[0, sc:sc + 1, :]) + modh_ref[0, sh:sh + 1, :]
        rest[1][0] = h.astype(rest[1].dtype)


def _combine_ln(ys, route, p1, p2, x, mod, ln_w, ln_b, *, alpha, gate, h_mod=None, mod_h=None,
                h_dtype=BF16, ctx_tiles, ctx_row):
    B, T, D = x.shape
    tt = ROW_TILE
    nt = T // tt
    smap = lambda f: (lambda b, t, p1, p2: f(b, t))
    row = pl.BlockSpec((1, tt, D), smap(lambda b, t: (b, t, 0)))
    vec = pl.BlockSpec((1, D), smap(lambda b, t: (0, 0)))
    mspec = pl.BlockSpec((1, N_MOD, D), smap(_mod_index(ctx_tiles, ctx_row, 0)))
    out_specs = [row]
    out_shape = [jax.ShapeDtypeStruct((B, T, D), F32)]
    sh = sc = 0
    if h_mod is not None:
        sh, sc = h_mod
        out_specs.append(row)
        out_shape.append(jax.ShapeDtypeStruct((B, T, D), h_dtype))
    return pl.pallas_call(
        functools.partial(_combine_ln_kernel, alpha=alpha, gate=gate, sh=sh, sc=sc,
                          has_h=h_mod is not None),
        grid_spec=pltpu.PrefetchScalarGridSpec(
            num_scalar_prefetch=2,
            grid=(B, nt),
            in_specs=[pl.BlockSpec((1, tt, LANES), smap(lambda b, t: (b, t, 0))),
                      pl.BlockSpec(memory_space=pl.ANY), row, mspec, mspec, vec, vec],
            out_specs=out_specs,
            scratch_shapes=[pltpu.VMEM((tt, D), F32), pltpu.VMEM((tt, D), F32),
                            pltpu.SemaphoreType.DMA(())]),
        out_shape=out_shape,
        compiler_params=_params("arbitrary", "arbitrary"),
        name="moe_combine_ln",
    )(p1, p2, route.reshape(B, T, LANES), ys, x, mod, mod if mod_h is None else mod_h,
      ln_w.reshape(1, D), ln_b.reshape(1, D))


def _moe(h, logits, w1, w3, w2, layer):
    M, D = h.shape
    E = w1.shape[1]
    tm = MOE_TILE
    route, sel = _gates(logits, E)
    rank, cnt = _rank(sel)
    counts = cnt[0, :E].astype(jnp.int32)
    padded = (counts + tm - 1) // tm * tm
    ends = jnp.cumsum(padded)
    n_tiles = (TOP_K * M) // tm + E
    tile_expert = jnp.minimum(
        jnp.searchsorted(ends, jnp.arange(n_tiles, dtype=jnp.int32) * tm, side="right"), E - 1
    ).astype(jnp.int32)
    n_valid = (ends[-1:] // tm).astype(jnp.int32)
    offs = jnp.zeros((1, LANES), F32).at[0, :E].set((ends - padded).astype(F32))
    pos = _positions(route, rank, offs)
    p1, p2 = pos[:, 0], pos[:, 1]
    xs = _dispatch(h, p1, p2, n_tiles * tm)
    mid = _gmm(xs, [w1, w3], layer, tile_expert, n_valid, out_dtype=BF16)
    ys = _gmm(mid, [w2], layer, tile_expert, n_valid, out_dtype=F32)
    return ys, route, p1, p2


def kernel(x, c, ctx, c_ctx, ada_w, ada_b, ln_w, ln_b, hy_w_in, hy_b_in, hy_conv_w, hy_conv_b, hy_f_w1, hy_f_b1, hy_f_w2, hy_f_b2, hy_f_w3, hy_f_freq, hy_skip, hy_w_out, hy_b_out, rw_mu, rw_w_rkv, rw_w_o, rw_w0, rw_w1, rw_w2, rw_a0, rw_a1, rw_a2, rw_g1, rw_g2, rw_k_k, rw_k_a, rw_r_k, rw_gn_w, rw_gn_b, at_w_qkv, at_b_qkv, at_w_o, at_b_o, at_sink, ff_w1, ff_w3, ff_w2, moe_router, moe_w1, moe_w3, moe_w2):
    B, L, D = x.shape
    Lc = ctx.shape[1]
    depth = ada_w.shape[0]
    alpha = (2 * depth) ** 0.25
    assert Lc % ROW_TILE == 0 and L % ROW_TILE == 0 and D % (RW_GROUP * HEAD) == 0
    bf = lambda t: t.astype(BF16)
    w_bf = dict(hy_out=bf(hy_w_out), rw_o=bf(rw_w_o), at_qkv=bf(at_w_qkv), at_o=bf(at_w_o),
                ff1=bf(ff_w1), ff3=bf(ff_w3), ff2=bf(ff_w2),
                moe1=bf(moe_w1), moe3=bf(moe_w3), moe2=bf(moe_w2))

    ctx_row = B
    rows = -(-(B + 1) // 8) * 8
    cond = jnp.zeros((rows, D), F32).at[:B].set(c).at[B].set(c_ctx)
    mods = _ada(cond, ada_w, ada_b).reshape(depth, rows, N_MOD, D)

    tables = {}
    if depth > 0:
        tables[L] = _dft_tables(L)
        if depth > 1:
            tables[Lc] = _dft_tables(Lc)

    xs = jnp.concatenate([ctx, x], axis=1)
    ctx_tiles = Lc // ROW_TILE
    h = _modulate(xs, mods[0], sh=0, sc=1, ctx_tiles=ctx_tiles, ctx_row=ctx_row,
                  out_dtype=BF16)
    for i in range(depth):
        last = i == depth - 1
        kind = i % 3
        j = i // 3
        mod = mods[i]
        has_ctx = xs.shape[1] != L
        ct = ctx_tiles if has_ctx else 0
        T = xs.shape[1]
        if kind == 0:
            p = dict(w_in=bf(hy_w_in[j]), b_in=hy_b_in[j], conv_w=hy_conv_w[j], conv_b=hy_conv_b[j],
                     filter=(hy_f_w1[j], hy_f_b1[j], hy_f_w2[j], hy_f_b2[j], hy_f_w3[j], hy_f_freq[j]),
                     skip=hy_skip[j])
            y = _hyena(h, p, tables, Lc=Lc, has_ctx=has_ctx)
            proj = (w_bf["hy_out"], j, hy_b_out[j])
        elif kind == 1:
            assert has_ctx and not last
            p = dict(mu=rw_mu[j], w_rkv=bf(rw_w_rkv[j]), w0=rw_w0[j], w1=rw_w1[j],
                     w2=rw_w2[j], a0=rw_a0[j], a1=rw_a1[j], a2=rw_a2[j], g1=rw_g1[j], g2=rw_g2[j],
                     k_k=rw_k_k[j], k_a=rw_k_a[j], r_k=rw_r_k[j], gn_w=rw_gn_w[j], gn_b=rw_gn_b[j])
            y = _rwkv(h, p, Lc=Lc)
            proj = (w_bf["rw_o"], j, None)
        else:
            assert has_ctx and not last
            p = dict(w_qkv=w_bf["at_qkv"], index=j, b_qkv=at_b_qkv[j], sink=at_sink[j])
            y = _attention(h, p, Lc=Lc)
            proj = (w_bf["at_o"], j, at_b_o[j])

        drop1 = ct if last else 0
        fj = i // 2
        moe = i % 2 == 1
        if moe:
            pw, pi, pb = proj
            y = _mm(y.reshape(B * T, D), pw, wi=pi, bias=pb).reshape(B, T, D)
            proj = None
        res = _ln(xs, y, mod, ln_w[i, 0], ln_b[i, 0], alpha=alpha, gate=2, proj=proj, h_mod=(3, 4),
                  h_dtype=F32 if moe else BF16, router=moe_router[fj] if moe else None,
                  ctx_tiles=ct, ctx_row=ctx_row, drop_tiles=drop1)
        xs, h2 = res[0], res[1]
        ct = ct - drop1
        T = xs.shape[1]
        M = B * T
        nxt_last = i + 1 == depth - 1
        nkind = (i + 1) % 3
        drop2 = ct if (not last and nxt_last and nkind == 0) else 0
        nxt = {} if last else dict(h_mod=(0, 1), mod_h=mods[i + 1], h_dtype=F32 if nkind == 1 else BF16)
        if moe:
            routed = _moe(h2.reshape(M, D), res[2].reshape(M, LANES), w_bf["moe1"], w_bf["moe3"],
                          w_bf["moe2"], fj)
            res = _combine_ln(*routed, xs, mod, ln_w[i, 1], ln_b[i, 1], alpha=alpha, gate=5,
                              ctx_tiles=ct, ctx_row=ctx_row, **nxt)
            if drop2:
                res = [lax.slice_in_dim(t, drop2 * ROW_TILE, T, axis=1) for t in res]
        else:
            f = _mm(h2.reshape(M, D), w_bf["ff1"], w3=w_bf["ff3"], wi=fj, out_dtype=BF16).reshape(B, T, -1)
            res = _ln(xs, f, mod, ln_w[i, 1], ln_b[i, 1], alpha=alpha, gate=5, proj=(w_bf["ff2"], fj, None),
                      ctx_tiles=ct, ctx_row=ctx_row, drop_tiles=drop2, **nxt)
        xs = res[0]
        if not last:
            h = res[1]
    return xs
```

```python
import functools
import math

import jax
import jax.numpy as jnp
from jax import lax
from jax.experimental import pallas as pl
from jax.experimental.pallas import tpu as pltpu

F32 = jnp.float32
BF16 = jnp.bfloat16
HIGHEST = lax.Precision.HIGHEST

VMEM_LIMIT_BYTES = 56 * 1024 * 1024
LANES = 128
ROW_TILE = 256
BF16_ROWS = 16

LN_EPS = 1e-5
N_MOD = 6
HEAD = 64
RW_GN_EPS = 64e-5
RW_CHUNK = 64
RW_GROUP = 4
AT_KV_HEADS = 4
AT_WINDOW = 128
AT_BLOCK = 128
GRID_W = 64
ROPE_BASE = 10000.0
NEG_INF = -1e30
TOP_K = 2
HY_BANDS = 16
HY_EMB = 2 * HY_BANDS + 1
HY_MIN_DECAY = -math.log(1e-2) / 1.5
HY_MAX_DECAY = -math.log(1e-2) / 0.3


def _params(*sem):
    return pltpu.CompilerParams(dimension_semantics=sem, vmem_limit_bytes=VMEM_LIMIT_BYTES)


def _tile(n, pref, mult=LANES):
    if n <= pref:
        return n
    t = (pref // mult) * mult
    while t >= mult:
        if n % t == 0:
            return t
        t -= mult
    return n


def _mm_kernel(*refs, n_w, has_bias, act):
    x = refs[0][...].astype(BF16)
    o_ref = refs[-1]
    acc = jnp.dot(x, refs[1][0].astype(BF16), preferred_element_type=F32)
    if has_bias:
        acc = acc + refs[1 + n_w][...]
    if act == "swiglu":
        acc3 = jnp.dot(x, refs[2][0].astype(BF16), preferred_element_type=F32)
        acc = acc * jax.nn.sigmoid(acc) * acc3
    o_ref[...] = acc.astype(o_ref.dtype)


def _mm(x, w, *, w3=None, wi=0, bias=None, out_dtype=F32, tm=1024, tn=512):
    M, K = x.shape
    N = w.shape[2]
    tm = _tile(M, tm, 8)
    tn = _tile(N, tn)
    ws = [w] if w3 is None else [w, w3]
    in_specs = [pl.BlockSpec((tm, K), lambda i, j: (i, 0))]
    in_specs += [pl.BlockSpec((1, K, tn), lambda i, j: (wi, 0, j)) for _ in ws]
    args = [x] + ws
    if bias is not None:
        in_specs.append(pl.BlockSpec((1, tn), lambda i, j: (0, j)))
        args.append(bias.reshape(1, N).astype(F32))
    return pl.pallas_call(
        functools.partial(_mm_kernel, n_w=len(ws), has_bias=bias is not None,
                          act="swiglu" if w3 is not None else None),
        grid=(M // tm, N // tn),
        in_specs=in_specs,
        out_specs=pl.BlockSpec((tm, tn), lambda i, j: (i, j)),
        out_shape=jax.ShapeDtypeStruct((M, N), out_dtype),
        compiler_params=_params("parallel", "parallel"),
        name="mm_swiglu" if w3 is not None else "mm",
    )(*args)


def _ada_kernel(c_ref, w_ref, b_ref, o_ref):
    c = c_ref[...]
    s = (c * jax.nn.sigmoid(c)).astype(BF16)
    o_ref[0] = jnp.dot(s, w_ref[0].astype(BF16), preferred_element_type=F32) + b_ref[0]


def _ada(cond, ada_w, ada_b):
    depth, D, N = ada_w.shape
    R = cond.shape[0]
    tn = _tile(N, 1024)
    return pl.pallas_call(
        _ada_kernel,
        grid=(depth, N // tn),
        in_specs=[pl.BlockSpec((R, D), lambda i, j: (0, 0)),
                  pl.BlockSpec((1, D, tn), lambda i, j: (i, 0, j)),
                  pl.BlockSpec((1, 1, tn), lambda i, j: (i, 0, j))],
        out_specs=pl.BlockSpec((1, R, tn), lambda i, j: (i, 0, j)),
        out_shape=jax.ShapeDtypeStruct((depth, R, N), F32),
        compiler_params=_params("parallel", "parallel"),
        name="ada",
    )(cond, ada_w, ada_b.reshape(depth, 1, N))


def _mod_index(ctx_tiles, ctx_row, off):
    def index(b, t):
        return (jnp.where(t + off < ctx_tiles, ctx_row, b), 0, 0)
    return index


def _stream_kernel(ctx_ref, x_ref, mod_ref, xs_ref, h_ref, *, sh, sc, ctx_tiles):
    x = jnp.where(pl.program_id(1) < ctx_tiles, ctx_ref[0], x_ref[0])
    xs_ref[0] = x
    h_ref[0] = (x * (1.0 + mod_ref[0, sc:sc + 1, :]) + mod_ref[0, sh:sh + 1, :]).astype(h_ref.dtype)


def _stream(ctx, x, mod, *, sh, sc, ctx_row, out_dtype):
    B, L, D = x.shape
    ctx_tiles = ctx.shape[1] // ROW_TILE
    T = ctx.shape[1] + L
    blk = pl.BlockSpec((1, ROW_TILE, D), lambda b, t: (b, t, 0))
    return pl.pallas_call(
        functools.partial(_stream_kernel, sh=sh, sc=sc, ctx_tiles=ctx_tiles),
        grid=(B, T // ROW_TILE),
        in_specs=[pl.BlockSpec((1, ROW_TILE, D), lambda b, t: (b, jnp.minimum(t, ctx_tiles - 1), 0)),
                  pl.BlockSpec((1, ROW_TILE, D), lambda b, t: (b, jnp.maximum(t - ctx_tiles, 0), 0)),
                  pl.BlockSpec((1, N_MOD, D), _mod_index(ctx_tiles, ctx_row, 0))],
        out_specs=[blk, blk],
        out_shape=[jax.ShapeDtypeStruct((B, T, D), F32), jax.ShapeDtypeStruct((B, T, D), out_dtype)],
        compiler_params=_params("parallel", "parallel"),
        name="stream",
    )(ctx, x, mod)


def _post_norm(x, y, mod_ref, w_ref, b_ref, *, alpha, gate):
    z = alpha * x + mod_ref[0, gate:gate + 1, :] * y
    mu = jnp.mean(z, axis=-1, keepdims=True)
    d = z - mu
    var = jnp.mean(d * d, axis=-1, keepdims=True)
    return d * lax.rsqrt(var + LN_EPS) * w_ref[...] + b_ref[...]


def _ln_kernel(*refs, alpha, gate, sh, sc, has_h, has_router, has_proj):
    x_ref, y_ref, mod_ref, modh_ref, w_ref, b_ref = refs[:6]
    pos = 6
    if has_proj:
        pw_ref, pb_ref = refs[pos:pos + 2]
        pos += 2
    router_ref = None
    if has_router:
        router_ref = refs[pos]
        pos += 1
    xo_ref = refs[pos]
    pos += 1
    if has_proj:
        y = jnp.dot(y_ref[0], pw_ref[0], preferred_element_type=F32) + pb_ref[...]
    else:
        y = y_ref[0].astype(F32)
    xn = _post_norm(x_ref[0], y, mod_ref, w_ref, b_ref, alpha=alpha, gate=gate)
    xo_ref[0] = xn
    if has_h:
        h = xn * (1.0 + modh_ref[0, sc:sc + 1, :]) + modh_ref[0, sh:sh + 1, :]
        h_ref = refs[pos]
        pos += 1
        h_ref[0] = h.astype(h_ref.dtype)
        if has_router:
            refs[pos][0] = jnp.dot(h, router_ref[...], precision=HIGHEST, preferred_element_type=F32)


def _ln(x, y, mod, ln_w, ln_b, *, alpha, gate, proj=None, h_mod=None, mod_h=None, h_dtype=BF16,
        router=None, ctx_tiles, ctx_row, drop_tiles=0):
    B, T, D = x.shape
    K = y.shape[2]
    nt = T // ROW_TILE - drop_tiles
    To = nt * ROW_TILE
    off = drop_tiles
    in_specs = [pl.BlockSpec((1, ROW_TILE, D), lambda b, t: (b, t + off, 0)),
                pl.BlockSpec((1, ROW_TILE, K), lambda b, t: (b, t + off, 0)),
                pl.BlockSpec((1, N_MOD, D), _mod_index(ctx_tiles, ctx_row, off)),
                pl.BlockSpec((1, N_MOD, D), _mod_index(ctx_tiles, ctx_row, off)),
                pl.BlockSpec((1, D), lambda b, t: (0, 0)),
                pl.BlockSpec((1, D), lambda b, t: (0, 0))]
    args = [x, y, mod, mod if mod_h is None else mod_h, ln_w.reshape(1, D), ln_b.reshape(1, D)]
    if proj is not None:
        pw, pi, pb = proj
        in_specs += [pl.BlockSpec((1, K, D), lambda b, t: (pi, 0, 0), pipeline_mode=pl.Buffered(1)),
                     pl.BlockSpec((1, D), lambda b, t: (0, 0))]
        args += [pw, jnp.zeros((1, D), F32) if pb is None else pb.reshape(1, D).astype(F32)]
    out_specs = [pl.BlockSpec((1, ROW_TILE, D), lambda b, t: (b, t, 0))]
    out_shape = [jax.ShapeDtypeStruct((B, To, D), F32)]
    sh = sc = 0
    if h_mod is not None:
        sh, sc = h_mod
        out_specs.append(pl.BlockSpec((1, ROW_TILE, D), lambda b, t: (b, t, 0)))
        out_shape.append(jax.ShapeDtypeStruct((B, To, D), h_dtype))
    if router is not None:
        E = router.shape[1]
        router_p = jnp.zeros((D, LANES), F32).at[:, :E].set(router)
        in_specs.append(pl.BlockSpec((D, LANES), lambda b, t: (0, 0)))
        args.append(router_p)
        out_specs.append(pl.BlockSpec((1, ROW_TILE, LANES), lambda b, t: (b, t, 0)))
        out_shape.append(jax.ShapeDtypeStruct((B, To, LANES), F32))
    return pl.pallas_call(
        functools.partial(_ln_kernel, alpha=alpha, gate=gate, sh=sh, sc=sc, has_h=h_mod is not None,
                          has_router=router is not None, has_proj=proj is not None),
        grid=(B, nt),
        in_specs=in_specs,
        out_specs=out_specs,
        out_shape=out_shape,
        compiler_params=_params("parallel", "parallel"),
        name="ln_residual",
    )(*args)


def _halo_specs(T, C, col):
    r8 = ROW_TILE // 8
    last8 = T // 8 - 1
    return [pl.BlockSpec((1, ROW_TILE, C), lambda b, t, j: (b, t, col(j))),
            pl.BlockSpec((1, 8, C), lambda b, t, j: (b, jnp.maximum(t * r8 - 1, 0), col(j))),
            pl.BlockSpec((1, 8, C), lambda b, t, j: (b, jnp.minimum(t * r8 + r8, last8), col(j)))]


def _neighbours(cur, prev8, next8, t, n_tiles, ctx_tiles):
    rows = lax.broadcasted_iota(jnp.int32, cur.shape, 0)
    has_prev = jnp.logical_and(t != 0, t != ctx_tiles)
    has_next = jnp.logical_and(t != n_tiles - 1, t != ctx_tiles - 1)
    top = jnp.where(has_prev, prev8[7:8, :], 0.0)
    bot = jnp.where(has_next, next8[0:1, :], 0.0)
    up = jnp.where(rows == 0, top, pltpu.roll(cur, 1, axis=0))
    dn = jnp.where(rows == cur.shape[0] - 1, bot, pltpu.roll(cur, cur.shape[0] - 1, axis=0))
    return up, dn


def _hy_in_kernel(x_ref, xp_ref, xn_ref, *refs, tm, T, Lc):
    ws, bs, cws, cbs = refs[0:3], refs[3:6], refs[6:9], refs[9:12]
    x0_ref, vv_ref, vvb_ref = refs[12:15]
    H = BF16_ROWS
    x = jnp.concatenate([xp_ref[0], x_ref[0], xn_ref[0]], axis=0)
    row = lax.broadcasted_iota(jnp.int32, (tm, 1), 0) + pl.program_id(1) * tm
    no_prev = (row == 0) | (row == Lc)
    no_next = (row == Lc - 1) | (row == T - 1)
    out = []
    for s in range(3):
        acc = jnp.dot(x, ws[s][...], preferred_element_type=F32) + bs[s][...]
        n = acc.shape[0]
        cur = acc[H:H + tm]
        up = jnp.where(no_prev, 0.0, pltpu.roll(acc, 1, axis=0)[H:H + tm])
        dn = jnp.where(no_next, 0.0, pltpu.roll(acc, n - 1, axis=0)[H:H + tm])
        cw = cws[s]
        out.append(up * cw[0:1, :] + cur * cw[1:2, :] + dn * cw[2:3, :] + cbs[s][...])
    x0_ref[0] = out[0]
    vv = out[1] * out[2]
    vv_ref[0] = vv
    vvb_ref[0] = vv.astype(BF16)


def _hy_in(h, w_in, b_in, conv_w, conv_b, *, Lc):
    B, T, K = h.shape
    D = w_in.shape[1] // 3
    H = BF16_ROWS
    tm = _tile(T, 1280, 64)
    tc = _tile(D, 512)
    nj = D // tc
    r16 = tm // H
    last16 = T // H - 1
    sec = lambda shape: [pl.BlockSpec(shape, lambda b, t, j, s=s: (0, s * nj + j)) for s in range(3)]
    in_specs = [pl.BlockSpec((1, tm, K), lambda b, t, j: (b, t, 0)),
                pl.BlockSpec((1, H, K), lambda b, t, j: (b, jnp.maximum(t * r16 - 1, 0), 0)),
                pl.BlockSpec((1, H, K), lambda b, t, j: (b, jnp.minimum(t * r16 + r16, last16), 0))]
    in_specs += sec((K, tc)) + sec((1, tc)) + sec((3, tc)) + sec((1, tc))
    blk = pl.BlockSpec((1, tm, tc), lambda b, t, j: (b, t, j))
    b2 = b_in.reshape(1, 3 * D)
    cb2 = conv_b.reshape(1, 3 * D)
    return pl.pallas_call(
        functools.partial(_hy_in_kernel, tm=tm, T=T, Lc=Lc),
        grid=(B, T // tm, nj),
        in_specs=in_specs,
        out_specs=[blk, blk, blk],
        out_shape=[jax.ShapeDtypeStruct((B, T, D), F32), jax.ShapeDtypeStruct((B, T, D), F32),
                   jax.ShapeDtypeStruct((B, T, D), BF16)],
        compiler_params=_params("parallel", "parallel", "parallel"),
        name="hy_in",
    )(h, h, h, w_in, w_in, w_in, b2, b2, b2, conv_w, conv_w, conv_w, cb2, cb2, cb2)


def _hy_filter_kernel(w1_ref, b1_ref, w2_ref, b2_ref, w3_ref, fr_ref, h_ref, s_ref, *, L, D, tl):
    i = pl.program_id(0)
    row = (lax.broadcasted_iota(jnp.int32, (tl, LANES), 0) + i * tl).astype(F32)
    lane = lax.broadcasted_iota(jnp.int32, (tl, LANES), 1)
    band = jnp.where(lane <= HY_BANDS, lane - 1, lane - 1 - HY_BANDS).astype(F32)
    freq = 1e-4 + band * ((HY_BANDS - 1 - 1e-4) / (HY_BANDS - 1))
    ang = freq * (row * (2.0 * math.pi / L))
    z = jnp.where(lane == 0, row / (L - 1),
                  jnp.where(lane <= HY_BANDS, jnp.cos(ang),
                            jnp.where(lane < HY_EMB, -jnp.sin(ang), 0.0)))
    h = jnp.sin(fr_ref[0:1, :] * (jnp.dot(z, w1_ref[...], precision=HIGHEST,
                                           preferred_element_type=F32) + b1_ref[...]))
    h = jnp.sin(fr_ref[1:2, :] * (jnp.dot(h, w2_ref[...], precision=HIGHEST,
                                           preferred_element_type=F32) + b2_ref[...]))
    h = jnp.dot(h, w3_ref[...], precision=HIGHEST, preferred_element_type=F32)
    half = L // 2
    rowd = (lax.broadcasted_iota(jnp.int32, (tl, D), 0) + i * tl).astype(F32)
    dist = jnp.abs(rowd - half) / half
    chan = lax.broadcasted_iota(jnp.int32, (tl, D), 1).astype(F32)
    deltas = HY_MIN_DECAY + chan * ((HY_MAX_DECAY - HY_MIN_DECAY) / (D - 1))
    h = h * jnp.exp(-dist * deltas)
    h_ref[...] = h

    @pl.when(i == 0)
    def _():
        s_ref[...] = jnp.zeros_like(s_ref)

    s_ref[...] += jnp.sum(jnp.abs(h), axis=0, keepdims=True)


def _hy_filter(L, f_w1, f_b1, f_w2, f_b2, f_w3, f_freq):
    D = f_w3.shape[1]
    hid = f_w1.shape[1]
    w1 = jnp.zeros((LANES, LANES), F32).at[:HY_EMB, :hid].set(f_w1)
    b1 = jnp.zeros((1, LANES), F32).at[0, :hid].set(f_b1)
    w2 = jnp.zeros((LANES, LANES), F32).at[:hid, :hid].set(f_w2)
    b2 = jnp.zeros((1, LANES), F32).at[0, :hid].set(f_b2)
    w3 = jnp.zeros((LANES, D), F32).at[:hid].set(f_w3)
    fr = jnp.zeros((2, LANES), F32).at[:, :hid].set(f_freq)
    tl = _tile(L, 256, 8)
    full = lambda shape: pl.BlockSpec(shape, lambda i: (0, 0))
    return pl.pallas_call(
        functools.partial(_hy_filter_kernel, L=L, D=D, tl=tl),
        grid=(L // tl,),
        in_specs=[full((LANES, LANES)), full((1, LANES)), full((LANES, LANES)), full((1, LANES)),
                  full((LANES, D)), full((2, LANES))],
        out_specs=[pl.BlockSpec((tl, D), lambda i: (i, 0)), full((1, D))],
        out_shape=[jax.ShapeDtypeStruct((L, D), F32), jax.ShapeDtypeStruct((1, D), F32)],
        compiler_params=_params("arbitrary"),
        name="hy_filter",
    )(w1, b1, w2, b2, w3, fr)


DFT_ROWS = 64


def _dft_tables_kernel(fa_ref, ia_ref, b_ref, fwd_ref, inv_ref, *, L):
    n = 2 * L
    i = pl.program_id(0)
    cb, sb = b_ref[0], b_ref[1]
    q = lax.broadcasted_iota(jnp.int32, (DFT_ROWS, L), 0)
    col = lax.broadcasted_iota(jnp.int32, (DFT_ROWS, L), 1)
    sign_c = jnp.where(col % 2 == 0, 1.0, -1.0)
    sign_r = jnp.where(q % 2 == 0, 1.0, -1.0)
    ca, sa = fa_ref[0, 0:1, :], fa_ref[0, 1:2, :]
    cos_f = ca * cb - sa * sb
    sin_f = sa * cb + ca * sb
    first_row = (q == 0) & (i == 0)
    fwd_ref[0] = cos_f.astype(fwd_ref.dtype)
    fwd_ref[1] = jnp.where(first_row, sign_c, -sin_f).astype(fwd_ref.dtype)
    ca, sa = ia_ref[0, 0:1, :], ia_ref[0, 1:2, :]
    cos_i = ca * cb - sa * sb
    sin_i = sa * cb + ca * sb
    inv_ref[:, :L] = jnp.where(col == 0, 1.0 / n, (2.0 / n) * cos_i).astype(inv_ref.dtype)
    inv_ref[:, L:] = jnp.where(col == 0, sign_r / n, (-2.0 / n) * sin_i).astype(inv_ref.dtype)


def _dft_tables(L):
    n = 2 * L
    R = DFT_ROWS
    assert L % (2 * R) == 0
    w = 2.0 * math.pi / n
    c = jnp.arange(L, dtype=jnp.int32)
    hi = jnp.arange(L // R, dtype=jnp.int32) * R

    def cos_sin(rows):
        ang = ((rows[:, None] * c[None, :]) % n).astype(F32) * w
        return jnp.stack([jnp.cos(ang), jnp.sin(ang)], axis=1)

    fa = cos_sin(hi)
    ia = cos_sin(hi + L // 2)
    b = jnp.swapaxes(cos_sin(jnp.arange(R, dtype=jnp.int32)), 0, 1)
    return pl.pallas_call(
        functools.partial(_dft_tables_kernel, L=L),
        grid=(L // R,),
        in_specs=[pl.BlockSpec((1, 2, L), lambda i: (i, 0, 0)),
                  pl.BlockSpec((1, 2, L), lambda i: (i, 0, 0)),
                  pl.BlockSpec((2, R, L), lambda i: (0, 0, 0))],
        out_specs=[pl.BlockSpec((2, R, L), lambda i: (0, i, 0)),
                   pl.BlockSpec((R, 2 * L), lambda i: (i, 0))],
        out_shape=[jax.ShapeDtypeStruct((2, L, L), BF16), jax.ShapeDtypeStruct((L, 2 * L), BF16)],
        compiler_params=_params("parallel"),
        name="dft_tables",
    )(fa, ia, b)


def _dft_fwd_kernel(a_ref, w_ref, *rest, mode):
    o_ref = rest[-1]
    tm = a_ref.shape[1]
    v = jnp.dot(a_ref[...].reshape(2 * tm, a_ref.shape[2]), w_ref[0], preferred_element_type=F32)
    vre, vim = v[:tm], v[tm:]
    if mode == "scale":
        inv = 1.0 / (rest[0][...] + 1e-6)
        o_ref[0, 0] = vre * inv
        o_ref[0, 1] = vim * inv
    else:
        hre = rest[0][0]
        him = rest[0][1]
        first = jnp.logical_and(pl.program_id(2) == 0,
                                lax.broadcasted_iota(jnp.int32, vre.shape, 0) == 0)
        zre = jnp.where(first, vre * hre, vre * hre - vim * him)
        zim = jnp.where(first, vim * him, vre * him + vim * hre)
        o_ref[0, 0] = zre.astype(o_ref.dtype)
        o_ref[0, 1] = zim.astype(o_ref.dtype)


def _dft_fwd(fwd, w, extra, *, mode, out_dtype):
    B, L, D = w.shape
    tm = _tile(L, 256, 8)
    tn = _tile(D, 1024)
    if mode == "scale":
        extra_spec = pl.BlockSpec((1, tn), lambda b, j, m: (0, j))
    else:
        extra_spec = pl.BlockSpec((2, tm, tn), lambda b, j, m: (0, m, j))
    return pl.pallas_call(
        functools.partial(_dft_fwd_kernel, mode=mode),
        grid=(B, D // tn, L // tm),
        in_specs=[pl.BlockSpec((2, tm, L), lambda b, j, m: (0, m, 0)),
                  pl.BlockSpec((1, L, tn), lambda b, j, m: (b, 0, j)),
                  extra_spec],
        out_specs=pl.BlockSpec((1, 2, tm, tn), lambda b, j, m: (b, 0, m, j)),
        out_shape=jax.ShapeDtypeStruct((B, 2, L, D), out_dtype),
        compiler_params=_params("parallel", "parallel", "parallel"),
        name="dft_fwd",
    )(fwd, w, extra)


def _dft_inv_kernel(b_ref, z_ref, x0_ref, vv_ref, skip_ref, u_ref):
    y = jnp.dot(b_ref[...], z_ref[0], preferred_element_type=F32)
    u_ref[0] = (x0_ref[0] * (y + vv_ref[0] * skip_ref[...])).astype(u_ref.dtype)


def _dft_inv(inv, z, x0, vv, skip, *, row_off):
    B, n, D = z.shape
    L = n // 2
    tm = _tile(L, ROW_TILE, 8)
    tn = _tile(D, 1024)
    off = row_off // tm
    return pl.pallas_call(
        _dft_inv_kernel,
        grid=(B, D // tn, L // tm),
        in_specs=[pl.BlockSpec((tm, n), lambda b, j, m: (m, 0)),
                  pl.BlockSpec((1, n, tn), lambda b, j, m: (b, 0, j)),
                  pl.BlockSpec((1, tm, tn), lambda b, j, m: (b, m + off, j)),
                  pl.BlockSpec((1, tm, tn), lambda b, j, m: (b, m + off, j)),
                  pl.BlockSpec((1, tn), lambda b, j, m: (0, j))],
        out_specs=pl.BlockSpec((1, tm, tn), lambda b, j, m: (b, m, j)),
        out_shape=jax.ShapeDtypeStruct((B, L, D), BF16),
        compiler_params=_params("parallel", "parallel", "parallel"),
        name="dft_inv",
    )(inv, z, x0, vv, skip.reshape(1, D))


def _hyena(h, p, tables, *, Lc, has_ctx):
    B, T, D = h.shape
    x0, vv, vvb = _hy_in(h, p["w_in"], p["b_in"], p["conv_w"], p["conv_b"], Lc=Lc if has_ctx else -1)
    segs = [(Lc, T - Lc)] if has_ctx else [(0, T)]
    if has_ctx:
        segs = [(0, Lc)] + segs
    us = []
    for start, L in segs:
        fwd, inv = tables[L]
        filt, asum = _hy_filter(L, *p["filter"])
        hf = _dft_fwd(fwd, filt.astype(BF16)[None], asum, mode="scale", out_dtype=F32)[0]
        seg = vvb if (start == 0 and L == T) else lax.slice_in_dim(vvb, start, start + L, axis=1)
        z = _dft_fwd(fwd, seg, hf, mode="mul", out_dtype=BF16).reshape(B, 2 * L, D)
        us.append(_dft_inv(inv, z, x0, vv, p["skip"], row_off=start))
    return us[0] if len(us) == 1 else jnp.concatenate(us, axis=1)


MU_R, MU_W, MU_K, MU_V, MU_A, MU_G = range(6)


def _rw_mix(cur_ref, prev_ref, next_ref, mu_ref, which, *, n_tiles, ctx_tiles):
    cur = cur_ref[0]
    up, dn = _neighbours(cur, prev_ref[0], next_ref[0], pl.program_id(1), n_tiles, ctx_tiles)
    dx = 0.5 * (up + dn) - cur
    return [(cur + dx * mu_ref[j:j + 1, :]).astype(BF16) for j in which]


def _rw_rkvg_kernel(cur_ref, prev_ref, next_ref, mu_ref, wr_ref, wk_ref, wv_ref, g1_ref, g2_ref,
                    r_ref, k_ref, v_ref, g_ref, **tiles):
    xr, xk, xv, xg = _rw_mix(cur_ref, prev_ref, next_ref, mu_ref, (MU_R, MU_K, MU_V, MU_G), **tiles)
    r_ref[0] = jnp.dot(xr, wr_ref[...], preferred_element_type=F32)
    k_ref[0] = jnp.dot(xk, wk_ref[...], preferred_element_type=F32)
    v_ref[0] = jnp.dot(xv, wv_ref[...], preferred_element_type=F32)
    t = jax.nn.sigmoid(jnp.dot(xg, g1_ref[...], preferred_element_type=F32)).astype(BF16)
    g_ref[0] = jnp.dot(t, g2_ref[...], preferred_element_type=F32)


def _rw_decay_kernel(cur_ref, prev_ref, next_ref, mu_ref, w1_ref, w2_ref, w0_ref, a1_ref, a2_ref, a0_ref,
                     lwf_ref, af_ref, lwb_ref, ab_ref, **tiles):
    xw, xa = _rw_mix(cur_ref, prev_ref, next_ref, mu_ref, (MU_W, MU_A), **tiles)
    R = w2_ref.shape[1]
    tw = jnp.tanh(jnp.dot(xw, w1_ref[...], preferred_element_type=F32)).astype(BF16)
    ta = jnp.dot(xa, a1_ref[...], preferred_element_type=F32).astype(BF16)
    for d, (lw_ref, a_ref) in enumerate(((lwf_ref, af_ref), (lwb_ref, ab_ref))):
        wl = jnp.dot(tw[:, d * R:(d + 1) * R], w2_ref[d], preferred_element_type=F32) + w0_ref[d:d + 1, :]
        lw_ref[0] = -math.exp(-0.5) * jax.nn.sigmoid(wl)
        al = jnp.dot(ta[:, d * R:(d + 1) * R], a2_ref[d], preferred_element_type=F32) + a0_ref[d:d + 1, :]
        a_ref[0] = jax.nn.sigmoid(al)


def _pad_rank(first, second):
    _, D, r = first.shape
    R = -(-r // LANES) * LANES
    f = jnp.zeros((D, 2 * R), BF16)
    s = jnp.zeros((2, R, D), BF16)
    for d in range(2):
        f = f.at[:, d * R:d * R + r].set(first[d].astype(BF16))
        s = s.at[d, :r].set(second[d].astype(BF16))
    return f, s


def _rw_project(hs, p, *, ctx_tiles):
    B, T, D = hs.shape
    n_tiles = T // ROW_TILE
    tiles = dict(n_tiles=n_tiles, ctx_tiles=ctx_tiles)
    halo = _halo_specs(T, D, lambda j: 0)
    once = pl.Buffered(1)
    full = lambda a: pl.BlockSpec(a.shape, lambda b, t, j: (0,) * a.ndim, pipeline_mode=once)
    blk = pl.BlockSpec((1, ROW_TILE, D), lambda b, t, j: (b, t, 0))
    out = jax.ShapeDtypeStruct((B, T, D), F32)
    mu = p["mu"]
    wr, wk, wv = p["w_rkv"][0], p["w_rkv"][1], p["w_rkv"][2]
    g1, g2 = p["g1"].astype(BF16), p["g2"].astype(BF16)
    r, k, v, g = pl.pallas_call(
        functools.partial(_rw_rkvg_kernel, **tiles),
        grid=(B, n_tiles, 1),
        in_specs=halo + [full(a) for a in (mu, wr, wk, wv, g1, g2)],
        out_specs=[blk] * 4,
        out_shape=[out] * 4,
        compiler_params=_params("parallel", "parallel", "parallel"),
        name="rw_rkvg",
    )(hs, hs, hs, mu, wr, wk, wv, g1, g2)
    w1, w2 = _pad_rank(p["w1"], p["w2"])
    a1, a2 = _pad_rank(p["a1"], p["a2"])
    consts = (mu, w1, w2, p["w0"], a1, a2, p["a0"])
    lwf, af, lwb, ab = pl.pallas_call(
        functools.partial(_rw_decay_kernel, **tiles),
        grid=(B, n_tiles, 1),
        in_specs=halo + [full(a) for a in consts],
        out_specs=[blk] * 4,
        out_shape=[out] * 4,
        compiler_params=_params("parallel", "parallel", "parallel"),
        name="rw_decay",
    )(hs, hs, hs, *consts)
    return r, k, v, g, ((lwf, af), (lwb, ab))


def _block_diag(x, mask):
    return jnp.where(mask, jnp.concatenate([x] * RW_GROUP, axis=0), 0.0).astype(BF16)


def _rw_chunks(r, k, v, lw, a, k_k, k_a, state, same_head, incl, strict, reverse):
    C = RW_CHUNK
    W = RW_GROUP * HEAD
    G = range(len(r))
    nt = (((1,), (1,)), ((), ()))
    ones = jnp.where(same_head, 1.0, 0.0).astype(BF16)
    tri = jnp.where(incl[:, :C], 1.0, 0.0).astype(BF16)
    bd = lambda t: _block_diag(t, same_head)
    mm = lambda x, y: jnp.dot(x.astype(BF16), y, preferred_element_type=F32)

    def split(t):
        hi = t.astype(BF16)
        return hi, (t - hi.astype(F32)).astype(BF16)

    kkr = [k[g] * k_k[g] for g in G]
    sq = [split(kkr[g] * kkr[g]) for g in G]
    ng = len(r)
    sums = jnp.dot(jnp.concatenate([sq[g][0] for g in G] + [sq[g][1] for g in G], axis=0), ones,
                   preferred_element_type=F32)
    ss = [sums[g * C:(g + 1) * C] + sums[(ng + g) * C:(ng + g + 1) * C] for g in G]
    lws = [split(lw[g]) for g in G]
    lp = [jnp.dot(tri, lws[g][0], preferred_element_type=F32)
          + jnp.dot(tri, lws[g][1], preferred_element_type=F32) for g in G]
    kk = [kkr[g] / jnp.maximum(jnp.sqrt(ss[g]), 1e-12) for g in G]
    kd = [k[g] * (1.0 + (a[g] - 1.0) * k_a[g]) for g in G]
    lp_end = [lp[g][0:1, :] if reverse else lp[g][C - 1:C, :] for g in G]
    e_neg = [jnp.exp(-lp[g]) for g in G]
    lhs = [jnp.concatenate([-kk[g] * jnp.exp(lp[g] - lw[g]), r[g] * jnp.exp(lp[g])], axis=0).astype(BF16)
           for g in G]
    rhs = [jnp.concatenate([bd(kk[g] * a[g] * e_neg[g]), bd(kd[g] * e_neg[g])], axis=0) for g in G]
    cross = [lax.dot_general(lhs[g], rhs[g], nt, preferred_element_type=F32) for g in G]
    from_state = [lax.dot_general(lhs[g], state[g].astype(BF16), nt, preferred_element_type=F32)
                  for g in G]
    v_bd = [bd(v[g]) for g in G]
    p = [jnp.where(strict, cross[g][:C, :W], 0.0) for g in G]
    on_v = [mm(jnp.concatenate([jnp.where(strict, cross[g][:C, W:], 0.0),
                                jnp.where(incl, cross[g][C:, W:], 0.0)], axis=0), v_bd[g]) for g in G]
    x = [from_state[g][:C] + on_v[g][:C] for g in G]
    n = 1
    while n < C:
        x = [x[g] + mm(p[g], bd(x[g])) for g in G]
        n *= 2
        if n < C:
            p = [mm(p[g], bd(p[g])) for g in G]
    y = [from_state[g][C:] + mm(jnp.where(incl, cross[g][C:, :W], 0.0), bd(x[g])) + on_v[g][C:]
         for g in G]

    tail = [jnp.exp(lp_end[g] - lp[g]) for g in G]
    uv = [jnp.concatenate([x[g], v[g]], axis=0).astype(BF16) for g in G]
    bk = [jnp.concatenate([kk[g] * a[g] * tail[g], kd[g] * tail[g]], axis=0).astype(BF16) for g in G]
    upd = [lax.dot_general(uv[g], bk[g], (((0,), (0,)), ((), ())), preferred_element_type=F32)
           for g in G]
    new_state = [state[g] * jnp.exp(lp_end[g]) + jnp.where(same_head, upd[g], 0.0) for g in G]
    return y, new_state


RW_GROUPS_PER_STEP = 8
RW_CHUNKS_PER_STEP = 2


def _rw_scan_kernel(r_ref, k_ref, v_ref, lw_ref, a_ref, kk_ref, ka_ref, o_ref, state_ref, *, reverse):
    C = RW_CHUNK
    W = RW_GROUP * HEAD

    @pl.when(pl.program_id(2) == 0)
    def _():
        state_ref[...] = jnp.zeros_like(state_ref)

    ri = lax.broadcasted_iota(jnp.int32, (W, W), 0)
    ci = lax.broadcasted_iota(jnp.int32, (W, W), 1)
    same_head = (ri // HEAD) == (ci // HEAD)
    t_i = lax.broadcasted_iota(jnp.int32, (C, W), 0)
    s_i = lax.broadcasted_iota(jnp.int32, (C, W), 1) % C
    incl = (s_i >= t_i) if reverse else (s_i <= t_i)
    strict = (s_i > t_i) if reverse else (s_i < t_i)

    ng = state_ref.shape[0]
    sl = [slice(g * W, (g + 1) * W) for g in range(ng)]
    state = [state_ref[g] for g in range(ng)]
    order = range(RW_CHUNKS_PER_STEP)
    for c in (reversed(order) if reverse else order):
        rows = slice(c * C, (c + 1) * C)
        load = lambda ref: [ref[0, rows, s] for s in sl]
        y, state = _rw_chunks(load(r_ref), load(k_ref), load(v_ref), load(lw_ref), load(a_ref),
                              [kk_ref[:, s] for s in sl], [ka_ref[:, s] for s in sl],
                              state, same_head, incl, strict, reverse)
        for g in range(ng):
            o_ref[0, rows, sl[g]] = y[g]
    for g in range(ng):
        state_ref[g] = state[g]


def _rw_scan(r, k, v, lw, a, k_k, k_a, *, Lc, reverse):
    B, T, D = r.shape
    assert RW_CHUNK == HEAD
    rows = RW_CHUNK * RW_CHUNKS_PER_STEP
    assert Lc % rows == 0 and T % rows == 0
    W = RW_GROUP * HEAD
    ng = math.gcd(D // W, RW_GROUPS_PER_STEP)
    nb = T // rows
    ncb = Lc // rows

    def block(c):
        if not reverse:
            return c
        return jnp.where(c < ncb, ncb - 1 - c, nb - 1 - (c - ncb))

    blk = pl.BlockSpec((1, rows, ng * W), lambda b, g, c: (b, block(c), g))
    vec = pl.BlockSpec((1, ng * W), lambda b, g, c: (0, g))
    return pl.pallas_call(
        functools.partial(_rw_scan_kernel, reverse=reverse),
        grid=(B, D // (ng * W), nb),
        in_specs=[blk] * 5 + [vec, vec],
        out_specs=blk,
        out_shape=jax.ShapeDtypeStruct((B, T, D), F32),
        scratch_shapes=[pltpu.VMEM((ng, W, W), F32)],
        compiler_params=_params("parallel", "parallel", "arbitrary"),
        name="rw_scan",
    )(r, k, v, lw, a, k_k.reshape(1, D), k_a.reshape(1, D))


def _head_sums(x, ones_bd):
    W = ones_bd.shape[0]
    hi = x.astype(BF16)
    lo = (x - hi.astype(F32)).astype(BF16)
    cols = []
    for g in range(x.shape[1] // W):
        sl = slice(g * W, (g + 1) * W)
        cols.append(jnp.dot(hi[:, sl], ones_bd, preferred_element_type=F32)
                    + jnp.dot(lo[:, sl], ones_bd, preferred_element_type=F32))
    return jnp.concatenate(cols, axis=1)


def _rw_post_kernel(of_ref, ob_ref, r_ref, k_ref, v_ref, g_ref, rk_ref, gw_ref, gb_ref, o_ref):
    W = RW_GROUP * HEAD
    ri = lax.broadcasted_iota(jnp.int32, (W, W), 0)
    ci = lax.broadcasted_iota(jnp.int32, (W, W), 1)
    ones_bd = jnp.where((ri // HEAD) == (ci // HEAD), 1.0, 0.0).astype(BF16)
    o = of_ref[0] + ob_ref[0]
    m = _head_sums(o, ones_bd) * (1.0 / HEAD)
    d = o - m
    var = _head_sums(d * d, ones_bd) * (1.0 / HEAD)
    on = d * lax.rsqrt(var + RW_GN_EPS) * gw_ref[...] + gb_ref[...]
    bonus = _head_sums(r_ref[0] * k_ref[0] * rk_ref[...], ones_bd) * v_ref[0]
    o_ref[0] = ((on + bonus) * g_ref[0]).astype(o_ref.dtype)


def _rw_post(o_f, o_b, r, k, v, g, r_k, gn_w, gn_b):
    B, T, D = r.shape
    tt = _tile(T, 128, 8)
    blk = pl.BlockSpec((1, tt, D), lambda b, t: (b, t, 0))
    par = pl.BlockSpec((1, D), lambda b, t: (0, 0))
    return pl.pallas_call(
        _rw_post_kernel,
        grid=(B, T // tt),
        in_specs=[blk] * 6 + [par] * 3,
        out_specs=blk,
        out_shape=jax.ShapeDtypeStruct((B, T, D), BF16),
        compiler_params=_params("parallel", "parallel"),
        name="rw_post",
    )(o_f, o_b, r, k, v, g, r_k.reshape(1, D), gn_w.reshape(1, D), gn_b.reshape(1, D))


def _rwkv(hs, p, *, Lc):
    r, k, v, g, dirs = _rw_project(hs, p, ctx_tiles=Lc // ROW_TILE)
    outs = [_rw_scan(r, k, v, lw, a, p["k_k"], p["k_a"], Lc=Lc, reverse=d == 1)
            for d, (lw, a) in enumerate(dirs)]
    return _rw_post(outs[0], outs[1], r, k, v, g, p["r_k"], p["gn_w"], p["gn_b"])


def _rope_tables(T, Lc):
    quarter = HEAD // 4
    pos = jnp.arange(T - Lc, dtype=jnp.int32)
    rows = (pos // GRID_W).astype(F32)
    cols = (pos % GRID_W).astype(F32)
    inv = ROPE_BASE ** (-jnp.arange(quarter, dtype=F32) / quarter)
    ang = jnp.concatenate([rows[:, None] * inv, rows[:, None] * inv,
                           cols[:, None] * inv, cols[:, None] * inv], axis=1)
    ang = jnp.concatenate([jnp.zeros((Lc, HEAD), F32), ang], axis=0)
    ang = jnp.concatenate([ang, ang], axis=1)
    return jnp.cos(ang), jnp.sin(ang)


def _qkv_rope_kernel(h_ref, w_ref, b_ref, cos_ref, sin_ref, q_ref, k_ref, v_ref, *, n_q, n_kv, scale):
    x = jnp.dot(h_ref[0], w_ref[0], preferred_element_type=F32) + b_ref[...]
    n_rot = (n_q + n_kv) * HEAD
    xr = x[:, :n_rot]
    reps = n_rot // LANES
    cos = jnp.concatenate([cos_ref[...]] * reps, axis=1)
    sin = jnp.concatenate([sin_ref[...]] * reps, axis=1)
    quarter = HEAD // 4
    lane = lax.broadcasted_iota(jnp.int32, xr.shape, 1)
    first = (lane % (2 * quarter)) < quarter
    rot = jnp.where(first, -pltpu.roll(xr, n_rot - quarter, axis=1), pltpu.roll(xr, quarter, axis=1))
    y = xr * cos + rot * sin
    for h in range(n_q):
        q_ref[0, h] = (y[:, h * HEAD:(h + 1) * HEAD] * scale).astype(q_ref.dtype)
    for h in range(n_kv):
        lo = (n_q + h) * HEAD
        k_ref[0, h] = y[:, lo:lo + HEAD].astype(k_ref.dtype)
        lo = (n_q + n_kv + h) * HEAD
        v_ref[0, h, :, :HEAD] = x[:, lo:lo + HEAD].astype(v_ref.dtype)
        v_ref[0, h, :, HEAD:] = jnp.ones((x.shape[0], HEAD), v_ref.dtype)


def _qkv_rope(h, w, wi, bias, cos, sin, *, n_q, n_kv):
    B, T, K = h.shape
    W = w.shape[2]
    tt = _tile(T, ROW_TILE, BF16_ROWS)
    out = lambda n, width=HEAD: pl.BlockSpec((1, n, tt, width), lambda b, t: (b, 0, t, 0))
    return pl.pallas_call(
        functools.partial(_qkv_rope_kernel, n_q=n_q, n_kv=n_kv, scale=HEAD ** -0.5),
        grid=(B, T // tt),
        in_specs=[pl.BlockSpec((1, tt, K), lambda b, t: (b, t, 0)),
                  pl.BlockSpec((1, K, W), lambda b, t: (wi, 0, 0), pipeline_mode=pl.Buffered(1)),
                  pl.BlockSpec((1, W), lambda b, t: (0, 0)),
                  pl.BlockSpec((tt, LANES), lambda b, t: (t, 0)),
                  pl.BlockSpec((tt, LANES), lambda b, t: (t, 0))],
        out_specs=[out(n_q), out(n_kv), out(n_kv, 2 * HEAD)],
        out_shape=[jax.ShapeDtypeStruct((B, n_q, T, HEAD), BF16),
                   jax.ShapeDtypeStruct((B, n_kv, T, HEAD), BF16),
                   jax.ShapeDtypeStruct((B, n_kv, T, 2 * HEAD), BF16)],
        compiler_params=_params("parallel", "parallel"),
        name="qkv_rope",
    )(h, w, bias.reshape(1, W).astype(F32), cos, sin)


def _attn_kernel(*refs, n_q, n_kv, local, q_off, n_blocks):
    if local:
        q_ref, kp_ref, kc_ref, kn_ref, vp_ref, vc_ref, vn_ref, kx_ref, vx_ref, sink_ref, o_ref = refs
    else:
        q_ref, kx_ref, vx_ref, sink_ref, o_ref = refs
    G = n_q // n_kv
    R = G * AT_BLOCK
    n = pl.program_id(1)
    nt = (((1,), (1,)), ((), ()))
    if local:
        qi = lax.broadcasted_iota(jnp.int32, (R, 3 * AT_BLOCK), 0) % AT_BLOCK
        kj = lax.broadcasted_iota(jnp.int32, (R, 3 * AT_BLOCK), 1) - AT_BLOCK
        kpos = kj + n * AT_BLOCK
        ok = (jnp.abs(qi - kj) <= AT_WINDOW) & (kpos >= 0) & (kpos < n_blocks * AT_BLOCK)
    H = range(n_kv)
    q = [q_ref[0, h * G:(h + 1) * G].reshape(R, HEAD) for h in H]
    sink = [jnp.concatenate(
        [jnp.broadcast_to(sink_ref[h * G + g:h * G + g + 1, 0:1], (AT_BLOCK, 1)) for g in range(G)],
        axis=0) for h in H]
    s_ctx = [lax.dot_general(q[h], kx_ref[0, h], nt, preferred_element_type=F32) for h in H]
    m = [jnp.maximum(jnp.max(s_ctx[h], axis=-1, keepdims=True), sink[h]) for h in H]
    if local:
        k_loc = [jnp.concatenate([kp_ref[0, h], kc_ref[0, h], kn_ref[0, h]], axis=0) for h in H]
        v_loc = [jnp.concatenate([vp_ref[0, h], vc_ref[0, h], vn_ref[0, h]], axis=0) for h in H]
        s_loc = [jnp.where(ok, lax.dot_general(q[h], k_loc[h], nt, preferred_element_type=F32), NEG_INF)
                 for h in H]
        m = [jnp.maximum(m[h], jnp.max(s_loc[h], axis=-1, keepdims=True)) for h in H]
    o = [jnp.dot(jnp.exp(s_ctx[h] - m[h]).astype(BF16), vx_ref[0, h], preferred_element_type=F32)
         for h in H]
    if local:
        o = [o[h] + jnp.dot(jnp.exp(s_loc[h] - m[h]).astype(BF16), v_loc[h], preferred_element_type=F32)
             for h in H]
    den = [o[h][:, HEAD:HEAD + 1] + jnp.exp(sink[h] - m[h]) for h in H]
    o = [o[h][:, :HEAD] * (1.0 / den[h]) for h in H]
    for h in H:
        for g in range(G):
            hh = h * G + g
            o_ref[0, :, hh * HEAD:(hh + 1) * HEAD] = o[h][g * AT_BLOCK:(g + 1) * AT_BLOCK].astype(o_ref.dtype)


def _attn(q, k, v, sink, *, Lc, local):
    B, n_q, T, _ = q.shape
    n_kv = k.shape[1]
    cb = Lc // AT_BLOCK
    nb = (T - Lc) // AT_BLOCK if local else cb
    q_off = cb if local else 0
    last = T // AT_BLOCK - 1
    qspec = pl.BlockSpec((1, n_q, AT_BLOCK, HEAD), lambda b, n: (b, 0, n + q_off, 0))
    xspec = lambda a: pl.BlockSpec((1, n_kv, Lc, a.shape[3]), lambda b, n: (b, 0, 0, 0))
    sspec = pl.BlockSpec((n_q, LANES), lambda b, n: (0, 0))
    sink_b = jnp.broadcast_to(sink.astype(F32)[:, None], (n_q, LANES))
    if local:
        blk = lambda a, f: pl.BlockSpec((1, n_kv, AT_BLOCK, a.shape[3]), lambda b, n: (b, 0, f(n), 0))
        prev = lambda n: jnp.maximum(n + cb - 1, cb)
        cur = lambda n: n + cb
        nxt = lambda n: jnp.minimum(n + cb + 1, last)
        in_specs = ([qspec] + [blk(k, f) for f in (prev, cur, nxt)] + [blk(v, f) for f in (prev, cur, nxt)]
                    + [xspec(k), xspec(v), sspec])
        args = (q, k, k, k, v, v, v, k, v, sink_b)
    else:
        in_specs = [qspec, xspec(k), xspec(v), sspec]
        args = (q, k, v, sink_b)
    return pl.pallas_call(
        functools.partial(_attn_kernel, n_q=n_q, n_kv=n_kv, local=local, q_off=q_off, n_blocks=nb),
        grid=(B, nb),
        in_specs=in_specs,
        out_specs=pl.BlockSpec((1, AT_BLOCK, n_q * HEAD), lambda b, n: (b, n, 0)),
        out_shape=jax.ShapeDtypeStruct((B, nb * AT_BLOCK, n_q * HEAD), BF16),
        compiler_params=_params("parallel", "parallel"),
        name="attn_local" if local else "attn_ctx",
    )(*args)


def _attention(h, p, *, Lc):
    B, T, D = h.shape
    n_q = D // HEAD
    cos, sin = _rope_tables(T, Lc)
    q, k, v = _qkv_rope(h, p["w_qkv"], p["index"], p["b_qkv"], cos, sin, n_q=n_q, n_kv=AT_KV_HEADS)
    o_ctx = _attn(q, k, v, p["sink"], Lc=Lc, local=False)
    o_lat = _attn(q, k, v, p["sink"], Lc=Lc, local=True)
    return jnp.concatenate([o_ctx, o_lat], axis=1)


MOE_TILE = 1024
ROUTE_G1, ROUTE_G2, ROUTE_I1, ROUTE_I2 = 0, 1, 2, 3


def _gates_kernel(l_ref, route_ref, sel_ref, *, n_experts):
    l = l_ref[...]
    lane = lax.broadcasted_iota(jnp.int32, l.shape, 1)
    l = jnp.where(lane < n_experts, l, -jnp.inf)
    m1 = jnp.max(l, axis=-1, keepdims=True)
    i1 = jnp.min(jnp.where(l == m1, lane, LANES), axis=-1, keepdims=True)
    l2 = jnp.where(lane == i1, -jnp.inf, l)
    m2 = jnp.max(l2, axis=-1, keepdims=True)
    i2 = jnp.min(jnp.where(l2 == m2, lane, LANES), axis=-1, keepdims=True)
    e2 = jnp.exp(m2 - m1)
    den = 1.0 + e2
    route_ref[...] = jnp.where(lane == ROUTE_G1, 1.0 / den,
                               jnp.where(lane == ROUTE_G2, e2 / den,
                                         jnp.where(lane == ROUTE_I1, i1.astype(F32),
                                                   jnp.where(lane == ROUTE_I2, i2.astype(F32), 0.0))))
    sel_ref[...] = jnp.where((lane == i1) | (lane == i2), 1.0, 0.0).astype(sel_ref.dtype)


def _gates(logits, n_experts):
    M = logits.shape[0]
    tm = _tile(M, 1024, 8)
    blk = pl.BlockSpec((tm, LANES), lambda i: (i, 0))
    return pl.pallas_call(
        functools.partial(_gates_kernel, n_experts=n_experts),
        grid=(M // tm,),
        in_specs=[blk],
        out_specs=[blk, blk],
        out_shape=[jax.ShapeDtypeStruct((M, LANES), F32), jax.ShapeDtypeStruct((M, LANES), BF16)],
        compiler_params=_params("parallel"),
        name="moe_gates",
    )(logits)


def _rank_kernel(sel_ref, rank_ref, cnt_ref, carry_ref):
    @pl.when(pl.program_id(0) == 0)
    def _():
        carry_ref[...] = jnp.zeros_like(carry_ref)

    s = sel_ref[...]
    n = s.shape[0]
    earlier = (lax.broadcasted_iota(jnp.int32, (n, n), 1) < lax.broadcasted_iota(jnp.int32, (n, n), 0))
    within = jnp.dot(jnp.where(earlier, 1.0, 0.0).astype(BF16), s, preferred_element_type=F32)
    rank_ref[...] = within + carry_ref[...]
    carry_ref[...] += jnp.sum(s.astype(F32), axis=0, keepdims=True)
    cnt_ref[...] = carry_ref[...]


def _rank(sel):
    M = sel.shape[0]
    tr = _tile(M, 512, 8)
    return pl.pallas_call(
        _rank_kernel,
        grid=(M // tr,),
        in_specs=[pl.BlockSpec((tr, LANES), lambda i: (i, 0))],
        out_specs=[pl.BlockSpec((tr, LANES), lambda i: (i, 0)), pl.BlockSpec((1, LANES), lambda i: (0, 0))],
        out_shape=[jax.ShapeDtypeStruct((M, LANES), F32), jax.ShapeDtypeStruct((1, LANES), F32)],
        scratch_shapes=[pltpu.VMEM((1, LANES), F32)],
        compiler_params=_params("arbitrary"),
        name="moe_rank",
    )(sel)


def _pos_kernel(route_ref, rank_ref, offs_ref, pos_ref):
    lane = lax.broadcasted_iota(jnp.int32, rank_ref.shape, 1)
    lane_f = lane.astype(F32)
    tot = rank_ref[...] + offs_ref[...]
    route = route_ref[...]
    p1 = jnp.sum(jnp.where(lane_f == route[:, ROUTE_I1:ROUTE_I1 + 1], tot, 0.0), axis=-1, keepdims=True)
    p2 = jnp.sum(jnp.where(lane_f == route[:, ROUTE_I2:ROUTE_I2 + 1], tot, 0.0), axis=-1, keepdims=True)
    pos_ref[...] = jnp.where(lane == 0, p1, jnp.where(lane == 1, p2, 0.0)).astype(jnp.int32)


def _positions(route, rank, offs):
    M = route.shape[0]
    tm = _tile(M, 1024, 8)
    blk = pl.BlockSpec((tm, LANES), lambda i: (i, 0))
    return pl.pallas_call(
        _pos_kernel,
        grid=(M // tm,),
        in_specs=[blk, blk, pl.BlockSpec((1, LANES), lambda i: (0, 0))],
        out_specs=blk,
        out_shape=jax.ShapeDtypeStruct((M, LANES), jnp.int32),
        compiler_params=_params("parallel"),
        name="moe_pos",
    )(route, rank, offs)


def _row_copy(src, dst, sem):
    return pltpu.make_async_copy(src, dst, sem)


def _dispatch_kernel(p1_ref, p2_ref, h_ref, xs_in_ref, xs_ref, sem):
    del xs_in_ref
    tt = h_ref.shape[0]
    base = pl.program_id(0) * tt

    def start(r, carry):
        row = h_ref.at[pl.ds(r, 1)]
        _row_copy(row, xs_ref.at[pl.ds(p1_ref[base + r], 1)], sem).start()
        _row_copy(row, xs_ref.at[pl.ds(p2_ref[base + r], 1)], sem).start()
        return carry

    lax.fori_loop(0, tt, start, 0, unroll=8)
    for _ in range(2):
        _row_copy(h_ref, xs_ref.at[pl.ds(0, tt)], sem).wait()


def _dispatch(h, p1, p2, n_rows):
    M, D = h.shape
    tt = _tile(M, ROW_TILE, 8)
    return pl.pallas_call(
        _dispatch_kernel,
        grid_spec=pltpu.PrefetchScalarGridSpec(
            num_scalar_prefetch=2,
            grid=(M // tt,),
            in_specs=[pl.BlockSpec((tt, D), lambda i, p1, p2: (i, 0)),
                      pl.BlockSpec(memory_space=pl.ANY)],
            out_specs=pl.BlockSpec(memory_space=pl.ANY),
            scratch_shapes=[pltpu.SemaphoreType.DMA(())]),
        out_shape=jax.ShapeDtypeStruct((n_rows, D), h.dtype),
        input_output_aliases={3: 0},
        compiler_params=_params("arbitrary"),
        name="moe_dispatch",
    )(p1, p2, h, jnp.zeros((n_rows, D), h.dtype))


def _gmm_kernel(te_ref, nv_ref, x_ref, *refs, swiglu):
    o_ref = refs[-1]

    @pl.when(pl.program_id(0) < nv_ref[0])
    def _():
        x = x_ref[...].astype(BF16)
        acc = jnp.dot(x, refs[0][0, 0], preferred_element_type=F32)
        if swiglu:
            acc = acc * jax.nn.sigmoid(acc) * jnp.dot(x, refs[1][0, 0], preferred_element_type=F32)
        o_ref[...] = acc.astype(o_ref.dtype)

    @pl.when(pl.program_id(0) >= nv_ref[0])
    def _():
        o_ref[...] = jnp.zeros_like(o_ref)


def _gmm(x, ws, layer, tile_expert, n_valid, *, out_dtype, tn=512):
    K = x.shape[1]
    N = ws[0].shape[3]
    tm = MOE_TILE
    P = tile_expert.shape[0] * tm
    tn = _tile(N, tn)
    wspec = pl.BlockSpec((1, 1, K, tn), lambda i, j, te, nv: (layer, te[i], 0, j))
    return pl.pallas_call(
        functools.partial(_gmm_kernel, swiglu=len(ws) == 2),
        grid_spec=pltpu.PrefetchScalarGridSpec(
            num_scalar_prefetch=2,
            grid=(P // tm, N // tn),
            in_specs=[pl.BlockSpec((tm, K), lambda i, j, te, nv: (jnp.minimum(i, nv[0] - 1), 0))]
            + [wspec] * len(ws),
            out_specs=pl.BlockSpec((tm, tn), lambda i, j, te, nv: (i, j))),
        out_shape=jax.ShapeDtypeStruct((P, N), out_dtype),
        compiler_params=_params("parallel", "parallel"),
        name="moe_gmm",
    )(tile_expert, n_valid, x, *ws)


def _combine_ln_kernel(p1_ref, p2_ref, route_ref, ys_ref, x_ref, mod_ref, modh_ref, w_ref, b_ref, *rest,
                       alpha, gate, sh, sc, has_h):
    a_ref, b2_ref, sem = rest[-3:]
    xo_ref = rest[0]
    tt = xo_ref.shape[1]
    base = (pl.program_id(0) * pl.num_programs(1) + pl.program_id(1)) * tt

    def start(r, carry):
        _row_copy(ys_ref.at[pl.ds(p1_ref[base + r], 1)], a_ref.at[pl.ds(r, 1)], sem).start()
        _row_copy(ys_ref.at[pl.ds(p2_ref[base + r], 1)], b2_ref.at[pl.ds(r, 1)], sem).start()
        return carry

    lax.fori_loop(0, tt, start, 0, unroll=8)
    _row_copy(ys_ref.at[pl.ds(0, tt)], a_ref, sem).wait()
    _row_copy(ys_ref.at[pl.ds(0, tt)], b2_ref, sem).wait()
    route = route_ref[0]
    f = route[:, ROUTE_G1:ROUTE_G1 + 1] * a_ref[...] + route[:, ROUTE_G2:ROUTE_G2 + 1] * b2_ref[...]
    xn = _post_norm(x_ref[0], f, mod_ref, w_ref, b_ref, alpha=alpha, gate=gate)
    xo_ref[0] = xn
    if has_h:
        h = xn * (1.0 + modh_ref[0, sc:sc + 1, :]) + modh_ref[0, sh:sh + 1, :]
        rest[1][0] = h.astype(rest[1].dtype)


def _combine_ln(ys, route, p1, p2, x, mod, ln_w, ln_b, *, alpha, gate, h_mod=None, mod_h=None,
                h_dtype=BF16, ctx_tiles, ctx_row):
    B, T, D = x.shape
    tt = ROW_TILE
    nt = T // tt
    smap = lambda f: (lambda b, t, p1, p2: f(b, t))
    row = pl.BlockSpec((1, tt, D), smap(lambda b, t: (b, t, 0)))
    vec = pl.BlockSpec((1, D), smap(lambda b, t: (0, 0)))
    mspec = pl.BlockSpec((1, N_MOD, D), smap(_mod_index(ctx_tiles, ctx_row, 0)))
    out_specs = [row]
    out_shape = [jax.ShapeDtypeStruct((B, T, D), F32)]
    sh = sc = 0
    if h_mod is not None:
        sh, sc = h_mod
        out_specs.append(row)
        out_shape.append(jax.ShapeDtypeStruct((B, T, D), h_dtype))
    return pl.pallas_call(
        functools.partial(_combine_ln_kernel, alpha=alpha, gate=gate, sh=sh, sc=sc,
                          has_h=h_mod is not None),
        grid_spec=pltpu.PrefetchScalarGridSpec(
            num_scalar_prefetch=2,
            grid=(B, nt),
            in_specs=[pl.BlockSpec((1, tt, LANES), smap(lambda b, t: (b, t, 0))),
                      pl.BlockSpec(memory_space=pl.ANY), row, mspec, mspec, vec, vec],
            out_specs=out_specs,
            scratch_shapes=[pltpu.VMEM((tt, D), F32), pltpu.VMEM((tt, D), F32),
                            pltpu.SemaphoreType.DMA(())]),
        out_shape=out_shape,
        compiler_params=_params("arbitrary", "arbitrary"),
        name="moe_combine_ln",
    )(p1, p2, route.reshape(B, T, LANES), ys, x, mod, mod if mod_h is None else mod_h,
      ln_w.reshape(1, D), ln_b.reshape(1, D))


def _moe(h, logits, w1, w3, w2, layer):
    M, D = h.shape
    E = w1.shape[1]
    tm = MOE_TILE
    route, sel = _gates(logits, E)
    rank, cnt = _rank(sel)
    counts = cnt[0, :E].astype(jnp.int32)
    padded = (counts + tm - 1) // tm * tm
    ends = jnp.cumsum(padded)
    n_tiles = (TOP_K * M) // tm + E
    tile_expert = jnp.minimum(
        jnp.searchsorted(ends, jnp.arange(n_tiles, dtype=jnp.int32) * tm, side="right"), E - 1
    ).astype(jnp.int32)
    n_valid = (ends[-1:] // tm).astype(jnp.int32)
    offs = jnp.zeros((1, LANES), F32).at[0, :E].set((ends - padded).astype(F32))
    pos = _positions(route, rank, offs)
    p1, p2 = pos[:, 0], pos[:, 1]
    xs = _dispatch(h, p1, p2, n_tiles * tm)
    mid = _gmm(xs, [w1, w3], layer, tile_expert, n_valid, out_dtype=BF16)
    ys = _gmm(mid, [w2], layer, tile_expert, n_valid, out_dtype=F32)
    return ys, route, p1, p2


def kernel(x, c, ctx, c_ctx, ada_w, ada_b, ln_w, ln_b, hy_w_in, hy_b_in, hy_conv_w, hy_conv_b, hy_f_w1, hy_f_b1, hy_f_w2, hy_f_b2, hy_f_w3, hy_f_freq, hy_skip, hy_w_out, hy_b_out, rw_mu, rw_w_rkv, rw_w_o, rw_w0, rw_w1, rw_w2, rw_a0, rw_a1, rw_a2, rw_g1, rw_g2, rw_k_k, rw_k_a, rw_r_k, rw_gn_w, rw_gn_b, at_w_qkv, at_b_qkv, at_w_o, at_b_o, at_sink, ff_w1, ff_w3, ff_w2, moe_router, moe_w1, moe_w3, moe_w2):
    B, L, D = x.shape
    Lc = ctx.shape[1]
    depth = ada_w.shape[0]
    alpha = (2 * depth) ** 0.25
    assert Lc % ROW_TILE == 0 and L % ROW_TILE == 0 and D % (RW_GROUP * HEAD) == 0
    bf = lambda t: t.astype(BF16)
    w_bf = dict(hy_out=bf(hy_w_out), rw_o=bf(rw_w_o), at_qkv=bf(at_w_qkv), at_o=bf(at_w_o),
                ff1=bf(ff_w1), ff3=bf(ff_w3), ff2=bf(ff_w2),
                moe1=bf(moe_w1), moe3=bf(moe_w3), moe2=bf(moe_w2))

    ctx_row = B
    rows = -(-(B + 1) // 8) * 8
    cond = jnp.zeros((rows, D), F32).at[:B].set(c).at[B].set(c_ctx)
    mods = _ada(cond, ada_w, ada_b).reshape(depth, rows, N_MOD, D)

    tables = {}
    if depth > 0:
        tables[L] = _dft_tables(L)
        if depth > 1:
            tables[Lc] = _dft_tables(Lc)

    ctx_tiles = Lc // ROW_TILE
    xs, h = _stream(ctx, x, mods[0], sh=0, sc=1, ctx_row=ctx_row, out_dtype=BF16)
    for i in range(depth):
        last = i == depth - 1
        kind = i % 3
        j = i // 3
        mod = mods[i]
        has_ctx = xs.shape[1] != L
        ct = ctx_tiles if has_ctx else 0
        T = xs.shape[1]
        if kind == 0:
            p = dict(w_in=bf(hy_w_in[j]), b_in=hy_b_in[j], conv_w=hy_conv_w[j], conv_b=hy_conv_b[j],
                     filter=(hy_f_w1[j], hy_f_b1[j], hy_f_w2[j], hy_f_b2[j], hy_f_w3[j], hy_f_freq[j]),
                     skip=hy_skip[j])
            y = _hyena(h, p, tables, Lc=Lc, has_ctx=has_ctx)
            proj = (w_bf["hy_out"], j, hy_b_out[j])
        elif kind == 1:
            assert has_ctx and not last
            p = dict(mu=rw_mu[j], w_rkv=bf(rw_w_rkv[j]), w0=rw_w0[j], w1=rw_w1[j],
                     w2=rw_w2[j], a0=rw_a0[j], a1=rw_a1[j], a2=rw_a2[j], g1=rw_g1[j], g2=rw_g2[j],
                     k_k=rw_k_k[j], k_a=rw_k_a[j], r_k=rw_r_k[j], gn_w=rw_gn_w[j], gn_b=rw_gn_b[j])
            y = _rwkv(h, p, Lc=Lc)
            proj = (w_bf["rw_o"], j, None)
        else:
            assert has_ctx and not last
            p = dict(w_qkv=w_bf["at_qkv"], index=j, b_qkv=at_b_qkv[j], sink=at_sink[j])
            y = _attention(h, p, Lc=Lc)
            proj = (w_bf["at_o"], j, at_b_o[j])

        drop1 = ct if last else 0
        fj = i // 2
        moe = i % 2 == 1
        if moe:
            pw, pi, pb = proj
            y = _mm(y.reshape(B * T, D), pw, wi=pi, bias=pb).reshape(B, T, D)
            proj = None
        res = _ln(xs, y, mod, ln_w[i, 0], ln_b[i, 0], alpha=alpha, gate=2, proj=proj, h_mod=(3, 4),
                  h_dtype=F32 if moe else BF16, router=moe_router[fj] if moe else None,
                  ctx_tiles=ct, ctx_row=ctx_row, drop_tiles=drop1)
        xs, h2 = res[0], res[1]
        ct = ct - drop1
        T = xs.shape[1]
        M = B * T
        nxt_last = i + 1 == depth - 1
        nkind = (i + 1) % 3
        drop2 = ct if (not last and nxt_last and nkind == 0) else 0
        nxt = {} if last else dict(h_mod=(0, 1), mod_h=mods[i + 1], h_dtype=F32 if nkind == 1 else BF16)
        if moe:
            routed = _moe(h2.reshape(M, D), res[2].reshape(M, LANES), w_bf["moe1"], w_bf["moe3"],
                          w_bf["moe2"], fj)
            res = _combine_ln(*routed, xs, mod, ln_w[i, 1], ln_b[i, 1], alpha=alpha, gate=5,
                              ctx_tiles=ct, ctx_row=ctx_row, **nxt)
            if drop2:
                res = [lax.slice_in_dim(t, drop2 * ROW_TILE, T, axis=1) for t in res]
        else:
            f = _mm(h2.reshape(M, D), w_bf["ff1"], w3=w_bf["ff3"], wi=fj, out_dtype=BF16).reshape(B, T, -1)
            res = _ln(xs, f, mod, ln_w[i, 1], ln_b[i, 1], alpha=alpha, gate=5, proj=(w_bf["ff2"], fj, None),
                      ctx_tiles=ct, ctx_row=ctx_row, drop_tiles=drop2, **nxt)
        xs = res[0]
        if not last:
            h = res[1]
    return xs
```

```python
import functools
import math

import jax
import jax.numpy as jnp
from jax import lax
from jax.experimental import pallas as pl
from jax.experimental.pallas import tpu as pltpu

F32 = jnp.float32
BF16 = jnp.bfloat16
HIGHEST = lax.Precision.HIGHEST

VMEM_LIMIT_BYTES = 56 * 1024 * 1024
LANES = 128
SUBLANES = 8
BF16_ROWS = 16
ROW_TILE = 256
MM_ROWS, MM_COLS = 1024, 512
WIDE_COLS = 1024
HY_IN_ROWS = 1280
RANK_ROWS = 512

LN_EPS = 1e-5
N_MOD = 6
HEAD = 64
RW_GN_EPS = 64e-5
RW_CHUNK = 64
RW_GROUP = 4
AT_KV_HEADS = 4
AT_WINDOW = 128
AT_BLOCK = 128
GRID_W = 64
ROPE_BASE = 10000.0
NEG_INF = -1e30
TOP_K = 2
HY_BANDS = 16
HY_EMB = 2 * HY_BANDS + 1
HY_MIN_DECAY = -math.log(1e-2) / 1.5
HY_MAX_DECAY = -math.log(1e-2) / 0.3


def _params(*sem):
    return pltpu.CompilerParams(dimension_semantics=sem, vmem_limit_bytes=VMEM_LIMIT_BYTES)


def _tile(n, pref, mult=LANES):
    if n <= pref:
        return n
    t = (pref // mult) * mult
    while t >= mult:
        if n % t == 0:
            return t
        t -= mult
    return n


def _mm_kernel(*refs, n_w, has_bias, act):
    x = refs[0][...].astype(BF16)
    o_ref = refs[-1]
    acc = jnp.dot(x, refs[1][0].astype(BF16), preferred_element_type=F32)
    if has_bias:
        acc = acc + refs[1 + n_w][...]
    if act == "swiglu":
        acc3 = jnp.dot(x, refs[2][0].astype(BF16), preferred_element_type=F32)
        acc = acc * jax.nn.sigmoid(acc) * acc3
    o_ref[...] = acc.astype(o_ref.dtype)


def _mm(x, w, *, w3=None, wi=0, bias=None, out_dtype=F32, tm=MM_ROWS, tn=MM_COLS):
    M, K = x.shape
    N = w.shape[2]
    tm = _tile(M, tm, SUBLANES)
    tn = _tile(N, tn)
    ws = [w] if w3 is None else [w, w3]
    in_specs = [pl.BlockSpec((tm, K), lambda i, j: (i, 0))]
    in_specs += [pl.BlockSpec((1, K, tn), lambda i, j: (wi, 0, j)) for _ in ws]
    args = [x] + ws
    if bias is not None:
        in_specs.append(pl.BlockSpec((1, tn), lambda i, j: (0, j)))
        args.append(bias.reshape(1, N).astype(F32))
    return pl.pallas_call(
        functools.partial(_mm_kernel, n_w=len(ws), has_bias=bias is not None,
                          act="swiglu" if w3 is not None else None),
        grid=(M // tm, N // tn),
        in_specs=in_specs,
        out_specs=pl.BlockSpec((tm, tn), lambda i, j: (i, j)),
        out_shape=jax.ShapeDtypeStruct((M, N), out_dtype),
        compiler_params=_params("parallel", "parallel"),
        name="mm_swiglu" if w3 is not None else "mm",
    )(*args)


def _ada_kernel(c_ref, w_ref, b_ref, o_ref):
    c = c_ref[...]
    s = (c * jax.nn.sigmoid(c)).astype(BF16)
    o_ref[0] = jnp.dot(s, w_ref[0].astype(BF16), preferred_element_type=F32) + b_ref[0]


def _ada(cond, ada_w, ada_b):
    depth, D, N = ada_w.shape
    R = cond.shape[0]
    tn = _tile(N, WIDE_COLS)
    return pl.pallas_call(
        _ada_kernel,
        grid=(depth, N // tn),
        in_specs=[pl.BlockSpec((R, D), lambda i, j: (0, 0)),
                  pl.BlockSpec((1, D, tn), lambda i, j: (i, 0, j)),
                  pl.BlockSpec((1, 1, tn), lambda i, j: (i, 0, j))],
        out_specs=pl.BlockSpec((1, R, tn), lambda i, j: (i, 0, j)),
        out_shape=jax.ShapeDtypeStruct((depth, R, N), F32),
        compiler_params=_params("parallel", "parallel"),
        name="ada",
    )(cond, ada_w, ada_b.reshape(depth, 1, N))


def _mod_index(ctx_tiles, ctx_row, off):
    def index(b, t):
        return (jnp.where(t + off < ctx_tiles, ctx_row, b), 0, 0)
    return index


def _stream_kernel(ctx_ref, x_ref, mod_ref, xs_ref, h_ref, *, sh, sc, ctx_tiles):
    x = jnp.where(pl.program_id(1) < ctx_tiles, ctx_ref[0], x_ref[0])
    xs_ref[0] = x
    h_ref[0] = (x * (1.0 + mod_ref[0, sc:sc + 1, :]) + mod_ref[0, sh:sh + 1, :]).astype(h_ref.dtype)


def _stream(ctx, x, mod, *, sh, sc, ctx_row, out_dtype):
    B, L, D = x.shape
    ctx_tiles = ctx.shape[1] // ROW_TILE
    T = ctx.shape[1] + L
    blk = pl.BlockSpec((1, ROW_TILE, D), lambda b, t: (b, t, 0))
    return pl.pallas_call(
        functools.partial(_stream_kernel, sh=sh, sc=sc, ctx_tiles=ctx_tiles),
        grid=(B, T // ROW_TILE),
        in_specs=[pl.BlockSpec((1, ROW_TILE, D), lambda b, t: (b, jnp.minimum(t, ctx_tiles - 1), 0)),
                  pl.BlockSpec((1, ROW_TILE, D), lambda b, t: (b, jnp.maximum(t - ctx_tiles, 0), 0)),
                  pl.BlockSpec((1, N_MOD, D), _mod_index(ctx_tiles, ctx_row, 0))],
        out_specs=[blk, blk],
        out_shape=[jax.ShapeDtypeStruct((B, T, D), F32), jax.ShapeDtypeStruct((B, T, D), out_dtype)],
        compiler_params=_params("parallel", "parallel"),
        name="stream",
    )(ctx, x, mod)


def _post_norm(x, y, mod_ref, w_ref, b_ref, *, alpha, gate):
    z = alpha * x + mod_ref[0, gate:gate + 1, :] * y
    mu = jnp.mean(z, axis=-1, keepdims=True)
    d = z - mu
    var = jnp.mean(d * d, axis=-1, keepdims=True)
    return d * lax.rsqrt(var + LN_EPS) * w_ref[...] + b_ref[...]


def _ln_kernel(*refs, alpha, gate, sh, sc, has_h, has_router, has_proj):
    x_ref, y_ref, mod_ref, modh_ref, w_ref, b_ref = refs[:6]
    pos = 6
    if has_proj:
        pw_ref, pb_ref = refs[pos:pos + 2]
        pos += 2
    router_ref = None
    if has_router:
        router_ref = refs[pos]
        pos += 1
    xo_ref = refs[pos]
    pos += 1
    if has_proj:
        y = jnp.dot(y_ref[0], pw_ref[0], preferred_element_type=F32) + pb_ref[...]
    else:
        y = y_ref[0].astype(F32)
    xn = _post_norm(x_ref[0], y, mod_ref, w_ref, b_ref, alpha=alpha, gate=gate)
    xo_ref[0] = xn
    if has_h:
        h = xn * (1.0 + modh_ref[0, sc:sc + 1, :]) + modh_ref[0, sh:sh + 1, :]
        h_ref = refs[pos]
        pos += 1
        h_ref[0] = h.astype(h_ref.dtype)
        if has_router:
            refs[pos][0] = jnp.dot(h, router_ref[...], precision=HIGHEST, preferred_element_type=F32)


def _ln(x, y, mod, ln_w, ln_b, *, alpha, gate, proj=None, h_mod=None, mod_h=None, h_dtype=BF16,
        router=None, ctx_tiles, ctx_row, drop_tiles=0):
    B, T, D = x.shape
    K = y.shape[2]
    nt = T // ROW_TILE - drop_tiles
    To = nt * ROW_TILE
    off = drop_tiles
    in_specs = [pl.BlockSpec((1, ROW_TILE, D), lambda b, t: (b, t + off, 0)),
                pl.BlockSpec((1, ROW_TILE, K), lambda b, t: (b, t + off, 0)),
                pl.BlockSpec((1, N_MOD, D), _mod_index(ctx_tiles, ctx_row, off)),
                pl.BlockSpec((1, N_MOD, D), _mod_index(ctx_tiles, ctx_row, off)),
                pl.BlockSpec((1, D), lambda b, t: (0, 0)),
                pl.BlockSpec((1, D), lambda b, t: (0, 0))]
    args = [x, y, mod, mod if mod_h is None else mod_h, ln_w.reshape(1, D), ln_b.reshape(1, D)]
    if proj is not None:
        pw, pi, pb = proj
        in_specs += [pl.BlockSpec((1, K, D), lambda b, t: (pi, 0, 0), pipeline_mode=pl.Buffered(1)),
                     pl.BlockSpec((1, D), lambda b, t: (0, 0))]
        args += [pw, jnp.zeros((1, D), F32) if pb is None else pb.reshape(1, D).astype(F32)]
    out_specs = [pl.BlockSpec((1, ROW_TILE, D), lambda b, t: (b, t, 0))]
    out_shape = [jax.ShapeDtypeStruct((B, To, D), F32)]
    sh = sc = 0
    if h_mod is not None:
        sh, sc = h_mod
        out_specs.append(pl.BlockSpec((1, ROW_TILE, D), lambda b, t: (b, t, 0)))
        out_shape.append(jax.ShapeDtypeStruct((B, To, D), h_dtype))
    if router is not None:
        E = router.shape[1]
        router_p = jnp.zeros((D, LANES), F32).at[:, :E].set(router)
        in_specs.append(pl.BlockSpec((D, LANES), lambda b, t: (0, 0)))
        args.append(router_p)
        out_specs.append(pl.BlockSpec((1, ROW_TILE, LANES), lambda b, t: (b, t, 0)))
        out_shape.append(jax.ShapeDtypeStruct((B, To, LANES), F32))
    return pl.pallas_call(
        functools.partial(_ln_kernel, alpha=alpha, gate=gate, sh=sh, sc=sc, has_h=h_mod is not None,
                          has_router=router is not None, has_proj=proj is not None),
        grid=(B, nt),
        in_specs=in_specs,
        out_specs=out_specs,
        out_shape=out_shape,
        compiler_params=_params("parallel", "parallel"),
        name="ln_residual",
    )(*args)


def _halo_specs(T, C, col):
    S = SUBLANES
    r8 = ROW_TILE // S
    last8 = T // S - 1
    return [pl.BlockSpec((1, ROW_TILE, C), lambda b, t, j: (b, t, col(j))),
            pl.BlockSpec((1, S, C), lambda b, t, j: (b, jnp.maximum(t * r8 - 1, 0), col(j))),
            pl.BlockSpec((1, S, C), lambda b, t, j: (b, jnp.minimum(t * r8 + r8, last8), col(j)))]


def _neighbours(cur, prev8, next8, t, n_tiles, ctx_tiles):
    rows = lax.broadcasted_iota(jnp.int32, cur.shape, 0)
    has_prev = jnp.logical_and(t != 0, t != ctx_tiles)
    has_next = jnp.logical_and(t != n_tiles - 1, t != ctx_tiles - 1)
    top = jnp.where(has_prev, prev8[7:8, :], 0.0)
    bot = jnp.where(has_next, next8[0:1, :], 0.0)
    up = jnp.where(rows == 0, top, pltpu.roll(cur, 1, axis=0))
    dn = jnp.where(rows == cur.shape[0] - 1, bot, pltpu.roll(cur, cur.shape[0] - 1, axis=0))
    return up, dn


def _hy_in_kernel(x_ref, xp_ref, xn_ref, *refs, tm, T, Lc):
    ws, bs, cws, cbs = refs[0:3], refs[3:6], refs[6:9], refs[9:12]
    x0_ref, vv_ref, vvb_ref = refs[12:15]
    H = BF16_ROWS
    x = jnp.concatenate([xp_ref[0], x_ref[0], xn_ref[0]], axis=0)
    row = lax.broadcasted_iota(jnp.int32, (tm, 1), 0) + pl.program_id(1) * tm
    no_prev = (row == 0) | (row == Lc)
    no_next = (row == Lc - 1) | (row == T - 1)
    out = []
    for s in range(3):
        acc = jnp.dot(x, ws[s][...], preferred_element_type=F32) + bs[s][...]
        n = acc.shape[0]
        cur = acc[H:H + tm]
        up = jnp.where(no_prev, 0.0, pltpu.roll(acc, 1, axis=0)[H:H + tm])
        dn = jnp.where(no_next, 0.0, pltpu.roll(acc, n - 1, axis=0)[H:H + tm])
        cw = cws[s]
        out.append(up * cw[0:1, :] + cur * cw[1:2, :] + dn * cw[2:3, :] + cbs[s][...])
    x0_ref[0] = out[0]
    vv = out[1] * out[2]
    vv_ref[0] = vv
    vvb_ref[0] = vv.astype(BF16)


def _hy_in(h, w_in, b_in, conv_w, conv_b, *, Lc):
    B, T, K = h.shape
    D = w_in.shape[1] // 3
    H = BF16_ROWS
    tm = _tile(T, HY_IN_ROWS, 4 * BF16_ROWS)
    tc = _tile(D, MM_COLS)
    nj = D // tc
    r16 = tm // H
    last16 = T // H - 1
    sec = lambda shape: [pl.BlockSpec(shape, lambda b, t, j, s=s: (0, s * nj + j)) for s in range(3)]
    in_specs = [pl.BlockSpec((1, tm, K), lambda b, t, j: (b, t, 0)),
                pl.BlockSpec((1, H, K), lambda b, t, j: (b, jnp.maximum(t * r16 - 1, 0), 0)),
                pl.BlockSpec((1, H, K), lambda b, t, j: (b, jnp.minimum(t * r16 + r16, last16), 0))]
    in_specs += sec((K, tc)) + sec((1, tc)) + sec((3, tc)) + sec((1, tc))
    blk = pl.BlockSpec((1, tm, tc), lambda b, t, j: (b, t, j))
    b2 = b_in.reshape(1, 3 * D)
    cb2 = conv_b.reshape(1, 3 * D)
    return pl.pallas_call(
        functools.partial(_hy_in_kernel, tm=tm, T=T, Lc=Lc),
        grid=(B, T // tm, nj),
        in_specs=in_specs,
        out_specs=[blk, blk, blk],
        out_shape=[jax.ShapeDtypeStruct((B, T, D), F32), jax.ShapeDtypeStruct((B, T, D), F32),
                   jax.ShapeDtypeStruct((B, T, D), BF16)],
        compiler_params=_params("parallel", "parallel", "parallel"),
        name="hy_in",
    )(h, h, h, w_in, w_in, w_in, b2, b2, b2, conv_w, conv_w, conv_w, cb2, cb2, cb2)


def _hy_filter_kernel(w1_ref, b1_ref, w2_ref, b2_ref, w3_ref, fr_ref, h_ref, s_ref, *, L, D, tl):
    i = pl.program_id(0)
    row = (lax.broadcasted_iota(jnp.int32, (tl, LANES), 0) + i * tl).astype(F32)
    lane = lax.broadcasted_iota(jnp.int32, (tl, LANES), 1)
    band = jnp.where(lane <= HY_BANDS, lane - 1, lane - 1 - HY_BANDS).astype(F32)
    freq = 1e-4 + band * ((HY_BANDS - 1 - 1e-4) / (HY_BANDS - 1))
    ang = freq * (row * (2.0 * math.pi / L))
    z = jnp.where(lane == 0, row / (L - 1),
                  jnp.where(lane <= HY_BANDS, jnp.cos(ang),
                            jnp.where(lane < HY_EMB, -jnp.sin(ang), 0.0)))
    h = jnp.sin(fr_ref[0:1, :] * (jnp.dot(z, w1_ref[...], precision=HIGHEST,
                                           preferred_element_type=F32) + b1_ref[...]))
    h = jnp.sin(fr_ref[1:2, :] * (jnp.dot(h, w2_ref[...], precision=HIGHEST,
                                           preferred_element_type=F32) + b2_ref[...]))
    h = jnp.dot(h, w3_ref[...], precision=HIGHEST, preferred_element_type=F32)
    half = L // 2
    rowd = (lax.broadcasted_iota(jnp.int32, (tl, D), 0) + i * tl).astype(F32)
    dist = jnp.abs(rowd - half) / half
    chan = lax.broadcasted_iota(jnp.int32, (tl, D), 1).astype(F32)
    deltas = HY_MIN_DECAY + chan * ((HY_MAX_DECAY - HY_MIN_DECAY) / (D - 1))
    h = h * jnp.exp(-dist * deltas)
    h_ref[...] = h

    @pl.when(i == 0)
    def _():
        s_ref[...] = jnp.zeros_like(s_ref)

    s_ref[...] += jnp.sum(jnp.abs(h), axis=0, keepdims=True)


def _hy_filter(L, f_w1, f_b1, f_w2, f_b2, f_w3, f_freq):
    D = f_w3.shape[1]
    hid = f_w1.shape[1]
    w1 = jnp.zeros((LANES, LANES), F32).at[:HY_EMB, :hid].set(f_w1)
    b1 = jnp.zeros((1, LANES), F32).at[0, :hid].set(f_b1)
    w2 = jnp.zeros((LANES, LANES), F32).at[:hid, :hid].set(f_w2)
    b2 = jnp.zeros((1, LANES), F32).at[0, :hid].set(f_b2)
    w3 = jnp.zeros((LANES, D), F32).at[:hid].set(f_w3)
    fr = jnp.zeros((2, LANES), F32).at[:, :hid].set(f_freq)
    tl = _tile(L, ROW_TILE, SUBLANES)
    full = lambda shape: pl.BlockSpec(shape, lambda i: (0, 0))
    return pl.pallas_call(
        functools.partial(_hy_filter_kernel, L=L, D=D, tl=tl),
        grid=(L // tl,),
        in_specs=[full((LANES, LANES)), full((1, LANES)), full((LANES, LANES)), full((1, LANES)),
                  full((LANES, D)), full((2, LANES))],
        out_specs=[pl.BlockSpec((tl, D), lambda i: (i, 0)), full((1, D))],
        out_shape=[jax.ShapeDtypeStruct((L, D), F32), jax.ShapeDtypeStruct((1, D), F32)],
        compiler_params=_params("arbitrary"),
        name="hy_filter",
    )(w1, b1, w2, b2, w3, fr)


DFT_ROWS = 64


def _dft_tables_kernel(fa_ref, ia_ref, b_ref, fwd_ref, inv_ref, *, L):
    n = 2 * L
    i = pl.program_id(0)
    cb, sb = b_ref[0], b_ref[1]
    q = lax.broadcasted_iota(jnp.int32, (DFT_ROWS, L), 0)
    col = lax.broadcasted_iota(jnp.int32, (DFT_ROWS, L), 1)
    sign_c = jnp.where(col % 2 == 0, 1.0, -1.0)
    sign_r = jnp.where(q % 2 == 0, 1.0, -1.0)
    ca, sa = fa_ref[0, 0:1, :], fa_ref[0, 1:2, :]
    cos_f = ca * cb - sa * sb
    sin_f = sa * cb + ca * sb
    first_row = (q == 0) & (i == 0)
    fwd_ref[0] = cos_f.astype(fwd_ref.dtype)
    fwd_ref[1] = jnp.where(first_row, sign_c, -sin_f).astype(fwd_ref.dtype)
    ca, sa = ia_ref[0, 0:1, :], ia_ref[0, 1:2, :]
    cos_i = ca * cb - sa * sb
    sin_i = sa * cb + ca * sb
    inv_ref[:, :L] = jnp.where(col == 0, 1.0 / n, (2.0 / n) * cos_i).astype(inv_ref.dtype)
    inv_ref[:, L:] = jnp.where(col == 0, sign_r / n, (-2.0 / n) * sin_i).astype(inv_ref.dtype)


def _dft_tables(L):
    n = 2 * L
    R = DFT_ROWS
    assert L % (2 * R) == 0
    w = 2.0 * math.pi / n
    c = jnp.arange(L, dtype=jnp.int32)
    hi = jnp.arange(L // R, dtype=jnp.int32) * R

    def cos_sin(rows):
        ang = ((rows[:, None] * c[None, :]) % n).astype(F32) * w
        return jnp.stack([jnp.cos(ang), jnp.sin(ang)], axis=1)

    fa = cos_sin(hi)
    ia = cos_sin(hi + L // 2)
    b = jnp.swapaxes(cos_sin(jnp.arange(R, dtype=jnp.int32)), 0, 1)
    return pl.pallas_call(
        functools.partial(_dft_tables_kernel, L=L),
        grid=(L // R,),
        in_specs=[pl.BlockSpec((1, 2, L), lambda i: (i, 0, 0)),
                  pl.BlockSpec((1, 2, L), lambda i: (i, 0, 0)),
                  pl.BlockSpec((2, R, L), lambda i: (0, 0, 0))],
        out_specs=[pl.BlockSpec((2, R, L), lambda i: (0, i, 0)),
                   pl.BlockSpec((R, 2 * L), lambda i: (i, 0))],
        out_shape=[jax.ShapeDtypeStruct((2, L, L), BF16), jax.ShapeDtypeStruct((L, 2 * L), BF16)],
        compiler_params=_params("parallel"),
        name="dft_tables",
    )(fa, ia, b)


def _dft_fwd_kernel(a_ref, w_ref, *rest, mode):
    o_ref = rest[-1]
    tm = a_ref.shape[1]
    v = jnp.dot(a_ref[...].reshape(2 * tm, a_ref.shape[2]), w_ref[0], preferred_element_type=F32)
    vre, vim = v[:tm], v[tm:]
    if mode == "scale":
        inv = 1.0 / (rest[0][...] + 1e-6)
        o_ref[0, 0] = vre * inv
        o_ref[0, 1] = vim * inv
    else:
        hre = rest[0][0]
        him = rest[0][1]
        first = jnp.logical_and(pl.program_id(2) == 0,
                                lax.broadcasted_iota(jnp.int32, vre.shape, 0) == 0)
        zre = jnp.where(first, vre * hre, vre * hre - vim * him)
        zim = jnp.where(first, vim * him, vre * him + vim * hre)
        o_ref[0, 0] = zre.astype(o_ref.dtype)
        o_ref[0, 1] = zim.astype(o_ref.dtype)


def _dft_fwd(fwd, w, extra, *, mode, out_dtype):
    B, L, D = w.shape
    tm = _tile(L, ROW_TILE, SUBLANES)
    tn = _tile(D, WIDE_COLS)
    if mode == "scale":
        extra_spec = pl.BlockSpec((1, tn), lambda b, j, m: (0, j))
    else:
        extra_spec = pl.BlockSpec((2, tm, tn), lambda b, j, m: (0, m, j))
    return pl.pallas_call(
        functools.partial(_dft_fwd_kernel, mode=mode),
        grid=(B, D // tn, L // tm),
        in_specs=[pl.BlockSpec((2, tm, L), lambda b, j, m: (0, m, 0)),
                  pl.BlockSpec((1, L, tn), lambda b, j, m: (b, 0, j)),
                  extra_spec],
        out_specs=pl.BlockSpec((1, 2, tm, tn), lambda b, j, m: (b, 0, m, j)),
        out_shape=jax.ShapeDtypeStruct((B, 2, L, D), out_dtype),
        compiler_params=_params("parallel", "parallel", "parallel"),
        name="dft_fwd",
    )(fwd, w, extra)


def _dft_inv_kernel(b_ref, z_ref, x0_ref, vv_ref, skip_ref, u_ref):
    y = jnp.dot(b_ref[...], z_ref[0], preferred_element_type=F32)
    u_ref[0] = (x0_ref[0] * (y + vv_ref[0] * skip_ref[...])).astype(u_ref.dtype)


def _dft_inv(inv, z, x0, vv, skip, *, row_off):
    B, n, D = z.shape
    L = n // 2
    tm = _tile(L, ROW_TILE, SUBLANES)
    tn = _tile(D, WIDE_COLS)
    off = row_off // tm
    return pl.pallas_call(
        _dft_inv_kernel,
        grid=(B, D // tn, L // tm),
        in_specs=[pl.BlockSpec((tm, n), lambda b, j, m: (m, 0)),
                  pl.BlockSpec((1, n, tn), lambda b, j, m: (b, 0, j)),
                  pl.BlockSpec((1, tm, tn), lambda b, j, m: (b, m + off, j)),
                  pl.BlockSpec((1, tm, tn), lambda b, j, m: (b, m + off, j)),
                  pl.BlockSpec((1, tn), lambda b, j, m: (0, j))],
        out_specs=pl.BlockSpec((1, tm, tn), lambda b, j, m: (b, m, j)),
        out_shape=jax.ShapeDtypeStruct((B, L, D), BF16),
        compiler_params=_params("parallel", "parallel", "parallel"),
        name="dft_inv",
    )(inv, z, x0, vv, skip.reshape(1, D))


def _hyena(h, p, tables, *, Lc, has_ctx):
    B, T, D = h.shape
    x0, vv, vvb = _hy_in(h, p["w_in"], p["b_in"], p["conv_w"], p["conv_b"], Lc=Lc if has_ctx else -1)
    segs = [(Lc, T - Lc)] if has_ctx else [(0, T)]
    if has_ctx:
        segs = [(0, Lc)] + segs
    us = []
    for start, L in segs:
        fwd, inv = tables[L]
        filt, asum = _hy_filter(L, *p["filter"])
        hf = _dft_fwd(fwd, filt.astype(BF16)[None], asum, mode="scale", out_dtype=F32)[0]
        seg = vvb if (start == 0 and L == T) else lax.slice_in_dim(vvb, start, start + L, axis=1)
        z = _dft_fwd(fwd, seg, hf, mode="mul", out_dtype=BF16).reshape(B, 2 * L, D)
        us.append(_dft_inv(inv, z, x0, vv, p["skip"], row_off=start))
    return us[0] if len(us) == 1 else jnp.concatenate(us, axis=1)


MU_R, MU_W, MU_K, MU_V, MU_A, MU_G = range(6)


def _rw_mix(cur_ref, prev_ref, next_ref, mu_ref, which, *, n_tiles, ctx_tiles):
    cur = cur_ref[0]
    up, dn = _neighbours(cur, prev_ref[0], next_ref[0], pl.program_id(1), n_tiles, ctx_tiles)
    dx = 0.5 * (up + dn) - cur
    return [(cur + dx * mu_ref[j:j + 1, :]).astype(BF16) for j in which]


def _rw_rkvg_kernel(cur_ref, prev_ref, next_ref, mu_ref, wr_ref, wk_ref, wv_ref, g1_ref, g2_ref,
                    r_ref, k_ref, v_ref, g_ref, **tiles):
    xr, xk, xv, xg = _rw_mix(cur_ref, prev_ref, next_ref, mu_ref, (MU_R, MU_K, MU_V, MU_G), **tiles)
    r_ref[0] = jnp.dot(xr, wr_ref[...], preferred_element_type=F32)
    k_ref[0] = jnp.dot(xk, wk_ref[...], preferred_element_type=F32)
    v_ref[0] = jnp.dot(xv, wv_ref[...], preferred_element_type=F32)
    t = jax.nn.sigmoid(jnp.dot(xg, g1_ref[...], preferred_element_type=F32)).astype(BF16)
    g_ref[0] = jnp.dot(t, g2_ref[...], preferred_element_type=F32)


def _rw_decay_kernel(cur_ref, prev_ref, next_ref, mu_ref, w1_ref, w2_ref, w0_ref, a1_ref, a2_ref, a0_ref,
                     lwf_ref, af_ref, lwb_ref, ab_ref, **tiles):
    xw, xa = _rw_mix(cur_ref, prev_ref, next_ref, mu_ref, (MU_W, MU_A), **tiles)
    R = w2_ref.shape[1]
    tw = jnp.tanh(jnp.dot(xw, w1_ref[...], preferred_element_type=F32)).astype(BF16)
    ta = jnp.dot(xa, a1_ref[...], preferred_element_type=F32).astype(BF16)
    for d, (lw_ref, a_ref) in enumerate(((lwf_ref, af_ref), (lwb_ref, ab_ref))):
        wl = jnp.dot(tw[:, d * R:(d + 1) * R], w2_ref[d], preferred_element_type=F32) + w0_ref[d:d + 1, :]
        lw_ref[0] = -math.exp(-0.5) * jax.nn.sigmoid(wl)
        al = jnp.dot(ta[:, d * R:(d + 1) * R], a2_ref[d], preferred_element_type=F32) + a0_ref[d:d + 1, :]
        a_ref[0] = jax.nn.sigmoid(al)


def _pad_rank(first, second):
    _, D, r = first.shape
    R = -(-r // LANES) * LANES
    f = jnp.zeros((D, 2 * R), BF16)
    s = jnp.zeros((2, R, D), BF16)
    for d in range(2):
        f = f.at[:, d * R:d * R + r].set(first[d].astype(BF16))
        s = s.at[d, :r].set(second[d].astype(BF16))
    return f, s


def _rw_project(hs, p, *, ctx_tiles):
    B, T, D = hs.shape
    n_tiles = T // ROW_TILE
    tiles = dict(n_tiles=n_tiles, ctx_tiles=ctx_tiles)
    halo = _halo_specs(T, D, lambda j: 0)
    once = pl.Buffered(1)
    full = lambda a: pl.BlockSpec(a.shape, lambda b, t, j: (0,) * a.ndim, pipeline_mode=once)
    blk = pl.BlockSpec((1, ROW_TILE, D), lambda b, t, j: (b, t, 0))
    out = jax.ShapeDtypeStruct((B, T, D), F32)
    mu = p["mu"]
    wr, wk, wv = p["w_rkv"][0], p["w_rkv"][1], p["w_rkv"][2]
    g1, g2 = p["g1"].astype(BF16), p["g2"].astype(BF16)
    r, k, v, g = pl.pallas_call(
        functools.partial(_rw_rkvg_kernel, **tiles),
        grid=(B, n_tiles, 1),
        in_specs=halo + [full(a) for a in (mu, wr, wk, wv, g1, g2)],
        out_specs=[blk] * 4,
        out_shape=[out] * 4,
        compiler_params=_params("parallel", "parallel", "parallel"),
        name="rw_rkvg",
    )(hs, hs, hs, mu, wr, wk, wv, g1, g2)
    w1, w2 = _pad_rank(p["w1"], p["w2"])
    a1, a2 = _pad_rank(p["a1"], p["a2"])
    consts = (mu, w1, w2, p["w0"], a1, a2, p["a0"])
    lwf, af, lwb, ab = pl.pallas_call(
        functools.partial(_rw_decay_kernel, **tiles),
        grid=(B, n_tiles, 1),
        in_specs=halo + [full(a) for a in consts],
        out_specs=[blk] * 4,
        out_shape=[out] * 4,
        compiler_params=_params("parallel", "parallel", "parallel"),
        name="rw_decay",
    )(hs, hs, hs, *consts)
    return r, k, v, g, ((lwf, af), (lwb, ab))


def _block_diag(x, mask):
    return jnp.where(mask, jnp.concatenate([x] * RW_GROUP, axis=0), 0.0).astype(BF16)


def _rw_chunks(r, k, v, lw, a, k_k, k_a, state, same_head, incl, strict, reverse):
    C = RW_CHUNK
    W = RW_GROUP * HEAD
    G = range(len(r))
    nt = (((1,), (1,)), ((), ()))
    ones = jnp.where(same_head, 1.0, 0.0).astype(BF16)
    tri = jnp.where(incl[:, :C], 1.0, 0.0).astype(BF16)
    bd = lambda t: _block_diag(t, same_head)
    mm = lambda x, y: jnp.dot(x.astype(BF16), y, preferred_element_type=F32)

    def split(t):
        hi = t.astype(BF16)
        return hi, (t - hi.astype(F32)).astype(BF16)

    kkr = [k[g] * k_k[g] for g in G]
    sq = [split(kkr[g] * kkr[g]) for g in G]
    ng = len(r)
    sums = jnp.dot(jnp.concatenate([sq[g][0] for g in G] + [sq[g][1] for g in G], axis=0), ones,
                   preferred_element_type=F32)
    ss = [sums[g * C:(g + 1) * C] + sums[(ng + g) * C:(ng + g + 1) * C] for g in G]
    lws = [split(lw[g]) for g in G]
    lp = [jnp.dot(tri, lws[g][0], preferred_element_type=F32)
          + jnp.dot(tri, lws[g][1], preferred_element_type=F32) for g in G]
    kk = [kkr[g] / jnp.maximum(jnp.sqrt(ss[g]), 1e-12) for g in G]
    kd = [k[g] * (1.0 + (a[g] - 1.0) * k_a[g]) for g in G]
    lp_end = [lp[g][0:1, :] if reverse else lp[g][C - 1:C, :] for g in G]
    e_neg = [jnp.exp(-lp[g]) for g in G]
    lhs = [jnp.concatenate([-kk[g] * jnp.exp(lp[g] - lw[g]), r[g] * jnp.exp(lp[g])], axis=0).astype(BF16)
           for g in G]
    rhs = [jnp.concatenate([bd(kk[g] * a[g] * e_neg[g]), bd(kd[g] * e_neg[g])], axis=0) for g in G]
    cross = [lax.dot_general(lhs[g], rhs[g], nt, preferred_element_type=F32) for g in G]
    from_state = [lax.dot_general(lhs[g], state[g].astype(BF16), nt, preferred_element_type=F32)
                  for g in G]
    v_bd = [bd(v[g]) for g in G]
    p = [jnp.where(strict, cross[g][:C, :W], 0.0) for g in G]
    on_v = [mm(jnp.concatenate([jnp.where(strict, cross[g][:C, W:], 0.0),
                                jnp.where(incl, cross[g][C:, W:], 0.0)], axis=0), v_bd[g]) for g in G]
    x = [from_state[g][:C] + on_v[g][:C] for g in G]
    n = 1
    while n < C:
        x = [x[g] + mm(p[g], bd(x[g])) for g in G]
        n *= 2
        if n < C:
            p = [mm(p[g], bd(p[g])) for g in G]
    y = [from_state[g][C:] + mm(jnp.where(incl, cross[g][C:, :W], 0.0), bd(x[g])) + on_v[g][C:]
         for g in G]

    tail = [jnp.exp(lp_end[g] - lp[g]) for g in G]
    uv = [jnp.concatenate([x[g], v[g]], axis=0).astype(BF16) for g in G]
    bk = [jnp.concatenate([kk[g] * a[g] * tail[g], kd[g] * tail[g]], axis=0).astype(BF16) for g in G]
    upd = [lax.dot_general(uv[g], bk[g], (((0,), (0,)), ((), ())), preferred_element_type=F32)
           for g in G]
    new_state = [state[g] * jnp.exp(lp_end[g]) + jnp.where(same_head, upd[g], 0.0) for g in G]
    return y, new_state


RW_GROUPS_PER_STEP = 8
RW_CHUNKS_PER_STEP = 2


def _rw_scan_kernel(r_ref, k_ref, v_ref, lw_ref, a_ref, kk_ref, ka_ref, o_ref, state_ref, *, reverse):
    C = RW_CHUNK
    W = RW_GROUP * HEAD

    @pl.when(pl.program_id(2) == 0)
    def _():
        state_ref[...] = jnp.zeros_like(state_ref)

    ri = lax.broadcasted_iota(jnp.int32, (W, W), 0)
    ci = lax.broadcasted_iota(jnp.int32, (W, W), 1)
    same_head = (ri // HEAD) == (ci // HEAD)
    t_i = lax.broadcasted_iota(jnp.int32, (C, W), 0)
    s_i = lax.broadcasted_iota(jnp.int32, (C, W), 1) % C
    incl = (s_i >= t_i) if reverse else (s_i <= t_i)
    strict = (s_i > t_i) if reverse else (s_i < t_i)

    ng = state_ref.shape[0]
    sl = [slice(g * W, (g + 1) * W) for g in range(ng)]
    state = [state_ref[g] for g in range(ng)]
    order = range(RW_CHUNKS_PER_STEP)
    for c in (reversed(order) if reverse else order):
        rows = slice(c * C, (c + 1) * C)
        load = lambda ref: [ref[0, rows, s] for s in sl]
        y, state = _rw_chunks(load(r_ref), load(k_ref), load(v_ref), load(lw_ref), load(a_ref),
                              [kk_ref[:, s] for s in sl], [ka_ref[:, s] for s in sl],
                              state, same_head, incl, strict, reverse)
        for g in range(ng):
            o_ref[0, rows, sl[g]] = y[g]
    for g in range(ng):
        state_ref[g] = state[g]


def _rw_scan(r, k, v, lw, a, k_k, k_a, *, Lc, reverse):
    B, T, D = r.shape
    assert RW_CHUNK == HEAD
    rows = RW_CHUNK * RW_CHUNKS_PER_STEP
    assert Lc % rows == 0 and T % rows == 0
    W = RW_GROUP * HEAD
    ng = math.gcd(D // W, RW_GROUPS_PER_STEP)
    nb = T // rows
    ncb = Lc // rows

    def block(c):
        if not reverse:
            return c
        return jnp.where(c < ncb, ncb - 1 - c, nb - 1 - (c - ncb))

    blk = pl.BlockSpec((1, rows, ng * W), lambda b, g, c: (b, block(c), g))
    vec = pl.BlockSpec((1, ng * W), lambda b, g, c: (0, g))
    return pl.pallas_call(
        functools.partial(_rw_scan_kernel, reverse=reverse),
        grid=(B, D // (ng * W), nb),
        in_specs=[blk] * 5 + [vec, vec],
        out_specs=blk,
        out_shape=jax.ShapeDtypeStruct((B, T, D), F32),
        scratch_shapes=[pltpu.VMEM((ng, W, W), F32)],
        compiler_params=_params("parallel", "parallel", "arbitrary"),
        name="rw_scan",
    )(r, k, v, lw, a, k_k.reshape(1, D), k_a.reshape(1, D))


def _head_sums(x, ones_bd):
    W = ones_bd.shape[0]
    n = x.shape[0]
    groups = range(x.shape[1] // W)
    hi = x.astype(BF16)
    lo = (x - hi.astype(F32)).astype(BF16)
    rows = jnp.concatenate([t[:, g * W:(g + 1) * W] for t in (hi, lo) for g in groups], axis=0)
    sums = jnp.dot(rows, ones_bd, preferred_element_type=F32)
    half = len(groups) * n
    return jnp.concatenate([sums[g * n:(g + 1) * n] + sums[half + g * n:half + (g + 1) * n]
                            for g in groups], axis=1)


def _rw_post_kernel(of_ref, ob_ref, r_ref, k_ref, v_ref, g_ref, rk_ref, gw_ref, gb_ref, o_ref):
    W = RW_GROUP * HEAD
    ri = lax.broadcasted_iota(jnp.int32, (W, W), 0)
    ci = lax.broadcasted_iota(jnp.int32, (W, W), 1)
    ones_bd = jnp.where((ri // HEAD) == (ci // HEAD), 1.0, 0.0).astype(BF16)
    o = of_ref[0] + ob_ref[0]
    m = _head_sums(o, ones_bd) * (1.0 / HEAD)
    d = o - m
    var = _head_sums(d * d, ones_bd) * (1.0 / HEAD)
    on = d * lax.rsqrt(var + RW_GN_EPS) * gw_ref[...] + gb_ref[...]
    bonus = _head_sums(r_ref[0] * k_ref[0] * rk_ref[...], ones_bd) * v_ref[0]
    o_ref[0] = ((on + bonus) * g_ref[0]).astype(o_ref.dtype)


def _rw_post(o_f, o_b, r, k, v, g, r_k, gn_w, gn_b):
    B, T, D = r.shape
    tt = _tile(T, ROW_TILE // 2, SUBLANES)
    blk = pl.BlockSpec((1, tt, D), lambda b, t: (b, t, 0))
    par = pl.BlockSpec((1, D), lambda b, t: (0, 0))
    return pl.pallas_call(
        _rw_post_kernel,
        grid=(B, T // tt),
        in_specs=[blk] * 6 + [par] * 3,
        out_specs=blk,
        out_shape=jax.ShapeDtypeStruct((B, T, D), BF16),
        compiler_params=_params("parallel", "parallel"),
        name="rw_post",
    )(o_f, o_b, r, k, v, g, r_k.reshape(1, D), gn_w.reshape(1, D), gn_b.reshape(1, D))


def _rwkv(hs, p, *, Lc):
    r, k, v, g, dirs = _rw_project(hs, p, ctx_tiles=Lc // ROW_TILE)
    outs = [_rw_scan(r, k, v, lw, a, p["k_k"], p["k_a"], Lc=Lc, reverse=d == 1)
            for d, (lw, a) in enumerate(dirs)]
    return _rw_post(outs[0], outs[1], r, k, v, g, p["r_k"], p["gn_w"], p["gn_b"])


def _rope_tables(T, Lc):
    quarter = HEAD // 4
    pos = jnp.arange(T - Lc, dtype=jnp.int32)
    rows = (pos // GRID_W).astype(F32)
    cols = (pos % GRID_W).astype(F32)
    inv = ROPE_BASE ** (-jnp.arange(quarter, dtype=F32) / quarter)
    ang = jnp.concatenate([rows[:, None] * inv, rows[:, None] * inv,
                           cols[:, None] * inv, cols[:, None] * inv], axis=1)
    ang = jnp.concatenate([jnp.zeros((Lc, HEAD), F32), ang], axis=0)
    ang = jnp.concatenate([ang, ang], axis=1)
    return jnp.cos(ang), jnp.sin(ang)


def _qkv_rope_kernel(h_ref, w_ref, b_ref, cos_ref, sin_ref, q_ref, k_ref, v_ref, *, n_q, n_kv, scale):
    x = jnp.dot(h_ref[0], w_ref[0], preferred_element_type=F32) + b_ref[...]
    n_rot = (n_q + n_kv) * HEAD
    xr = x[:, :n_rot]
    reps = n_rot // LANES
    cos = jnp.concatenate([cos_ref[...]] * reps, axis=1)
    sin = jnp.concatenate([sin_ref[...]] * reps, axis=1)
    quarter = HEAD // 4
    lane = lax.broadcasted_iota(jnp.int32, xr.shape, 1)
    first = (lane % (2 * quarter)) < quarter
    rot = jnp.where(first, -pltpu.roll(xr, n_rot - quarter, axis=1), pltpu.roll(xr, quarter, axis=1))
    y = xr * cos + rot * sin
    for h in range(n_q):
        q_ref[0, h] = (y[:, h * HEAD:(h + 1) * HEAD] * scale).astype(q_ref.dtype)
    for h in range(n_kv):
        lo = (n_q + h) * HEAD
        k_ref[0, h] = y[:, lo:lo + HEAD].astype(k_ref.dtype)
        lo = (n_q + n_kv + h) * HEAD
        v_ref[0, h, :, :HEAD] = x[:, lo:lo + HEAD].astype(v_ref.dtype)
        v_ref[0, h, :, HEAD:] = jnp.ones((x.shape[0], HEAD), v_ref.dtype)


def _qkv_rope(h, w, wi, bias, cos, sin, *, n_q, n_kv):
    B, T, K = h.shape
    W = w.shape[2]
    tt = _tile(T, ROW_TILE, BF16_ROWS)
    out = lambda n, width=HEAD: pl.BlockSpec((1, n, tt, width), lambda b, t: (b, 0, t, 0))
    return pl.pallas_call(
        functools.partial(_qkv_rope_kernel, n_q=n_q, n_kv=n_kv, scale=HEAD ** -0.5),
        grid=(B, T // tt),
        in_specs=[pl.BlockSpec((1, tt, K), lambda b, t: (b, t, 0)),
                  pl.BlockSpec((1, K, W), lambda b, t: (wi, 0, 0), pipeline_mode=pl.Buffered(1)),
                  pl.BlockSpec((1, W), lambda b, t: (0, 0)),
                  pl.BlockSpec((tt, LANES), lambda b, t: (t, 0)),
                  pl.BlockSpec((tt, LANES), lambda b, t: (t, 0))],
        out_specs=[out(n_q), out(n_kv), out(n_kv, 2 * HEAD)],
        out_shape=[jax.ShapeDtypeStruct((B, n_q, T, HEAD), BF16),
                   jax.ShapeDtypeStruct((B, n_kv, T, HEAD), BF16),
                   jax.ShapeDtypeStruct((B, n_kv, T, 2 * HEAD), BF16)],
        compiler_params=_params("parallel", "parallel"),
        name="qkv_rope",
    )(h, w, bias.reshape(1, W).astype(F32), cos, sin)


def _attn_kernel(*refs, n_q, n_kv, local, n_blocks):
    if local:
        q_ref, kp_ref, kc_ref, kn_ref, vp_ref, vc_ref, vn_ref, kx_ref, vx_ref, sink_ref, o_ref = refs
    else:
        q_ref, kx_ref, vx_ref, sink_ref, o_ref = refs
    G = n_q // n_kv
    R = G * AT_BLOCK
    n = pl.program_id(1)
    nt = (((1,), (1,)), ((), ()))
    if local:
        qi = lax.broadcasted_iota(jnp.int32, (R, 3 * AT_BLOCK), 0) % AT_BLOCK
        kj = lax.broadcasted_iota(jnp.int32, (R, 3 * AT_BLOCK), 1) - AT_BLOCK
        kpos = kj + n * AT_BLOCK
        ok = (jnp.abs(qi - kj) <= AT_WINDOW) & (kpos >= 0) & (kpos < n_blocks * AT_BLOCK)
    H = range(n_kv)
    q = [q_ref[0, h * G:(h + 1) * G].reshape(R, HEAD) for h in H]
    sink = [jnp.concatenate(
        [jnp.broadcast_to(sink_ref[h * G + g:h * G + g + 1, 0:1], (AT_BLOCK, 1)) for g in range(G)],
        axis=0) for h in H]
    s_ctx = [lax.dot_general(q[h], kx_ref[0, h], nt, preferred_element_type=F32) for h in H]
    m = [jnp.maximum(jnp.max(s_ctx[h], axis=-1, keepdims=True), sink[h]) for h in H]
    if local:
        k_loc = [jnp.concatenate([kp_ref[0, h], kc_ref[0, h], kn_ref[0, h]], axis=0) for h in H]
        v_loc = [jnp.concatenate([vp_ref[0, h], vc_ref[0, h], vn_ref[0, h]], axis=0) for h in H]
        s_loc = [jnp.where(ok, lax.dot_general(q[h], k_loc[h], nt, preferred_element_type=F32), NEG_INF)
                 for h in H]
        m = [jnp.maximum(m[h], jnp.max(s_loc[h], axis=-1, keepdims=True)) for h in H]
    o = [jnp.dot(jnp.exp(s_ctx[h] - m[h]).astype(BF16), vx_ref[0, h], preferred_element_type=F32)
         for h in H]
    if local:
        o = [o[h] + jnp.dot(jnp.exp(s_loc[h] - m[h]).astype(BF16), v_loc[h], preferred_element_type=F32)
             for h in H]
    den = [o[h][:, HEAD:HEAD + 1] + jnp.exp(sink[h] - m[h]) for h in H]
    o = [o[h][:, :HEAD] * (1.0 / den[h]) for h in H]
    for h in H:
        for g in range(G):
            hh = h * G + g
            o_ref[0, :, hh * HEAD:(hh + 1) * HEAD] = o[h][g * AT_BLOCK:(g + 1) * AT_BLOCK].astype(o_ref.dtype)


def _attn(q, k, v, sink, *, Lc, local):
    B, n_q, T, _ = q.shape
    n_kv = k.shape[1]
    cb = Lc // AT_BLOCK
    nb = (T - Lc) // AT_BLOCK if local else cb
    q_off = cb if local else 0
    last = T // AT_BLOCK - 1
    qspec = pl.BlockSpec((1, n_q, AT_BLOCK, HEAD), lambda b, n: (b, 0, n + q_off, 0))
    xspec = lambda a: pl.BlockSpec((1, n_kv, Lc, a.shape[3]), lambda b, n: (b, 0, 0, 0))
    sspec = pl.BlockSpec((n_q, LANES), lambda b, n: (0, 0))
    sink_b = jnp.broadcast_to(sink.astype(F32)[:, None], (n_q, LANES))
    if local:
        blk = lambda a, f: pl.BlockSpec((1, n_kv, AT_BLOCK, a.shape[3]), lambda b, n: (b, 0, f(n), 0))
        prev = lambda n: jnp.maximum(n + cb - 1, cb)
        cur = lambda n: n + cb
        nxt = lambda n: jnp.minimum(n + cb + 1, last)
        in_specs = ([qspec] + [blk(k, f) for f in (prev, cur, nxt)] + [blk(v, f) for f in (prev, cur, nxt)]
                    + [xspec(k), xspec(v), sspec])
        args = (q, k, k, k, v, v, v, k, v, sink_b)
    else:
        in_specs = [qspec, xspec(k), xspec(v), sspec]
        args = (q, k, v, sink_b)
    return pl.pallas_call(
        functools.partial(_attn_kernel, n_q=n_q, n_kv=n_kv, local=local, n_blocks=nb),
        grid=(B, nb),
        in_specs=in_specs,
        out_specs=pl.BlockSpec((1, AT_BLOCK, n_q * HEAD), lambda b, n: (b, n, 0)),
        out_shape=jax.ShapeDtypeStruct((B, nb * AT_BLOCK, n_q * HEAD), BF16),
        compiler_params=_params("parallel", "parallel"),
        name="attn_local" if local else "attn_ctx",
    )(*args)


def _attention(h, p, *, Lc):
    B, T, D = h.shape
    n_q = D // HEAD
    cos, sin = _rope_tables(T, Lc)
    q, k, v = _qkv_rope(h, p["w_qkv"], p["index"], p["b_qkv"], cos, sin, n_q=n_q, n_kv=AT_KV_HEADS)
    o_ctx = _attn(q, k, v, p["sink"], Lc=Lc, local=False)
    o_lat = _attn(q, k, v, p["sink"], Lc=Lc, local=True)
    return jnp.concatenate([o_ctx, o_lat], axis=1)


MOE_TILE = 1024
ROUTE_G1, ROUTE_G2, ROUTE_I1, ROUTE_I2 = 0, 1, 2, 3


def _gates_kernel(l_ref, route_ref, sel_ref, *, n_experts):
    l = l_ref[...]
    lane = lax.broadcasted_iota(jnp.int32, l.shape, 1)
    l = jnp.where(lane < n_experts, l, -jnp.inf)
    m1 = jnp.max(l, axis=-1, keepdims=True)
    i1 = jnp.min(jnp.where(l == m1, lane, LANES), axis=-1, keepdims=True)
    l2 = jnp.where(lane == i1, -jnp.inf, l)
    m2 = jnp.max(l2, axis=-1, keepdims=True)
    i2 = jnp.min(jnp.where(l2 == m2, lane, LANES), axis=-1, keepdims=True)
    e2 = jnp.exp(m2 - m1)
    den = 1.0 + e2
    route_ref[...] = jnp.where(lane == ROUTE_G1, 1.0 / den,
                               jnp.where(lane == ROUTE_G2, e2 / den,
                                         jnp.where(lane == ROUTE_I1, i1.astype(F32),
                                                   jnp.where(lane == ROUTE_I2, i2.astype(F32), 0.0))))
    sel_ref[...] = jnp.where((lane == i1) | (lane == i2), 1.0, 0.0).astype(sel_ref.dtype)


def _gates(logits, n_experts):
    M = logits.shape[0]
    tm = _tile(M, MM_ROWS, SUBLANES)
    blk = pl.BlockSpec((tm, LANES), lambda i: (i, 0))
    return pl.pallas_call(
        functools.partial(_gates_kernel, n_experts=n_experts),
        grid=(M // tm,),
        in_specs=[blk],
        out_specs=[blk, blk],
        out_shape=[jax.ShapeDtypeStruct((M, LANES), F32), jax.ShapeDtypeStruct((M, LANES), BF16)],
        compiler_params=_params("parallel"),
        name="moe_gates",
    )(logits)


def _rank_kernel(sel_ref, rank_ref, cnt_ref, carry_ref):
    @pl.when(pl.program_id(0) == 0)
    def _():
        carry_ref[...] = jnp.zeros_like(carry_ref)

    s = sel_ref[...]
    n = s.shape[0]
    earlier = (lax.broadcasted_iota(jnp.int32, (n, n), 1) < lax.broadcasted_iota(jnp.int32, (n, n), 0))
    within = jnp.dot(jnp.where(earlier, 1.0, 0.0).astype(BF16), s, preferred_element_type=F32)
    rank_ref[...] = within + carry_ref[...]
    carry_ref[...] += jnp.sum(s.astype(F32), axis=0, keepdims=True)
    cnt_ref[...] = carry_ref[...]


def _rank(sel):
    M = sel.shape[0]
    tr = _tile(M, RANK_ROWS, SUBLANES)
    return pl.pallas_call(
        _rank_kernel,
        grid=(M // tr,),
        in_specs=[pl.BlockSpec((tr, LANES), lambda i: (i, 0))],
        out_specs=[pl.BlockSpec((tr, LANES), lambda i: (i, 0)), pl.BlockSpec((1, LANES), lambda i: (0, 0))],
        out_shape=[jax.ShapeDtypeStruct((M, LANES), F32), jax.ShapeDtypeStruct((1, LANES), F32)],
        scratch_shapes=[pltpu.VMEM((1, LANES), F32)],
        compiler_params=_params("arbitrary"),
        name="moe_rank",
    )(sel)


def _pos_kernel(route_ref, rank_ref, offs_ref, pos_ref):
    lane = lax.broadcasted_iota(jnp.int32, rank_ref.shape, 1)
    lane_f = lane.astype(F32)
    tot = rank_ref[...] + offs_ref[...]
    route = route_ref[...]
    p1 = jnp.sum(jnp.where(lane_f == route[:, ROUTE_I1:ROUTE_I1 + 1], tot, 0.0), axis=-1, keepdims=True)
    p2 = jnp.sum(jnp.where(lane_f == route[:, ROUTE_I2:ROUTE_I2 + 1], tot, 0.0), axis=-1, keepdims=True)
    pos_ref[...] = jnp.where(lane == 0, p1, jnp.where(lane == 1, p2, 0.0)).astype(jnp.int32)


def _positions(route, rank, offs):
    M = route.shape[0]
    tm = _tile(M, MM_ROWS, SUBLANES)
    blk = pl.BlockSpec((tm, LANES), lambda i: (i, 0))
    return pl.pallas_call(
        _pos_kernel,
        grid=(M // tm,),
        in_specs=[blk, blk, pl.BlockSpec((1, LANES), lambda i: (0, 0))],
        out_specs=blk,
        out_shape=jax.ShapeDtypeStruct((M, LANES), jnp.int32),
        compiler_params=_params("parallel"),
        name="moe_pos",
    )(route, rank, offs)


def _row_copy(src, dst, sem):
    return pltpu.make_async_copy(src, dst, sem)


def _dispatch_kernel(p1_ref, p2_ref, h_ref, xs_in_ref, xs_ref, sem):
    del xs_in_ref
    tt = h_ref.shape[0]
    base = pl.program_id(0) * tt

    def start(r, carry):
        row = h_ref.at[pl.ds(r, 1)]
        _row_copy(row, xs_ref.at[pl.ds(p1_ref[base + r], 1)], sem).start()
        _row_copy(row, xs_ref.at[pl.ds(p2_ref[base + r], 1)], sem).start()
        return carry

    lax.fori_loop(0, tt, start, 0, unroll=8)
    for _ in range(2):
        _row_copy(h_ref, xs_ref.at[pl.ds(0, tt)], sem).wait()


def _dispatch(h, p1, p2, n_rows):
    M, D = h.shape
    tt = _tile(M, ROW_TILE, SUBLANES)
    return pl.pallas_call(
        _dispatch_kernel,
        grid_spec=pltpu.PrefetchScalarGridSpec(
            num_scalar_prefetch=2,
            grid=(M // tt,),
            in_specs=[pl.BlockSpec((tt, D), lambda i, p1, p2: (i, 0)),
                      pl.BlockSpec(memory_space=pl.ANY)],
            out_specs=pl.BlockSpec(memory_space=pl.ANY),
            scratch_shapes=[pltpu.SemaphoreType.DMA(())]),
        out_shape=jax.ShapeDtypeStruct((n_rows, D), h.dtype),
        input_output_aliases={3: 0},
        compiler_params=_params("arbitrary"),
        name="moe_dispatch",
    )(p1, p2, h, jnp.zeros((n_rows, D), h.dtype))


def _gmm_kernel(te_ref, nv_ref, x_ref, *refs, swiglu):
    o_ref = refs[-1]

    @pl.when(pl.program_id(0) < nv_ref[0])
    def _():
        x = x_ref[...].astype(BF16)
        acc = jnp.dot(x, refs[0][0, 0], preferred_element_type=F32)
        if swiglu:
            acc = acc * jax.nn.sigmoid(acc) * jnp.dot(x, refs[1][0, 0], preferred_element_type=F32)
        o_ref[...] = acc.astype(o_ref.dtype)

    @pl.when(pl.program_id(0) >= nv_ref[0])
    def _():
        o_ref[...] = jnp.zeros_like(o_ref)


def _gmm(x, ws, layer, tile_expert, n_valid, *, out_dtype, tn=MM_COLS):
    K = x.shape[1]
    N = ws[0].shape[3]
    tm = MOE_TILE
    P = tile_expert.shape[0] * tm
    tn = _tile(N, tn)
    wspec = pl.BlockSpec((1, 1, K, tn), lambda i, j, te, nv: (layer, te[i], 0, j))
    return pl.pallas_call(
        functools.partial(_gmm_kernel, swiglu=len(ws) == 2),
        grid_spec=pltpu.PrefetchScalarGridSpec(
            num_scalar_prefetch=2,
            grid=(P // tm, N // tn),
            in_specs=[pl.BlockSpec((tm, K), lambda i, j, te, nv: (jnp.minimum(i, nv[0] - 1), 0))]
            + [wspec] * len(ws),
            out_specs=pl.BlockSpec((tm, tn), lambda i, j, te, nv: (i, j))),
        out_shape=jax.ShapeDtypeStruct((P, N), out_dtype),
        compiler_params=_params("parallel", "parallel"),
        name="moe_gmm",
    )(tile_expert, n_valid, x, *ws)


def _combine_ln_kernel(p1_ref, p2_ref, route_ref, ys_ref, x_ref, mod_ref, modh_ref, w_ref, b_ref, *rest,
                       alpha, gate, sh, sc, has_h):
    a_ref, b2_ref, sem = rest[-3:]
    xo_ref = rest[0]
    tt = xo_ref.shape[1]
    base = (pl.program_id(0) * pl.num_programs(1) + pl.program_id(1)) * tt

    def start(r, carry):
        _row_copy(ys_ref.at[pl.ds(p1_ref[base + r], 1)], a_ref.at[pl.ds(r, 1)], sem).start()
        _row_copy(ys_ref.at[pl.ds(p2_ref[base + r], 1)], b2_ref.at[pl.ds(r, 1)], sem).start()
        return carry

    lax.fori_loop(0, tt, start, 0, unroll=8)
    _row_copy(ys_ref.at[pl.ds(0, tt)], a_ref, sem).wait()
    _row_copy(ys_ref.at[pl.ds(0, tt)], b2_ref, sem).wait()
    route = route_ref[0]
    f = route[:, ROUTE_G1:ROUTE_G1 + 1] * a_ref[...] + route[:, ROUTE_G2:ROUTE_G2 + 1] * b2_ref[...]
    xn = _post_norm(x_ref[0], f, mod_ref, w_ref, b_ref, alpha=alpha, gate=gate)
    xo_ref[0] = xn
    if has_h:
        h = xn * (1.0 + modh_ref[0, sc:sc + 1, :]) + modh_ref[0, sh:sh + 1, :]
        rest[1][0] = h.astype(rest[1].dtype)


def _combine_ln(ys, route, p1, p2, x, mod, ln_w, ln_b, *, alpha, gate, h_mod=None, mod_h=None,
                h_dtype=BF16, ctx_tiles, ctx_row):
    B, T, D = x.shape
    tt = ROW_TILE
    nt = T // tt
    smap = lambda f: (lambda b, t, p1, p2: f(b, t))
    row = pl.BlockSpec((1, tt, D), smap(lambda b, t: (b, t, 0)))
    vec = pl.BlockSpec((1, D), smap(lambda b, t: (0, 0)))
    mspec = pl.BlockSpec((1, N_MOD, D), smap(_mod_index(ctx_tiles, ctx_row, 0)))
    out_specs = [row]
    out_shape = [jax.ShapeDtypeStruct((B, T, D), F32)]
    sh = sc = 0
    if h_mod is not None:
        sh, sc = h_mod
        out_specs.append(row)
        out_shape.append(jax.ShapeDtypeStruct((B, T, D), h_dtype))
    return pl.pallas_call(
        functools.partial(_combine_ln_kernel, alpha=alpha, gate=gate, sh=sh, sc=sc,
                          has_h=h_mod is not None),
        grid_spec=pltpu.PrefetchScalarGridSpec(
            num_scalar_prefetch=2,
            grid=(B, nt),
            in_specs=[pl.BlockSpec((1, tt, LANES), smap(lambda b, t: (b, t, 0))),
                      pl.BlockSpec(memory_space=pl.ANY), row, mspec, mspec, vec, vec],
            out_specs=out_specs,
            scratch_shapes=[pltpu.VMEM((tt, D), F32), pltpu.VMEM((tt, D), F32),
                            pltpu.SemaphoreType.DMA(())]),
        out_shape=out_shape,
        compiler_params=_params("arbitrary", "arbitrary"),
        name="moe_combine_ln",
    )(p1, p2, route.reshape(B, T, LANES), ys, x, mod, mod if mod_h is None else mod_h,
      ln_w.reshape(1, D), ln_b.reshape(1, D))


def _moe(h, logits, w1, w3, w2, layer):
    M, D = h.shape
    E = w1.shape[1]
    tm = MOE_TILE
    route, sel = _gates(logits, E)
    rank, cnt = _rank(sel)
    counts = cnt[0, :E].astype(jnp.int32)
    padded = (counts + tm - 1) // tm * tm
    ends = jnp.cumsum(padded)
    n_tiles = (TOP_K * M) // tm + E
    tile_expert = jnp.minimum(
        jnp.searchsorted(ends, jnp.arange(n_tiles, dtype=jnp.int32) * tm, side="right"), E - 1
    ).astype(jnp.int32)
    n_valid = (ends[-1:] // tm).astype(jnp.int32)
    offs = jnp.zeros((1, LANES), F32).at[0, :E].set((ends - padded).astype(F32))
    pos = _positions(route, rank, offs)
    p1, p2 = pos[:, 0], pos[:, 1]
    xs = _dispatch(h, p1, p2, n_tiles * tm)
    mid = _gmm(xs, [w1, w3], layer, tile_expert, n_valid, out_dtype=BF16)
    ys = _gmm(mid, [w2], layer, tile_expert, n_valid, out_dtype=F32)
    return ys, route, p1, p2


def kernel(x, c, ctx, c_ctx, ada_w, ada_b, ln_w, ln_b, hy_w_in, hy_b_in, hy_conv_w, hy_conv_b, hy_f_w1, hy_f_b1, hy_f_w2, hy_f_b2, hy_f_w3, hy_f_freq, hy_skip, hy_w_out, hy_b_out, rw_mu, rw_w_rkv, rw_w_o, rw_w0, rw_w1, rw_w2, rw_a0, rw_a1, rw_a2, rw_g1, rw_g2, rw_k_k, rw_k_a, rw_r_k, rw_gn_w, rw_gn_b, at_w_qkv, at_b_qkv, at_w_o, at_b_o, at_sink, ff_w1, ff_w3, ff_w2, moe_router, moe_w1, moe_w3, moe_w2):
    B, L, D = x.shape
    Lc = ctx.shape[1]
    depth = ada_w.shape[0]
    alpha = (2 * depth) ** 0.25
    assert Lc % ROW_TILE == 0 and L % ROW_TILE == 0 and D % (RW_GROUP * HEAD) == 0
    bf = lambda t: t.astype(BF16)
    w_bf = dict(hy_out=bf(hy_w_out), rw_o=bf(rw_w_o), at_qkv=bf(at_w_qkv), at_o=bf(at_w_o),
                ff1=bf(ff_w1), ff3=bf(ff_w3), ff2=bf(ff_w2),
                moe1=bf(moe_w1), moe3=bf(moe_w3), moe2=bf(moe_w2))

    ctx_row = B
    rows = -(-(B + 1) // 8) * 8
    cond = jnp.zeros((rows, D), F32).at[:B].set(c).at[B].set(c_ctx)
    mods = _ada(cond, ada_w, ada_b).reshape(depth, rows, N_MOD, D)

    tables = {}
    if depth > 0:
        tables[L] = _dft_tables(L)
        if depth > 1:
            tables[Lc] = _dft_tables(Lc)

    ctx_tiles = Lc // ROW_TILE
    xs, h = _stream(ctx, x, mods[0], sh=0, sc=1, ctx_row=ctx_row, out_dtype=BF16)
    for i in range(depth):
        last = i == depth - 1
        kind = i % 3
        j = i // 3
        mod = mods[i]
        has_ctx = xs.shape[1] != L
        ct = ctx_tiles if has_ctx else 0
        T = xs.shape[1]
        if kind == 0:
            p = dict(w_in=bf(hy_w_in[j]), b_in=hy_b_in[j], conv_w=hy_conv_w[j], conv_b=hy_conv_b[j],
                     filter=(hy_f_w1[j], hy_f_b1[j], hy_f_w2[j], hy_f_b2[j], hy_f_w3[j], hy_f_freq[j]),
                     skip=hy_skip[j])
            y = _hyena(h, p, tables, Lc=Lc, has_ctx=has_ctx)
            proj = (w_bf["hy_out"], j, hy_b_out[j])
        elif kind == 1:
            assert has_ctx and not last
            p = dict(mu=rw_mu[j], w_rkv=bf(rw_w_rkv[j]), w0=rw_w0[j], w1=rw_w1[j],
                     w2=rw_w2[j], a0=rw_a0[j], a1=rw_a1[j], a2=rw_a2[j], g1=rw_g1[j], g2=rw_g2[j],
                     k_k=rw_k_k[j], k_a=rw_k_a[j], r_k=rw_r_k[j], gn_w=rw_gn_w[j], gn_b=rw_gn_b[j])
            y = _rwkv(h, p, Lc=Lc)
            proj = (w_bf["rw_o"], j, None)
        else:
            assert has_ctx and not last
            p = dict(w_qkv=w_bf["at_qkv"], index=j, b_qkv=at_b_qkv[j], sink=at_sink[j])
            y = _attention(h, p, Lc=Lc)
            proj = (w_bf["at_o"], j, at_b_o[j])

        drop1 = ct if last else 0
        fj = i // 2
        moe = i % 2 == 1
        if moe:
            pw, pi, pb = proj
            y = _mm(y.reshape(B * T, D), pw, wi=pi, bias=pb).reshape(B, T, D)
            proj = None
        res = _ln(xs, y, mod, ln_w[i, 0], ln_b[i, 0], alpha=alpha, gate=2, proj=proj, h_mod=(3, 4),
                  h_dtype=F32 if moe else BF16, router=moe_router[fj] if moe else None,
                  ctx_tiles=ct, ctx_row=ctx_row, drop_tiles=drop1)
        xs, h2 = res[0], res[1]
        ct = ct - drop1
        T = xs.shape[1]
        M = B * T
        nxt_last = i + 1 == depth - 1
        nkind = (i + 1) % 3
        drop2 = ct if (not last and nxt_last and nkind == 0) else 0
        nxt = {} if last else dict(h_mod=(0, 1), mod_h=mods[i + 1], h_dtype=F32 if nkind == 1 else BF16)
        if moe:
            routed = _moe(h2.reshape(M, D), res[2].reshape(M, LANES), w_bf["moe1"], w_bf["moe3"],
                          w_bf["moe2"], fj)
            res = _combine_ln(*routed, xs, mod, ln_w[i, 1], ln_b[i, 1], alpha=alpha, gate=5,
                              ctx_tiles=ct, ctx_row=ctx_row, **nxt)
            if drop2:
                res = [lax.slice_in_dim(t, drop2 * ROW_TILE, T, axis=1) for t in res]
        else:
            f = _mm(h2.reshape(M, D), w_bf["ff1"], w3=w_bf["ff3"], wi=fj, out_dtype=BF16).reshape(B, T, -1)
            res = _ln(xs, f, mod, ln_w[i, 1], ln_b[i, 1], alpha=alpha, gate=5, proj=(w_bf["ff2"], fj, None),
                      ctx_tiles=ct, ctx_row=ctx_row, drop_tiles=drop2, **nxt)
        xs = res[0]
        if not last:
            h = res[1]
    return xs
```

```python
import functools
import math

import jax
import jax.numpy as jnp
from jax import lax
from jax.experimental import pallas as pl
from jax.experimental.pallas import tpu as pltpu

F32 = jnp.float32
BF16 = jnp.bfloat16
HIGHEST = lax.Precision.HIGHEST

VMEM_LIMIT_BYTES = 56 * 1024 * 1024
LANES = 128
SUBLANES = 8
BF16_ROWS = 16
ROW_TILE = 256
MM_ROWS, MM_COLS = 1024, 512
WIDE_COLS = 1024
HY_IN_ROWS = 1280
RANK_ROWS = 512

LN_EPS = 1e-5
N_MOD = 6
HEAD = 64
RW_GN_EPS = 64e-5
RW_CHUNK = 64
RW_GROUP = 4
AT_KV_HEADS = 4
AT_WINDOW = 128
AT_BLOCK = 128
GRID_W = 64
ROPE_BASE = 10000.0
NEG_INF = -1e30
TOP_K = 2
HY_BANDS = 16
HY_EMB = 2 * HY_BANDS + 1
HY_MIN_DECAY = -math.log(1e-2) / 1.5
HY_MAX_DECAY = -math.log(1e-2) / 0.3


def _params(*sem):
    return pltpu.CompilerParams(dimension_semantics=sem, vmem_limit_bytes=VMEM_LIMIT_BYTES)


def _tile(n, pref, mult=LANES):
    if n <= pref:
        return n
    t = (pref // mult) * mult
    while t >= mult:
        if n % t == 0:
            return t
        t -= mult
    return n


def _mm_kernel(*refs, n_w, has_bias, act):
    x = refs[0][...].astype(BF16)
    o_ref = refs[-1]
    acc = jnp.dot(x, refs[1][0].astype(BF16), preferred_element_type=F32)
    if has_bias:
        acc = acc + refs[1 + n_w][...]
    if act == "swiglu":
        acc3 = jnp.dot(x, refs[2][0].astype(BF16), preferred_element_type=F32)
        acc = acc * jax.nn.sigmoid(acc) * acc3
    o_ref[...] = acc.astype(o_ref.dtype)


def _mm(x, w, *, w3=None, wi=0, bias=None, out_dtype=F32, tm=MM_ROWS, tn=MM_COLS):
    M, K = x.shape
    N = w.shape[2]
    tm = _tile(M, tm, SUBLANES)
    tn = _tile(N, tn)
    ws = [w] if w3 is None else [w, w3]
    in_specs = [pl.BlockSpec((tm, K), lambda i, j: (i, 0))]
    in_specs += [pl.BlockSpec((1, K, tn), lambda i, j: (wi, 0, j)) for _ in ws]
    args = [x] + ws
    if bias is not None:
        in_specs.append(pl.BlockSpec((1, tn), lambda i, j: (0, j)))
        args.append(bias.reshape(1, N).astype(F32))
    return pl.pallas_call(
        functools.partial(_mm_kernel, n_w=len(ws), has_bias=bias is not None,
                          act="swiglu" if w3 is not None else None),
        grid=(M // tm, N // tn),
        in_specs=in_specs,
        out_specs=pl.BlockSpec((tm, tn), lambda i, j: (i, j)),
        out_shape=jax.ShapeDtypeStruct((M, N), out_dtype),
        compiler_params=_params("parallel", "parallel"),
        name="mm_swiglu" if w3 is not None else "mm",
    )(*args)


def _ada_kernel(c_ref, w_ref, b_ref, o_ref):
    c = c_ref[...]
    s = (c * jax.nn.sigmoid(c)).astype(BF16)
    o_ref[0] = jnp.dot(s, w_ref[0].astype(BF16), preferred_element_type=F32) + b_ref[0]


def _ada(cond, ada_w, ada_b):
    depth, D, N = ada_w.shape
    R = cond.shape[0]
    tn = _tile(N, WIDE_COLS)
    return pl.pallas_call(
        _ada_kernel,
        grid=(depth, N // tn),
        in_specs=[pl.BlockSpec((R, D), lambda i, j: (0, 0)),
                  pl.BlockSpec((1, D, tn), lambda i, j: (i, 0, j)),
                  pl.BlockSpec((1, 1, tn), lambda i, j: (i, 0, j))],
        out_specs=pl.BlockSpec((1, R, tn), lambda i, j: (i, 0, j)),
        out_shape=jax.ShapeDtypeStruct((depth, R, N), F32),
        compiler_params=_params("parallel", "parallel"),
        name="ada",
    )(cond, ada_w, ada_b.reshape(depth, 1, N))


def _mod_index(ctx_tiles, ctx_row, off):
    def index(b, t):
        return (jnp.where(t + off < ctx_tiles, ctx_row, b), 0, 0)
    return index


def _stream_kernel(ctx_ref, x_ref, mod_ref, xs_ref, h_ref, *, sh, sc, ctx_tiles):
    x = jnp.where(pl.program_id(1) < ctx_tiles, ctx_ref[0], x_ref[0])
    xs_ref[0] = x
    h_ref[0] = (x * (1.0 + mod_ref[0, sc:sc + 1, :]) + mod_ref[0, sh:sh + 1, :]).astype(h_ref.dtype)


def _stream(ctx, x, mod, *, sh, sc, ctx_row, out_dtype):
    B, L, D = x.shape
    ctx_tiles = ctx.shape[1] // ROW_TILE
    T = ctx.shape[1] + L
    blk = pl.BlockSpec((1, ROW_TILE, D), lambda b, t: (b, t, 0))
    return pl.pallas_call(
        functools.partial(_stream_kernel, sh=sh, sc=sc, ctx_tiles=ctx_tiles),
        grid=(B, T // ROW_TILE),
        in_specs=[pl.BlockSpec((1, ROW_TILE, D), lambda b, t: (b, jnp.minimum(t, ctx_tiles - 1), 0)),
                  pl.BlockSpec((1, ROW_TILE, D), lambda b, t: (b, jnp.maximum(t - ctx_tiles, 0), 0)),
                  pl.BlockSpec((1, N_MOD, D), _mod_index(ctx_tiles, ctx_row, 0))],
        out_specs=[blk, blk],
        out_shape=[jax.ShapeDtypeStruct((B, T, D), F32), jax.ShapeDtypeStruct((B, T, D), out_dtype)],
        compiler_params=_params("parallel", "parallel"),
        name="stream",
    )(ctx, x, mod)


def _post_norm(x, y, mod_ref, w_ref, b_ref, *, alpha, gate):
    z = alpha * x + mod_ref[0, gate:gate + 1, :] * y
    mu = jnp.mean(z, axis=-1, keepdims=True)
    d = z - mu
    var = jnp.mean(d * d, axis=-1, keepdims=True)
    return d * lax.rsqrt(var + LN_EPS) * w_ref[...] + b_ref[...]


def _ln_kernel(*refs, alpha, gate, sh, sc, has_h, has_router, has_proj):
    x_ref, y_ref, mod_ref, modh_ref, w_ref, b_ref = refs[:6]
    pos = 6
    if has_proj:
        pw_ref, pb_ref = refs[pos:pos + 2]
        pos += 2
    router_ref = None
    if has_router:
        router_ref = refs[pos]
        pos += 1
    xo_ref = refs[pos]
    pos += 1
    if has_proj:
        y = jnp.dot(y_ref[0], pw_ref[0], preferred_element_type=F32) + pb_ref[...]
    else:
        y = y_ref[0].astype(F32)
    xn = _post_norm(x_ref[0], y, mod_ref, w_ref, b_ref, alpha=alpha, gate=gate)
    xo_ref[0] = xn
    if has_h:
        h = xn * (1.0 + modh_ref[0, sc:sc + 1, :]) + modh_ref[0, sh:sh + 1, :]
        h_ref = refs[pos]
        pos += 1
        h_ref[0] = h.astype(h_ref.dtype)
        if has_router:
            refs[pos][0] = jnp.dot(h, router_ref[...], precision=HIGHEST, preferred_element_type=F32)


def _ln(x, y, mod, ln_w, ln_b, *, alpha, gate, proj=None, h_mod=None, mod_h=None, h_dtype=BF16,
        router=None, ctx_tiles, ctx_row, drop_tiles=0):
    B, T, D = x.shape
    K = y.shape[2]
    nt = T // ROW_TILE - drop_tiles
    To = nt * ROW_TILE
    off = drop_tiles
    in_specs = [pl.BlockSpec((1, ROW_TILE, D), lambda b, t: (b, t + off, 0)),
                pl.BlockSpec((1, ROW_TILE, K), lambda b, t: (b, t + off, 0)),
                pl.BlockSpec((1, N_MOD, D), _mod_index(ctx_tiles, ctx_row, off)),
                pl.BlockSpec((1, N_MOD, D), _mod_index(ctx_tiles, ctx_row, off)),
                pl.BlockSpec((1, D), lambda b, t: (0, 0)),
                pl.BlockSpec((1, D), lambda b, t: (0, 0))]
    args = [x, y, mod, mod if mod_h is None else mod_h, ln_w.reshape(1, D), ln_b.reshape(1, D)]
    if proj is not None:
        pw, pi, pb = proj
        in_specs += [pl.BlockSpec((1, K, D), lambda b, t: (pi, 0, 0), pipeline_mode=pl.Buffered(1)),
                     pl.BlockSpec((1, D), lambda b, t: (0, 0))]
        args += [pw, jnp.zeros((1, D), F32) if pb is None else pb.reshape(1, D).astype(F32)]
    out_specs = [pl.BlockSpec((1, ROW_TILE, D), lambda b, t: (b, t, 0))]
    out_shape = [jax.ShapeDtypeStruct((B, To, D), F32)]
    sh = sc = 0
    if h_mod is not None:
        sh, sc = h_mod
        out_specs.append(pl.BlockSpec((1, ROW_TILE, D), lambda b, t: (b, t, 0)))
        out_shape.append(jax.ShapeDtypeStruct((B, To, D), h_dtype))
    if router is not None:
        E = router.shape[1]
        router_p = jnp.zeros((D, LANES), F32).at[:, :E].set(router)
        in_specs.append(pl.BlockSpec((D, LANES), lambda b, t: (0, 0)))
        args.append(router_p)
        out_specs.append(pl.BlockSpec((1, ROW_TILE, LANES), lambda b, t: (b, t, 0)))
        out_shape.append(jax.ShapeDtypeStruct((B, To, LANES), F32))
    return pl.pallas_call(
        functools.partial(_ln_kernel, alpha=alpha, gate=gate, sh=sh, sc=sc, has_h=h_mod is not None,
                          has_router=router is not None, has_proj=proj is not None),
        grid=(B, nt),
        in_specs=in_specs,
        out_specs=out_specs,
        out_shape=out_shape,
        compiler_params=_params("parallel", "parallel"),
        name="ln_residual",
    )(*args)


def _halo_specs(T, C, col):
    S = SUBLANES
    r8 = ROW_TILE // S
    last8 = T // S - 1
    return [pl.BlockSpec((1, ROW_TILE, C), lambda b, t, j: (b, t, col(j))),
            pl.BlockSpec((1, S, C), lambda b, t, j: (b, jnp.maximum(t * r8 - 1, 0), col(j))),
            pl.BlockSpec((1, S, C), lambda b, t, j: (b, jnp.minimum(t * r8 + r8, last8), col(j)))]


def _neighbours(cur, prev8, next8, t, n_tiles, ctx_tiles):
    rows = lax.broadcasted_iota(jnp.int32, cur.shape, 0)
    has_prev = jnp.logical_and(t != 0, t != ctx_tiles)
    has_next = jnp.logical_and(t != n_tiles - 1, t != ctx_tiles - 1)
    top = jnp.where(has_prev, prev8[7:8, :], 0.0)
    bot = jnp.where(has_next, next8[0:1, :], 0.0)
    up = jnp.where(rows == 0, top, pltpu.roll(cur, 1, axis=0))
    dn = jnp.where(rows == cur.shape[0] - 1, bot, pltpu.roll(cur, cur.shape[0] - 1, axis=0))
    return up, dn


def _hy_in_kernel(x_ref, xp_ref, xn_ref, *refs, tm, T, Lc):
    ws, bs, cws, cbs = refs[0:3], refs[3:6], refs[6:9], refs[9:12]
    x0_ref, vv_ref, vvb_ref = refs[12:15]
    H = BF16_ROWS
    x = jnp.concatenate([xp_ref[0], x_ref[0], xn_ref[0]], axis=0)
    row = lax.broadcasted_iota(jnp.int32, (tm, 1), 0) + pl.program_id(1) * tm
    no_prev = (row == 0) | (row == Lc)
    no_next = (row == Lc - 1) | (row == T - 1)
    out = []
    for s in range(3):
        acc = jnp.dot(x, ws[s][...], preferred_element_type=F32) + bs[s][...]
        n = acc.shape[0]
        cur = acc[H:H + tm]
        up = jnp.where(no_prev, 0.0, pltpu.roll(acc, 1, axis=0)[H:H + tm])
        dn = jnp.where(no_next, 0.0, pltpu.roll(acc, n - 1, axis=0)[H:H + tm])
        cw = cws[s]
        out.append(up * cw[0:1, :] + cur * cw[1:2, :] + dn * cw[2:3, :] + cbs[s][...])
    x0_ref[0] = out[0]
    vv = out[1] * out[2]
    vv_ref[0] = vv
    vvb_ref[0] = vv.astype(BF16)


def _hy_in(h, w_in, b_in, conv_w, conv_b, *, Lc):
    B, T, K = h.shape
    D = w_in.shape[1] // 3
    H = BF16_ROWS
    tm = _tile(T, HY_IN_ROWS, 4 * BF16_ROWS)
    tc = _tile(D, MM_COLS)
    nj = D // tc
    r16 = tm // H
    last16 = T // H - 1
    sec = lambda shape: [pl.BlockSpec(shape, lambda b, t, j, s=s: (0, s * nj + j)) for s in range(3)]
    in_specs = [pl.BlockSpec((1, tm, K), lambda b, t, j: (b, t, 0)),
                pl.BlockSpec((1, H, K), lambda b, t, j: (b, jnp.maximum(t * r16 - 1, 0), 0)),
                pl.BlockSpec((1, H, K), lambda b, t, j: (b, jnp.minimum(t * r16 + r16, last16), 0))]
    in_specs += sec((K, tc)) + sec((1, tc)) + sec((3, tc)) + sec((1, tc))
    blk = pl.BlockSpec((1, tm, tc), lambda b, t, j: (b, t, j))
    b2 = b_in.reshape(1, 3 * D)
    cb2 = conv_b.reshape(1, 3 * D)
    return pl.pallas_call(
        functools.partial(_hy_in_kernel, tm=tm, T=T, Lc=Lc),
        grid=(B, T // tm, nj),
        in_specs=in_specs,
        out_specs=[blk, blk, blk],
        out_shape=[jax.ShapeDtypeStruct((B, T, D), F32), jax.ShapeDtypeStruct((B, T, D), F32),
                   jax.ShapeDtypeStruct((B, T, D), BF16)],
        compiler_params=_params("parallel", "parallel", "parallel"),
        name="hy_in",
    )(h, h, h, w_in, w_in, w_in, b2, b2, b2, conv_w, conv_w, conv_w, cb2, cb2, cb2)


def _hy_filter_kernel(w1_ref, b1_ref, w2_ref, b2_ref, w3_ref, fr_ref, h_ref, s_ref, *, L, D, tl):
    i = pl.program_id(0)
    row = (lax.broadcasted_iota(jnp.int32, (tl, LANES), 0) + i * tl).astype(F32)
    lane = lax.broadcasted_iota(jnp.int32, (tl, LANES), 1)
    band = jnp.where(lane <= HY_BANDS, lane - 1, lane - 1 - HY_BANDS).astype(F32)
    freq = 1e-4 + band * ((HY_BANDS - 1 - 1e-4) / (HY_BANDS - 1))
    ang = freq * (row * (2.0 * math.pi / L))
    z = jnp.where(lane == 0, row / (L - 1),
                  jnp.where(lane <= HY_BANDS, jnp.cos(ang),
                            jnp.where(lane < HY_EMB, -jnp.sin(ang), 0.0)))
    h = jnp.sin(fr_ref[0:1, :] * (jnp.dot(z, w1_ref[...], precision=HIGHEST,
                                           preferred_element_type=F32) + b1_ref[...]))
    h = jnp.sin(fr_ref[1:2, :] * (jnp.dot(h, w2_ref[...], precision=HIGHEST,
                                           preferred_element_type=F32) + b2_ref[...]))
    h = jnp.dot(h, w3_ref[...], precision=HIGHEST, preferred_element_type=F32)
    half = L // 2
    rowd = (lax.broadcasted_iota(jnp.int32, (tl, D), 0) + i * tl).astype(F32)
    dist = jnp.abs(rowd - half) / half
    chan = lax.broadcasted_iota(jnp.int32, (tl, D), 1).astype(F32)
    deltas = HY_MIN_DECAY + chan * ((HY_MAX_DECAY - HY_MIN_DECAY) / (D - 1))
    h = h * jnp.exp(-dist * deltas)
    h_ref[...] = h

    @pl.when(i == 0)
    def _():
        s_ref[...] = jnp.zeros_like(s_ref)

    s_ref[...] += jnp.sum(jnp.abs(h), axis=0, keepdims=True)


def _hy_filter(L, f_w1, f_b1, f_w2, f_b2, f_w3, f_freq):
    D = f_w3.shape[1]
    hid = f_w1.shape[1]
    w1 = jnp.zeros((LANES, LANES), F32).at[:HY_EMB, :hid].set(f_w1)
    b1 = jnp.zeros((1, LANES), F32).at[0, :hid].set(f_b1)
    w2 = jnp.zeros((LANES, LANES), F32).at[:hid, :hid].set(f_w2)
    b2 = jnp.zeros((1, LANES), F32).at[0, :hid].set(f_b2)
    w3 = jnp.zeros((LANES, D), F32).at[:hid].set(f_w3)
    fr = jnp.zeros((2, LANES), F32).at[:, :hid].set(f_freq)
    tl = _tile(L, ROW_TILE, SUBLANES)
    full = lambda shape: pl.BlockSpec(shape, lambda i: (0, 0))
    return pl.pallas_call(
        functools.partial(_hy_filter_kernel, L=L, D=D, tl=tl),
        grid=(L // tl,),
        in_specs=[full((LANES, LANES)), full((1, LANES)), full((LANES, LANES)), full((1, LANES)),
                  full((LANES, D)), full((2, LANES))],
        out_specs=[pl.BlockSpec((tl, D), lambda i: (i, 0)), full((1, D))],
        out_shape=[jax.ShapeDtypeStruct((L, D), F32), jax.ShapeDtypeStruct((1, D), F32)],
        compiler_params=_params("arbitrary"),
        name="hy_filter",
    )(w1, b1, w2, b2, w3, fr)


DFT_ROWS = 64


def _dft_tables_kernel(fa_ref, ia_ref, b_ref, fwd_ref, inv_ref, *, L):
    n = 2 * L
    i = pl.program_id(0)
    cb, sb = b_ref[0], b_ref[1]
    q = lax.broadcasted_iota(jnp.int32, (DFT_ROWS, L), 0)
    col = lax.broadcasted_iota(jnp.int32, (DFT_ROWS, L), 1)
    sign_c = jnp.where(col % 2 == 0, 1.0, -1.0)
    sign_r = jnp.where(q % 2 == 0, 1.0, -1.0)
    ca, sa = fa_ref[0, 0:1, :], fa_ref[0, 1:2, :]
    cos_f = ca * cb - sa * sb
    sin_f = sa * cb + ca * sb
    first_row = (q == 0) & (i == 0)
    fwd_ref[0] = cos_f.astype(fwd_ref.dtype)
    fwd_ref[1] = jnp.where(first_row, sign_c, -sin_f).astype(fwd_ref.dtype)
    ca, sa = ia_ref[0, 0:1, :], ia_ref[0, 1:2, :]
    cos_i = ca * cb - sa * sb
    sin_i = sa * cb + ca * sb
    inv_ref[:, :L] = jnp.where(col == 0, 1.0 / n, (2.0 / n) * cos_i).astype(inv_ref.dtype)
    inv_ref[:, L:] = jnp.where(col == 0, sign_r / n, (-2.0 / n) * sin_i).astype(inv_ref.dtype)


def _dft_tables(L):
    n = 2 * L
    R = DFT_ROWS
    assert L % (2 * R) == 0
    w = 2.0 * math.pi / n
    c = jnp.arange(L, dtype=jnp.int32)
    hi = jnp.arange(L // R, dtype=jnp.int32) * R

    def cos_sin(rows):
        ang = ((rows[:, None] * c[None, :]) % n).astype(F32) * w
        return jnp.stack([jnp.cos(ang), jnp.sin(ang)], axis=1)

    fa = cos_sin(hi)
    ia = cos_sin(hi + L // 2)
    b = jnp.swapaxes(cos_sin(jnp.arange(R, dtype=jnp.int32)), 0, 1)
    return pl.pallas_call(
        functools.partial(_dft_tables_kernel, L=L),
        grid=(L // R,),
        in_specs=[pl.BlockSpec((1, 2, L), lambda i: (i, 0, 0)),
                  pl.BlockSpec((1, 2, L), lambda i: (i, 0, 0)),
                  pl.BlockSpec((2, R, L), lambda i: (0, 0, 0))],
        out_specs=[pl.BlockSpec((2, R, L), lambda i: (0, i, 0)),
                   pl.BlockSpec((R, 2 * L), lambda i: (i, 0))],
        out_shape=[jax.ShapeDtypeStruct((2, L, L), BF16), jax.ShapeDtypeStruct((L, 2 * L), BF16)],
        compiler_params=_params("parallel"),
        name="dft_tables",
    )(fa, ia, b)


def _dft_fwd_kernel(a_ref, w_ref, *rest, mode):
    o_ref = rest[-1]
    tm = a_ref.shape[1]
    v = jnp.dot(a_ref[...].reshape(2 * tm, a_ref.shape[2]), w_ref[0], preferred_element_type=F32)
    vre, vim = v[:tm], v[tm:]
    if mode == "scale":
        inv = 1.0 / (rest[0][...] + 1e-6)
        o_ref[0, 0] = vre * inv
        o_ref[0, 1] = vim * inv
    else:
        hre = rest[0][0]
        him = rest[0][1]
        first = jnp.logical_and(pl.program_id(2) == 0,
                                lax.broadcasted_iota(jnp.int32, vre.shape, 0) == 0)
        zre = jnp.where(first, vre * hre, vre * hre - vim * him)
        zim = jnp.where(first, vim * him, vre * him + vim * hre)
        o_ref[0, 0] = zre.astype(o_ref.dtype)
        o_ref[0, 1] = zim.astype(o_ref.dtype)


def _dft_fwd(fwd, w, extra, *, mode, out_dtype):
    B, L, D = w.shape
    tm = _tile(L, ROW_TILE, SUBLANES)
    tn = _tile(D, WIDE_COLS)
    if mode == "scale":
        extra_spec = pl.BlockSpec((1, tn), lambda b, j, m: (0, j))
    else:
        extra_spec = pl.BlockSpec((2, tm, tn), lambda b, j, m: (0, m, j))
    return pl.pallas_call(
        functools.partial(_dft_fwd_kernel, mode=mode),
        grid=(B, D // tn, L // tm),
        in_specs=[pl.BlockSpec((2, tm, L), lambda b, j, m: (0, m, 0)),
                  pl.BlockSpec((1, L, tn), lambda b, j, m: (b, 0, j)),
                  extra_spec],
        out_specs=pl.BlockSpec((1, 2, tm, tn), lambda b, j, m: (b, 0, m, j)),
        out_shape=jax.ShapeDtypeStruct((B, 2, L, D), out_dtype),
        compiler_params=_params("parallel", "parallel", "parallel"),
        name="dft_fwd",
    )(fwd, w, extra)


def _dft_inv_kernel(b_ref, z_ref, x0_ref, vv_ref, skip_ref, u_ref):
    y = jnp.dot(b_ref[...], z_ref[0], preferred_element_type=F32)
    u_ref[0] = (x0_ref[0] * (y + vv_ref[0] * skip_ref[...])).astype(u_ref.dtype)


def _dft_inv(inv, z, x0, vv, skip, *, row_off):
    B, n, D = z.shape
    L = n // 2
    tm = _tile(L, ROW_TILE, SUBLANES)
    tn = _tile(D, WIDE_COLS)
    off = row_off // tm
    return pl.pallas_call(
        _dft_inv_kernel,
        grid=(B, D // tn, L // tm),
        in_specs=[pl.BlockSpec((tm, n), lambda b, j, m: (m, 0)),
                  pl.BlockSpec((1, n, tn), lambda b, j, m: (b, 0, j)),
                  pl.BlockSpec((1, tm, tn), lambda b, j, m: (b, m + off, j)),
                  pl.BlockSpec((1, tm, tn), lambda b, j, m: (b, m + off, j)),
                  pl.BlockSpec((1, tn), lambda b, j, m: (0, j))],
        out_specs=pl.BlockSpec((1, tm, tn), lambda b, j, m: (b, m, j)),
        out_shape=jax.ShapeDtypeStruct((B, L, D), BF16),
        compiler_params=_params("parallel", "parallel", "parallel"),
        name="dft_inv",
    )(inv, z, x0, vv, skip.reshape(1, D))


def _hyena(h, p, tables, *, Lc, has_ctx):
    B, T, D = h.shape
    x0, vv, vvb = _hy_in(h, p["w_in"], p["b_in"], p["conv_w"], p["conv_b"], Lc=Lc if has_ctx else -1)
    segs = [(Lc, T - Lc)] if has_ctx else [(0, T)]
    if has_ctx:
        segs = [(0, Lc)] + segs
    us = []
    for start, L in segs:
        fwd, inv = tables[L]
        filt, asum = _hy_filter(L, *p["filter"])
        hf = _dft_fwd(fwd, filt.astype(BF16)[None], asum, mode="scale", out_dtype=F32)[0]
        seg = vvb if (start == 0 and L == T) else lax.slice_in_dim(vvb, start, start + L, axis=1)
        z = _dft_fwd(fwd, seg, hf, mode="mul", out_dtype=BF16).reshape(B, 2 * L, D)
        us.append(_dft_inv(inv, z, x0, vv, p["skip"], row_off=start))
    return us[0] if len(us) == 1 else jnp.concatenate(us, axis=1)


MU_R, MU_W, MU_K, MU_V, MU_A, MU_G = range(6)


def _rw_mix(cur_ref, prev_ref, next_ref, mu_ref, which, *, n_tiles, ctx_tiles):
    cur = cur_ref[0]
    up, dn = _neighbours(cur, prev_ref[0], next_ref[0], pl.program_id(1), n_tiles, ctx_tiles)
    dx = 0.5 * (up + dn) - cur
    return [(cur + dx * mu_ref[j:j + 1, :]).astype(BF16) for j in which]


def _rw_rkvg_kernel(cur_ref, prev_ref, next_ref, mu_ref, wr_ref, wk_ref, wv_ref, g1_ref, g2_ref,
                    r_ref, k_ref, v_ref, g_ref, **tiles):
    xr, xk, xv, xg = _rw_mix(cur_ref, prev_ref, next_ref, mu_ref, (MU_R, MU_K, MU_V, MU_G), **tiles)
    r_ref[0] = jnp.dot(xr, wr_ref[...], preferred_element_type=F32)
    k_ref[0] = jnp.dot(xk, wk_ref[...], preferred_element_type=F32)
    v_ref[0] = jnp.dot(xv, wv_ref[...], preferred_element_type=F32)
    t = jax.nn.sigmoid(jnp.dot(xg, g1_ref[...], preferred_element_type=F32)).astype(BF16)
    g_ref[0] = jnp.dot(t, g2_ref[...], preferred_element_type=F32)


def _rw_decay_kernel(cur_ref, prev_ref, next_ref, mu_ref, w1_ref, w2_ref, w0_ref, a1_ref, a2_ref, a0_ref,
                     lwf_ref, af_ref, lwb_ref, ab_ref, **tiles):
    xw, xa = _rw_mix(cur_ref, prev_ref, next_ref, mu_ref, (MU_W, MU_A), **tiles)
    R = w2_ref.shape[1]
    tw = jnp.tanh(jnp.dot(xw, w1_ref[...], preferred_element_type=F32)).astype(BF16)
    ta = jnp.dot(xa, a1_ref[...], preferred_element_type=F32).astype(BF16)
    for d, (lw_ref, a_ref) in enumerate(((lwf_ref, af_ref), (lwb_ref, ab_ref))):
        wl = jnp.dot(tw[:, d * R:(d + 1) * R], w2_ref[d], preferred_element_type=F32) + w0_ref[d:d + 1, :]
        lw_ref[0] = -math.exp(-0.5) * jax.nn.sigmoid(wl)
        al = jnp.dot(ta[:, d * R:(d + 1) * R], a2_ref[d], preferred_element_type=F32) + a0_ref[d:d + 1, :]
        a_ref[0] = jax.nn.sigmoid(al)


def _pad_rank(first, second):
    _, D, r = first.shape
    R = -(-r // LANES) * LANES
    f = jnp.zeros((D, 2 * R), BF16)
    s = jnp.zeros((2, R, D), BF16)
    for d in range(2):
        f = f.at[:, d * R:d * R + r].set(first[d].astype(BF16))
        s = s.at[d, :r].set(second[d].astype(BF16))
    return f, s


def _rw_project(hs, p, *, ctx_tiles):
    B, T, D = hs.shape
    n_tiles = T // ROW_TILE
    tiles = dict(n_tiles=n_tiles, ctx_tiles=ctx_tiles)
    halo = _halo_specs(T, D, lambda j: 0)
    once = pl.Buffered(1)
    full = lambda a: pl.BlockSpec(a.shape, lambda b, t, j: (0,) * a.ndim, pipeline_mode=once)
    blk = pl.BlockSpec((1, ROW_TILE, D), lambda b, t, j: (b, t, 0))
    out = jax.ShapeDtypeStruct((B, T, D), F32)
    mu = p["mu"]
    wr, wk, wv = p["w_rkv"][0], p["w_rkv"][1], p["w_rkv"][2]
    g1, g2 = p["g1"].astype(BF16), p["g2"].astype(BF16)
    r, k, v, g = pl.pallas_call(
        functools.partial(_rw_rkvg_kernel, **tiles),
        grid=(B, n_tiles, 1),
        in_specs=halo + [full(a) for a in (mu, wr, wk, wv, g1, g2)],
        out_specs=[blk] * 4,
        out_shape=[out] * 4,
        compiler_params=_params("parallel", "parallel", "parallel"),
        name="rw_rkvg",
    )(hs, hs, hs, mu, wr, wk, wv, g1, g2)
    w1, w2 = _pad_rank(p["w1"], p["w2"])
    a1, a2 = _pad_rank(p["a1"], p["a2"])
    consts = (mu, w1, w2, p["w0"], a1, a2, p["a0"])
    lwf, af, lwb, ab = pl.pallas_call(
        functools.partial(_rw_decay_kernel, **tiles),
        grid=(B, n_tiles, 1),
        in_specs=halo + [full(a) for a in consts],
        out_specs=[blk] * 4,
        out_shape=[out] * 4,
        compiler_params=_params("parallel", "parallel", "parallel"),
        name="rw_decay",
    )(hs, hs, hs, *consts)
    return r, k, v, g, ((lwf, af), (lwb, ab))


def _block_diag(x, mask):
    return jnp.where(mask, jnp.concatenate([x] * RW_GROUP, axis=0), 0.0).astype(BF16)


def _rw_chunks(r, k, v, lw, a, k_k, k_a, state, same_head, incl, strict, reverse):
    C = RW_CHUNK
    W = RW_GROUP * HEAD
    G = range(len(r))
    nt = (((1,), (1,)), ((), ()))
    ones = jnp.where(same_head, 1.0, 0.0).astype(BF16)
    tri = jnp.where(incl[:, :C], 1.0, 0.0).astype(BF16)
    bd = lambda t: _block_diag(t, same_head)
    mm = lambda x, y: jnp.dot(x.astype(BF16), y, preferred_element_type=F32)

    def split(t):
        hi = t.astype(BF16)
        return hi, (t - hi.astype(F32)).astype(BF16)

    kkr = [k[g] * k_k[g] for g in G]
    sq = [split(kkr[g] * kkr[g]) for g in G]
    ng = len(r)
    sums = jnp.dot(jnp.concatenate([sq[g][0] for g in G] + [sq[g][1] for g in G], axis=0), ones,
                   preferred_element_type=F32)
    ss = [sums[g * C:(g + 1) * C] + sums[(ng + g) * C:(ng + g + 1) * C] for g in G]
    lws = [split(lw[g]) for g in G]
    lp = [jnp.dot(tri, lws[g][0], preferred_element_type=F32)
          + jnp.dot(tri, lws[g][1], preferred_element_type=F32) for g in G]
    kk = [kkr[g] / jnp.maximum(jnp.sqrt(ss[g]), 1e-12) for g in G]
    kd = [k[g] * (1.0 + (a[g] - 1.0) * k_a[g]) for g in G]
    lp_end = [lp[g][0:1, :] if reverse else lp[g][C - 1:C, :] for g in G]
    e_neg = [jnp.exp(-lp[g]) for g in G]
    lhs = [jnp.concatenate([-kk[g] * jnp.exp(lp[g] - lw[g]), r[g] * jnp.exp(lp[g])], axis=0).astype(BF16)
           for g in G]
    rhs = [jnp.concatenate([bd(kk[g] * a[g] * e_neg[g]), bd(kd[g] * e_neg[g])], axis=0) for g in G]
    cross = [lax.dot_general(lhs[g], rhs[g], nt, preferred_element_type=F32) for g in G]
    from_state = [lax.dot_general(lhs[g], state[g].astype(BF16), nt, preferred_element_type=F32)
                  for g in G]
    v_bd = [bd(v[g]) for g in G]
    p = [jnp.where(strict, cross[g][:C, :W], 0.0) for g in G]
    on_v = [mm(jnp.concatenate([jnp.where(strict, cross[g][:C, W:], 0.0),
                                jnp.where(incl, cross[g][C:, W:], 0.0)], axis=0), v_bd[g]) for g in G]
    x = [from_state[g][:C] + on_v[g][:C] for g in G]
    n = 1
    while n < C:
        x = [x[g] + mm(p[g], bd(x[g])) for g in G]
        n *= 2
        if n < C:
            p = [mm(p[g], bd(p[g])) for g in G]
    y = [from_state[g][C:] + mm(jnp.where(incl, cross[g][C:, :W], 0.0), bd(x[g])) + on_v[g][C:]
         for g in G]

    tail = [jnp.exp(lp_end[g] - lp[g]) for g in G]
    uv = [jnp.concatenate([x[g], v[g]], axis=0).astype(BF16) for g in G]
    bk = [jnp.concatenate([kk[g] * a[g] * tail[g], kd[g] * tail[g]], axis=0).astype(BF16) for g in G]
    upd = [lax.dot_general(uv[g], bk[g], (((0,), (0,)), ((), ())), preferred_element_type=F32)
           for g in G]
    new_state = [state[g] * jnp.exp(lp_end[g]) + jnp.where(same_head, upd[g], 0.0) for g in G]
    return y, new_state


RW_GROUPS_PER_STEP = 8
RW_CHUNKS_PER_STEP = 2


def _rw_scan_kernel(r_ref, k_ref, v_ref, lw_ref, a_ref, kk_ref, ka_ref, o_ref, state_ref, *, reverse):
    C = RW_CHUNK
    W = RW_GROUP * HEAD

    @pl.when(pl.program_id(2) == 0)
    def _():
        state_ref[...] = jnp.zeros_like(state_ref)

    ri = lax.broadcasted_iota(jnp.int32, (W, W), 0)
    ci = lax.broadcasted_iota(jnp.int32, (W, W), 1)
    same_head = (ri // HEAD) == (ci // HEAD)
    t_i = lax.broadcasted_iota(jnp.int32, (C, W), 0)
    s_i = lax.broadcasted_iota(jnp.int32, (C, W), 1) % C
    incl = (s_i >= t_i) if reverse else (s_i <= t_i)
    strict = (s_i > t_i) if reverse else (s_i < t_i)

    ng = state_ref.shape[0]
    sl = [slice(g * W, (g + 1) * W) for g in range(ng)]
    state = [state_ref[g] for g in range(ng)]
    order = range(RW_CHUNKS_PER_STEP)
    for c in (reversed(order) if reverse else order):
        rows = slice(c * C, (c + 1) * C)
        load = lambda ref: [ref[0, rows, s] for s in sl]
        y, state = _rw_chunks(load(r_ref), load(k_ref), load(v_ref), load(lw_ref), load(a_ref),
                              [kk_ref[:, s] for s in sl], [ka_ref[:, s] for s in sl],
                              state, same_head, incl, strict, reverse)
        for g in range(ng):
            o_ref[0, rows, sl[g]] = y[g]
    for g in range(ng):
        state_ref[g] = state[g]


def _rw_scan(r, k, v, lw, a, k_k, k_a, *, Lc, reverse):
    B, T, D = r.shape
    assert RW_CHUNK == HEAD
    rows = RW_CHUNK * RW_CHUNKS_PER_STEP
    assert Lc % rows == 0 and T % rows == 0
    W = RW_GROUP * HEAD
    ng = math.gcd(D // W, RW_GROUPS_PER_STEP)
    nb = T // rows
    ncb = Lc // rows

    def block(c):
        if not reverse:
            return c
        return jnp.where(c < ncb, ncb - 1 - c, nb - 1 - (c - ncb))

    blk = pl.BlockSpec((1, rows, ng * W), lambda b, g, c: (b, block(c), g))
    vec = pl.BlockSpec((1, ng * W), lambda b, g, c: (0, g))
    return pl.pallas_call(
        functools.partial(_rw_scan_kernel, reverse=reverse),
        grid=(B, D // (ng * W), nb),
        in_specs=[blk] * 5 + [vec, vec],
        out_specs=blk,
        out_shape=jax.ShapeDtypeStruct((B, T, D), F32),
        scratch_shapes=[pltpu.VMEM((ng, W, W), F32)],
        compiler_params=_params("parallel", "parallel", "arbitrary"),
        name="rw_scan",
    )(r, k, v, lw, a, k_k.reshape(1, D), k_a.reshape(1, D))


def _head_sums(x, ones_bd):
    W = ones_bd.shape[0]
    n = x.shape[0]
    groups = range(x.shape[1] // W)
    hi = x.astype(BF16)
    lo = (x - hi.astype(F32)).astype(BF16)
    rows = jnp.concatenate([t[:, g * W:(g + 1) * W] for t in (hi, lo) for g in groups], axis=0)
    sums = jnp.dot(rows, ones_bd, preferred_element_type=F32)
    half = len(groups) * n
    return jnp.concatenate([sums[g * n:(g + 1) * n] + sums[half + g * n:half + (g + 1) * n]
                            for g in groups], axis=1)


def _rw_post_kernel(of_ref, ob_ref, r_ref, k_ref, v_ref, g_ref, rk_ref, gw_ref, gb_ref, o_ref):
    W = RW_GROUP * HEAD
    ri = lax.broadcasted_iota(jnp.int32, (W, W), 0)
    ci = lax.broadcasted_iota(jnp.int32, (W, W), 1)
    ones_bd = jnp.where((ri // HEAD) == (ci // HEAD), 1.0, 0.0).astype(BF16)
    o = of_ref[0] + ob_ref[0]
    m = _head_sums(o, ones_bd) * (1.0 / HEAD)
    d = o - m
    var = _head_sums(d * d, ones_bd) * (1.0 / HEAD)
    on = d * lax.rsqrt(var + RW_GN_EPS) * gw_ref[...] + gb_ref[...]
    bonus = _head_sums(r_ref[0] * k_ref[0] * rk_ref[...], ones_bd) * v_ref[0]
    o_ref[0] = ((on + bonus) * g_ref[0]).astype(o_ref.dtype)


def _rw_post(o_f, o_b, r, k, v, g, r_k, gn_w, gn_b):
    B, T, D = r.shape
    tt = _tile(T, ROW_TILE // 2, SUBLANES)
    blk = pl.BlockSpec((1, tt, D), lambda b, t: (b, t, 0))
    par = pl.BlockSpec((1, D), lambda b, t: (0, 0))
    return pl.pallas_call(
        _rw_post_kernel,
        grid=(B, T // tt),
        in_specs=[blk] * 6 + [par] * 3,
        out_specs=blk,
        out_shape=jax.ShapeDtypeStruct((B, T, D), BF16),
        compiler_params=_params("parallel", "parallel"),
        name="rw_post",
    )(o_f, o_b, r, k, v, g, r_k.reshape(1, D), gn_w.reshape(1, D), gn_b.reshape(1, D))


def _rwkv(hs, p, *, Lc):
    r, k, v, g, dirs = _rw_project(hs, p, ctx_tiles=Lc // ROW_TILE)
    outs = [_rw_scan(r, k, v, lw, a, p["k_k"], p["k_a"], Lc=Lc, reverse=d == 1)
            for d, (lw, a) in enumerate(dirs)]
    return _rw_post(outs[0], outs[1], r, k, v, g, p["r_k"], p["gn_w"], p["gn_b"])


def _rope_tables(T, Lc):
    quarter = HEAD // 4
    pos = jnp.arange(T - Lc, dtype=jnp.int32)
    rows = (pos // GRID_W).astype(F32)
    cols = (pos % GRID_W).astype(F32)
    inv = ROPE_BASE ** (-jnp.arange(quarter, dtype=F32) / quarter)
    ang = jnp.concatenate([rows[:, None] * inv, rows[:, None] * inv,
                           cols[:, None] * inv, cols[:, None] * inv], axis=1)
    ang = jnp.concatenate([jnp.zeros((Lc, HEAD), F32), ang], axis=0)
    ang = jnp.concatenate([ang, ang], axis=1)
    return jnp.cos(ang), jnp.sin(ang)


def _qkv_rope_kernel(h_ref, w_ref, b_ref, cos_ref, sin_ref, q_ref, k_ref, v_ref, *, n_q, n_kv, scale):
    x = jnp.dot(h_ref[0], w_ref[0], preferred_element_type=F32) + b_ref[...]
    n_rot = (n_q + n_kv) * HEAD
    xr = x[:, :n_rot]
    reps = n_rot // LANES
    cos = jnp.concatenate([cos_ref[...]] * reps, axis=1)
    sin = jnp.concatenate([sin_ref[...]] * reps, axis=1)
    quarter = HEAD // 4
    lane = lax.broadcasted_iota(jnp.int32, xr.shape, 1)
    first = (lane % (2 * quarter)) < quarter
    rot = jnp.where(first, -pltpu.roll(xr, n_rot - quarter, axis=1), pltpu.roll(xr, quarter, axis=1))
    y = xr * cos + rot * sin
    for h in range(n_q):
        q_ref[0, h] = (y[:, h * HEAD:(h + 1) * HEAD] * scale).astype(q_ref.dtype)
    for h in range(n_kv):
        lo = (n_q + h) * HEAD
        k_ref[0, h] = y[:, lo:lo + HEAD].astype(k_ref.dtype)
        lo = (n_q + n_kv + h) * HEAD
        v_ref[0, h, :, :HEAD] = x[:, lo:lo + HEAD].astype(v_ref.dtype)
        v_ref[0, h, :, HEAD:] = jnp.ones((x.shape[0], HEAD), v_ref.dtype)


def _qkv_rope(h, w, wi, bias, cos, sin, *, n_q, n_kv):
    B, T, K = h.shape
    W = w.shape[2]
    tt = _tile(T, ROW_TILE, BF16_ROWS)
    out = lambda n, width=HEAD: pl.BlockSpec((1, n, tt, width), lambda b, t: (b, 0, t, 0))
    return pl.pallas_call(
        functools.partial(_qkv_rope_kernel, n_q=n_q, n_kv=n_kv, scale=HEAD ** -0.5),
        grid=(B, T // tt),
        in_specs=[pl.BlockSpec((1, tt, K), lambda b, t: (b, t, 0)),
                  pl.BlockSpec((1, K, W), lambda b, t: (wi, 0, 0), pipeline_mode=pl.Buffered(1)),
                  pl.BlockSpec((1, W), lambda b, t: (0, 0)),
                  pl.BlockSpec((tt, LANES), lambda b, t: (t, 0)),
                  pl.BlockSpec((tt, LANES), lambda b, t: (t, 0))],
        out_specs=[out(n_q), out(n_kv), out(n_kv, 2 * HEAD)],
        out_shape=[jax.ShapeDtypeStruct((B, n_q, T, HEAD), BF16),
                   jax.ShapeDtypeStruct((B, n_kv, T, HEAD), BF16),
                   jax.ShapeDtypeStruct((B, n_kv, T, 2 * HEAD), BF16)],
        compiler_params=_params("parallel", "parallel"),
        name="qkv_rope",
    )(h, w, bias.reshape(1, W).astype(F32), cos, sin)


def _attn_kernel(*refs, n_q, n_kv, local, n_blocks):
    if local:
        q_ref, kp_ref, kc_ref, kn_ref, vp_ref, vc_ref, vn_ref, kx_ref, vx_ref, sink_ref, o_ref = refs
    else:
        q_ref, kx_ref, vx_ref, sink_ref, o_ref = refs
    G = n_q // n_kv
    R = G * AT_BLOCK
    n = pl.program_id(1)
    nt = (((1,), (1,)), ((), ()))
    if local:
        qi = lax.broadcasted_iota(jnp.int32, (R, 3 * AT_BLOCK), 0) % AT_BLOCK
        kj = lax.broadcasted_iota(jnp.int32, (R, 3 * AT_BLOCK), 1) - AT_BLOCK
        kpos = kj + n * AT_BLOCK
        ok = (jnp.abs(qi - kj) <= AT_WINDOW) & (kpos >= 0) & (kpos < n_blocks * AT_BLOCK)
    H = range(n_kv)
    q = [q_ref[0, h * G:(h + 1) * G].reshape(R, HEAD) for h in H]
    sink = [jnp.concatenate(
        [jnp.broadcast_to(sink_ref[h * G + g:h * G + g + 1, 0:1], (AT_BLOCK, 1)) for g in range(G)],
        axis=0) for h in H]
    s_ctx = [lax.dot_general(q[h], kx_ref[0, h], nt, preferred_element_type=F32) for h in H]
    m = [jnp.maximum(jnp.max(s_ctx[h], axis=-1, keepdims=True), sink[h]) for h in H]
    if local:
        k_loc = [jnp.concatenate([kp_ref[0, h], kc_ref[0, h], kn_ref[0, h]], axis=0) for h in H]
        v_loc = [jnp.concatenate([vp_ref[0, h], vc_ref[0, h], vn_ref[0, h]], axis=0) for h in H]
        s_loc = [jnp.where(ok, lax.dot_general(q[h], k_loc[h], nt, preferred_element_type=F32), NEG_INF)
                 for h in H]
        m = [jnp.maximum(m[h], jnp.max(s_loc[h], axis=-1, keepdims=True)) for h in H]
    o = [jnp.dot(jnp.exp(s_ctx[h] - m[h]).astype(BF16), vx_ref[0, h], preferred_element_type=F32)
         for h in H]
    if local:
        o = [o[h] + jnp.dot(jnp.exp(s_loc[h] - m[h]).astype(BF16), v_loc[h], preferred_element_type=F32)
             for h in H]
    den = [o[h][:, HEAD:HEAD + 1] + jnp.exp(sink[h] - m[h]) for h in H]
    o = [o[h][:, :HEAD] * (1.0 / den[h]) for h in H]
    for h in H:
        for g in range(G):
            hh = h * G + g
            o_ref[0, :, hh * HEAD:(hh + 1) * HEAD] = o[h][g * AT_BLOCK:(g + 1) * AT_BLOCK].astype(o_ref.dtype)


def _attn(q, k, v, sink, *, Lc, local):
    B, n_q, T, _ = q.shape
    n_kv = k.shape[1]
    cb = Lc // AT_BLOCK
    nb = (T - Lc) // AT_BLOCK if local else cb
    q_off = cb if local else 0
    last = T // AT_BLOCK - 1
    qspec = pl.BlockSpec((1, n_q, AT_BLOCK, HEAD), lambda b, n: (b, 0, n + q_off, 0))
    xspec = lambda a: pl.BlockSpec((1, n_kv, Lc, a.shape[3]), lambda b, n: (b, 0, 0, 0))
    sspec = pl.BlockSpec((n_q, LANES), lambda b, n: (0, 0))
    sink_b = jnp.broadcast_to(sink.astype(F32)[:, None], (n_q, LANES))
    if local:
        blk = lambda a, f: pl.BlockSpec((1, n_kv, AT_BLOCK, a.shape[3]), lambda b, n: (b, 0, f(n), 0))
        prev = lambda n: jnp.maximum(n + cb - 1, cb)
        cur = lambda n: n + cb
        nxt = lambda n: jnp.minimum(n + cb + 1, last)
        in_specs = ([qspec] + [blk(k, f) for f in (prev, cur, nxt)] + [blk(v, f) for f in (prev, cur, nxt)]
                    + [xspec(k), xspec(v), sspec])
        args = (q, k, k, k, v, v, v, k, v, sink_b)
    else:
        in_specs = [qspec, xspec(k), xspec(v), sspec]
        args = (q, k, v, sink_b)
    return pl.pallas_call(
        functools.partial(_attn_kernel, n_q=n_q, n_kv=n_kv, local=local, n_blocks=nb),
        grid=(B, nb),
        in_specs=in_specs,
        out_specs=pl.BlockSpec((1, AT_BLOCK, n_q * HEAD), lambda b, n: (b, n, 0)),
        out_shape=jax.ShapeDtypeStruct((B, nb * AT_BLOCK, n_q * HEAD), BF16),
        compiler_params=_params("parallel", "parallel"),
        name="attn_local" if local else "attn_ctx",
    )(*args)


def _attention(h, p, *, Lc):
    B, T, D = h.shape
    n_q = D // HEAD
    cos, sin = _rope_tables(T, Lc)
    q, k, v = _qkv_rope(h, p["w_qkv"], p["index"], p["b_qkv"], cos, sin, n_q=n_q, n_kv=AT_KV_HEADS)
    o_ctx = _attn(q, k, v, p["sink"], Lc=Lc, local=False)
    o_lat = _attn(q, k, v, p["sink"], Lc=Lc, local=True)
    return jnp.concatenate([o_ctx, o_lat], axis=1)


MOE_TILE = 1024
ROUTE_G1, ROUTE_G2, ROUTE_I1, ROUTE_I2 = 0, 1, 2, 3


def _gates_kernel(l_ref, route_ref, sel_ref, *, n_experts):
    l = l_ref[...]
    lane = lax.broadcasted_iota(jnp.int32, l.shape, 1)
    l = jnp.where(lane < n_experts, l, -jnp.inf)
    m1 = jnp.max(l, axis=-1, keepdims=True)
    i1 = jnp.min(jnp.where(l == m1, lane, LANES), axis=-1, keepdims=True)
    l2 = jnp.where(lane == i1, -jnp.inf, l)
    m2 = jnp.max(l2, axis=-1, keepdims=True)
    i2 = jnp.min(jnp.where(l2 == m2, lane, LANES), axis=-1, keepdims=True)
    e2 = jnp.exp(m2 - m1)
    den = 1.0 + e2
    route_ref[...] = jnp.where(lane == ROUTE_G1, 1.0 / den,
                               jnp.where(lane == ROUTE_G2, e2 / den,
                                         jnp.where(lane == ROUTE_I1, i1.astype(F32),
                                                   jnp.where(lane == ROUTE_I2, i2.astype(F32), 0.0))))
    sel_ref[...] = jnp.where((lane == i1) | (lane == i2), 1.0, 0.0).astype(sel_ref.dtype)


def _gates(logits, n_experts):
    M = logits.shape[0]
    tm = _tile(M, MM_ROWS, SUBLANES)
    blk = pl.BlockSpec((tm, LANES), lambda i: (i, 0))
    return pl.pallas_call(
        functools.partial(_gates_kernel, n_experts=n_experts),
        grid=(M // tm,),
        in_specs=[blk],
        out_specs=[blk, blk],
        out_shape=[jax.ShapeDtypeStruct((M, LANES), F32), jax.ShapeDtypeStruct((M, LANES), BF16)],
        compiler_params=_params("parallel"),
        name="moe_gates",
    )(logits)


def _rank_kernel(sel_ref, rank_ref, cnt_ref, carry_ref):
    @pl.when(pl.program_id(0) == 0)
    def _():
        carry_ref[...] = jnp.zeros_like(carry_ref)

    s = sel_ref[...]
    n = s.shape[0]
    earlier = (lax.broadcasted_iota(jnp.int32, (n, n), 1) < lax.broadcasted_iota(jnp.int32, (n, n), 0))
    within = jnp.dot(jnp.where(earlier, 1.0, 0.0).astype(BF16), s, preferred_element_type=F32)
    rank_ref[...] = within + carry_ref[...]
    carry_ref[...] += jnp.sum(s.astype(F32), axis=0, keepdims=True)
    cnt_ref[...] = carry_ref[...]


def _rank(sel):
    M = sel.shape[0]
    tr = _tile(M, RANK_ROWS, SUBLANES)
    return pl.pallas_call(
        _rank_kernel,
        grid=(M // tr,),
        in_specs=[pl.BlockSpec((tr, LANES), lambda i: (i, 0))],
        out_specs=[pl.BlockSpec((tr, LANES), lambda i: (i, 0)), pl.BlockSpec((1, LANES), lambda i: (0, 0))],
        out_shape=[jax.ShapeDtypeStruct((M, LANES), F32), jax.ShapeDtypeStruct((1, LANES), F32)],
        scratch_shapes=[pltpu.VMEM((1, LANES), F32)],
        compiler_params=_params("arbitrary"),
        name="moe_rank",
    )(sel)


def _pos_kernel(route_ref, rank_ref, offs_ref, pos_ref):
    lane = lax.broadcasted_iota(jnp.int32, rank_ref.shape, 1)
    lane_f = lane.astype(F32)
    tot = rank_ref[...] + offs_ref[...]
    route = route_ref[...]
    p1 = jnp.sum(jnp.where(lane_f == route[:, ROUTE_I1:ROUTE_I1 + 1], tot, 0.0), axis=-1, keepdims=True)
    p2 = jnp.sum(jnp.where(lane_f == route[:, ROUTE_I2:ROUTE_I2 + 1], tot, 0.0), axis=-1, keepdims=True)
    pos_ref[...] = jnp.where(lane == 0, p1, jnp.where(lane == 1, p2, 0.0)).astype(jnp.int32)


def _positions(route, rank, offs):
    M = route.shape[0]
    tm = _tile(M, MM_ROWS, SUBLANES)
    blk = pl.BlockSpec((tm, LANES), lambda i: (i, 0))
    return pl.pallas_call(
        _pos_kernel,
        grid=(M // tm,),
        in_specs=[blk, blk, pl.BlockSpec((1, LANES), lambda i: (0, 0))],
        out_specs=blk,
        out_shape=jax.ShapeDtypeStruct((M, LANES), jnp.int32),
        compiler_params=_params("parallel"),
        name="moe_pos",
    )(route, rank, offs)


def _row_copy(src, dst, sem):
    return pltpu.make_async_copy(src, dst, sem)


def _dispatch_kernel(p1_ref, p2_ref, h_ref, xs_in_ref, xs_ref, sem):
    del xs_in_ref
    tt = h_ref.shape[0]
    base = pl.program_id(0) * tt

    def start(r, carry):
        row = h_ref.at[pl.ds(r, 1)]
        _row_copy(row, xs_ref.at[pl.ds(p1_ref[base + r], 1)], sem).start()
        _row_copy(row, xs_ref.at[pl.ds(p2_ref[base + r], 1)], sem).start()
        return carry

    lax.fori_loop(0, tt, start, 0, unroll=8)
    for _ in range(2):
        _row_copy(h_ref, xs_ref.at[pl.ds(0, tt)], sem).wait()


def _dispatch(h, p1, p2, n_rows):
    M, D = h.shape
    tt = _tile(M, ROW_TILE, SUBLANES)
    return pl.pallas_call(
        _dispatch_kernel,
        grid_spec=pltpu.PrefetchScalarGridSpec(
            num_scalar_prefetch=2,
            grid=(M // tt,),
            in_specs=[pl.BlockSpec((tt, D), lambda i, p1, p2: (i, 0)),
                      pl.BlockSpec(memory_space=pl.ANY)],
            out_specs=pl.BlockSpec(memory_space=pl.ANY),
            scratch_shapes=[pltpu.SemaphoreType.DMA(())]),
        out_shape=jax.ShapeDtypeStruct((n_rows, D), h.dtype),
        input_output_aliases={3: 0},
        compiler_params=_params("arbitrary"),
        name="moe_dispatch",
    )(p1, p2, h, jnp.zeros((n_rows, D), h.dtype))


def _gmm_kernel(te_ref, nv_ref, x_ref, *refs, swiglu):
    o_ref = refs[-1]

    @pl.when(pl.program_id(0) < nv_ref[0])
    def _():
        x = x_ref[...].astype(BF16)
        acc = jnp.dot(x, refs[0][0, 0], preferred_element_type=F32)
        if swiglu:
            acc = acc * jax.nn.sigmoid(acc) * jnp.dot(x, refs[1][0, 0], preferred_element_type=F32)
        o_ref[...] = acc.astype(o_ref.dtype)

    @pl.when(pl.program_id(0) >= nv_ref[0])
    def _():
        o_ref[...] = jnp.zeros_like(o_ref)


def _gmm(x, ws, layer, tile_expert, n_valid, *, out_dtype, tn=WIDE_COLS):
    K = x.shape[1]
    N = ws[0].shape[3]
    tm = MOE_TILE
    P = tile_expert.shape[0] * tm
    tn = _tile(N, tn)
    wspec = pl.BlockSpec((1, 1, K, tn), lambda i, j, te, nv: (layer, te[i], 0, j))
    return pl.pallas_call(
        functools.partial(_gmm_kernel, swiglu=len(ws) == 2),
        grid_spec=pltpu.PrefetchScalarGridSpec(
            num_scalar_prefetch=2,
            grid=(P // tm, N // tn),
            in_specs=[pl.BlockSpec((tm, K), lambda i, j, te, nv: (jnp.minimum(i, nv[0] - 1), 0))]
            + [wspec] * len(ws),
            out_specs=pl.BlockSpec((tm, tn), lambda i, j, te, nv: (i, j))),
        out_shape=jax.ShapeDtypeStruct((P, N), out_dtype),
        compiler_params=_params("parallel", "parallel"),
        name="moe_gmm",
    )(tile_expert, n_valid, x, *ws)


def _combine_ln_kernel(p1_ref, p2_ref, route_ref, ys_ref, x_ref, mod_ref, modh_ref, w_ref, b_ref, *rest,
                       alpha, gate, sh, sc, has_h):
    a_ref, b2_ref, sem = rest[-3:]
    xo_ref = rest[0]
    tt = xo_ref.shape[1]
    base = (pl.program_id(0) * pl.num_programs(1) + pl.program_id(1)) * tt

    def start(r, carry):
        _row_copy(ys_ref.at[pl.ds(p1_ref[base + r], 1)], a_ref.at[pl.ds(r, 1)], sem).start()
        _row_copy(ys_ref.at[pl.ds(p2_ref[base + r], 1)], b2_ref.at[pl.ds(r, 1)], sem).start()
        return carry

    lax.fori_loop(0, tt, start, 0, unroll=8)
    _row_copy(ys_ref.at[pl.ds(0, tt)], a_ref, sem).wait()
    _row_copy(ys_ref.at[pl.ds(0, tt)], b2_ref, sem).wait()
    route = route_ref[0]
    f = route[:, ROUTE_G1:ROUTE_G1 + 1] * a_ref[...] + route[:, ROUTE_G2:ROUTE_G2 + 1] * b2_ref[...]
    xn = _post_norm(x_ref[0], f, mod_ref, w_ref, b_ref, alpha=alpha, gate=gate)
    xo_ref[0] = xn
    if has_h:
        h = xn * (1.0 + modh_ref[0, sc:sc + 1, :]) + modh_ref[0, sh:sh + 1, :]
        rest[1][0] = h.astype(rest[1].dtype)


def _combine_ln(ys, route, p1, p2, x, mod, ln_w, ln_b, *, alpha, gate, h_mod=None, mod_h=None,
                h_dtype=BF16, ctx_tiles, ctx_row):
    B, T, D = x.shape
    tt = ROW_TILE
    nt = T // tt
    smap = lambda f: (lambda b, t, p1, p2: f(b, t))
    row = pl.BlockSpec((1, tt, D), smap(lambda b, t: (b, t, 0)))
    vec = pl.BlockSpec((1, D), smap(lambda b, t: (0, 0)))
    mspec = pl.BlockSpec((1, N_MOD, D), smap(_mod_index(ctx_tiles, ctx_row, 0)))
    out_specs = [row]
    out_shape = [jax.ShapeDtypeStruct((B, T, D), F32)]
    sh = sc = 0
    if h_mod is not None:
        sh, sc = h_mod
        out_specs.append(row)
        out_shape.append(jax.ShapeDtypeStruct((B, T, D), h_dtype))
    return pl.pallas_call(
        functools.partial(_combine_ln_kernel, alpha=alpha, gate=gate, sh=sh, sc=sc,
                          has_h=h_mod is not None),
        grid_spec=pltpu.PrefetchScalarGridSpec(
            num_scalar_prefetch=2,
            grid=(B, nt),
            in_specs=[pl.BlockSpec((1, tt, LANES), smap(lambda b, t: (b, t, 0))),
                      pl.BlockSpec(memory_space=pl.ANY), row, mspec, mspec, vec, vec],
            out_specs=out_specs,
            scratch_shapes=[pltpu.VMEM((tt, D), F32), pltpu.VMEM((tt, D), F32),
                            pltpu.SemaphoreType.DMA(())]),
        out_shape=out_shape,
        compiler_params=_params("arbitrary", "arbitrary"),
        name="moe_combine_ln",
    )(p1, p2, route.reshape(B, T, LANES), ys, x, mod, mod if mod_h is None else mod_h,
      ln_w.reshape(1, D), ln_b.reshape(1, D))


def _moe(h, logits, w1, w3, w2, layer):
    M, D = h.shape
    E = w1.shape[1]
    tm = MOE_TILE
    route, sel = _gates(logits, E)
    rank, cnt = _rank(sel)
    counts = cnt[0, :E].astype(jnp.int32)
    padded = (counts + tm - 1) // tm * tm
    ends = jnp.cumsum(padded)
    n_tiles = (TOP_K * M) // tm + E
    tile_expert = jnp.minimum(
        jnp.searchsorted(ends, jnp.arange(n_tiles, dtype=jnp.int32) * tm, side="right"), E - 1
    ).astype(jnp.int32)
    n_valid = (ends[-1:] // tm).astype(jnp.int32)
    offs = jnp.zeros((1, LANES), F32).at[0, :E].set((ends - padded).astype(F32))
    pos = _positions(route, rank, offs)
    p1, p2 = pos[:, 0], pos[:, 1]
    xs = _dispatch(h, p1, p2, n_tiles * tm)
    mid = _gmm(xs, [w1, w3], layer, tile_expert, n_valid, out_dtype=BF16)
    ys = _gmm(mid, [w2], layer, tile_expert, n_valid, out_dtype=F32)
    return ys, route, p1, p2


def kernel(x, c, ctx, c_ctx, ada_w, ada_b, ln_w, ln_b, hy_w_in, hy_b_in, hy_conv_w, hy_conv_b, hy_f_w1, hy_f_b1, hy_f_w2, hy_f_b2, hy_f_w3, hy_f_freq, hy_skip, hy_w_out, hy_b_out, rw_mu, rw_w_rkv, rw_w_o, rw_w0, rw_w1, rw_w2, rw_a0, rw_a1, rw_a2, rw_g1, rw_g2, rw_k_k, rw_k_a, rw_r_k, rw_gn_w, rw_gn_b, at_w_qkv, at_b_qkv, at_w_o, at_b_o, at_sink, ff_w1, ff_w3, ff_w2, moe_router, moe_w1, moe_w3, moe_w2):
    B, L, D = x.shape
    Lc = ctx.shape[1]
    depth = ada_w.shape[0]
    alpha = (2 * depth) ** 0.25
    assert Lc % ROW_TILE == 0 and L % ROW_TILE == 0 and D % (RW_GROUP * HEAD) == 0
    bf = lambda t: t.astype(BF16)
    w_bf = dict(hy_out=bf(hy_w_out), rw_o=bf(rw_w_o), at_qkv=bf(at_w_qkv), at_o=bf(at_w_o),
                ff1=bf(ff_w1), ff3=bf(ff_w3), ff2=bf(ff_w2),
                moe1=bf(moe_w1), moe3=bf(moe_w3), moe2=bf(moe_w2))

    ctx_row = B
    rows = -(-(B + 1) // 8) * 8
    cond = jnp.zeros((rows, D), F32).at[:B].set(c).at[B].set(c_ctx)
    mods = _ada(cond, ada_w, ada_b).reshape(depth, rows, N_MOD, D)

    tables = {}
    if depth > 0:
        tables[L] = _dft_tables(L)
        if depth > 1:
            tables[Lc] = _dft_tables(Lc)

    ctx_tiles = Lc // ROW_TILE
    xs, h = _stream(ctx, x, mods[0], sh=0, sc=1, ctx_row=ctx_row, out_dtype=BF16)
    for i in range(depth):
        last = i == depth - 1
        kind = i % 3
        j = i // 3
        mod = mods[i]
        has_ctx = xs.shape[1] != L
        ct = ctx_tiles if has_ctx else 0
        T = xs.shape[1]
        if kind == 0:
            p = dict(w_in=bf(hy_w_in[j]), b_in=hy_b_in[j], conv_w=hy_conv_w[j], conv_b=hy_conv_b[j],
                     filter=(hy_f_w1[j], hy_f_b1[j], hy_f_w2[j], hy_f_b2[j], hy_f_w3[j], hy_f_freq[j]),
                     skip=hy_skip[j])
            y = _hyena(h, p, tables, Lc=Lc, has_ctx=has_ctx)
            proj = (w_bf["hy_out"], j, hy_b_out[j])
        elif kind == 1:
            assert has_ctx and not last
            p = dict(mu=rw_mu[j], w_rkv=bf(rw_w_rkv[j]), w0=rw_w0[j], w1=rw_w1[j],
                     w2=rw_w2[j], a0=rw_a0[j], a1=rw_a1[j], a2=rw_a2[j], g1=rw_g1[j], g2=rw_g2[j],
                     k_k=rw_k_k[j], k_a=rw_k_a[j], r_k=rw_r_k[j], gn_w=rw_gn_w[j], gn_b=rw_gn_b[j])
            y = _rwkv(h, p, Lc=Lc)
            proj = (w_bf["rw_o"], j, None)
        else:
            assert has_ctx and not last
            p = dict(w_qkv=w_bf["at_qkv"], index=j, b_qkv=at_b_qkv[j], sink=at_sink[j])
            y = _attention(h, p, Lc=Lc)
            proj = (w_bf["at_o"], j, at_b_o[j])

        drop1 = ct if last else 0
        fj = i // 2
        moe = i % 2 == 1
        if moe:
            pw, pi, pb = proj
            y = _mm(y.reshape(B * T, D), pw, wi=pi, bias=pb).reshape(B, T, D)
            proj = None
        res = _ln(xs, y, mod, ln_w[i, 0], ln_b[i, 0], alpha=alpha, gate=2, proj=proj, h_mod=(3, 4),
                  h_dtype=F32 if moe else BF16, router=moe_router[fj] if moe else None,
                  ctx_tiles=ct, ctx_row=ctx_row, drop_tiles=drop1)
        xs, h2 = res[0], res[1]
        ct = ct - drop1
        T = xs.shape[1]
        M = B * T
        nxt_last = i + 1 == depth - 1
        nkind = (i + 1) % 3
        drop2 = ct if (not last and nxt_last and nkind == 0) else 0
        nxt = {} if last else dict(h_mod=(0, 1), mod_h=mods[i + 1], h_dtype=F32 if nkind == 1 else BF16)
        if moe:
            routed = _moe(h2.reshape(M, D), res[2].reshape(M, LANES), w_bf["moe1"], w_bf["moe3"],
                          w_bf["moe2"], fj)
            res = _combine_ln(*routed, xs, mod, ln_w[i, 1], ln_b[i, 1], alpha=alpha, gate=5,
                              ctx_tiles=ct, ctx_row=ctx_row, **nxt)
            if drop2:
                res = [lax.slice_in_dim(t, drop2 * ROW_TILE, T, axis=1) for t in res]
        else:
            f = _mm(h2.reshape(M, D), w_bf["ff1"], w3=w_bf["ff3"], wi=fj, out_dtype=BF16).reshape(B, T, -1)
            res = _ln(xs, f, mod, ln_w[i, 1], ln_b[i, 1], alpha=alpha, gate=5, proj=(w_bf["ff2"], fj, None),
                      ctx_tiles=ct, ctx_row=ctx_row, drop_tiles=drop2, **nxt)
        xs = res[0]
        if not last:
            h = res[1]
    return xs
```
